```python
import jax, jax.numpy as jnp
from jax import lax
import numpy as np

D_MODEL = 2048
BATCH = 8
SEQ = 8192
DEPTH = 4

N_MIXERS = 2
N_ATTN_LAYERS = (DEPTH + N_MIXERS - 1) // N_MIXERS
N_POOL_LAYERS = DEPTH // N_MIXERS
HEAD_DIM = 64
N_HEADS = D_MODEL // HEAD_DIM
N_KV_HEADS = N_HEADS // 8
GQA_GROUP = N_HEADS // N_KV_HEADS
WINDOW = 128
BLOCK = WINDOW
N_BUCKETS = 32
MAX_DISTANCE = 128
POOL_WINDOWS = (2, 4, 8, 16)
N_POOL_GROUPS = len(POOL_WINDOWS)
POOL_GROUP_DIM = D_MODEL // N_POOL_GROUPS
D_FF = 5632
CONV_WIDTH = 3
EPS = 1e-6
NEG_INF = -1e30

kernel_name = "hybrid_swa_sink_pool_convffn"


def rmsnorm(x, gain):
    xf = x.astype(jnp.float32)
    y = xf * lax.rsqrt(jnp.mean(xf * xf, axis=-1, keepdims=True) + EPS)
    return (y * gain.astype(jnp.float32)).astype(x.dtype)


def _t5_band_buckets():
    i = np.arange(BLOCK)[:, None]
    j = np.arange(2 * BLOCK)[None, :]
    n = np.maximum(BLOCK + i - j, 0)
    max_exact = N_BUCKETS // 2
    nf = np.maximum(n, 1).astype(np.float32)
    large = max_exact + (np.log(nf / max_exact) / np.log(MAX_DISTANCE / max_exact)
                         * (N_BUCKETS - max_exact)).astype(np.int32)
    large = np.minimum(large, N_BUCKETS - 1)
    return np.where(n < max_exact, n, large).astype(np.int32)


def _band_mask(n_blocks):
    i = np.arange(BLOCK)[:, None]
    j = np.arange(2 * BLOCK)[None, :]
    dist = BLOCK + i - j
    in_win = (dist >= 0) & (dist < WINDOW)
    key_pos = np.arange(n_blocks)[:, None, None] * BLOCK - BLOCK + j[None]
    return in_win[None] & (key_pos >= 0)


def sliding_window_attention(h, w_qkv, q_gain, k_gain, sinks, rel_bias, w_o):
    B, S, _ = h.shape
    nb = S // BLOCK
    qkv = h @ w_qkv
    q, k, v = jnp.split(qkv, [N_HEADS * HEAD_DIM, (N_HEADS + N_KV_HEADS) * HEAD_DIM], axis=-1)
    q = rmsnorm(q.reshape(B, S, N_HEADS, HEAD_DIM), q_gain)
    k = rmsnorm(k.reshape(B, S, N_KV_HEADS, HEAD_DIM), k_gain)
    v = v.reshape(B, S, N_KV_HEADS, HEAD_DIM)
    q = q.reshape(B, nb, BLOCK, N_KV_HEADS, GQA_GROUP, HEAD_DIM)

    def band(t):
        t = t.reshape(B, nb, BLOCK, N_KV_HEADS, HEAD_DIM)
        prev = jnp.pad(t[:, :-1], ((0, 0), (1, 0), (0, 0), (0, 0), (0, 0)))
        return jnp.concatenate([prev, t], axis=2)

    kb, vb = band(k), band(v)
    s = jnp.einsum('bnqkgd,bnskd->bnkgqs', q, kb).astype(jnp.float32) * (HEAD_DIM ** -0.5)
    bias = rel_bias[:, _t5_band_buckets()].astype(jnp.float32)
    bias = bias.reshape(N_KV_HEADS, GQA_GROUP, BLOCK, 2 * BLOCK)
    mask = jnp.asarray(_band_mask(nb))[None, :, None, None]
    s = jnp.where(mask, s + bias, NEG_INF)
    sink = sinks.astype(jnp.float32).reshape(N_KV_HEADS, GQA_GROUP)[:, :, None, None]
    m = jnp.maximum(jnp.max(s, axis=-1, keepdims=True), sink)
    p = jnp.exp(s - m)
    p = p / (jnp.sum(p, axis=-1, keepdims=True) + jnp.exp(sink - m))
    o = jnp.einsum('bnkgqs,bnskd->bnqkgd', p.astype(vb.dtype), vb)
    return o.reshape(B, S, N_HEADS * HEAD_DIM) @ w_o


def multiscale_pool_mixer(h, w_pool, scale):
    B, S, _ = h.shape
    hg = h.reshape(B, S, N_POOL_GROUPS, POOL_GROUP_DIM)
    hf = hg.astype(jnp.float32)
    c = jnp.cumsum(hf, axis=1)
    t = jnp.arange(S)
    means = []
    for g, w in enumerate(POOL_WINDOWS):
        cg = c[:, :, g]
        lag = jnp.pad(cg, ((0, 0), (w, 0), (0, 0)))[:, :S]
        cnt = jnp.minimum(t + 1, w).astype(jnp.float32)[None, :, None]
        means.append((cg - lag) / cnt)
    d = (jnp.stack(means, axis=2) - hf).astype(h.dtype)
    y = jnp.einsum('bsgc,gce->bsge', d, w_pool).reshape(B, S, D_MODEL)
    return y * scale


def conv_gated_mlp(h, w_up, conv_w, conv_b, w_down):
    S = h.shape[1]
    u = h @ w_up
    up = jnp.pad(u, ((0, 0), (CONV_WIDTH - 1, 0), (0, 0)))
    u = sum(conv_w[k] * up[:, k:k + S] for k in range(CONV_WIDTH)) + conv_b
    gate, val = jnp.split(u, 2, axis=-1)
    return (jax.nn.silu(gate) * val) @ w_down


def _fwd_setup_inputs(seed: int = 0) -> dict:
    key = jax.random.key(seed)
    ks = jax.random.split(key, 16)
    f32 = jnp.float32
    qkv_out = (N_HEADS + 2 * N_KV_HEADS) * HEAD_DIM
    return {
        "x": jax.random.normal(ks[0], (BATCH, SEQ, D_MODEL), f32),
        "norm_mix": 1.0 + 0.05 * jax.random.normal(ks[1], (DEPTH, D_MODEL), f32),
        "norm_ffn": 1.0 + 0.05 * jax.random.normal(ks[2], (DEPTH, D_MODEL), f32),
        "rel_bias": 0.5 * jax.random.normal(ks[3], (N_HEADS, N_BUCKETS), f32),
        "attn_w_qkv": jax.random.normal(ks[4], (N_ATTN_LAYERS, D_MODEL, qkv_out), f32) * D_MODEL ** -0.5,
        "attn_q_gain": 1.0 + 0.05 * jax.random.normal(ks[5], (N_ATTN_LAYERS, HEAD_DIM), f32),
        "attn_k_gain": 1.0 + 0.05 * jax.random.normal(ks[6], (N_ATTN_LAYERS, HEAD_DIM), f32),
        "attn_sinks": 0.5 * jax.random.normal(ks[7], (N_ATTN_LAYERS, N_HEADS), f32),
        "attn_w_o": jax.random.normal(ks[8], (N_ATTN_LAYERS, N_HEADS * HEAD_DIM, D_MODEL), f32) * (N_HEADS * HEAD_DIM) ** -0.5,
        "pool_w": jax.random.normal(ks[9], (N_POOL_LAYERS, N_POOL_GROUPS, POOL_GROUP_DIM, POOL_GROUP_DIM), f32) * POOL_GROUP_DIM ** -0.5,
        "pool_scale": 1.0 + 0.05 * jax.random.normal(ks[10], (N_POOL_LAYERS, D_MODEL), f32),
        "ffn_w_up": jax.random.normal(ks[11], (DEPTH, D_MODEL, 2 * D_FF), f32) * D_MODEL ** -0.5,
        "ffn_conv_w": jax.random.normal(ks[12], (DEPTH, CONV_WIDTH, 2 * D_FF), f32) * CONV_WIDTH ** -0.5,
        "ffn_conv_b": 0.02 * jax.random.normal(ks[13], (DEPTH, 2 * D_FF), f32),
        "ffn_w_down": jax.random.normal(ks[14], (DEPTH, D_FF, D_MODEL), f32) * D_FF ** -0.5,
    }


def _fwd_reference(x, norm_mix, norm_ffn, rel_bias, attn_w_qkv, attn_q_gain, attn_k_gain,
              attn_sinks, attn_w_o, pool_w, pool_scale, ffn_w_up, ffn_conv_w, ffn_conv_b,
              ffn_w_down):
    for i in range(DEPTH):
        h = rmsnorm(x, norm_mix[i])
        j = i // N_MIXERS
        if i % N_MIXERS == 0:
            mix = sliding_window_attention(h, attn_w_qkv[j], attn_q_gain[j], attn_k_gain[j],
                                           attn_sinks[j], rel_bias, attn_w_o[j])
        else:
            mix = multiscale_pool_mixer(h, pool_w[j], pool_scale[j])
        x = x + mix
        h = rmsnorm(x, norm_ffn[i])
        x = x + conv_gated_mlp(h, ffn_w_up[i], ffn_conv_w[i], ffn_conv_b[i], ffn_w_down[i])
    return x


import jax as _jax
import jax.numpy as _jnp

TWIN_FORMAT = 'train_step'
FWD_PARAMS = ['x', 'norm_mix', 'norm_ffn', 'rel_bias', 'attn_w_qkv', 'attn_q_gain', 'attn_k_gain', 'attn_sinks', 'attn_w_o', 'pool_w', 'pool_scale', 'ffn_w_up', 'ffn_conv_w', 'ffn_conv_b', 'ffn_w_down']
TWIN_WEIGHTS = ['norm_mix', 'norm_ffn', 'rel_bias', 'attn_w_qkv', 'attn_q_gain', 'attn_k_gain', 'attn_sinks', 'attn_w_o', 'pool_w', 'pool_scale', 'ffn_w_up', 'ffn_conv_w', 'ffn_conv_b', 'ffn_w_down']
TWIN_DIFF_INPUT = 'x'
TWIN_INPUTS = ['x', 'norm_mix', 'norm_ffn', 'rel_bias', 'attn_w_qkv', 'attn_q_gain', 'attn_k_gain', 'attn_sinks', 'attn_w_o', 'pool_w', 'pool_scale', 'ffn_w_up', 'ffn_conv_w', 'ffn_conv_b', 'ffn_w_down', 'loss_target', 'm_norm_mix', 'm_norm_ffn', 'm_rel_bias', 'm_attn_w_qkv', 'm_attn_q_gain', 'm_attn_k_gain', 'm_attn_sinks', 'm_attn_w_o', 'm_pool_w', 'm_pool_scale', 'm_ffn_w_up', 'm_ffn_conv_w', 'm_ffn_conv_b', 'm_ffn_w_down', 'v_norm_mix', 'v_norm_ffn', 'v_rel_bias', 'v_attn_w_qkv', 'v_attn_q_gain', 'v_attn_k_gain', 'v_attn_sinks', 'v_attn_w_o', 'v_pool_w', 'v_pool_scale', 'v_ffn_w_up', 'v_ffn_conv_w', 'v_ffn_conv_b', 'v_ffn_w_down']
TWIN_OUTPUTS = ['loss', 'grad_x', 'grad_norm_mix', 'grad_norm_ffn', 'grad_rel_bias', 'grad_attn_w_qkv', 'grad_attn_q_gain', 'grad_attn_k_gain', 'grad_attn_sinks', 'grad_attn_w_o', 'grad_pool_w', 'grad_pool_scale', 'grad_ffn_w_up', 'grad_ffn_conv_w', 'grad_ffn_conv_b', 'grad_ffn_w_down', 'delta_norm_mix', 'delta_norm_ffn', 'delta_rel_bias', 'delta_attn_w_qkv', 'delta_attn_q_gain', 'delta_attn_k_gain', 'delta_attn_sinks', 'delta_attn_w_o', 'delta_pool_w', 'delta_pool_scale', 'delta_ffn_w_up', 'delta_ffn_conv_w', 'delta_ffn_conv_b', 'delta_ffn_w_down', 'new_m_norm_mix', 'new_m_norm_ffn', 'new_m_rel_bias', 'new_m_attn_w_qkv', 'new_m_attn_q_gain', 'new_m_attn_k_gain', 'new_m_attn_sinks', 'new_m_attn_w_o', 'new_m_pool_w', 'new_m_pool_scale', 'new_m_ffn_w_up', 'new_m_ffn_conv_w', 'new_m_ffn_conv_b', 'new_m_ffn_w_down', 'new_v_norm_mix', 'new_v_norm_ffn', 'new_v_rel_bias', 'new_v_attn_w_qkv', 'new_v_attn_q_gain', 'new_v_attn_k_gain', 'new_v_attn_sinks', 'new_v_attn_w_o', 'new_v_pool_w', 'new_v_pool_scale', 'new_v_ffn_w_up', 'new_v_ffn_conv_w', 'new_v_ffn_conv_b', 'new_v_ffn_w_down']
TWIN_LEAF_KINDS = {'loss': 'loss', 'grad_x': 'grad_x', 'grad_norm_mix': 'grad_w', 'grad_norm_ffn': 'grad_w', 'grad_rel_bias': 'grad_w', 'grad_attn_w_qkv': 'grad_w', 'grad_attn_q_gain': 'grad_w', 'grad_attn_k_gain': 'grad_w', 'grad_attn_sinks': 'grad_w', 'grad_attn_w_o': 'grad_w', 'grad_pool_w': 'grad_w', 'grad_pool_scale': 'grad_w', 'grad_ffn_w_up': 'grad_w', 'grad_ffn_conv_w': 'grad_w', 'grad_ffn_conv_b': 'grad_w', 'grad_ffn_w_down': 'grad_w', 'delta_norm_mix': 'delta_w', 'delta_norm_ffn': 'delta_w', 'delta_rel_bias': 'delta_w', 'delta_attn_w_qkv': 'delta_w', 'delta_attn_q_gain': 'delta_w', 'delta_attn_k_gain': 'delta_w', 'delta_attn_sinks': 'delta_w', 'delta_attn_w_o': 'delta_w', 'delta_pool_w': 'delta_w', 'delta_pool_scale': 'delta_w', 'delta_ffn_w_up': 'delta_w', 'delta_ffn_conv_w': 'delta_w', 'delta_ffn_conv_b': 'delta_w', 'delta_ffn_w_down': 'delta_w', 'new_m_norm_mix': 'new_m', 'new_m_norm_ffn': 'new_m', 'new_m_rel_bias': 'new_m', 'new_m_attn_w_qkv': 'new_m', 'new_m_attn_q_gain': 'new_m', 'new_m_attn_k_gain': 'new_m', 'new_m_attn_sinks': 'new_m', 'new_m_attn_w_o': 'new_m', 'new_m_pool_w': 'new_m', 'new_m_pool_scale': 'new_m', 'new_m_ffn_w_up': 'new_m', 'new_m_ffn_conv_w': 'new_m', 'new_m_ffn_conv_b': 'new_m', 'new_m_ffn_w_down': 'new_m', 'new_v_norm_mix': 'new_v', 'new_v_norm_ffn': 'new_v', 'new_v_rel_bias': 'new_v', 'new_v_attn_w_qkv': 'new_v', 'new_v_attn_q_gain': 'new_v', 'new_v_attn_k_gain': 'new_v', 'new_v_attn_sinks': 'new_v', 'new_v_attn_w_o': 'new_v', 'new_v_pool_w': 'new_v', 'new_v_pool_scale': 'new_v', 'new_v_ffn_w_up': 'new_v', 'new_v_ffn_conv_w': 'new_v', 'new_v_ffn_conv_b': 'new_v', 'new_v_ffn_w_down': 'new_v'}


def _forward(args):
    return _fwd_reference(*[args[k] for k in FWD_PARAMS])


def _output_shape():
    def fwd():
        inp = _fwd_setup_inputs(0)
        return _fwd_reference(*[inp[k] for k in FWD_PARAMS])
    out = _jax.eval_shape(fwd)
    return out.shape, out.dtype

N_MICROBATCH = 1
ADAM_LR = 0.001
ADAM_B1 = 0.9
ADAM_B2 = 0.999
ADAM_EPS = 1e-08
ADAM_WD = 0.01
ADAM_STEP = 10
PER_EXAMPLE_BATCH_AXIS = {'x': 0, 'loss_target': 0}
SHARED_INPUTS = []
_WEIGHT_DTYPES = {'norm_mix': _jnp.float32, 'norm_ffn': _jnp.float32, 'rel_bias': _jnp.float32, 'attn_w_qkv': _jnp.float32, 'attn_q_gain': _jnp.float32, 'attn_k_gain': _jnp.float32, 'attn_sinks': _jnp.float32, 'attn_w_o': _jnp.float32, 'pool_w': _jnp.float32, 'pool_scale': _jnp.float32, 'ffn_w_up': _jnp.float32, 'ffn_conv_w': _jnp.float32, 'ffn_conv_b': _jnp.float32, 'ffn_w_down': _jnp.float32}
MOMENT_SCALE = {'norm_mix': 1.856668e+01, 'norm_ffn': 2.594839e+01, 'rel_bias': 1.117679e+00, 'attn_w_qkv': 2.412801e-01, 'attn_q_gain': 1.341134e+01, 'attn_k_gain': 1.348153e+01, 'attn_sinks': 7.199837e-01, 'attn_w_o': 1.777965e-01, 'pool_w': 2.075722e+00, 'pool_scale': 2.621574e+01, 'ffn_w_up': 2.393513e-01, 'ffn_conv_w': 3.510046e+00, 'ffn_conv_b': 3.203062e+00, 'ffn_w_down': 3.973817e-01}


def _to_microbatches(a, axis):
    t = _jnp.moveaxis(a, axis, 0)
    t = t.reshape((N_MICROBATCH, t.shape[0] // N_MICROBATCH) + t.shape[1:])
    return _jnp.moveaxis(t, 1, axis + 1)


def setup_inputs(seed: int = 0) -> dict:
    inp = _fwd_setup_inputs(seed)
    key = _jax.random.fold_in(_jax.random.key(seed), 7919)
    shape, _ = _output_shape()
    out = dict(inp)
    out["loss_target"] = _jax.random.normal(_jax.random.fold_in(key, 0), shape, _jnp.float32)
    for i, name in enumerate(TWIN_WEIGHTS):
        w = inp[name].astype(_jnp.float32)
        if MOMENT_SCALE is None:
            s = _jnp.sqrt(_jnp.mean(_jnp.square(w)) + 1e-30)
        else:
            s = MOMENT_SCALE[name]
        km, kv = _jax.random.split(_jax.random.fold_in(key, i + 1))
        out[name] = w
        out["m_" + name] = s * _jax.random.normal(km, w.shape, _jnp.float32)
        out["v_" + name] = (s * s) * _jax.random.uniform(kv, w.shape, _jnp.float32, 0.5, 1.5)
    if N_MICROBATCH > 1:
        for name, axis in PER_EXAMPLE_BATCH_AXIS.items():
            out[name] = _to_microbatches(out[name], axis)
    return {'x': out['x'], 'norm_mix': out['norm_mix'], 'norm_ffn': out['norm_ffn'], 'rel_bias': out['rel_bias'], 'attn_w_qkv': out['attn_w_qkv'], 'attn_q_gain': out['attn_q_gain'], 'attn_k_gain': out['attn_k_gain'], 'attn_sinks': out['attn_sinks'], 'attn_w_o': out['attn_w_o'], 'pool_w': out['pool_w'], 'pool_scale': out['pool_scale'], 'ffn_w_up': out['ffn_w_up'], 'ffn_conv_w': out['ffn_conv_w'], 'ffn_conv_b': out['ffn_conv_b'], 'ffn_w_down': out['ffn_w_down'], 'loss_target': out['loss_target'], 'm_norm_mix': out['m_norm_mix'], 'm_norm_ffn': out['m_norm_ffn'], 'm_rel_bias': out['m_rel_bias'], 'm_attn_w_qkv': out['m_attn_w_qkv'], 'm_attn_q_gain': out['m_attn_q_gain'], 'm_attn_k_gain': out['m_attn_k_gain'], 'm_attn_sinks': out['m_attn_sinks'], 'm_attn_w_o': out['m_attn_w_o'], 'm_pool_w': out['m_pool_w'], 'm_pool_scale': out['m_pool_scale'], 'm_ffn_w_up': out['m_ffn_w_up'], 'm_ffn_conv_w': out['m_ffn_conv_w'], 'm_ffn_conv_b': out['m_ffn_conv_b'], 'm_ffn_w_down': out['m_ffn_w_down'], 'v_norm_mix': out['v_norm_mix'], 'v_norm_ffn': out['v_norm_ffn'], 'v_rel_bias': out['v_rel_bias'], 'v_attn_w_qkv': out['v_attn_w_qkv'], 'v_attn_q_gain': out['v_attn_q_gain'], 'v_attn_k_gain': out['v_attn_k_gain'], 'v_attn_sinks': out['v_attn_sinks'], 'v_attn_w_o': out['v_attn_w_o'], 'v_pool_w': out['v_pool_w'], 'v_pool_scale': out['v_pool_scale'], 'v_ffn_w_up': out['v_ffn_w_up'], 'v_ffn_conv_w': out['v_ffn_conv_w'], 'v_ffn_conv_b': out['v_ffn_conv_b'], 'v_ffn_w_down': out['v_ffn_w_down']}


def _loss(weights, diff, rest, loss_target):
    with _jax.named_scope("forward"):
        args = {**rest, TWIN_DIFF_INPUT: diff, **{k: w.astype(_WEIGHT_DTYPES[k]) for k, w in weights.items()}}
        y = _forward(args)
    with _jax.named_scope("loss_head"):
        err = _jnp.square(y.astype(_jnp.float32) - loss_target)
        return 0.5 * _jnp.sum(_jnp.mean(err, axis=-1)) if err.ndim else 0.5 * err


def _adamw(w, g, m, v):
    m = ADAM_B1 * m + (1.0 - ADAM_B1) * g
    v = ADAM_B2 * v + (1.0 - ADAM_B2) * _jnp.square(g)
    m_hat = m / (1.0 - ADAM_B1 ** ADAM_STEP)
    v_hat = v / (1.0 - ADAM_B2 ** ADAM_STEP)
    delta = -ADAM_LR * (m_hat / (_jnp.sqrt(v_hat) + ADAM_EPS) + ADAM_WD * w)
    return delta, m, v


def reference(x, norm_mix, norm_ffn, rel_bias, attn_w_qkv, attn_q_gain, attn_k_gain, attn_sinks, attn_w_o, pool_w, pool_scale, ffn_w_up, ffn_conv_w, ffn_conv_b, ffn_w_down, loss_target, m_norm_mix, m_norm_ffn, m_rel_bias, m_attn_w_qkv, m_attn_q_gain, m_attn_k_gain, m_attn_sinks, m_attn_w_o, m_pool_w, m_pool_scale, m_ffn_w_up, m_ffn_conv_w, m_ffn_conv_b, m_ffn_w_down, v_norm_mix, v_norm_ffn, v_rel_bias, v_attn_w_qkv, v_attn_q_gain, v_attn_k_gain, v_attn_sinks, v_attn_w_o, v_pool_w, v_pool_scale, v_ffn_w_up, v_ffn_conv_w, v_ffn_conv_b, v_ffn_w_down):
    given = dict(x=x, norm_mix=norm_mix, norm_ffn=norm_ffn, rel_bias=rel_bias, attn_w_qkv=attn_w_qkv, attn_q_gain=attn_q_gain, attn_k_gain=attn_k_gain, attn_sinks=attn_sinks, attn_w_o=attn_w_o, pool_w=pool_w, pool_scale=pool_scale, ffn_w_up=ffn_w_up, ffn_conv_w=ffn_conv_w, ffn_conv_b=ffn_conv_b, ffn_w_down=ffn_w_down, loss_target=loss_target, m_norm_mix=m_norm_mix, m_norm_ffn=m_norm_ffn, m_rel_bias=m_rel_bias, m_attn_w_qkv=m_attn_w_qkv, m_attn_q_gain=m_attn_q_gain, m_attn_k_gain=m_attn_k_gain, m_attn_sinks=m_attn_sinks, m_attn_w_o=m_attn_w_o, m_pool_w=m_pool_w, m_pool_scale=m_pool_scale, m_ffn_w_up=m_ffn_w_up, m_ffn_conv_w=m_ffn_conv_w, m_ffn_conv_b=m_ffn_conv_b, m_ffn_w_down=m_ffn_w_down, v_norm_mix=v_norm_mix, v_norm_ffn=v_norm_ffn, v_rel_bias=v_rel_bias, v_attn_w_qkv=v_attn_w_qkv, v_attn_q_gain=v_attn_q_gain, v_attn_k_gain=v_attn_k_gain, v_attn_sinks=v_attn_sinks, v_attn_w_o=v_attn_w_o, v_pool_w=v_pool_w, v_pool_scale=v_pool_scale, v_ffn_w_up=v_ffn_w_up, v_ffn_conv_w=v_ffn_conv_w, v_ffn_conv_b=v_ffn_conv_b, v_ffn_w_down=v_ffn_w_down)
    weights = {n: given[n] for n in TWIN_WEIGHTS}
    shared = {n: given[n] for n in SHARED_INPUTS}
    per_example = {n: given[n] for n in ['x']}
    grad_fn = _jax.value_and_grad(_loss, argnums=(0, 1))

    def one_microbatch(ex, loss_target):
        ex = dict(ex)
        diff = ex.pop(TWIN_DIFF_INPUT)
        return grad_fn(weights, diff, {**shared, **ex}, loss_target)

    if N_MICROBATCH == 1:
        loss, (grad_w, grad_x) = one_microbatch(per_example, given["loss_target"])
    else:
        def body(carry, xs):
            loss_sum, grad_sum = carry
            l_k, (gw_k, gx_k) = one_microbatch(xs[0], xs[1])
            with _jax.named_scope("update"):
                return (loss_sum + l_k, _jax.tree.map(_jnp.add, grad_sum, gw_k)), gx_k

        init = (_jnp.zeros((), _jnp.float32), _jax.tree.map(_jnp.zeros_like, weights))
        (loss, grad_w), grad_x = _jax.lax.scan(body, init, (per_example, given["loss_target"]))
    with _jax.named_scope("update"):
        delta_w, new_m, new_v = {}, {}, {}
        for n in TWIN_WEIGHTS:
            delta_w[n], new_m[n], new_v[n] = _adamw(weights[n], grad_w[n], given["m_" + n], given["v_" + n])
    return (loss, grad_x, *[grad_w[n] for n in TWIN_WEIGHTS], *[delta_w[n] for n in TWIN_WEIGHTS],
            *[new_m[n] for n in TWIN_WEIGHTS], *[new_v[n] for n in TWIN_WEIGHTS])
```

```python
import functools

import numpy as np
import jax
import jax.numpy as jnp
from jax import lax
from jax.experimental import pallas as pl
from jax.experimental.pallas import tpu as pltpu

F32 = jnp.float32
BF16 = jnp.bfloat16
MESH = pl.DeviceIdType.MESH

N_CHIPS = 4
HEAD_DIM = 64
GQA_GROUP = 8
WINDOW = 128
N_BUCKETS = 32
MAX_DISTANCE = 128
POOL_WINDOWS = (2, 4, 8, 16)
POOL_HALO = 16
EPS = 1e-6
NEG_INF = -1e30
LANES = 128
VMEM_LIMIT = 56 * 1024 * 1024

ADAM_LR = 0.001
ADAM_B1 = 0.9
ADAM_B2 = 0.999
ADAM_EPS = 1e-08
ADAM_WD = 0.01
ADAM_STEP = 10


def _tile(n, prefs):
    for p in prefs:
        if p <= n and n % p == 0:
            return p
    return n


def _cparams(sem):
    return pltpu.CompilerParams(dimension_semantics=sem, vmem_limit_bytes=VMEM_LIMIT)


_DN = {
    "nn": (((1,), (0,)), ((), ())),
    "nt": (((1,), (1,)), ((), ())),
    "tn": (((0,), (0,)), ((), ())),
}


def _mm(name, kind, a, b, grid, a_spec, b_spec, outs, acc_shape, extras=(), epilogue=None):
    nk = grid[2]
    n_ex, n_out = len(extras), len(outs)

    def body(a_ref, b_ref, *rest):
        ex_refs = rest[:n_ex]
        out_refs = rest[n_ex:n_ex + n_out]
        acc_ref = rest[n_ex + n_out] if nk > 1 else None
        part = lax.dot_general(a_ref[...], b_ref[...], _DN[kind], preferred_element_type=F32)

        def finish(val):
            vals = epilogue(val, *[r[...] for r in ex_refs]) if epilogue else (val,)
            for r, v in zip(out_refs, vals):
                r[...] = v.astype(r.dtype)

        if nk == 1:
            finish(part)
        else:
            k = pl.program_id(2)

            @pl.when(k == 0)
            def _():
                acc_ref[...] = part

            @pl.when(k > 0)
            def _():
                acc_ref[...] += part

            @pl.when(k == nk - 1)
            def _():
                finish(acc_ref[...])

    res = pl.pallas_call(
        body,
        name=name,
        grid=grid,
        in_specs=[a_spec, b_spec] + [s for _, s in extras],
        out_specs=[s for _, _, s in outs],
        out_shape=[jax.ShapeDtypeStruct(sh, dt) for sh, dt, _ in outs],
        scratch_shapes=[pltpu.VMEM(acc_shape, F32)] if nk > 1 else [],
        compiler_params=_cparams(("parallel", "parallel", "arbitrary")),
    )(a, b, *[e for e, _ in extras])
    return res if n_out > 1 else res[0]


def _mm_colblk(name, a, wg, l, out_dtype, split2=False):
    S, K = a.shape
    nb, _, _, nc = wg.shape
    tm = _tile(S, (512,))
    tn = _tile(nc, (1408, 1024, 640, 512, 256, 128))
    npb = nc // tn
    ncol = nb * npb
    half = ncol // 2
    if split2:
        o_shape, o_spec = (2, S, nb * nc // 2), pl.BlockSpec((None, tm, tn), lambda p, q, k: (p // half, q, p % half))
    else:
        o_shape, o_spec = (S, nb * nc), pl.BlockSpec((tm, tn), lambda p, q, k: (q, p))
    return _mm(
        name, "nn", a, wg, (ncol, S // tm, 1),
        pl.BlockSpec((tm, K), lambda p, q, k: (q, 0)),
        pl.BlockSpec((None, None, K, tn), lambda p, q, k: (p // npb, l, 0, p % npb)),
        [(o_shape, out_dtype, o_spec)], (tm, tn))


def _mm_rowblk_res(name, a, wg, l, res):
    S = a.shape[0]
    nb, _, kc, N = wg.shape
    tm = _tile(S, (512,))
    tn = _tile(N, (2048, 1024, 512))
    tk = _tile(kc, (1408, 512, 256, 128))
    kpb = kc // tk
    o_spec = pl.BlockSpec((tm, tn), lambda p, q, k: (p, q))
    return _mm(
        name, "nn", a, wg, (S // tm, N // tn, nb * kpb),
        pl.BlockSpec((tm, tk), lambda p, q, k: (p, k)),
        pl.BlockSpec((None, None, tk, tn), lambda p, q, k: (k // kpb, l, k % kpb, q)),
        [((S, N), F32, o_spec)], (tm, tn),
        extras=[(res, o_spec)], epilogue=lambda acc, r: (r + acc,))


def _mm_nt_rowblk(name, g, wg, l, out_dtype):
    S, N = g.shape
    nb, _, kc, _ = wg.shape
    tm = _tile(S, (512,))
    tn = _tile(kc, (1408, 512, 256, 128))
    kpb = kc // tn
    return _mm(
        name, "nt", g, wg, (nb * kpb, S // tm, 1),
        pl.BlockSpec((tm, N), lambda p, q, k: (q, 0)),
        pl.BlockSpec((None, None, tn, N), lambda p, q, k: (p // kpb, l, p % kpb, 0)),
        [((S, nb * kc), out_dtype, pl.BlockSpec((tm, tn), lambda p, q, k: (q, p)))], (tm, tn))


def _mm_nt_colblk(name, g, wg, l, split2=False):
    S = g.shape[-2]
    nb, _, K, nc = wg.shape
    tm = _tile(S, (512,))
    tn = _tile(K, (2048, 1024))
    tk = _tile(nc, (1408, 640, 512, 256, 128))
    npb = nc // tk
    half = nb * npb // 2
    if split2:
        a_spec = pl.BlockSpec((None, tm, tk), lambda p, q, k: (k // half, p, k % half))
    else:
        a_spec = pl.BlockSpec((tm, tk), lambda p, q, k: (p, k))
    return _mm(
        name, "nt", g, wg, (S // tm, K // tn, nb * npb),
        a_spec,
        pl.BlockSpec((None, None, tn, tk), lambda p, q, k: (k // npb, l, q, k % npb)),
        [((S, K), F32, pl.BlockSpec((tm, tn), lambda p, q, k: (p, q)))], (tm, tn))


def _mm_tn_colblk(name, a, g, nb, split2=False):
    S, K = a.shape
    ntot = g.shape[-1] * (2 if split2 else 1)
    nc = ntot // nb
    ti = _tile(K, (1024,))
    tn = _tile(nc, (1408, 640, 512, 256, 128))
    ts = _tile(S, (1024, 512, 256))
    npb = nc // tn
    half = nb * npb // 2
    if split2:
        b_spec = pl.BlockSpec((None, ts, tn), lambda p, q, k: (q // half, k, q % half))
    else:
        b_spec = pl.BlockSpec((ts, tn), lambda p, q, k: (k, q))
    return _mm(
        name, "tn", a, g, (K // ti, nb * npb, S // ts),
        pl.BlockSpec((ts, ti), lambda p, q, k: (k, p)),
        b_spec,
        [((nb, K, nc), BF16, pl.BlockSpec((None, ti, tn), lambda p, q, k: (q // npb, p, q % npb)))], (ti, tn))


def _mm_tn_rowblk(name, a, g, nb):
    S, ktot = a.shape
    N = g.shape[1]
    kc = ktot // nb
    ti = _tile(kc, (1408, 512, 256, 128))
    tj = _tile(N, (1024,))
    ts = _tile(S, (1024, 512, 256))
    ipb = kc // ti
    return _mm(
        name, "tn", a, g, (nb * ipb, N // tj, S // ts),
        pl.BlockSpec((ts, ti), lambda p, q, k: (k, p)),
        pl.BlockSpec((ts, tj), lambda p, q, k: (k, q)),
        [((nb, kc, N), BF16, pl.BlockSpec((None, ti, tj), lambda p, q, k: (p // ipb, p % ipb, q)))], (ti, tj))


def _rms_fwd(name, x, gain):
    S, D = x.shape
    tm = _tile(S, (512,))

    def body(x_ref, g_ref, o_ref):
        xv = x_ref[...]
        r = lax.rsqrt(jnp.mean(xv * xv, axis=-1, keepdims=True) + EPS)
        o_ref[...] = (xv * r * g_ref[...]).astype(o_ref.dtype)

    return pl.pallas_call(
        body, name=name, grid=(S // tm,),
        in_specs=[pl.BlockSpec((tm, D), lambda i: (i, 0)), pl.BlockSpec((1, D), lambda i: (0, 0))],
        out_specs=pl.BlockSpec((tm, D), lambda i: (i, 0)),
        out_shape=jax.ShapeDtypeStruct((S, D), BF16),
        compiler_params=_cparams(("parallel",)),
    )(x, gain)


def _rms_bwd_math(xv, gain, dh):
    r = lax.rsqrt(jnp.mean(xv * xv, axis=-1, keepdims=True) + EPS)
    xhat = xv * r
    dxhat = dh * gain
    c = jnp.mean(dxhat * xhat, axis=-1, keepdims=True)
    return r * (dxhat - xhat * c), dh * xhat


def _rows_to_8(v):
    tm, C = v.shape
    return jnp.sum(v.reshape(tm // 8, 8, C), axis=0)


def _rms_bwd(name, x, gain, dh, dres):
    S, D = x.shape
    tm = _tile(S, (256,))
    n = S // tm

    def body(x_ref, g_ref, dh_ref, dr_ref, dx_ref, dxb_ref, dg_ref, acc_ref):
        i = pl.program_id(0)
        dxn, dgr = _rms_bwd_math(x_ref[...], g_ref[...], dh_ref[...])
        dx = dr_ref[...] + dxn
        dx_ref[...] = dx
        dxb_ref[...] = dx.astype(BF16)

        @pl.when(i == 0)
        def _():
            acc_ref[...] = jnp.zeros_like(acc_ref)

        acc_ref[...] += _rows_to_8(dgr)

        @pl.when(i == n - 1)
        def _():
            dg_ref[...] = jnp.sum(acc_ref[...], axis=0, keepdims=True)

    row = pl.BlockSpec((tm, D), lambda i: (i, 0))
    vec = pl.BlockSpec((1, D), lambda i: (0, 0))
    return pl.pallas_call(
        body, name=name, grid=(n,),
        in_specs=[row, vec, row, row],
        out_specs=[row, row, vec],
        out_shape=[jax.ShapeDtypeStruct((S, D), F32), jax.ShapeDtypeStruct((S, D), BF16),
                   jax.ShapeDtypeStruct((1, D), F32)],
        scratch_shapes=[pltpu.VMEM((8, D), F32)],
        compiler_params=_cparams(("arbitrary",)),
    )(x, gain, dh, dres)


def _loss_head(y, tgt):
    S, D = y.shape
    tm = _tile(S, (256,))
    n = S // tm

    def body(y_ref, t_ref, dy_ref, dyb_ref, l_ref):
        i = pl.program_id(0)
        e = y_ref[...] - t_ref[...]
        dy = e * (1.0 / D)
        dy_ref[...] = dy
        dyb_ref[...] = dy.astype(BF16)

        @pl.when(i == 0)
        def _():
            l_ref[...] = jnp.zeros_like(l_ref)

        sq = _rows_to_8(e * e)
        part = sq[:, 0:LANES]
        for t in range(1, D // LANES):
            part = part + sq[:, t * LANES:(t + 1) * LANES]
        l_ref[...] += part * (0.5 / D)

    row = pl.BlockSpec((tm, D), lambda i: (i, 0))
    return pl.pallas_call(
        body, name="loss_head", grid=(n,),
        in_specs=[row, row],
        out_specs=[row, row, pl.BlockSpec((8, LANES), lambda i: (0, 0))],
        out_shape=[jax.ShapeDtypeStruct((S, D), F32), jax.ShapeDtypeStruct((S, D), BF16),
                   jax.ShapeDtypeStruct((8, LANES), F32)],
        compiler_params=_cparams(("arbitrary",)),
    )(y, tgt)


def _shift_down(u, halo, k):
    row = lax.broadcasted_iota(jnp.int32, u.shape, 0)
    out = pltpu.roll(u, k, 0)
    for j in range(k):
        out = jnp.where(row == j, halo[8 - k + j:8 - k + j + 1, :], out)
    return out


def _shift_up(u, halo, k):
    tm = u.shape[0]
    row = lax.broadcasted_iota(jnp.int32, u.shape, 0)
    out = pltpu.roll(u, tm - k, 0)
    for j in range(k):
        out = jnp.where(row == tm - k + j, halo[j:j + 1, :], out)
    return out


def _conv_fwd(u, halo, cw, cb):
    um1 = _shift_down(u, halo, 1)
    um2 = _shift_down(u, halo, 2)
    return cw[0:1] * um2 + cw[1:2] * um1 + cw[2:3] * u + cb, um1, um2


def _ffn_specs(S, Fh, tm, tc):
    n8 = tm // 8
    blk = pl.BlockSpec((2, tm, tc), lambda j, i: (0, i, j))
    prev = pl.BlockSpec((2, 8, tc), lambda j, i: (0, jnp.maximum(i * n8 - 1, 0), j))
    nxt = pl.BlockSpec((2, 8, tc), lambda j, i: (0, jnp.minimum((i + 1) * n8, S // 8 - 1), j))
    cw = pl.BlockSpec((2, 3, tc), lambda j, i: (0, 0, j))
    cb = pl.BlockSpec((2, 1, tc), lambda j, i: (0, 0, j))
    one = pl.BlockSpec((tm, tc), lambda j, i: (i, j))
    return blk, prev, nxt, cw, cb, one


def _ffn_act_fwd(u, cw, cb):
    _, S, Fh = u.shape
    tm, tc = _tile(S, (512,)), _tile(Fh, (512, 256, 128))
    blk, prev, _, cws, cbs, one = _ffn_specs(S, Fh, tm, tc)

    def body(u_ref, h_ref, cw_ref, cb_ref, a_ref):
        i = pl.program_id(1)
        keep = (i > 0).astype(F32)
        ucs = []
        for s in range(2):
            uc, _, _ = _conv_fwd(u_ref[s], h_ref[s] * keep, cw_ref[s], cb_ref[s])
            ucs.append(uc)
        gate, val = ucs
        sig = 1.0 / (1.0 + jnp.exp(-gate))
        a_ref[...] = (gate * sig * val).astype(BF16)

    return pl.pallas_call(
        body, name="ffn_act_fwd", grid=(Fh // tc, S // tm),
        in_specs=[blk, prev, cws, cbs], out_specs=one,
        out_shape=jax.ShapeDtypeStruct((S, Fh), BF16),
        compiler_params=_cparams(("parallel", "parallel")),
    )(u, u, cw, cb)


def _ffn_act_bwd(u, cw, cb, da):
    _, S, Fh = u.shape
    tm, tc = _tile(S, (512,)), _tile(Fh, (512, 256, 128))
    blk, prev, _, cws, cbs, one = _ffn_specs(S, Fh, tm, tc)

    def body(u_ref, h_ref, cw_ref, cb_ref, da_ref, duc_ref, acc_ref):
        i = pl.program_id(1)
        keep = (i > 0).astype(F32)
        uc, um1, um2 = [], [], []
        for s in range(2):
            c, m1, m2 = _conv_fwd(u_ref[s], h_ref[s] * keep, cw_ref[s], cb_ref[s])
            uc.append(c), um1.append(m1), um2.append(m2)
        gate, val = uc
        dav = da_ref[...]
        sig = 1.0 / (1.0 + jnp.exp(-gate))
        dgate = dav * val * (sig * (1.0 + gate * (1.0 - sig)))
        dval = dav * (gate * sig)
        duc_ref[0] = dgate
        duc_ref[1] = dval

        @pl.when(i == 0)
        def _():
            acc_ref[...] = jnp.zeros_like(acc_ref)

        for s, d in enumerate((dgate, dval)):
            acc_ref[s, 0:1, :] += jnp.sum(d * um2[s], axis=0, keepdims=True)
            acc_ref[s, 1:2, :] += jnp.sum(d * um1[s], axis=0, keepdims=True)
            acc_ref[s, 2:3, :] += jnp.sum(d * u_ref[s], axis=0, keepdims=True)
            acc_ref[s, 3:4, :] += jnp.sum(d, axis=0, keepdims=True)

    return pl.pallas_call(
        body, name="ffn_act_bwd", grid=(Fh // tc, S // tm),
        in_specs=[blk, prev, cws, cbs, one],
        out_specs=[blk, pl.BlockSpec((2, 8, tc), lambda j, i: (0, 0, j))],
        out_shape=[jax.ShapeDtypeStruct((2, S, Fh), F32), jax.ShapeDtypeStruct((2, 8, Fh), F32)],
        compiler_params=_cparams(("parallel", "arbitrary")),
    )(u, u, cw, cb, da)


def _ffn_conv_bwd(duc, cw):
    _, S, Fh = duc.shape
    tm, tc = _tile(S, (512,)), _tile(Fh, (512, 256, 128))
    blk, _, nxt, cws, _, _ = _ffn_specs(S, Fh, tm, tc)
    n = S // tm

    def body(d_ref, h_ref, cw_ref, du_ref):
        i = pl.program_id(1)
        keep = (i < n - 1).astype(F32)
        for s in range(2):
            d, halo, w = d_ref[s], h_ref[s] * keep, cw_ref[s]
            du = w[2:3] * d + w[1:2] * _shift_up(d, halo, 1) + w[0:1] * _shift_up(d, halo, 2)
            du_ref[s] = du.astype(BF16)

    return pl.pallas_call(
        body, name="ffn_conv_bwd", grid=(Fh // tc, S // tm),
        in_specs=[blk, nxt, cws], out_specs=blk,
        out_shape=jax.ShapeDtypeStruct((2, S, Fh), BF16),
        compiler_params=_cparams(("parallel", "parallel")),
    )(duc, duc, cw)


def _pool_counts(i, tm, w):
    t = i * tm + lax.broadcasted_iota(jnp.int32, (tm, 1), 0)
    return jnp.minimum(t + 1, w).astype(F32)


def _pool_fwd(x, gain):
    S, D = x.shape
    tm = _tile(S, (256,))
    gd = D // len(POOL_WINDOWS)
    nh = tm // POOL_HALO

    def body(x_ref, xh_ref, g_ref, d_ref):
        i = pl.program_id(0)

        def norm(v):
            return v * lax.rsqrt(jnp.mean(v * v, axis=-1, keepdims=True) + EPS) * g_ref[...]

        h = norm(x_ref[...])
        hh = norm(xh_ref[...]) * (i > 0).astype(F32)
        ext = jnp.concatenate([hh, h], axis=0)
        for gi, w in enumerate(POOL_WINDOWS):
            sl = slice(gi * gd, (gi + 1) * gd)
            win = ext[:, sl]
            k = 1
            while k < w:
                win = win + pltpu.roll(win, k, 0)
                k *= 2
            mean = win[POOL_HALO:] / _pool_counts(i, tm, w)
            d_ref[:, sl] = (mean - h[:, sl]).astype(BF16)

    return pl.pallas_call(
        body, name="pool_fwd", grid=(S // tm,),
        in_specs=[pl.BlockSpec((tm, D), lambda i: (i, 0)),
                  pl.BlockSpec((POOL_HALO, D), lambda i: (jnp.maximum(i * nh - 1, 0), 0)),
                  pl.BlockSpec((1, D), lambda i: (0, 0))],
        out_specs=pl.BlockSpec((tm, D), lambda i: (i, 0)),
        out_shape=jax.ShapeDtypeStruct((S, D), BF16),
        compiler_params=_cparams(("parallel",)),
    )(x, x, gain)


def _pool_mm_fwd(d, wpg, j, x, scale):
    S, D = d.shape
    nb, _, G, kcb, gd = wpg.shape
    tm = _tile(S, (512,))
    o_spec = pl.BlockSpec((tm, gd), lambda p, q, k: (p, q))
    return _mm(
        "pool_mm_fwd", "nn", d, wpg, (S // tm, G, nb),
        pl.BlockSpec((tm, kcb), lambda p, q, k: (p, q * nb + k)),
        pl.BlockSpec((None, None, None, kcb, gd), lambda p, q, k: (k, j, q, 0, 0)),
        [((S, D), F32, o_spec), ((S, D), F32, o_spec)], (tm, gd),
        extras=[(x, o_spec), (scale, pl.BlockSpec((1, gd), lambda p, q, k: (0, q)))],
        epilogue=lambda acc, xv, sc: (acc, xv + acc * sc))


def _pool_bwd_pre(dx1, ypre, scale):
    S, D = dx1.shape
    tm = _tile(S, (256,))
    n = S // tm

    def body(dx_ref, y_ref, s_ref, dy_ref, ds_ref, acc_ref):
        i = pl.program_id(0)
        dx = dx_ref[...]
        dy_ref[...] = (dx * s_ref[...]).astype(BF16)

        @pl.when(i == 0)
        def _():
            acc_ref[...] = jnp.zeros_like(acc_ref)

        acc_ref[...] += _rows_to_8(dx * y_ref[...])

        @pl.when(i == n - 1)
        def _():
            ds_ref[...] = jnp.sum(acc_ref[...], axis=0, keepdims=True)

    row = pl.BlockSpec((tm, D), lambda i: (i, 0))
    vec = pl.BlockSpec((1, D), lambda i: (0, 0))
    return pl.pallas_call(
        body, name="pool_bwd_pre", grid=(n,),
        in_specs=[row, row, vec], out_specs=[row, vec],
        out_shape=[jax.ShapeDtypeStruct((S, D), BF16), jax.ShapeDtypeStruct((1, D), F32)],
        scratch_shapes=[pltpu.VMEM((8, D), F32)],
        compiler_params=_cparams(("arbitrary",)),
    )(dx1, ypre, scale)


def _pool_mm_bwd_in(dyp, wpg, j):
    S, D = dyp.shape
    nb, _, G, kcb, gd = wpg.shape
    tm = _tile(S, (512,))
    return _mm(
        "pool_mm_bwd_in", "nt", dyp, wpg, (S // tm, G * nb, 1),
        pl.BlockSpec((tm, gd), lambda p, q, k: (p, q // nb)),
        pl.BlockSpec((None, None, None, kcb, gd), lambda p, q, k: (q % nb, j, q // nb, 0, 0)),
        [((S, D), F32, pl.BlockSpec((tm, kcb), lambda p, q, k: (p, q)))], (tm, kcb))


def _pool_mm_bwd_w(d, dyp, nb):
    S, D = d.shape
    G = len(POOL_WINDOWS)
    gd = D // G
    kcb = gd // nb
    ts = _tile(S, (1024, 512, 256))
    return _mm(
        "pool_mm_bwd_w", "tn", d, dyp, (G * nb, 1, S // ts),
        pl.BlockSpec((ts, kcb), lambda p, q, k: (k, p)),
        pl.BlockSpec((ts, gd), lambda p, q, k: (k, p // nb)),
        [((nb, G, kcb, gd), BF16, pl.BlockSpec((None, None, kcb, gd), lambda p, q, k: (p % nb, p // nb, 0, 0)))],
        (kcb, gd))


def _pool_bwd_post(dd, x, gain, dres):
    S, D = x.shape
    tm = _tile(S, (256,))
    n = S // tm
    gd = D // len(POOL_WINDOWS)
    nh = tm // POOL_HALO

    def body(dd_ref, ddh_ref, x_ref, g_ref, dr_ref, dx_ref, dxb_ref, dg_ref, acc_ref):
        i = pl.program_id(0)
        dd = dd_ref[...]
        halo = ddh_ref[...] * (i < n - 1).astype(F32)
        parts = []
        for gi, w in enumerate(POOL_WINDOWS):
            sl = slice(gi * gd, (gi + 1) * gd)
            ext = jnp.concatenate([dd[:, sl] / _pool_counts(i, tm, w), halo[:, sl] * (1.0 / w)], axis=0)
            k = 1
            while k < w:
                ext = ext + pltpu.roll(ext, tm + POOL_HALO - k, 0)
                k *= 2
            parts.append(ext[:tm] - dd[:, sl])
        dh = jnp.concatenate(parts, axis=1)
        dxn, dgr = _rms_bwd_math(x_ref[...], g_ref[...], dh)
        dx = dr_ref[...] + dxn
        dx_ref[...] = dx
        dxb_ref[...] = dx.astype(BF16)

        @pl.when(i == 0)
        def _():
            acc_ref[...] = jnp.zeros_like(acc_ref)

        acc_ref[...] += _rows_to_8(dgr)

        @pl.when(i == n - 1)
        def _():
            dg_ref[...] = jnp.sum(acc_ref[...], axis=0, keepdims=True)

    row = pl.BlockSpec((tm, D), lambda i: (i, 0))
    vec = pl.BlockSpec((1, D), lambda i: (0, 0))
    nxt = pl.BlockSpec((POOL_HALO, D), lambda i: (jnp.minimum((i + 1) * nh, S // POOL_HALO - 1), 0))
    return pl.pallas_call(
        body, name="pool_bwd_post", grid=(n,),
        in_specs=[row, nxt, row, vec, row],
        out_specs=[row, row, vec],
        out_shape=[jax.ShapeDtypeStruct((S, D), F32), jax.ShapeDtypeStruct((S, D), BF16),
                   jax.ShapeDtypeStruct((1, D), F32)],
        scratch_shapes=[pltpu.VMEM((8, D), F32)],
        compiler_params=_cparams(("arbitrary",)),
    )(dd, dd, x, gain, dres)


def _band_tables():
    i = np.arange(WINDOW)[:, None]
    j = np.arange(2 * WINDOW)[None, :]
    n = np.maximum(WINDOW + i - j, 0)
    max_exact = N_BUCKETS // 2
    nf = np.maximum(n, 1).astype(np.float32)
    large = max_exact + (np.log(nf / max_exact) / np.log(MAX_DISTANCE / max_exact)
                         * (N_BUCKETS - max_exact)).astype(np.int32)
    large = np.minimum(large, N_BUCKETS - 1)
    buckets = np.where(n < max_exact, n, large).astype(np.int32)
    dist = WINDOW + i - j
    in_win = ((dist >= 0) & (dist < WINDOW)).astype(np.int32)
    return buckets, in_win


def _bias_expand(rel_bias):
    H = rel_bias.shape[0]
    buckets, in_win = _band_tables()

    def body(rb_ref, bk_ref, win_ref, o_ref):
        h = pl.program_id(0)
        bk = bk_ref[...]
        acc = jnp.zeros(bk.shape, F32)
        for b in range(N_BUCKETS):
            acc = jnp.where(bk == b, rb_ref[h, b], acc)
        o_ref[...] = jnp.where(win_ref[...] > 0, acc, NEG_INF)

    full = pl.BlockSpec((WINDOW, 2 * WINDOW), lambda h: (0, 0))
    return pl.pallas_call(
        body, name="bias_expand", grid=(H,),
        in_specs=[pl.BlockSpec(memory_space=pltpu.SMEM), full, full],
        out_specs=pl.BlockSpec((None, WINDOW, 2 * WINDOW), lambda h: (h, 0, 0)),
        out_shape=jax.ShapeDtypeStruct((H, WINDOW, 2 * WINDOW), F32),
        compiler_params=_cparams(("parallel",)),
    )(rel_bias, jnp.asarray(buckets), jnp.asarray(in_win))


def _bias_reduce(dbias):
    H = dbias.shape[0]
    buckets, in_win = _band_tables()

    def body(d_ref, bk_ref, win_ref, o_ref):
        bk = jnp.where(win_ref[...] > 0, bk_ref[...], -1)
        d = d_ref[...]
        lane = lax.broadcasted_iota(jnp.int32, (8, LANES), 1)
        out = jnp.zeros((8, LANES), F32)
        for b in range(N_BUCKETS):
            s = jnp.sum(jnp.where(bk == b, d, 0.0))
            out = jnp.where(lane == b, s, out)
        o_ref[...] = out

    full = pl.BlockSpec((WINDOW, 2 * WINDOW), lambda h: (0, 0))
    return pl.pallas_call(
        body, name="bias_reduce", grid=(H,),
        in_specs=[pl.BlockSpec((None, WINDOW, 2 * WINDOW), lambda h: (h, 0, 0)), full, full],
        out_specs=pl.BlockSpec((None, 8, LANES), lambda h: (h, 0, 0)),
        out_shape=jax.ShapeDtypeStruct((H, 8, LANES), F32),
        compiler_params=_cparams(("parallel",)),
    )(dbias, jnp.asarray(buckets), jnp.asarray(in_win))


def _half_rsqrt(t, lo):
    sq = t * t
    s_lo = jnp.sum(jnp.where(lo, sq, 0.0), axis=-1, keepdims=True)
    s_hi = jnp.sum(jnp.where(lo, 0.0, sq), axis=-1, keepdims=True)
    return jnp.where(lo, lax.rsqrt(s_lo * (1.0 / HEAD_DIM) + EPS), lax.rsqrt(s_hi * (1.0 / HEAD_DIM) + EPS))


def _qk_norm_fwd(qkv, qg, kg, n_q, n_kv):
    S, W = qkv.shape
    tm = _tile(S, (256,))
    qw = n_q * HEAD_DIM
    kw = n_kv * HEAD_DIM
    scale = HEAD_DIM ** -0.5

    def body(x_ref, qg_ref, kg_ref, q_ref, k_ref, v_ref):
        lo = lax.broadcasted_iota(jnp.int32, (tm, LANES), 1) < HEAD_DIM
        for t in range(qw // LANES):
            q = x_ref[:, t * LANES:(t + 1) * LANES]
            q_ref[:, t * LANES:(t + 1) * LANES] = (q * _half_rsqrt(q, lo) * qg_ref[...] * scale).astype(BF16)
        for p in range(kw // LANES):
            k = x_ref[:, qw + p * LANES:qw + (p + 1) * LANES]
            kn = k * _half_rsqrt(k, lo) * kg_ref[...]
            v = x_ref[:, qw + kw + p * LANES:qw + kw + (p + 1) * LANES]
            for src, dst in ((kn, k_ref), (v, v_ref)):
                rolled = pltpu.roll(src, HEAD_DIM, 1)
                dst[2 * p] = jnp.where(lo, src, rolled).astype(BF16)
                dst[2 * p + 1] = jnp.where(lo, rolled, src).astype(BF16)

    vec = pl.BlockSpec((1, LANES), lambda i: (0, 0))
    kv = pl.BlockSpec((n_kv, tm, LANES), lambda i: (0, i, 0))
    return pl.pallas_call(
        body, name="qk_norm_fwd", grid=(S // tm,),
        in_specs=[pl.BlockSpec((tm, W), lambda i: (i, 0)), vec, vec],
        out_specs=[pl.BlockSpec((tm, qw), lambda i: (i, 0)), kv, kv],
        out_shape=[jax.ShapeDtypeStruct((S, qw), BF16), jax.ShapeDtypeStruct((n_kv, S, LANES), BF16),
                   jax.ShapeDtypeStruct((n_kv, S, LANES), BF16)],
        compiler_params=_cparams(("parallel",)),
    )(qkv, qg, kg)


def _qk_norm_bwd(qkv, qg, kg, dq, dkc, dkp, dvc, dvp, n_q, n_kv):
    S, W = qkv.shape
    tm = WINDOW
    n = S // tm
    qw = n_q * HEAD_DIM
    kw = n_kv * HEAD_DIM
    scale = HEAD_DIM ** -0.5

    def body(x_ref, qg_ref, kg_ref, dq_ref, dkc_ref, dkp_ref, dvc_ref, dvp_ref, dx_ref, dqg_ref, dkg_ref,
             accq_ref, acck_ref):
        i = pl.program_id(0)
        lo = lax.broadcasted_iota(jnp.int32, (tm, LANES), 1) < HEAD_DIM
        has_next = (i < n - 1).astype(F32)

        @pl.when(i == 0)
        def _():
            accq_ref[...] = jnp.zeros_like(accq_ref)
            acck_ref[...] = jnp.zeros_like(acck_ref)

        def norm_bwd(t, dy, gain):
            r = _half_rsqrt(t, lo)
            xhat = t * r
            dxhat = dy * gain
            prod = dxhat * xhat
            c_lo = jnp.sum(jnp.where(lo, prod, 0.0), axis=-1, keepdims=True) * (1.0 / HEAD_DIM)
            c_hi = jnp.sum(jnp.where(lo, 0.0, prod), axis=-1, keepdims=True) * (1.0 / HEAD_DIM)
            return r * (dxhat - xhat * jnp.where(lo, c_lo, c_hi)), _rows_to_8(dy * xhat)

        for t in range(qw // LANES):
            sl = slice(t * LANES, (t + 1) * LANES)
            dt, dg = norm_bwd(x_ref[:, sl], dq_ref[:, sl] * scale, qg_ref[...])
            dx_ref[:, sl] = dt.astype(BF16)
            accq_ref[...] += dg

        def pair(cur_ref, prev_ref, p):
            folded = []
            for h in (2 * p, 2 * p + 1):
                tot = cur_ref[h] + prev_ref[h] * has_next
                folded.append(tot + pltpu.roll(tot, HEAD_DIM, 1))
            return jnp.where(lo, folded[0], folded[1])

        for p in range(kw // LANES):
            sl = slice(qw + p * LANES, qw + (p + 1) * LANES)
            dt, dg = norm_bwd(x_ref[:, sl], pair(dkc_ref, dkp_ref, p), kg_ref[...])
            dx_ref[:, sl] = dt.astype(BF16)
            acck_ref[...] += dg
            sl = slice(qw + kw + p * LANES, qw + kw + (p + 1) * LANES)
            dx_ref[:, sl] = pair(dvc_ref, dvp_ref, p).astype(BF16)

        @pl.when(i == n - 1)
        def _():
            for acc_ref, o_ref in ((accq_ref, dqg_ref), (acck_ref, dkg_ref)):
                s = jnp.sum(acc_ref[...], axis=0, keepdims=True)
                o_ref[...] = s + pltpu.roll(s, HEAD_DIM, 1)

    vec = pl.BlockSpec((1, LANES), lambda i: (0, 0))
    cur = pl.BlockSpec((n_kv, tm, LANES), lambda i: (0, i, 0))
    nxt = pl.BlockSpec((n_kv, tm, LANES), lambda i: (0, jnp.minimum(i + 1, n - 1), 0))
    return pl.pallas_call(
        body, name="qk_norm_bwd", grid=(n,),
        in_specs=[pl.BlockSpec((tm, W), lambda i: (i, 0)), vec, vec, pl.BlockSpec((tm, qw), lambda i: (i, 0)),
                  cur, nxt, cur, nxt],
        out_specs=[pl.BlockSpec((tm, W), lambda i: (i, 0)), vec, vec],
        out_shape=[jax.ShapeDtypeStruct((S, W), BF16), jax.ShapeDtypeStruct((1, LANES), F32),
                   jax.ShapeDtypeStruct((1, LANES), F32)],
        scratch_shapes=[pltpu.VMEM((8, LANES), F32), pltpu.VMEM((8, LANES), F32)],
        compiler_params=_cparams(("arbitrary",)),
    )(qkv, qg, kg, dq, dkc, dkp, dvc, dvp)


def _attn_specs(n_kv):
    gw = GQA_GROUP * HEAD_DIM
    q = pl.BlockSpec((WINDOW, gw), lambda kh, n: (n, kh))
    cur = pl.BlockSpec((None, WINDOW, LANES), lambda kh, n: (kh, n, 0))
    prev = pl.BlockSpec((None, WINDOW, LANES), lambda kh, n: (kh, jnp.maximum(n - 1, 0), 0))
    bias = pl.BlockSpec((GQA_GROUP, WINDOW, 2 * WINDOW), lambda kh, n: (kh, 0, 0))
    sink = pl.BlockSpec(memory_space=pltpu.SMEM)
    return q, cur, prev, bias, sink


def _attn_probs(qa, kc, kp, bias_ref, g, sink, has_prev):
    sc = lax.dot_general(qa, kc, _DN["nt"], preferred_element_type=F32) + bias_ref[g, :, WINDOW:]
    sp = lax.dot_general(qa, kp, _DN["nt"], preferred_element_type=F32) + bias_ref[g, :, :WINDOW]
    sp = jnp.where(has_prev, sp, NEG_INF)
    m = jnp.maximum(jnp.maximum(jnp.max(sc, axis=-1, keepdims=True), jnp.max(sp, axis=-1, keepdims=True)), sink)
    pc = jnp.exp(sc - m)
    pp = jnp.exp(sp - m)
    es = jnp.exp(sink - m)
    den = jnp.sum(pc, axis=-1, keepdims=True) + jnp.sum(pp, axis=-1, keepdims=True) + es
    return pc / den, pp / den, es / den


def _attn_fwd(q, kcat, vcat, biasm, sinks):
    S, qw = q.shape
    n_kv = kcat.shape[0]
    qs, cur, prev, bias, sink = _attn_specs(n_kv)

    def body(q_ref, kc_ref, kp_ref, vc_ref, vp_ref, b_ref, s_ref, o_ref):
        kh, n = pl.program_id(0), pl.program_id(1)
        lo = lax.broadcasted_iota(jnp.int32, (WINDOW, LANES), 1) < HEAD_DIM
        has_prev = n > 0
        kc, kp, vc, vp = kc_ref[...], kp_ref[...], vc_ref[...], vp_ref[...]
        for p in range(GQA_GROUP // 2):
            q2 = q_ref[:, p * LANES:(p + 1) * LANES]
            outs = []
            for a in range(2):
                g = 2 * p + a
                qa = jnp.where(lo if a == 0 else jnp.logical_not(lo), q2, jnp.zeros_like(q2))
                pc, pp, _ = _attn_probs(qa, kc, kp, b_ref, g, s_ref[kh * GQA_GROUP + g], has_prev)
                outs.append(jnp.dot(pc.astype(BF16), vc, preferred_element_type=F32)
                            + jnp.dot(pp.astype(BF16), vp, preferred_element_type=F32))
            o_ref[:, p * LANES:(p + 1) * LANES] = jnp.where(lo, outs[0], outs[1]).astype(BF16)

    return pl.pallas_call(
        body, name="attn_fwd", grid=(n_kv, S // WINDOW),
        in_specs=[qs, cur, prev, cur, prev, bias, sink],
        out_specs=qs,
        out_shape=jax.ShapeDtypeStruct((S, qw), BF16),
        compiler_params=_cparams(("parallel", "parallel")),
    )(q, kcat, kcat, vcat, vcat, biasm, sinks)


def _attn_bwd(q, kcat, vcat, biasm, sinks, do):
    S, qw = q.shape
    n_kv = kcat.shape[0]
    H = n_kv * GQA_GROUP
    qs, cur, prev, bias, sink = _attn_specs(n_kv)

    def body(q_ref, kc_ref, kp_ref, vc_ref, vp_ref, b_ref, s_ref, do_ref,
             dq_ref, dkc_ref, dkp_ref, dvc_ref, dvp_ref, db_ref, ds_ref):
        kh, n = pl.program_id(0), pl.program_id(1)
        lo = lax.broadcasted_iota(jnp.int32, (WINDOW, LANES), 1) < HEAD_DIM
        has_prev = n > 0
        kc, kp, vc, vp = kc_ref[...], kp_ref[...], vc_ref[...], vp_ref[...]

        @pl.when(n == 0)
        def _():
            db_ref[...] = jnp.zeros_like(db_ref)
            ds_ref[...] = jnp.zeros_like(ds_ref)

        dkc = jnp.zeros((WINDOW, LANES), F32)
        dkp = jnp.zeros((WINDOW, LANES), F32)
        dvc = jnp.zeros((WINDOW, LANES), F32)
        dvp = jnp.zeros((WINDOW, LANES), F32)
        for p in range(GQA_GROUP // 2):
            q2 = q_ref[:, p * LANES:(p + 1) * LANES]
            do2 = do_ref[:, p * LANES:(p + 1) * LANES]
            dqs = []
            for a in range(2):
                g = 2 * p + a
                sel = lo if a == 0 else jnp.logical_not(lo)
                qa = jnp.where(sel, q2, jnp.zeros_like(q2))
                doa = jnp.where(sel, do2, jnp.zeros_like(do2))
                pc, pp, ps = _attn_probs(qa, kc, kp, b_ref, g, s_ref[kh * GQA_GROUP + g], has_prev)
                dpc = lax.dot_general(doa, vc, _DN["nt"], preferred_element_type=F32)
                dpp = lax.dot_general(doa, vp, _DN["nt"], preferred_element_type=F32)
                delta = jnp.sum(pc * dpc, axis=-1, keepdims=True) + jnp.sum(pp * dpp, axis=-1, keepdims=True)
                dsc = pc * (dpc - delta)
                dsp = pp * (dpp - delta)
                db_ref[g, :, WINDOW:] += dsc
                db_ref[g, :, :WINDOW] += dsp
                ds_ref[g:g + 1, :] += jnp.zeros((1, LANES), F32) - jnp.sum(ps * delta)
                dscb, dspb = dsc.astype(BF16), dsp.astype(BF16)
                dqs.append(jnp.dot(dscb, kc, preferred_element_type=F32)
                           + jnp.dot(dspb, kp, preferred_element_type=F32))
                dkc += lax.dot_general(dscb, qa, _DN["tn"], preferred_element_type=F32)
                dkp += lax.dot_general(dspb, qa, _DN["tn"], preferred_element_type=F32)
                dvc += lax.dot_general(pc.astype(BF16), doa, _DN["tn"], preferred_element_type=F32)
                dvp += lax.dot_general(pp.astype(BF16), doa, _DN["tn"], preferred_element_type=F32)
            dq_ref[:, p * LANES:(p + 1) * LANES] = jnp.where(lo, dqs[0], dqs[1])
        dkc_ref[...] = dkc
        dkp_ref[...] = dkp
        dvc_ref[...] = dvc
        dvp_ref[...] = dvp

    part = jax.ShapeDtypeStruct((n_kv, S, LANES), F32)
    return pl.pallas_call(
        body, name="attn_bwd", grid=(n_kv, S // WINDOW),
        in_specs=[qs, cur, prev, cur, prev, bias, sink, qs],
        out_specs=[qs, cur, cur, cur, cur, bias, pl.BlockSpec((None, 8, LANES), lambda kh, n: (kh, 0, 0))],
        out_shape=[jax.ShapeDtypeStruct((S, qw), F32), part, part, part, part,
                   jax.ShapeDtypeStruct((H, WINDOW, 2 * WINDOW), F32),
                   jax.ShapeDtypeStruct((n_kv, 8, LANES), F32)],
        compiler_params=_cparams(("parallel", "arbitrary")),
    )(q, kcat, kcat, vcat, vcat, biasm, sinks, do)


def _adamw(name, w, g, m, v):
    shape = w.shape
    C = shape[-1]
    R = int(np.prod(shape[:-1]))
    tr = R
    if R * C * 4 > (1 << 20):
        tr = _tile(R, tuple(t for t in (512, 256, 128, 64, 32, 16, 8) if t * C * 4 <= (3 << 19)))
    c1 = 1.0 - ADAM_B1 ** ADAM_STEP
    c2 = 1.0 - ADAM_B2 ** ADAM_STEP

    def body(w_ref, g_ref, m_ref, v_ref, d_ref, nm_ref, nv_ref):
        gv = g_ref[...]
        nm = ADAM_B1 * m_ref[...] + (1.0 - ADAM_B1) * gv
        nv = ADAM_B2 * v_ref[...] + (1.0 - ADAM_B2) * (gv * gv)
        d_ref[...] = -ADAM_LR * ((nm / c1) / (jnp.sqrt(nv / c2) + ADAM_EPS) + ADAM_WD * w_ref[...])
        nm_ref[...] = nm
        nv_ref[...] = nv

    spec = pl.BlockSpec((tr, C), lambda i: (i, 0))
    outs = pl.pallas_call(
        body, name=name, grid=(R // tr,),
        in_specs=[spec] * 4, out_specs=[spec] * 3,
        out_shape=[jax.ShapeDtypeStruct((R, C), F32)] * 3,
        compiler_params=_cparams(("parallel",)),
    )(*[t.reshape(R, C) for t in (w, g, m, v)])
    return [o.reshape(shape) for o in outs]


def _place():
    return lax.axis_index("x"), lax.axis_index("y"), lax.axis_index("c")


def _other_chips(x, y):
    return [(1 - x, y), (x, 1 - y), (1 - x, 1 - y)]


_ANY = pl.BlockSpec(memory_space=pl.ANY)


def _gather_chips(name, shards):
    T = len(shards)

    def body(*refs):
        ins, outs = refs[:T], refs[T:2 * T]
        send, recv, local = refs[2 * T:]
        x, y, c = _place()
        chips = _other_chips(x, y)
        me = 2 * x + y
        mine = [pltpu.make_async_copy(ins[t], outs[t].at[me], local.at[t]) for t in range(T)]
        for cp in mine:
            cp.start()

        def copy(t, j, block):
            return pltpu.make_async_remote_copy(
                src_ref=ins[t], dst_ref=outs[t].at[block], send_sem=send.at[t, j], recv_sem=recv.at[t, j],
                device_id=(chips[j][0], chips[j][1], c), device_id_type=MESH)

        for t in range(T):
            for j in range(3):
                copy(t, j, me).start()
        for t in range(T):
            for j in range(3):
                copy(t, j, 2 * chips[j][0] + chips[j][1]).wait_recv()
        for t in range(T):
            for j in range(3):
                copy(t, j, me).wait_send()
        for cp in mine:
            cp.wait()

    return pl.pallas_call(
        body, name=name,
        in_specs=[_ANY] * T, out_specs=[_ANY] * T,
        out_shape=[jax.ShapeDtypeStruct((N_CHIPS,) + s.shape, s.dtype) for s in shards],
        scratch_shapes=[pltpu.SemaphoreType.DMA((T, 3)), pltpu.SemaphoreType.DMA((T, 3)),
                        pltpu.SemaphoreType.DMA((T,))],
        compiler_params=pltpu.CompilerParams(has_side_effects=True),
    )(*shards)


def _swap_halves(name, grads):
    T = len(grads)

    def body(*refs):
        ins, outs = refs[:T], refs[T:2 * T]
        send, recv = refs[2 * T:]
        x, y, c = _place()

        def copy(t):
            half = grads[t].shape[1] // 2
            src = ins[t].at[:, pl.ds(pl.multiple_of((1 - c) * half, 16), half), :]
            return pltpu.make_async_remote_copy(
                src_ref=src, dst_ref=outs[t], send_sem=send.at[t], recv_sem=recv.at[t],
                device_id=(x, y, 1 - c), device_id_type=MESH)

        for t in range(T):
            copy(t).start()
        for t in range(T):
            copy(t).wait()

    return pl.pallas_call(
        body, name=name,
        in_specs=[_ANY] * T, out_specs=[_ANY] * T,
        out_shape=[jax.ShapeDtypeStruct((g.shape[0], g.shape[1] // 2, g.shape[2]), g.dtype) for g in grads],
        scratch_shapes=[pltpu.SemaphoreType.DMA((T,)), pltpu.SemaphoreType.DMA((T,))],
        compiler_params=pltpu.CompilerParams(has_side_effects=True),
    )(*grads)


def _add_half(name, g, other, c_arr):
    nb, R, C = g.shape
    half = R // 2
    tr = _tile(half, (256, 128, 64, 32, 16))
    n = half // tr

    def body(c_ref, g_ref, o_ref, s_ref):
        s_ref[...] = (g_ref[...].astype(F32) + o_ref[...].astype(F32)).astype(BF16)

    return pl.pallas_call(
        body, name=name,
        grid_spec=pltpu.PrefetchScalarGridSpec(
            num_scalar_prefetch=1, grid=(nb, n),
            in_specs=[pl.BlockSpec((None, tr, C), lambda b, i, c: (b, c[0] * n + i, 0)),
                      pl.BlockSpec((None, tr, C), lambda b, i, c: (b, i, 0))],
            out_specs=pl.BlockSpec((None, tr, C), lambda b, i, c: (b, i, 0))),
        out_shape=jax.ShapeDtypeStruct((nb, half, C), BF16),
        compiler_params=_cparams(("parallel", "parallel")),
    )(c_arr, g, other)


def _scatter_chips(name, sums):
    T = len(sums)

    def body(*refs):
        ins, outs = refs[:T], refs[T:2 * T]
        send, recv, local = refs[2 * T:]
        x, y, c = _place()
        chips = _other_chips(x, y)
        me = 2 * x + y
        mine = [pltpu.make_async_copy(ins[t].at[me], outs[t].at[me], local.at[t]) for t in range(T)]
        for cp in mine:
            cp.start()

        def copy(t, j, src_block, dst_slot):
            return pltpu.make_async_remote_copy(
                src_ref=ins[t].at[src_block], dst_ref=outs[t].at[dst_slot],
                send_sem=send.at[t, j], recv_sem=recv.at[t, j],
                device_id=(chips[j][0], chips[j][1], c), device_id_type=MESH)

        for t in range(T):
            for j in range(3):
                copy(t, j, 2 * chips[j][0] + chips[j][1], me).start()
        for t in range(T):
            for j in range(3):
                copy(t, j, me, 2 * chips[j][0] + chips[j][1]).wait_recv()
        for t in range(T):
            for j in range(3):
                copy(t, j, 2 * chips[j][0] + chips[j][1], me).wait_send()
        for cp in mine:
            cp.wait()

    return pl.pallas_call(
        body, name=name,
        in_specs=[_ANY] * T, out_specs=[_ANY] * T,
        out_shape=[jax.ShapeDtypeStruct(s.shape, s.dtype) for s in sums],
        scratch_shapes=[pltpu.SemaphoreType.DMA((T, 3)), pltpu.SemaphoreType.DMA((T, 3)),
                        pltpu.SemaphoreType.DMA((T,))],
        compiler_params=pltpu.CompilerParams(has_side_effects=True),
    )(*sums)


def _sum_slots(name, slots):
    nb, R2, C = slots.shape
    tr = _tile(R2, (256, 128, 64, 32, 16))

    def body(s_ref, o_ref):
        acc = s_ref[0].astype(F32)
        for s in range(1, nb):
            acc = acc + s_ref[s].astype(F32)
        o_ref[...] = acc

    return pl.pallas_call(
        body, name=name, grid=(R2 // tr,),
        in_specs=[pl.BlockSpec((nb, tr, C), lambda i: (0, i, 0))],
        out_specs=pl.BlockSpec((tr, C), lambda i: (i, 0)),
        out_shape=jax.ShapeDtypeStruct((R2, C), F32),
        compiler_params=_cparams(("parallel",)),
    )(slots)


def _join_halves(name, parts):
    T = len(parts)

    def body(*refs):
        ins, outs = refs[:T], refs[T:2 * T]
        send, recv, local = refs[2 * T:]
        x, y, c = _place()
        mine = [pltpu.make_async_copy(ins[t], outs[t].at[c], local.at[t]) for t in range(T)]
        for cp in mine:
            cp.start()

        def copy(t, half):
            return pltpu.make_async_remote_copy(
                src_ref=ins[t], dst_ref=outs[t].at[half], send_sem=send.at[t], recv_sem=recv.at[t],
                device_id=(x, y, 1 - c), device_id_type=MESH)

        for t in range(T):
            copy(t, c).start()
        for t in range(T):
            copy(t, 1 - c).wait_recv()
        for t in range(T):
            copy(t, c).wait_send()
        for cp in mine:
            cp.wait()

    return pl.pallas_call(
        body, name=name,
        in_specs=[_ANY] * T, out_specs=[_ANY] * T,
        out_shape=[jax.ShapeDtypeStruct((2,) + p.shape, p.dtype) for p in parts],
        scratch_shapes=[pltpu.SemaphoreType.DMA((T,)), pltpu.SemaphoreType.DMA((T,)),
                        pltpu.SemaphoreType.DMA((T,))],
        compiler_params=pltpu.CompilerParams(has_side_effects=True),
    )(*parts)


def _reduce_scatter(tag, grads, c_arr):
    got = _swap_halves("rs_swap_" + tag, grads)
    sums = [_add_half("rs_add_%s_%d" % (tag, t), g, o, c_arr) for t, (g, o) in enumerate(zip(grads, got))]
    slots = _scatter_chips("rs_scatter_" + tag, sums)
    parts = [_sum_slots("rs_sum_%s_%d" % (tag, t), s) for t, s in enumerate(slots)]
    joined = _join_halves("rs_join_" + tag, parts)
    return [j.reshape(g.shape[1], g.shape[2]) for j, g in zip(joined, grads)]


def _all_reduce_small(v):
    R = v.shape[0]

    def body(v_ref, o_ref, buf, send, recv):
        x, y, c = _place()
        me = 4 * x + 2 * y + c
        buf[me] = v_ref[...]

        def copy(k, slot):
            to = (x ^ ((k >> 2) & 1), y ^ ((k >> 1) & 1), c ^ (k & 1))
            return pltpu.make_async_remote_copy(
                src_ref=v_ref, dst_ref=buf.at[slot], send_sem=send.at[k - 1], recv_sem=recv.at[k - 1],
                device_id=to, device_id_type=MESH)

        for k in range(1, 8):
            copy(k, me).start()
        for k in range(1, 8):
            copy(k, me ^ k).wait_recv()
        for k in range(1, 8):
            copy(k, me).wait_send()
        acc = buf[0]
        for s in range(1, 8):
            acc = acc + buf[s]
        o_ref[...] = acc

    vm = pl.BlockSpec(memory_space=pltpu.VMEM)
    return pl.pallas_call(
        body, name="all_reduce_small",
        in_specs=[vm], out_specs=vm,
        out_shape=jax.ShapeDtypeStruct((R, LANES), F32),
        scratch_shapes=[pltpu.VMEM((8, R, LANES), F32), pltpu.SemaphoreType.DMA((7,)), pltpu.SemaphoreType.DMA((7,))],
        compiler_params=pltpu.CompilerParams(has_side_effects=True, vmem_limit_bytes=VMEM_LIMIT),
    )(v)


def _local_step(x, tgt, small, wqkv_g, wo_g, wp_g, wup_g, wdn_g, convw, pscale):
    S, D = x.shape
    nb = wup_g.shape[0]
    depth = wup_g.shape[1]
    n_q = D // HEAD_DIM
    n_kv = n_q // GQA_GROUP
    Fh = wup_g.shape[3] * nb // 2

    def dup(gain):
        return jnp.tile(gain, 2).reshape(1, LANES)

    biasm = _bias_expand(small["rel_bias"])
    saved = []
    for i in range(depth):
        j = i // 2
        st = {"x0": x}
        gm = small["norm_mix"][i].reshape(1, D)
        if i % 2 == 0:
            h = _rms_fwd("rms_mix_fwd", x, gm)
            qkv = _mm_colblk("qkv_fwd", h, wqkv_g, j, F32)
            qh, kcat, vcat = _qk_norm_fwd(qkv, dup(small["attn_q_gain"][j]), dup(small["attn_k_gain"][j]), n_q, n_kv)
            o = _attn_fwd(qh, kcat, vcat, biasm, small["attn_sinks"][j])
            x = _mm_rowblk_res("wo_fwd", o, wo_g, j, x)
            st.update(h=h, qkv=qkv, qh=qh, kcat=kcat, vcat=vcat, o=o)
        else:
            d = _pool_fwd(x, gm)
            ypre, x = _pool_mm_fwd(d, wp_g, j, x, pscale[j])
            st.update(d=d, ypre=ypre)
        st["x1"] = x
        h2 = _rms_fwd("rms_ffn_fwd", x, small["norm_ffn"][i].reshape(1, D))
        u = _mm_colblk("up_fwd", h2, wup_g, i, F32, split2=True)
        cb = small["ffn_conv_b"][i].reshape(2, 1, Fh)
        a = _ffn_act_fwd(u, convw[i], cb)
        x = _mm_rowblk_res("down_fwd", a, wdn_g, i, x)
        st.update(h2=h2, u=u, a=a, cb=cb)
        saved.append(st)

    dx, dxb, loss_part = _loss_head(x, tgt)

    big = [dict() for _ in range(depth)]
    sg = {k: [None] * depth for k in ("norm_mix", "norm_ffn", "conv_w", "conv_b")}
    sg.update({k: [None] * (depth // 2) for k in ("q_gain", "k_gain", "sinks", "pool_scale")})
    dbias_tot = None
    for i in reversed(range(depth)):
        j = i // 2
        st = saved[i]
        da = _mm_nt_rowblk("down_bwd_in", dxb, wdn_g, i, F32)
        big[i]["wdn"] = _mm_tn_rowblk("down_bwd_w", st["a"], dxb, nb)
        duc, dcwb = _ffn_act_bwd(st["u"], convw[i], st["cb"], da)
        du = _ffn_conv_bwd(duc, convw[i])
        sg["conv_w"][i] = jnp.transpose(dcwb[:, 0:3, :], (1, 0, 2)).reshape(3, 2 * Fh)
        sg["conv_b"][i] = dcwb[:, 3, :].reshape(2 * Fh)
        big[i]["wup"] = _mm_tn_colblk("up_bwd_w", st["h2"], du, nb, split2=True)
        dh2 = _mm_nt_colblk("up_bwd_in", du, wup_g, i, split2=True)
        dx, dxb, dg = _rms_bwd("rms_ffn_bwd", st["x1"], small["norm_ffn"][i].reshape(1, D), dh2, dx)
        sg["norm_ffn"][i] = dg.reshape(D)
        gm = small["norm_mix"][i].reshape(1, D)
        if i % 2 == 0:
            do = _mm_nt_rowblk("wo_bwd_in", dxb, wo_g, j, BF16)
            big[i]["wo"] = _mm_tn_rowblk("wo_bwd_w", st["o"], dxb, nb)
            dq, dkc, dkp, dvc, dvp, dbias, dsink = _attn_bwd(
                st["qh"], st["kcat"], st["vcat"], biasm, small["attn_sinks"][j], do)
            dbias_tot = dbias if dbias_tot is None else dbias_tot + dbias
            sg["sinks"][j] = dsink[:, :, 0].reshape(n_q)
            dqkv, dqg, dkg = _qk_norm_bwd(st["qkv"], dup(small["attn_q_gain"][j]), dup(small["attn_k_gain"][j]),
                                          dq, dkc, dkp, dvc, dvp, n_q, n_kv)
            sg["q_gain"][j] = dqg[0, :HEAD_DIM]
            sg["k_gain"][j] = dkg[0, :HEAD_DIM]
            big[i]["wqkv"] = _mm_tn_colblk("qkv_bwd_w", st["h"], dqkv, nb)
            dh = _mm_nt_colblk("qkv_bwd_in", dqkv, wqkv_g, j)
            dx, dxb, dg = _rms_bwd("rms_mix_bwd", st["x0"], gm, dh, dx)
        else:
            dyp, dsc = _pool_bwd_pre(dx, st["ypre"], pscale[j])
            sg["pool_scale"][j] = dsc.reshape(D)
            big[i]["wp"] = _pool_mm_bwd_w(st["d"], dyp, nb)
            dd = _pool_mm_bwd_in(dyp, wp_g, j)
            dx, dxb, dg = _pool_bwd_post(dd, st["x0"], gm, dx)
        sg["norm_mix"][i] = dg.reshape(D)
    sg["rel_bias"] = _bias_reduce(dbias_tot)[:, 0, :N_BUCKETS]
    return loss_part, dx, big, sg


_SMALL_ORDER = ("norm_mix", "norm_ffn", "rel_bias", "q_gain", "k_gain", "sinks", "conv_b", "conv_w", "pool_scale")


def kernel(x, norm_mix, norm_ffn, rel_bias, attn_w_qkv, attn_q_gain, attn_k_gain, attn_sinks, attn_w_o, pool_w, pool_scale, ffn_w_up, ffn_conv_w, ffn_conv_b, ffn_w_down, loss_target, m_norm_mix, m_norm_ffn, m_rel_bias, m_attn_w_qkv, m_attn_q_gain, m_attn_k_gain, m_attn_sinks, m_attn_w_o, m_pool_w, m_pool_scale, m_ffn_w_up, m_ffn_conv_w, m_ffn_conv_b, m_ffn_w_down, v_norm_mix, v_norm_ffn, v_rel_bias, v_attn_w_qkv, v_attn_q_gain, v_attn_k_gain, v_attn_sinks, v_attn_w_o, v_pool_w, v_pool_scale, v_ffn_w_up, v_ffn_conv_w, v_ffn_conv_b, v_ffn_w_down):
    _, S, D = x.shape
    depth = ffn_w_up.shape[0]
    n_attn, n_pool = attn_w_qkv.shape[0], pool_w.shape[0]
    nb = N_CHIPS
    nc_up = ffn_w_up.shape[2]
    Fh = nc_up * nb // 2
    cx, cy, cc = _place()
    chip = 2 * cx + cy
    c_arr = jnp.reshape(cc, (1,)).astype(jnp.int32)

    wqkv_g, wo_g, wp_g, wup_g, wdn_g, convw_g, pscale_g = _gather_chips(
        "gather_weights",
        [attn_w_qkv.astype(BF16), attn_w_o.astype(BF16), pool_w.astype(BF16), ffn_w_up.astype(BF16),
         ffn_w_down.astype(BF16), ffn_conv_w, pool_scale])
    convw = [jnp.transpose(convw_g[:, i], (1, 0, 2)).reshape(3, 2, Fh).transpose(1, 0, 2) for i in range(depth)]
    pscale = [pscale_g[:, j].reshape(1, D) for j in range(n_pool)]
    small = dict(norm_mix=norm_mix, norm_ffn=norm_ffn, rel_bias=rel_bias, attn_q_gain=attn_q_gain,
                 attn_k_gain=attn_k_gain, attn_sinks=attn_sinks, ffn_conv_b=ffn_conv_b)

    loss_part, dx, big, sg = _local_step(x[0], loss_target[0], small, wqkv_g, wo_g, wp_g, wup_g, wdn_g, convw, pscale)
    loss = lax.psum(jnp.sum(loss_part), ("x", "y", "c"))

    red = {k: [] for k in ("wqkv", "wo", "wp", "wup", "wdn")}
    for i in range(depth):
        names = sorted(big[i])
        flat = [big[i][k].reshape(nb, -1, big[i][k].shape[-1]) for k in names]
        outs = _reduce_scatter("attn" if i % 2 == 0 else "pool", flat, c_arr)
        for k, o in zip(names, outs):
            red[k].append(o)
    g_wqkv = jnp.stack(red["wqkv"])
    g_wo = jnp.stack(red["wo"])
    g_wp = jnp.stack(red["wp"]).reshape(pool_w.shape)
    g_wup = jnp.stack(red["wup"])
    g_wdn = jnp.stack(red["wdn"])

    parts = [jnp.stack(sg[k]) if isinstance(sg[k], list) else sg[k] for k in _SMALL_ORDER]
    sizes = [int(np.prod(p.shape)) for p in parts]
    total = sum(sizes)
    rows = -(-total // (8 * LANES)) * 8
    packed = jnp.concatenate([p.reshape(-1) for p in parts] + [jnp.zeros((rows * LANES - total,), F32)])
    summed = _all_reduce_small(packed.reshape(rows, LANES)).reshape(-1)
    sm, off = {}, 0
    for k, p, n in zip(_SMALL_ORDER, parts, sizes):
        sm[k] = summed[off:off + n].reshape(p.shape)
        off += n
    g_convw = lax.dynamic_slice_in_dim(sm["conv_w"], chip * nc_up, nc_up, axis=2)
    pc = pool_scale.shape[1]
    g_pscale = lax.dynamic_slice_in_dim(sm["pool_scale"], chip * pc, pc, axis=1)

    grads = [sm["norm_mix"], sm["norm_ffn"], sm["rel_bias"], g_wqkv, sm["q_gain"], sm["k_gain"], sm["sinks"], g_wo,
             g_wp, g_pscale, g_wup, g_convw, sm["conv_b"], g_wdn]
    ws = [norm_mix, norm_ffn, rel_bias, attn_w_qkv, attn_q_gain, attn_k_gain, attn_sinks, attn_w_o, pool_w, pool_scale,
          ffn_w_up, ffn_conv_w, ffn_conv_b, ffn_w_down]
    ms = [m_norm_mix, m_norm_ffn, m_rel_bias, m_attn_w_qkv, m_attn_q_gain, m_attn_k_gain, m_attn_sinks, m_attn_w_o,
          m_pool_w, m_pool_scale, m_ffn_w_up, m_ffn_conv_w, m_ffn_conv_b, m_ffn_w_down]
    vs = [v_norm_mix, v_norm_ffn, v_rel_bias, v_attn_w_qkv, v_attn_q_gain, v_attn_k_gain, v_attn_sinks, v_attn_w_o,
          v_pool_w, v_pool_scale, v_ffn_w_up, v_ffn_conv_w, v_ffn_conv_b, v_ffn_w_down]
    deltas, new_m, new_v = [], [], []
    for idx, (w, g, m, v) in enumerate(zip(ws, grads, ms, vs)):
        d, nm, nv = _adamw("adamw_%d" % idx, w, g, m, v)
        deltas.append(d), new_m.append(nm), new_v.append(nv)
    return (loss, dx.reshape(1, S, D), *grads, *deltas, *new_m, *new_v)
```

```python
import functools

import numpy as np
import jax
import jax.numpy as jnp
from jax import lax
from jax.experimental import pallas as pl
from jax.experimental.pallas import tpu as pltpu

F32 = jnp.float32
BF16 = jnp.bfloat16
MESH = pl.DeviceIdType.MESH
_ANY = pl.BlockSpec(memory_space=pl.ANY)
_HBM = pl.BlockSpec(memory_space=pltpu.HBM)
_SEM = pl.BlockSpec(memory_space=pltpu.SEMAPHORE)
_DATAFLOW = pltpu.SideEffectType.DATAFLOW_SIDE_EFFECTING

N_CHIPS = 4
HEAD_DIM = 64
GQA_GROUP = 8
WINDOW = 128
N_BUCKETS = 32
MAX_DISTANCE = 128
POOL_WINDOWS = (2, 4, 8, 16)
POOL_HALO = 16
EPS = 1e-6
NEG_INF = -1e30
LANES = 128
VMEM_LIMIT = 56 * 1024 * 1024

ADAM_LR = 0.001
ADAM_B1 = 0.9
ADAM_B2 = 0.999
ADAM_EPS = 1e-08
ADAM_WD = 0.01
ADAM_STEP = 10


def _tile(n, prefs):
    for p in prefs:
        if p <= n and n % p == 0:
            return p
    return n


def _cparams(sem):
    return pltpu.CompilerParams(dimension_semantics=sem, vmem_limit_bytes=VMEM_LIMIT)


_DN = {
    "nn": (((1,), (0,)), ((), ())),
    "nt": (((1,), (1,)), ((), ())),
    "tn": (((0,), (0,)), ((), ())),
}


def _mm(name, kind, a, b, grid, a_spec, b_spec, outs, acc_shape, extras=(), epilogue=None):
    nk = grid[2]
    n_ex, n_out = len(extras), len(outs)

    def body(a_ref, b_ref, *rest):
        ex_refs = rest[:n_ex]
        out_refs = rest[n_ex:n_ex + n_out]
        acc_ref = rest[n_ex + n_out] if nk > 1 else None
        part = lax.dot_general(a_ref[...], b_ref[...], _DN[kind], preferred_element_type=F32)

        def finish(val):
            vals = epilogue(val, *[r[...] for r in ex_refs]) if epilogue else (val,)
            for r, v in zip(out_refs, vals):
                r[...] = v.astype(r.dtype)

        if nk == 1:
            finish(part)
        else:
            k = pl.program_id(2)

            @pl.when(k == 0)
            def _():
                acc_ref[...] = part

            @pl.when(k > 0)
            def _():
                acc_ref[...] += part

            @pl.when(k == nk - 1)
            def _():
                finish(acc_ref[...])

    res = pl.pallas_call(
        body,
        name=name,
        grid=grid,
        in_specs=[a_spec, b_spec] + [s for _, s in extras],
        out_specs=[s for _, _, s in outs],
        out_shape=[jax.ShapeDtypeStruct(sh, dt) for sh, dt, _ in outs],
        scratch_shapes=[pltpu.VMEM(acc_shape, F32)] if nk > 1 else [],
        compiler_params=_cparams(("parallel", "parallel", "arbitrary")),
    )(a, b, *[e for e, _ in extras])
    return res if n_out > 1 else res[0]


def _mm_colblk(name, a, wg, out_dtype, split2=False):
    S, K = a.shape
    nb, _, nc = wg.shape
    tm = _tile(S, (512,))
    tn = _tile(nc, (1408, 1024, 640, 512, 256, 128))
    npb = nc // tn
    ncol = nb * npb
    half = ncol // 2
    if split2:
        o_shape, o_spec = (2, S, nb * nc // 2), pl.BlockSpec((None, tm, tn), lambda p, q, k: (p // half, q, p % half))
    else:
        o_shape, o_spec = (S, nb * nc), pl.BlockSpec((tm, tn), lambda p, q, k: (q, p))
    return _mm(
        name, "nn", a, wg, (ncol, S // tm, 1),
        pl.BlockSpec((tm, K), lambda p, q, k: (q, 0)),
        pl.BlockSpec((None, K, tn), lambda p, q, k: (p // npb, 0, p % npb)),
        [(o_shape, out_dtype, o_spec)], (tm, tn))


def _mm_rowblk_res(name, a, wg, res):
    S = a.shape[0]
    nb, kc, N = wg.shape
    tm = _tile(S, (512,))
    tn = _tile(N, (2048, 1024, 512))
    tk = _tile(kc, (1408, 512, 256, 128))
    kpb = kc // tk
    o_spec = pl.BlockSpec((tm, tn), lambda p, q, k: (p, q))
    return _mm(
        name, "nn", a, wg, (S // tm, N // tn, nb * kpb),
        pl.BlockSpec((tm, tk), lambda p, q, k: (p, k)),
        pl.BlockSpec((None, tk, tn), lambda p, q, k: (k // kpb, k % kpb, q)),
        [((S, N), F32, o_spec)], (tm, tn),
        extras=[(res, o_spec)], epilogue=lambda acc, r: (r + acc,))


def _mm_nt_rowblk(name, g, wg, out_dtype):
    S, N = g.shape
    nb, kc, _ = wg.shape
    tm = _tile(S, (512,))
    tn = _tile(kc, (1408, 512, 256, 128))
    kpb = kc // tn
    return _mm(
        name, "nt", g, wg, (nb * kpb, S // tm, 1),
        pl.BlockSpec((tm, N), lambda p, q, k: (q, 0)),
        pl.BlockSpec((None, tn, N), lambda p, q, k: (p // kpb, p % kpb, 0)),
        [((S, nb * kc), out_dtype, pl.BlockSpec((tm, tn), lambda p, q, k: (q, p)))], (tm, tn))


def _mm_nt_colblk(name, g, wg, split2=False):
    S = g.shape[-2]
    nb, K, nc = wg.shape
    tm = _tile(S, (512,))
    tn = _tile(K, (2048, 1024))
    tk = _tile(nc, (1408, 640, 512, 256, 128))
    npb = nc // tk
    half = nb * npb // 2
    if split2:
        a_spec = pl.BlockSpec((None, tm, tk), lambda p, q, k: (k // half, p, k % half))
    else:
        a_spec = pl.BlockSpec((tm, tk), lambda p, q, k: (p, k))
    return _mm(
        name, "nt", g, wg, (S // tm, K // tn, nb * npb),
        a_spec,
        pl.BlockSpec((None, tn, tk), lambda p, q, k: (k // npb, q, k % npb)),
        [((S, K), F32, pl.BlockSpec((tm, tn), lambda p, q, k: (p, q)))], (tm, tn))


def _mm_tn_colblk(name, a, g, nb, split2=False):
    S, K = a.shape
    ntot = g.shape[-1] * (2 if split2 else 1)
    nc = ntot // nb
    ti = _tile(K, (1024,))
    tn = _tile(nc, (1408, 640, 512, 256, 128))
    ts = _tile(S, (1024, 512, 256))
    npb = nc // tn
    half = nb * npb // 2
    if split2:
        b_spec = pl.BlockSpec((None, ts, tn), lambda p, q, k: (q // half, k, q % half))
    else:
        b_spec = pl.BlockSpec((ts, tn), lambda p, q, k: (k, q))
    return _mm(
        name, "tn", a, g, (K // ti, nb * npb, S // ts),
        pl.BlockSpec((ts, ti), lambda p, q, k: (k, p)),
        b_spec,
        [((nb, K, nc), BF16, pl.BlockSpec((None, ti, tn), lambda p, q, k: (q // npb, p, q % npb)))], (ti, tn))


def _mm_tn_rowblk(name, a, g, nb):
    S, ktot = a.shape
    N = g.shape[1]
    kc = ktot // nb
    ti = _tile(kc, (1408, 512, 256, 128))
    tj = _tile(N, (1024,))
    ts = _tile(S, (1024, 512, 256))
    ipb = kc // ti
    return _mm(
        name, "tn", a, g, (nb * ipb, N // tj, S // ts),
        pl.BlockSpec((ts, ti), lambda p, q, k: (k, p)),
        pl.BlockSpec((ts, tj), lambda p, q, k: (k, q)),
        [((nb, kc, N), BF16, pl.BlockSpec((None, ti, tj), lambda p, q, k: (p // ipb, p % ipb, q)))], (ti, tj))


def _rms_fwd(name, x, gain):
    S, D = x.shape
    tm = _tile(S, (512,))

    def body(x_ref, g_ref, o_ref):
        xv = x_ref[...]
        r = lax.rsqrt(jnp.mean(xv * xv, axis=-1, keepdims=True) + EPS)
        o_ref[...] = (xv * r * g_ref[...]).astype(o_ref.dtype)

    return pl.pallas_call(
        body, name=name, grid=(S // tm,),
        in_specs=[pl.BlockSpec((tm, D), lambda i: (i, 0)), pl.BlockSpec((1, D), lambda i: (0, 0))],
        out_specs=pl.BlockSpec((tm, D), lambda i: (i, 0)),
        out_shape=jax.ShapeDtypeStruct((S, D), BF16),
        compiler_params=_cparams(("parallel",)),
    )(x, gain)


def _rms_bwd_math(xv, gain, dh):
    r = lax.rsqrt(jnp.mean(xv * xv, axis=-1, keepdims=True) + EPS)
    xhat = xv * r
    dxhat = dh * gain
    c = jnp.mean(dxhat * xhat, axis=-1, keepdims=True)
    return r * (dxhat - xhat * c), dh * xhat


def _rows_to_8(v):
    tm, C = v.shape
    return jnp.sum(v.reshape(tm // 8, 8, C), axis=0)


def _rms_bwd(name, x, gain, dh, dres):
    S, D = x.shape
    tm = _tile(S, (256,))
    n = S // tm

    def body(x_ref, g_ref, dh_ref, dr_ref, dx_ref, dxb_ref, dg_ref, acc_ref):
        i = pl.program_id(0)
        dxn, dgr = _rms_bwd_math(x_ref[...], g_ref[...], dh_ref[...])
        dx = dr_ref[...] + dxn
        dx_ref[...] = dx
        dxb_ref[...] = dx.astype(BF16)

        @pl.when(i == 0)
        def _():
            acc_ref[...] = jnp.zeros_like(acc_ref)

        acc_ref[...] += _rows_to_8(dgr)

        @pl.when(i == n - 1)
        def _():
            dg_ref[...] = jnp.sum(acc_ref[...], axis=0, keepdims=True)

    row = pl.BlockSpec((tm, D), lambda i: (i, 0))
    vec = pl.BlockSpec((1, D), lambda i: (0, 0))
    return pl.pallas_call(
        body, name=name, grid=(n,),
        in_specs=[row, vec, row, row],
        out_specs=[row, row, vec],
        out_shape=[jax.ShapeDtypeStruct((S, D), F32), jax.ShapeDtypeStruct((S, D), BF16),
                   jax.ShapeDtypeStruct((1, D), F32)],
        scratch_shapes=[pltpu.VMEM((8, D), F32)],
        compiler_params=_cparams(("arbitrary",)),
    )(x, gain, dh, dres)


def _loss_head(y, tgt):
    S, D = y.shape
    tm = _tile(S, (256,))
    n = S // tm

    def body(y_ref, t_ref, dy_ref, dyb_ref, l_ref):
        i = pl.program_id(0)
        e = y_ref[...] - t_ref[...]
        dy = e * (1.0 / D)
        dy_ref[...] = dy
        dyb_ref[...] = dy.astype(BF16)

        @pl.when(i == 0)
        def _():
            l_ref[...] = jnp.zeros_like(l_ref)

        sq = _rows_to_8(e * e)
        part = sq[:, 0:LANES]
        for t in range(1, D // LANES):
            part = part + sq[:, t * LANES:(t + 1) * LANES]
        l_ref[...] += part * (0.5 / D)

    row = pl.BlockSpec((tm, D), lambda i: (i, 0))
    return pl.pallas_call(
        body, name="loss_head", grid=(n,),
        in_specs=[row, row],
        out_specs=[row, row, pl.BlockSpec((8, LANES), lambda i: (0, 0))],
        out_shape=[jax.ShapeDtypeStruct((S, D), F32), jax.ShapeDtypeStruct((S, D), BF16),
                   jax.ShapeDtypeStruct((8, LANES), F32)],
        compiler_params=_cparams(("arbitrary",)),
    )(y, tgt)


def _shift_down(u, halo, k):
    row = lax.broadcasted_iota(jnp.int32, u.shape, 0)
    out = pltpu.roll(u, k, 0)
    for j in range(k):
        out = jnp.where(row == j, halo[8 - k + j:8 - k + j + 1, :], out)
    return out


def _shift_up(u, halo, k):
    tm = u.shape[0]
    row = lax.broadcasted_iota(jnp.int32, u.shape, 0)
    out = pltpu.roll(u, tm - k, 0)
    for j in range(k):
        out = jnp.where(row == tm - k + j, halo[j:j + 1, :], out)
    return out


def _conv_fwd(u, halo, cw, cb):
    um1 = _shift_down(u, halo, 1)
    um2 = _shift_down(u, halo, 2)
    return cw[0:1] * um2 + cw[1:2] * um1 + cw[2:3] * u + cb, um1, um2


def _ffn_specs(S, Fh, tm, tc):
    n8 = tm // 8
    blk = pl.BlockSpec((2, tm, tc), lambda j, i: (0, i, j))
    prev = pl.BlockSpec((2, 8, tc), lambda j, i: (0, jnp.maximum(i * n8 - 1, 0), j))
    nxt = pl.BlockSpec((2, 8, tc), lambda j, i: (0, jnp.minimum((i + 1) * n8, S // 8 - 1), j))
    cw = pl.BlockSpec((2, 3, tc), lambda j, i: (0, 0, j))
    cb = pl.BlockSpec((2, 1, tc), lambda j, i: (0, 0, j))
    one = pl.BlockSpec((tm, tc), lambda j, i: (i, j))
    return blk, prev, nxt, cw, cb, one


def _ffn_act_fwd(u, cw, cb):
    _, S, Fh = u.shape
    tm, tc = _tile(S, (512,)), _tile(Fh, (512, 256, 128))
    blk, prev, _, cws, cbs, one = _ffn_specs(S, Fh, tm, tc)

    def body(u_ref, h_ref, cw_ref, cb_ref, a_ref):
        i = pl.program_id(1)
        keep = (i > 0).astype(F32)
        ucs = []
        for s in range(2):
            uc, _, _ = _conv_fwd(u_ref[s], h_ref[s] * keep, cw_ref[s], cb_ref[s])
            ucs.append(uc)
        gate, val = ucs
        sig = 1.0 / (1.0 + jnp.exp(-gate))
        a_ref[...] = (gate * sig * val).astype(BF16)

    return pl.pallas_call(
        body, name="ffn_act_fwd", grid=(Fh // tc, S // tm),
        in_specs=[blk, prev, cws, cbs], out_specs=one,
        out_shape=jax.ShapeDtypeStruct((S, Fh), BF16),
        compiler_params=_cparams(("parallel", "parallel")),
    )(u, u, cw, cb)


def _ffn_act_bwd(u, cw, cb, da):
    _, S, Fh = u.shape
    tm, tc = _tile(S, (512,)), _tile(Fh, (512, 256, 128))
    blk, prev, _, cws, cbs, one = _ffn_specs(S, Fh, tm, tc)

    def body(u_ref, h_ref, cw_ref, cb_ref, da_ref, duc_ref, acc_ref):
        i = pl.program_id(1)
        keep = (i > 0).astype(F32)
        uc, um1, um2 = [], [], []
        for s in range(2):
            c, m1, m2 = _conv_fwd(u_ref[s], h_ref[s] * keep, cw_ref[s], cb_ref[s])
            uc.append(c), um1.append(m1), um2.append(m2)
        gate, val = uc
        dav = da_ref[...]
        sig = 1.0 / (1.0 + jnp.exp(-gate))
        dgate = dav * val * (sig * (1.0 + gate * (1.0 - sig)))
        dval = dav * (gate * sig)
        duc_ref[0] = dgate
        duc_ref[1] = dval

        @pl.when(i == 0)
        def _():
            acc_ref[...] = jnp.zeros_like(acc_ref)

        for s, d in enumerate((dgate, dval)):
            acc_ref[s, 0:1, :] += jnp.sum(d * um2[s], axis=0, keepdims=True)
            acc_ref[s, 1:2, :] += jnp.sum(d * um1[s], axis=0, keepdims=True)
            acc_ref[s, 2:3, :] += jnp.sum(d * u_ref[s], axis=0, keepdims=True)
            acc_ref[s, 3:4, :] += jnp.sum(d, axis=0, keepdims=True)

    return pl.pallas_call(
        body, name="ffn_act_bwd", grid=(Fh // tc, S // tm),
        in_specs=[blk, prev, cws, cbs, one],
        out_specs=[blk, pl.BlockSpec((2, 8, tc), lambda j, i: (0, 0, j))],
        out_shape=[jax.ShapeDtypeStruct((2, S, Fh), F32), jax.ShapeDtypeStruct((2, 8, Fh), F32)],
        compiler_params=_cparams(("parallel", "arbitrary")),
    )(u, u, cw, cb, da)


def _ffn_conv_bwd(duc, cw):
    _, S, Fh = duc.shape
    tm, tc = _tile(S, (512,)), _tile(Fh, (512, 256, 128))
    blk, _, nxt, cws, _, _ = _ffn_specs(S, Fh, tm, tc)
    n = S // tm

    def body(d_ref, h_ref, cw_ref, du_ref):
        i = pl.program_id(1)
        keep = (i < n - 1).astype(F32)
        for s in range(2):
            d, halo, w = d_ref[s], h_ref[s] * keep, cw_ref[s]
            du = w[2:3] * d + w[1:2] * _shift_up(d, halo, 1) + w[0:1] * _shift_up(d, halo, 2)
            du_ref[s] = du.astype(BF16)

    return pl.pallas_call(
        body, name="ffn_conv_bwd", grid=(Fh // tc, S // tm),
        in_specs=[blk, nxt, cws], out_specs=blk,
        out_shape=jax.ShapeDtypeStruct((2, S, Fh), BF16),
        compiler_params=_cparams(("parallel", "parallel")),
    )(duc, duc, cw)


def _pool_counts(i, tm, w):
    t = i * tm + lax.broadcasted_iota(jnp.int32, (tm, 1), 0)
    return jnp.minimum(t + 1, w).astype(F32)


def _pool_fwd(x, gain):
    S, D = x.shape
    tm = _tile(S, (256,))
    gd = D // len(POOL_WINDOWS)
    nh = tm // POOL_HALO

    def body(x_ref, xh_ref, g_ref, d_ref):
        i = pl.program_id(0)

        def norm(v):
            return v * lax.rsqrt(jnp.mean(v * v, axis=-1, keepdims=True) + EPS) * g_ref[...]

        h = norm(x_ref[...])
        hh = norm(xh_ref[...]) * (i > 0).astype(F32)
        ext = jnp.concatenate([hh, h], axis=0)
        for gi, w in enumerate(POOL_WINDOWS):
            sl = slice(gi * gd, (gi + 1) * gd)
            win = ext[:, sl]
            k = 1
            while k < w:
                win = win + pltpu.roll(win, k, 0)
                k *= 2
            mean = win[POOL_HALO:] / _pool_counts(i, tm, w)
            d_ref[:, sl] = (mean - h[:, sl]).astype(BF16)

    return pl.pallas_call(
        body, name="pool_fwd", grid=(S // tm,),
        in_specs=[pl.BlockSpec((tm, D), lambda i: (i, 0)),
                  pl.BlockSpec((POOL_HALO, D), lambda i: (jnp.maximum(i * nh - 1, 0), 0)),
                  pl.BlockSpec((1, D), lambda i: (0, 0))],
        out_specs=pl.BlockSpec((tm, D), lambda i: (i, 0)),
        out_shape=jax.ShapeDtypeStruct((S, D), BF16),
        compiler_params=_cparams(("parallel",)),
    )(x, x, gain)


def _pool_mm_fwd(d, wpg, x, scale):
    S, D = d.shape
    nb, G, kcb, gd = wpg.shape
    tm = _tile(S, (512,))
    o_spec = pl.BlockSpec((tm, gd), lambda p, q, k: (p, q))
    return _mm(
        "pool_mm_fwd", "nn", d, wpg, (S // tm, G, nb),
        pl.BlockSpec((tm, kcb), lambda p, q, k: (p, q * nb + k)),
        pl.BlockSpec((None, None, kcb, gd), lambda p, q, k: (k, q, 0, 0)),
        [((S, D), F32, o_spec), ((S, D), F32, o_spec)], (tm, gd),
        extras=[(x, o_spec), (scale, pl.BlockSpec((1, gd), lambda p, q, k: (0, q)))],
        epilogue=lambda acc, xv, sc: (acc, xv + acc * sc))


def _pool_bwd_pre(dx1, ypre, scale):
    S, D = dx1.shape
    tm = _tile(S, (256,))
    n = S // tm

    def body(dx_ref, y_ref, s_ref, dy_ref, ds_ref, acc_ref):
        i = pl.program_id(0)
        dx = dx_ref[...]
        dy_ref[...] = (dx * s_ref[...]).astype(BF16)

        @pl.when(i == 0)
        def _():
            acc_ref[...] = jnp.zeros_like(acc_ref)

        acc_ref[...] += _rows_to_8(dx * y_ref[...])

        @pl.when(i == n - 1)
        def _():
            ds_ref[...] = jnp.sum(acc_ref[...], axis=0, keepdims=True)

    row = pl.BlockSpec((tm, D), lambda i: (i, 0))
    vec = pl.BlockSpec((1, D), lambda i: (0, 0))
    return pl.pallas_call(
        body, name="pool_bwd_pre", grid=(n,),
        in_specs=[row, row, vec], out_specs=[row, vec],
        out_shape=[jax.ShapeDtypeStruct((S, D), BF16), jax.ShapeDtypeStruct((1, D), F32)],
        scratch_shapes=[pltpu.VMEM((8, D), F32)],
        compiler_params=_cparams(("arbitrary",)),
    )(dx1, ypre, scale)


def _pool_mm_bwd_in(dyp, wpg):
    S, D = dyp.shape
    nb, G, kcb, gd = wpg.shape
    tm = _tile(S, (512,))
    return _mm(
        "pool_mm_bwd_in", "nt", dyp, wpg, (S // tm, G * nb, 1),
        pl.BlockSpec((tm, gd), lambda p, q, k: (p, q // nb)),
        pl.BlockSpec((None, None, kcb, gd), lambda p, q, k: (q % nb, q // nb, 0, 0)),
        [((S, D), F32, pl.BlockSpec((tm, kcb), lambda p, q, k: (p, q)))], (tm, kcb))


def _pool_mm_bwd_w(d, dyp, nb):
    S, D = d.shape
    G = len(POOL_WINDOWS)
    gd = D // G
    kcb = gd // nb
    ts = _tile(S, (1024, 512, 256))
    return _mm(
        "pool_mm_bwd_w", "tn", d, dyp, (G * nb, 1, S // ts),
        pl.BlockSpec((ts, kcb), lambda p, q, k: (k, p)),
        pl.BlockSpec((ts, gd), lambda p, q, k: (k, p // nb)),
        [((nb, G, kcb, gd), BF16, pl.BlockSpec((None, None, kcb, gd), lambda p, q, k: (p % nb, p // nb, 0, 0)))],
        (kcb, gd))


def _pool_bwd_post(dd, x, gain, dres):
    S, D = x.shape
    tm = _tile(S, (256,))
    n = S // tm
    gd = D // len(POOL_WINDOWS)
    nh = tm // POOL_HALO

    def body(dd_ref, ddh_ref, x_ref, g_ref, dr_ref, dx_ref, dxb_ref, dg_ref, acc_ref):
        i = pl.program_id(0)
        dd = dd_ref[...]
        halo = ddh_ref[...] * (i < n - 1).astype(F32)
        parts = []
        for gi, w in enumerate(POOL_WINDOWS):
            sl = slice(gi * gd, (gi + 1) * gd)
            ext = jnp.concatenate([dd[:, sl] / _pool_counts(i, tm, w), halo[:, sl] * (1.0 / w)], axis=0)
            k = 1
            while k < w:
                ext = ext + pltpu.roll(ext, tm + POOL_HALO - k, 0)
                k *= 2
            parts.append(ext[:tm] - dd[:, sl])
        dh = jnp.concatenate(parts, axis=1)
        dxn, dgr = _rms_bwd_math(x_ref[...], g_ref[...], dh)
        dx = dr_ref[...] + dxn
        dx_ref[...] = dx
        dxb_ref[...] = dx.astype(BF16)

        @pl.when(i == 0)
        def _():
            acc_ref[...] = jnp.zeros_like(acc_ref)

        acc_ref[...] += _rows_to_8(dgr)

        @pl.when(i == n - 1)
        def _():
            dg_ref[...] = jnp.sum(acc_ref[...], axis=0, keepdims=True)

    row = pl.BlockSpec((tm, D), lambda i: (i, 0))
    vec = pl.BlockSpec((1, D), lambda i: (0, 0))
    nxt = pl.BlockSpec((POOL_HALO, D), lambda i: (jnp.minimum((i + 1) * nh, S // POOL_HALO - 1), 0))
    return pl.pallas_call(
        body, name="pool_bwd_post", grid=(n,),
        in_specs=[row, nxt, row, vec, row],
        out_specs=[row, row, vec],
        out_shape=[jax.ShapeDtypeStruct((S, D), F32), jax.ShapeDtypeStruct((S, D), BF16),
                   jax.ShapeDtypeStruct((1, D), F32)],
        scratch_shapes=[pltpu.VMEM((8, D), F32)],
        compiler_params=_cparams(("arbitrary",)),
    )(dd, dd, x, gain, dres)


def _band_tables():
    i = np.arange(WINDOW)[:, None]
    j = np.arange(2 * WINDOW)[None, :]
    n = np.maximum(WINDOW + i - j, 0)
    max_exact = N_BUCKETS // 2
    nf = np.maximum(n, 1).astype(np.float32)
    large = max_exact + (np.log(nf / max_exact) / np.log(MAX_DISTANCE / max_exact)
                         * (N_BUCKETS - max_exact)).astype(np.int32)
    large = np.minimum(large, N_BUCKETS - 1)
    buckets = np.where(n < max_exact, n, large).astype(np.int32)
    dist = WINDOW + i - j
    in_win = ((dist >= 0) & (dist < WINDOW)).astype(np.int32)
    return buckets, in_win


def _bias_expand(rel_bias):
    H = rel_bias.shape[0]
    buckets, in_win = _band_tables()

    def body(rb_ref, bk_ref, win_ref, o_ref):
        h = pl.program_id(0)
        bk = bk_ref[...]
        acc = jnp.zeros(bk.shape, F32)
        for b in range(N_BUCKETS):
            acc = jnp.where(bk == b, rb_ref[h, b], acc)
        o_ref[...] = jnp.where(win_ref[...] > 0, acc, NEG_INF)

    full = pl.BlockSpec((WINDOW, 2 * WINDOW), lambda h: (0, 0))
    return pl.pallas_call(
        body, name="bias_expand", grid=(H,),
        in_specs=[pl.BlockSpec(memory_space=pltpu.SMEM), full, full],
        out_specs=pl.BlockSpec((None, WINDOW, 2 * WINDOW), lambda h: (h, 0, 0)),
        out_shape=jax.ShapeDtypeStruct((H, WINDOW, 2 * WINDOW), F32),
        compiler_params=_cparams(("parallel",)),
    )(rel_bias, jnp.asarray(buckets), jnp.asarray(in_win))


def _bias_reduce(dbias):
    H = dbias.shape[0]
    buckets, in_win = _band_tables()

    def body(d_ref, bk_ref, win_ref, o_ref):
        bk = jnp.where(win_ref[...] > 0, bk_ref[...], -1)
        d = d_ref[...]
        lane = lax.broadcasted_iota(jnp.int32, (8, LANES), 1)
        out = jnp.zeros((8, LANES), F32)
        for b in range(N_BUCKETS):
            s = jnp.sum(jnp.where(bk == b, d, 0.0))
            out = jnp.where(lane == b, s, out)
        o_ref[...] = out

    full = pl.BlockSpec((WINDOW, 2 * WINDOW), lambda h: (0, 0))
    return pl.pallas_call(
        body, name="bias_reduce", grid=(H,),
        in_specs=[pl.BlockSpec((None, WINDOW, 2 * WINDOW), lambda h: (h, 0, 0)), full, full],
        out_specs=pl.BlockSpec((None, 8, LANES), lambda h: (h, 0, 0)),
        out_shape=jax.ShapeDtypeStruct((H, 8, LANES), F32),
        compiler_params=_cparams(("parallel",)),
    )(dbias, jnp.asarray(buckets), jnp.asarray(in_win))


def _half_rsqrt(t, lo):
    sq = t * t
    s_lo = jnp.sum(jnp.where(lo, sq, 0.0), axis=-1, keepdims=True)
    s_hi = jnp.sum(jnp.where(lo, 0.0, sq), axis=-1, keepdims=True)
    return jnp.where(lo, lax.rsqrt(s_lo * (1.0 / HEAD_DIM) + EPS), lax.rsqrt(s_hi * (1.0 / HEAD_DIM) + EPS))


def _qk_norm_fwd(qkv, qg, kg, n_q, n_kv):
    S, W = qkv.shape
    tm = _tile(S, (256,))
    qw = n_q * HEAD_DIM
    kw = n_kv * HEAD_DIM
    scale = HEAD_DIM ** -0.5

    def body(x_ref, qg_ref, kg_ref, q_ref, k_ref, v_ref):
        lo = lax.broadcasted_iota(jnp.int32, (tm, LANES), 1) < HEAD_DIM
        for t in range(qw // LANES):
            q = x_ref[:, t * LANES:(t + 1) * LANES]
            q_ref[:, t * LANES:(t + 1) * LANES] = (q * _half_rsqrt(q, lo) * qg_ref[...] * scale).astype(BF16)
        for p in range(kw // LANES):
            k = x_ref[:, qw + p * LANES:qw + (p + 1) * LANES]
            kn = k * _half_rsqrt(k, lo) * kg_ref[...]
            v = x_ref[:, qw + kw + p * LANES:qw + kw + (p + 1) * LANES]
            for src, dst in ((kn, k_ref), (v, v_ref)):
                rolled = pltpu.roll(src, HEAD_DIM, 1)
                dst[2 * p] = jnp.where(lo, src, rolled).astype(BF16)
                dst[2 * p + 1] = jnp.where(lo, rolled, src).astype(BF16)

    vec = pl.BlockSpec((1, LANES), lambda i: (0, 0))
    kv = pl.BlockSpec((n_kv, tm, LANES), lambda i: (0, i, 0))
    return pl.pallas_call(
        body, name="qk_norm_fwd", grid=(S // tm,),
        in_specs=[pl.BlockSpec((tm, W), lambda i: (i, 0)), vec, vec],
        out_specs=[pl.BlockSpec((tm, qw), lambda i: (i, 0)), kv, kv],
        out_shape=[jax.ShapeDtypeStruct((S, qw), BF16), jax.ShapeDtypeStruct((n_kv, S, LANES), BF16),
                   jax.ShapeDtypeStruct((n_kv, S, LANES), BF16)],
        compiler_params=_cparams(("parallel",)),
    )(qkv, qg, kg)


def _qk_norm_bwd(qkv, qg, kg, dq, dkc, dkp, dvc, dvp, n_q, n_kv):
    S, W = qkv.shape
    tm = WINDOW
    n = S // tm
    qw = n_q * HEAD_DIM
    kw = n_kv * HEAD_DIM
    scale = HEAD_DIM ** -0.5

    def body(x_ref, qg_ref, kg_ref, dq_ref, dkc_ref, dkp_ref, dvc_ref, dvp_ref, dx_ref, dqg_ref, dkg_ref,
             accq_ref, acck_ref):
        i = pl.program_id(0)
        lo = lax.broadcasted_iota(jnp.int32, (tm, LANES), 1) < HEAD_DIM
        has_next = (i < n - 1).astype(F32)

        @pl.when(i == 0)
        def _():
            accq_ref[...] = jnp.zeros_like(accq_ref)
            acck_ref[...] = jnp.zeros_like(acck_ref)

        def norm_bwd(t, dy, gain):
            r = _half_rsqrt(t, lo)
            xhat = t * r
            dxhat = dy * gain
            prod = dxhat * xhat
            c_lo = jnp.sum(jnp.where(lo, prod, 0.0), axis=-1, keepdims=True) * (1.0 / HEAD_DIM)
            c_hi = jnp.sum(jnp.where(lo, 0.0, prod), axis=-1, keepdims=True) * (1.0 / HEAD_DIM)
            return r * (dxhat - xhat * jnp.where(lo, c_lo, c_hi)), _rows_to_8(dy * xhat)

        for t in range(qw // LANES):
            sl = slice(t * LANES, (t + 1) * LANES)
            dt, dg = norm_bwd(x_ref[:, sl], dq_ref[:, sl] * scale, qg_ref[...])
            dx_ref[:, sl] = dt.astype(BF16)
            accq_ref[...] += dg

        def pair(cur_ref, prev_ref, p):
            folded = []
            for h in (2 * p, 2 * p + 1):
                tot = cur_ref[h] + prev_ref[h] * has_next
                folded.append(tot + pltpu.roll(tot, HEAD_DIM, 1))
            return jnp.where(lo, folded[0], folded[1])

        for p in range(kw // LANES):
            sl = slice(qw + p * LANES, qw + (p + 1) * LANES)
            dt, dg = norm_bwd(x_ref[:, sl], pair(dkc_ref, dkp_ref, p), kg_ref[...])
            dx_ref[:, sl] = dt.astype(BF16)
            acck_ref[...] += dg
            sl = slice(qw + kw + p * LANES, qw + kw + (p + 1) * LANES)
            dx_ref[:, sl] = pair(dvc_ref, dvp_ref, p).astype(BF16)

        @pl.when(i == n - 1)
        def _():
            for acc_ref, o_ref in ((accq_ref, dqg_ref), (acck_ref, dkg_ref)):
                s = jnp.sum(acc_ref[...], axis=0, keepdims=True)
                o_ref[...] = s + pltpu.roll(s, HEAD_DIM, 1)

    vec = pl.BlockSpec((1, LANES), lambda i: (0, 0))
    cur = pl.BlockSpec((n_kv, tm, LANES), lambda i: (0, i, 0))
    nxt = pl.BlockSpec((n_kv, tm, LANES), lambda i: (0, jnp.minimum(i + 1, n - 1), 0))
    return pl.pallas_call(
        body, name="qk_norm_bwd", grid=(n,),
        in_specs=[pl.BlockSpec((tm, W), lambda i: (i, 0)), vec, vec, pl.BlockSpec((tm, qw), lambda i: (i, 0)),
                  cur, nxt, cur, nxt],
        out_specs=[pl.BlockSpec((tm, W), lambda i: (i, 0)), vec, vec],
        out_shape=[jax.ShapeDtypeStruct((S, W), BF16), jax.ShapeDtypeStruct((1, LANES), F32),
                   jax.ShapeDtypeStruct((1, LANES), F32)],
        scratch_shapes=[pltpu.VMEM((8, LANES), F32), pltpu.VMEM((8, LANES), F32)],
        compiler_params=_cparams(("arbitrary",)),
    )(qkv, qg, kg, dq, dkc, dkp, dvc, dvp)


def _attn_specs(n_kv):
    gw = GQA_GROUP * HEAD_DIM
    q = pl.BlockSpec((WINDOW, gw), lambda kh, n: (n, kh))
    cur = pl.BlockSpec((None, WINDOW, LANES), lambda kh, n: (kh, n, 0))
    prev = pl.BlockSpec((None, WINDOW, LANES), lambda kh, n: (kh, jnp.maximum(n - 1, 0), 0))
    bias = pl.BlockSpec((GQA_GROUP, WINDOW, 2 * WINDOW), lambda kh, n: (kh, 0, 0))
    sink = pl.BlockSpec(memory_space=pltpu.SMEM)
    return q, cur, prev, bias, sink


def _stack_heads(x_ref, lo):
    parts = []
    for g in range(GQA_GROUP):
        t = x_ref[:, (g // 2) * LANES:(g // 2 + 1) * LANES]
        parts.append(jnp.where(lo if g % 2 == 0 else jnp.logical_not(lo), t, jnp.zeros_like(t)))
    return jnp.concatenate(parts, axis=0)


def _unstack_heads(v, lo):
    return [jnp.where(lo, v[(2 * p) * WINDOW:(2 * p + 1) * WINDOW], v[(2 * p + 1) * WINDOW:(2 * p + 2) * WINDOW])
            for p in range(GQA_GROUP // 2)]


def _attn_probs(qs, kk, bias_ref, s_ref, kh, has_prev):
    rows = GQA_GROUP * WINDOW
    s = lax.dot_general(qs, kk, _DN["nt"], preferred_element_type=F32) + bias_ref[...].reshape(rows, 2 * WINDOW)
    col = lax.broadcasted_iota(jnp.int32, (rows, 2 * WINDOW), 1)
    s = jnp.where(jnp.logical_or(has_prev, col >= WINDOW), s, NEG_INF)
    sink = jnp.concatenate([jnp.full((WINDOW, 1), s_ref[kh * GQA_GROUP + g], F32) for g in range(GQA_GROUP)], axis=0)
    m = jnp.maximum(jnp.max(s, axis=-1, keepdims=True), sink)
    p = jnp.exp(s - m)
    es = jnp.exp(sink - m)
    den = jnp.sum(p, axis=-1, keepdims=True) + es
    return p / den, es / den


def _attn_fwd(q, kcat, vcat, biasm, sinks):
    S, qw = q.shape
    n_kv = kcat.shape[0]
    qs, cur, prev, bias, sink = _attn_specs(n_kv)

    def body(q_ref, kc_ref, kp_ref, vc_ref, vp_ref, b_ref, s_ref, o_ref):
        kh, n = pl.program_id(0), pl.program_id(1)
        lo = lax.broadcasted_iota(jnp.int32, (WINDOW, LANES), 1) < HEAD_DIM
        kk = jnp.concatenate([kp_ref[...], kc_ref[...]], axis=0)
        vv = jnp.concatenate([vp_ref[...], vc_ref[...]], axis=0)
        pn, _ = _attn_probs(_stack_heads(q_ref, lo), kk, b_ref, s_ref, kh, n > 0)
        o = jnp.dot(pn.astype(BF16), vv, preferred_element_type=F32)
        for p, t in enumerate(_unstack_heads(o, lo)):
            o_ref[:, p * LANES:(p + 1) * LANES] = t.astype(BF16)

    return pl.pallas_call(
        body, name="attn_fwd", grid=(n_kv, S // WINDOW),
        in_specs=[qs, cur, prev, cur, prev, bias, sink],
        out_specs=qs,
        out_shape=jax.ShapeDtypeStruct((S, qw), BF16),
        compiler_params=_cparams(("parallel", "parallel")),
    )(q, kcat, kcat, vcat, vcat, biasm, sinks)


def _attn_bwd(q, kcat, vcat, biasm, sinks, do):
    S, qw = q.shape
    n_kv = kcat.shape[0]
    H = n_kv * GQA_GROUP
    qs, cur, prev, bias, sink = _attn_specs(n_kv)

    def body(q_ref, kc_ref, kp_ref, vc_ref, vp_ref, b_ref, s_ref, do_ref,
             dq_ref, dkc_ref, dkp_ref, dvc_ref, dvp_ref, db_ref, ds_ref):
        kh, n = pl.program_id(0), pl.program_id(1)
        lo = lax.broadcasted_iota(jnp.int32, (WINDOW, LANES), 1) < HEAD_DIM
        kk = jnp.concatenate([kp_ref[...], kc_ref[...]], axis=0)
        vv = jnp.concatenate([vp_ref[...], vc_ref[...]], axis=0)

        @pl.when(n == 0)
        def _():
            db_ref[...] = jnp.zeros_like(db_ref)
            ds_ref[...] = jnp.zeros_like(ds_ref)

        qs_ = _stack_heads(q_ref, lo)
        dos = _stack_heads(do_ref, lo)
        pn, ps = _attn_probs(qs_, kk, b_ref, s_ref, kh, n > 0)
        dp = lax.dot_general(dos, vv, _DN["nt"], preferred_element_type=F32)
        delta = jnp.sum(pn * dp, axis=-1, keepdims=True)
        ds = pn * (dp - delta)
        db_ref[...] += ds.reshape(GQA_GROUP, WINDOW, 2 * WINDOW)
        ds_ref[...] += jnp.zeros((GQA_GROUP, LANES), F32) - jnp.sum((ps * delta).reshape(GQA_GROUP, WINDOW, 1), axis=1)
        dsb = ds.astype(BF16)
        dq = jnp.dot(dsb, kk, preferred_element_type=F32)
        for p, t in enumerate(_unstack_heads(dq, lo)):
            dq_ref[:, p * LANES:(p + 1) * LANES] = t
        dk = lax.dot_general(dsb, qs_, _DN["tn"], preferred_element_type=F32)
        dv = lax.dot_general(pn.astype(BF16), dos, _DN["tn"], preferred_element_type=F32)
        dkp_ref[...] = dk[:WINDOW]
        dkc_ref[...] = dk[WINDOW:]
        dvp_ref[...] = dv[:WINDOW]
        dvc_ref[...] = dv[WINDOW:]

    part = jax.ShapeDtypeStruct((n_kv, S, LANES), F32)
    return pl.pallas_call(
        body, name="attn_bwd", grid=(n_kv, S // WINDOW),
        in_specs=[qs, cur, prev, cur, prev, bias, sink, qs],
        out_specs=[qs, cur, cur, cur, cur, bias, pl.BlockSpec((None, 8, LANES), lambda kh, n: (kh, 0, 0))],
        out_shape=[jax.ShapeDtypeStruct((S, qw), F32), part, part, part, part,
                   jax.ShapeDtypeStruct((H, WINDOW, 2 * WINDOW), F32),
                   jax.ShapeDtypeStruct((n_kv, 8, LANES), F32)],
        compiler_params=_cparams(("parallel", "arbitrary")),
    )(q, kcat, kcat, vcat, vcat, biasm, sinks, do)


def _adamw(name, w, g, m, v):
    shape = w.shape
    C = shape[-1]
    R = int(np.prod(shape[:-1]))
    tr = R
    if R * C * 4 > (1 << 20):
        tr = _tile(R, tuple(t for t in (512, 256, 128, 64, 32, 16, 8) if t * C * 4 <= (3 << 19)))
    c1 = 1.0 - ADAM_B1 ** ADAM_STEP
    c2 = 1.0 - ADAM_B2 ** ADAM_STEP

    def body(w_ref, g_ref, m_ref, v_ref, d_ref, nm_ref, nv_ref):
        gv = g_ref[...]
        nm = ADAM_B1 * m_ref[...] + (1.0 - ADAM_B1) * gv
        nv = ADAM_B2 * v_ref[...] + (1.0 - ADAM_B2) * (gv * gv)
        d_ref[...] = -ADAM_LR * ((nm / c1) / (jnp.sqrt(nv / c2) + ADAM_EPS) + ADAM_WD * w_ref[...])
        nm_ref[...] = nm
        nv_ref[...] = nv

    spec = pl.BlockSpec((tr, C), lambda i: (i, 0))
    outs = pl.pallas_call(
        body, name=name, grid=(R // tr,),
        in_specs=[spec] * 4, out_specs=[spec] * 3,
        out_shape=[jax.ShapeDtypeStruct((R, C), F32)] * 3,
        compiler_params=_cparams(("parallel",)),
    )(*[t.reshape(R, C) for t in (w, g, m, v)])
    return [o.reshape(shape) for o in outs]


def _place():
    return lax.axis_index("x"), lax.axis_index("y"), lax.axis_index("c")


def _other_chips(x, y):
    return [(1 - x, y), (x, 1 - y), (1 - x, 1 - y)]


def _gather_chips(name, shards):
    T = len(shards)

    def body(*refs):
        ins, outs = refs[:T], refs[T:2 * T]
        send, recv, local = refs[2 * T:]
        x, y, c = _place()
        chips = _other_chips(x, y)
        me = 2 * x + y
        mine = [pltpu.make_async_copy(ins[t], outs[t].at[me], local.at[t]) for t in range(T)]
        for cp in mine:
            cp.start()

        def copy(t, j, block):
            return pltpu.make_async_remote_copy(
                src_ref=ins[t], dst_ref=outs[t].at[block], send_sem=send.at[t, j], recv_sem=recv.at[t, j],
                device_id=(chips[j][0], chips[j][1], c), device_id_type=MESH)

        for t in range(T):
            for j in range(3):
                copy(t, j, me).start()
        for t in range(T):
            for j in range(3):
                copy(t, j, 2 * chips[j][0] + chips[j][1]).wait_recv()
        for t in range(T):
            for j in range(3):
                copy(t, j, me).wait_send()
        for cp in mine:
            cp.wait()

    return pl.pallas_call(
        body, name=name,
        in_specs=[_ANY] * T, out_specs=[_ANY] * T,
        out_shape=[jax.ShapeDtypeStruct((N_CHIPS,) + s.shape, s.dtype) for s in shards],
        scratch_shapes=[pltpu.SemaphoreType.DMA((T, 3)), pltpu.SemaphoreType.DMA((T, 3)),
                        pltpu.SemaphoreType.DMA((T,))],
        compiler_params=pltpu.CompilerParams(has_side_effects=True),
    )(*shards)


def _swap_halves(name, grads):
    T = len(grads)

    def body(*refs):
        ins, outs = refs[:T], refs[T:2 * T]
        send, recv = refs[2 * T:]
        x, y, c = _place()

        def copy(t):
            half = grads[t].shape[1] // 2
            src = ins[t].at[:, pl.ds(pl.multiple_of((1 - c) * half, 16), half), :]
            return pltpu.make_async_remote_copy(
                src_ref=src, dst_ref=outs[t], send_sem=send.at[t], recv_sem=recv.at[t],
                device_id=(x, y, 1 - c), device_id_type=MESH)

        for t in range(T):
            copy(t).start()
        for t in range(T):
            copy(t).wait()

    return pl.pallas_call(
        body, name=name,
        in_specs=[_ANY] * T, out_specs=[_ANY] * T,
        out_shape=[jax.ShapeDtypeStruct((g.shape[0], g.shape[1] // 2, g.shape[2]), g.dtype) for g in grads],
        scratch_shapes=[pltpu.SemaphoreType.DMA((T,)), pltpu.SemaphoreType.DMA((T,))],
        compiler_params=pltpu.CompilerParams(has_side_effects=True),
    )(*grads)


def _add_half(name, g, other, c_arr):
    nb, R, C = g.shape
    half = R // 2
    tr = _tile(half, (256, 128, 64, 32, 16))
    n = half // tr

    def body(c_ref, g_ref, o_ref, s_ref):
        s_ref[...] = (g_ref[...].astype(F32) + o_ref[...].astype(F32)).astype(BF16)

    return pl.pallas_call(
        body, name=name,
        grid_spec=pltpu.PrefetchScalarGridSpec(
            num_scalar_prefetch=1, grid=(nb, n),
            in_specs=[pl.BlockSpec((None, tr, C), lambda b, i, c: (b, c[0] * n + i, 0)),
                      pl.BlockSpec((None, tr, C), lambda b, i, c: (b, i, 0))],
            out_specs=pl.BlockSpec((None, tr, C), lambda b, i, c: (b, i, 0))),
        out_shape=jax.ShapeDtypeStruct((nb, half, C), BF16),
        compiler_params=_cparams(("parallel", "parallel")),
    )(c_arr, g, other)


def _in_hbm(arrays):
    return [pltpu.with_memory_space_constraint(a, pltpu.HBM) for a in arrays]


def _chip_copy(src, dst, send, recv, chip, c):
    return pltpu.make_async_remote_copy(src_ref=src, dst_ref=dst, send_sem=send, recv_sem=recv,
                                        device_id=(chip[0], chip[1], c), device_id_type=MESH)


def _cast_slot(name, w, l, chip_arr):
    _, R, C = w.shape
    tr = _tile(R, tuple(t for t in (1024, 512, 256, 128, 64, 32, 16) if t * C * 4 <= (1 << 21)))

    def body(chip_ref, w_ref, o_ref):
        o_ref[...] = w_ref[...].astype(BF16)

    return pl.pallas_call(
        body, name=name,
        grid_spec=pltpu.PrefetchScalarGridSpec(
            num_scalar_prefetch=1, grid=(R // tr,),
            in_specs=[pl.BlockSpec((None, tr, C), lambda i, chip: (l, i, 0))],
            out_specs=pl.BlockSpec((None, tr, C), lambda i, chip: (chip[0], i, 0))),
        out_shape=jax.ShapeDtypeStruct((N_CHIPS, R, C), BF16),
        compiler_params=_cparams(("parallel",)),
    )(chip_arr, w)


def _gather_start(groups):
    sizes = [len(g) for g in groups]
    flat = [b for g in groups for b in g]
    n, G = len(flat), len(groups)

    def body(*refs):
        bufs = refs[:n]
        sems = refs[n:n + 2 * G]
        token = refs[-1]
        x, y, c = _place()
        chips = _other_chips(x, y)
        me = 2 * x + y
        k = 0
        for l in range(G):
            for t in range(sizes[l]):
                for j in range(3):
                    _chip_copy(bufs[k].at[me], bufs[k].at[me], sems[2 * l].at[3 * t + j], sems[2 * l + 1].at[3 * t + j],
                               chips[j], c).start()
                k += 1
        token[...] = jnp.zeros_like(token)

    sem_shapes = []
    for s in sizes:
        sem_shapes += [pltpu.SemaphoreType.DMA((3 * s,)), pltpu.SemaphoreType.DMA((3 * s,))]
    outs = pl.pallas_call(
        body, name="gather_start",
        out_shape=(*sem_shapes, *[pltpu.HBM(b.shape, b.dtype) for b in flat], jax.ShapeDtypeStruct((8, LANES), F32)),
        in_specs=[_HBM] * n,
        out_specs=(*[_SEM] * (2 * G), *[_HBM] * n, pl.BlockSpec(memory_space=pltpu.VMEM)),
        input_output_aliases={t: 2 * G + t for t in range(n)},
        compiler_params=pltpu.CompilerParams(has_side_effects=_DATAFLOW),
    )(*_in_hbm(flat))
    res, k = [], 2 * G
    for l in range(G):
        res.append((outs[2 * l], outs[2 * l + 1], list(outs[k:k + sizes[l]])))
        k += sizes[l]
    return res, outs[-1]


def _gather_wait(name, send, recv, bufs, after):
    T = len(bufs)

    def body(*refs):
        ins = refs[:T]
        send_ref, recv_ref = refs[T], refs[T + 1]
        x, y, c = _place()
        chips = _other_chips(x, y)
        me = 2 * x + y
        for t in range(T):
            for j in range(3):
                cp = _chip_copy(ins[t].at[me], ins[t].at[2 * chips[j][0] + chips[j][1]], send_ref.at[3 * t + j],
                                recv_ref.at[3 * t + j], chips[j], c)
                cp.wait_send()
                cp.wait_recv()

    return pl.pallas_call(
        body, name=name,
        out_shape=[pltpu.HBM(b.shape, b.dtype) for b in bufs],
        in_specs=[_HBM] * T + [_SEM, _SEM, _ANY],
        out_specs=[_HBM] * T,
        input_output_aliases={t: t for t in range(T)},
        compiler_params=pltpu.CompilerParams(has_side_effects=_DATAFLOW),
    )(*bufs, send, recv, after)


def _scatter_start(name, sums):
    T = len(sums)
    lands = [lax.empty((3,) + s.shape[1:], s.dtype) for s in sums]

    def body(*refs):
        srcs, lnds = refs[:T], refs[T:2 * T]
        send, recv = refs[2 * T], refs[2 * T + 1]
        token = refs[-1]
        x, y, c = _place()
        chips = _other_chips(x, y)
        for t in range(T):
            for j in range(3):
                _chip_copy(srcs[t].at[2 * chips[j][0] + chips[j][1]], lnds[t].at[j], send.at[3 * t + j],
                           recv.at[3 * t + j], chips[j], c).start()
        token[...] = jnp.zeros_like(token)

    both = list(sums) + lands
    outs = pl.pallas_call(
        body, name=name,
        out_shape=(pltpu.SemaphoreType.DMA((3 * T,)), pltpu.SemaphoreType.DMA((3 * T,)),
                   *[pltpu.HBM(b.shape, b.dtype) for b in both], jax.ShapeDtypeStruct((8, LANES), F32)),
        in_specs=[_HBM] * (2 * T),
        out_specs=(_SEM, _SEM, *[_HBM] * (2 * T), pl.BlockSpec(memory_space=pltpu.VMEM)),
        input_output_aliases={t: 2 + t for t in range(2 * T)},
        compiler_params=pltpu.CompilerParams(has_side_effects=_DATAFLOW),
    )(*_in_hbm(both))
    return outs[0], outs[1], list(outs[2:2 + T]), list(outs[2 + T:2 + 2 * T]), outs[-1]


def _scatter_wait(name, send, recv, sums, lands, after):
    T = len(sums)

    def body(*refs):
        srcs, lnds = refs[:T], refs[T:2 * T]
        send_ref, recv_ref = refs[2 * T], refs[2 * T + 1]
        x, y, c = _place()
        chips = _other_chips(x, y)
        for t in range(T):
            for j in range(3):
                cp = _chip_copy(srcs[t].at[2 * chips[j][0] + chips[j][1]], lnds[t].at[j], send_ref.at[3 * t + j],
                                recv_ref.at[3 * t + j], chips[j], c)
                cp.wait_send()
                cp.wait_recv()

    both = list(sums) + list(lands)
    outs = pl.pallas_call(
        body, name=name,
        out_shape=[pltpu.HBM(b.shape, b.dtype) for b in both],
        in_specs=[_HBM] * (2 * T) + [_SEM, _SEM, _ANY],
        out_specs=[_HBM] * (2 * T),
        input_output_aliases={t: t for t in range(2 * T)},
        compiler_params=pltpu.CompilerParams(has_side_effects=_DATAFLOW),
    )(*both, send, recv, after)
    return list(outs[:T]), list(outs[T:])


def _sum_parts(name, sums, land, chip_arr, c_arr):
    _, R2, C = sums.shape
    tr = _tile(R2, (256, 128, 64, 32, 16))

    def body(chip_ref, c_ref, s_ref, l_ref, o_ref):
        acc = s_ref[...].astype(F32)
        for j in range(3):
            acc = acc + l_ref[j].astype(F32)
        o_ref[...] = acc

    return pl.pallas_call(
        body, name=name,
        grid_spec=pltpu.PrefetchScalarGridSpec(
            num_scalar_prefetch=2, grid=(R2 // tr,),
            in_specs=[pl.BlockSpec((None, tr, C), lambda i, chip, c: (chip[0], i, 0)),
                      pl.BlockSpec((3, tr, C), lambda i, chip, c: (0, i, 0))],
            out_specs=pl.BlockSpec((None, tr, C), lambda i, chip, c: (c[0], i, 0))),
        out_shape=jax.ShapeDtypeStruct((2, R2, C), F32),
        compiler_params=_cparams(("parallel",)),
    )(chip_arr, c_arr, sums, land)


def _join_halves(name, bufs):
    T = len(bufs)

    def body(*refs):
        outs = refs[T:2 * T]
        send, recv = refs[2 * T:]
        x, y, c = _place()

        def copy(t, half):
            return pltpu.make_async_remote_copy(
                src_ref=outs[t].at[c], dst_ref=outs[t].at[half], send_sem=send.at[t], recv_sem=recv.at[t],
                device_id=(x, y, 1 - c), device_id_type=MESH)

        for t in range(T):
            copy(t, c).start()
        for t in range(T):
            copy(t, 1 - c).wait_recv()
        for t in range(T):
            copy(t, c).wait_send()

    return pl.pallas_call(
        body, name=name,
        in_specs=[_ANY] * T, out_specs=[_ANY] * T,
        out_shape=[jax.ShapeDtypeStruct(b.shape, b.dtype) for b in bufs],
        input_output_aliases={t: t for t in range(T)},
        scratch_shapes=[pltpu.SemaphoreType.DMA((T,)), pltpu.SemaphoreType.DMA((T,))],
        compiler_params=pltpu.CompilerParams(has_side_effects=True),
    )(*bufs)


def _rs_begin(tag, grads, c_arr):
    got = _swap_halves("rs_swap_" + tag, grads)
    sums = [_add_half("rs_add_%s_%d" % (tag, t), g, o, c_arr) for t, (g, o) in enumerate(zip(grads, got))]
    return _scatter_start("rs_scatter_start_" + tag, sums)


def _rs_finish(tag, state, shapes, chip_arr, c_arr, after=None):
    send, recv, sums, lands, token = state
    sums, lands = _scatter_wait("rs_scatter_wait_" + tag, send, recv, sums, lands, token if after is None else after)
    parts = [_sum_parts("rs_sum_%s_%d" % (tag, t), s, l, chip_arr, c_arr) for t, (s, l) in enumerate(zip(sums, lands))]
    joined = _join_halves("rs_join_" + tag, parts)
    return [j.reshape(sh[1], sh[2]) for j, sh in zip(joined, shapes)]


def _all_reduce_small(v):
    R = v.shape[0]

    def body(v_ref, o_ref, buf, send, recv):
        x, y, c = _place()
        me = 4 * x + 2 * y + c
        buf[me] = v_ref[...]

        def copy(k, slot):
            to = (x ^ ((k >> 2) & 1), y ^ ((k >> 1) & 1), c ^ (k & 1))
            return pltpu.make_async_remote_copy(
                src_ref=v_ref, dst_ref=buf.at[slot], send_sem=send.at[k - 1], recv_sem=recv.at[k - 1],
                device_id=to, device_id_type=MESH)

        for k in range(1, 8):
            copy(k, me).start()
        for k in range(1, 8):
            copy(k, me ^ k).wait_recv()
        for k in range(1, 8):
            copy(k, me).wait_send()
        acc = buf[0]
        for s in range(1, 8):
            acc = acc + buf[s]
        o_ref[...] = acc

    vm = pl.BlockSpec(memory_space=pltpu.VMEM)
    return pl.pallas_call(
        body, name="all_reduce_small",
        in_specs=[vm], out_specs=vm,
        out_shape=jax.ShapeDtypeStruct((R, LANES), F32),
        scratch_shapes=[pltpu.VMEM((8, R, LANES), F32), pltpu.SemaphoreType.DMA((7,)), pltpu.SemaphoreType.DMA((7,))],
        compiler_params=pltpu.CompilerParams(has_side_effects=True, vmem_limit_bytes=VMEM_LIMIT),
    )(v)


def _local_step(x, tgt, small, depth, nb, weights_of, convw, pscale, on_grads):
    S, D = x.shape
    n_q = D // HEAD_DIM
    n_kv = n_q // GQA_GROUP
    Fh = convw[0].shape[2]

    def dup(gain):
        return jnp.tile(gain, 2).reshape(1, LANES)

    biasm = _bias_expand(small["rel_bias"])
    saved = []
    for i in range(depth):
        j = i // 2
        w = weights_of(i, x)
        st = {"x0": x, "w": w}
        gm = small["norm_mix"][i].reshape(1, D)
        if i % 2 == 0:
            h = _rms_fwd("rms_mix_fwd", x, gm)
            qkv = _mm_colblk("qkv_fwd", h, w["wqkv"], F32)
            qh, kcat, vcat = _qk_norm_fwd(qkv, dup(small["attn_q_gain"][j]), dup(small["attn_k_gain"][j]), n_q, n_kv)
            o = _attn_fwd(qh, kcat, vcat, biasm, small["attn_sinks"][j])
            x = _mm_rowblk_res("wo_fwd", o, w["wo"], x)
            st.update(h=h, qkv=qkv, qh=qh, kcat=kcat, vcat=vcat, o=o)
        else:
            d = _pool_fwd(x, gm)
            ypre, x = _pool_mm_fwd(d, w["wp"], x, pscale[j])
            st.update(d=d, ypre=ypre)
        st["x1"] = x
        h2 = _rms_fwd("rms_ffn_fwd", x, small["norm_ffn"][i].reshape(1, D))
        u = _mm_colblk("up_fwd", h2, w["wup"], F32, split2=True)
        cb = small["ffn_conv_b"][i].reshape(2, 1, Fh)
        a = _ffn_act_fwd(u, convw[i], cb)
        x = _mm_rowblk_res("down_fwd", a, w["wdn"], x)
        st.update(h2=h2, u=u, a=a, cb=cb)
        saved.append(st)

    dx, dxb, loss_part = _loss_head(x, tgt)

    big = [dict() for _ in range(depth)]
    sg = {k: [None] * depth for k in ("norm_mix", "norm_ffn", "conv_w", "conv_b")}
    sg.update({k: [None] * (depth // 2) for k in ("q_gain", "k_gain", "sinks", "pool_scale")})
    dbias_tot = None
    for i in reversed(range(depth)):
        j = i // 2
        st = saved[i]
        w = st["w"]
        da = _mm_nt_rowblk("down_bwd_in", dxb, w["wdn"], F32)
        big[i]["wdn"] = _mm_tn_rowblk("down_bwd_w", st["a"], dxb, nb)
        duc, dcwb = _ffn_act_bwd(st["u"], convw[i], st["cb"], da)
        du = _ffn_conv_bwd(duc, convw[i])
        sg["conv_w"][i] = jnp.transpose(dcwb[:, 0:3, :], (1, 0, 2)).reshape(3, 2 * Fh)
        sg["conv_b"][i] = dcwb[:, 3, :].reshape(2 * Fh)
        big[i]["wup"] = _mm_tn_colblk("up_bwd_w", st["h2"], du, nb, split2=True)
        dh2 = _mm_nt_colblk("up_bwd_in", du, w["wup"], split2=True)
        dx, dxb, dg = _rms_bwd("rms_ffn_bwd", st["x1"], small["norm_ffn"][i].reshape(1, D), dh2, dx)
        sg["norm_ffn"][i] = dg.reshape(D)
        gm = small["norm_mix"][i].reshape(1, D)
        if i % 2 == 0:
            do = _mm_nt_rowblk("wo_bwd_in", dxb, w["wo"], BF16)
            big[i]["wo"] = _mm_tn_rowblk("wo_bwd_w", st["o"], dxb, nb)
            dq, dkc, dkp, dvc, dvp, dbias, dsink = _attn_bwd(
                st["qh"], st["kcat"], st["vcat"], biasm, small["attn_sinks"][j], do)
            dbias_tot = dbias if dbias_tot is None else dbias_tot + dbias
            sg["sinks"][j] = dsink[:, :, 0].reshape(n_q)
            dqkv, dqg, dkg = _qk_norm_bwd(st["qkv"], dup(small["attn_q_gain"][j]), dup(small["attn_k_gain"][j]),
                                          dq, dkc, dkp, dvc, dvp, n_q, n_kv)
            sg["q_gain"][j] = dqg[0, :HEAD_DIM]
            sg["k_gain"][j] = dkg[0, :HEAD_DIM]
            big[i]["wqkv"] = _mm_tn_colblk("qkv_bwd_w", st["h"], dqkv, nb)
            dh = _mm_nt_colblk("qkv_bwd_in", dqkv, w["wqkv"])
            dx, dxb, dg = _rms_bwd("rms_mix_bwd", st["x0"], gm, dh, dx)
        else:
            dyp, dsc = _pool_bwd_pre(dx, st["ypre"], pscale[j])
            sg["pool_scale"][j] = dsc.reshape(D)
            big[i]["wp"] = _pool_mm_bwd_w(st["d"], dyp, nb)
            dd = _pool_mm_bwd_in(dyp, w["wp"])
            dx, dxb, dg = _pool_bwd_post(dd, st["x0"], gm, dx)
        sg["norm_mix"][i] = dg.reshape(D)
        on_grads(i, big[i], dx)
    sg["rel_bias"] = _bias_reduce(dbias_tot)[:, 0, :N_BUCKETS]
    return loss_part, dx, sg


_SMALL_ORDER = ("norm_mix", "norm_ffn", "rel_bias", "q_gain", "k_gain", "sinks", "conv_b", "conv_w", "pool_scale")


def kernel(x, norm_mix, norm_ffn, rel_bias, attn_w_qkv, attn_q_gain, attn_k_gain, attn_sinks, attn_w_o, pool_w, pool_scale, ffn_w_up, ffn_conv_w, ffn_conv_b, ffn_w_down, loss_target, m_norm_mix, m_norm_ffn, m_rel_bias, m_attn_w_qkv, m_attn_q_gain, m_attn_k_gain, m_attn_sinks, m_attn_w_o, m_pool_w, m_pool_scale, m_ffn_w_up, m_ffn_conv_w, m_ffn_conv_b, m_ffn_w_down, v_norm_mix, v_norm_ffn, v_rel_bias, v_attn_w_qkv, v_attn_q_gain, v_attn_k_gain, v_attn_sinks, v_attn_w_o, v_pool_w, v_pool_scale, v_ffn_w_up, v_ffn_conv_w, v_ffn_conv_b, v_ffn_w_down):
    _, S, D = x.shape
    depth = ffn_w_up.shape[0]
    n_attn, n_pool = attn_w_qkv.shape[0], pool_w.shape[0]
    nb = N_CHIPS
    nc_up = ffn_w_up.shape[2]
    Fh = nc_up * nb // 2
    cx, cy, cc = _place()
    chip = 2 * cx + cy
    c_arr = jnp.reshape(cc, (1,)).astype(jnp.int32)
    chip_arr = jnp.reshape(chip, (1,)).astype(jnp.int32)

    pool_w3 = pool_w.reshape(n_pool, -1, pool_w.shape[-1])
    groups = []
    for i in range(depth):
        j = i // 2
        mixer = ([("wqkv", attn_w_qkv, j), ("wo", attn_w_o, j)] if i % 2 == 0 else [("wp", pool_w3, j)])
        groups.append([(k, _cast_slot("cast_%s" % k, w, l, chip_arr))
                       for k, w, l in mixer + [("wup", ffn_w_up, i), ("wdn", ffn_w_down, i)]])
    started, token = _gather_start([[b for _, b in g] for g in groups])
    convw_g, pscale_g = _gather_chips("gather_small", [ffn_conv_w, pool_scale])
    convw = [jnp.transpose(convw_g[:, i], (1, 0, 2)).reshape(3, 2, Fh).transpose(1, 0, 2) for i in range(depth)]
    pscale = [pscale_g[:, j].reshape(1, D) for j in range(n_pool)]
    small = dict(norm_mix=norm_mix, norm_ffn=norm_ffn, rel_bias=rel_bias, attn_q_gain=attn_q_gain,
                 attn_k_gain=attn_k_gain, attn_sinks=attn_sinks, ffn_conv_b=ffn_conv_b)

    def weights_of(i, after):
        send, recv, bufs = started[i]
        done = _gather_wait("gather_wait_%d" % i, send, recv, bufs, token if i == 0 else after)
        w = {k: b for (k, _), b in zip(groups[i], done)}
        if "wp" in w:
            w["wp"] = w["wp"].reshape((nb,) + pool_w.shape[1:])
        return w

    red = {k: [None] * (n_attn if k in ("wqkv", "wo") else n_pool if k == "wp" else depth)
           for k in ("wqkv", "wo", "wp", "wup", "wdn")}
    pending = []

    def finish(after):
        i, names, shapes, state = pending.pop()
        outs = _rs_finish("%d" % i, state, shapes, chip_arr, c_arr, after)
        for k, o in zip(names, outs):
            red[k][i // 2 if k in ("wqkv", "wo", "wp") else i] = o

    def on_grads(i, big_i, dx_i):
        if pending:
            finish(dx_i)
        names = sorted(big_i)
        flat = [big_i[k].reshape(nb, -1, big_i[k].shape[-1]) for k in names]
        pending.append((i, names, [f.shape for f in flat], _rs_begin("%d" % i, flat, c_arr)))

    loss_part, dx, sg = _local_step(x[0], loss_target[0], small, depth, nb, weights_of, convw, pscale, on_grads)
    finish(None)
    loss = lax.psum(jnp.sum(loss_part), ("x", "y", "c"))

    g_wqkv = jnp.stack(red["wqkv"])
    g_wo = jnp.stack(red["wo"])
    g_wp = jnp.stack(red["wp"]).reshape(pool_w.shape)
    g_wup = jnp.stack(red["wup"])
    g_wdn = jnp.stack(red["wdn"])

    parts = [jnp.stack(sg[k]) if isinstance(sg[k], list) else sg[k] for k in _SMALL_ORDER]
    sizes = [int(np.prod(p.shape)) for p in parts]
    total = sum(sizes)
    rows = -(-total // (8 * LANES)) * 8
    packed = jnp.concatenate([p.reshape(-1) for p in parts] + [jnp.zeros((rows * LANES - total,), F32)])
    summed = _all_reduce_small(packed.reshape(rows, LANES)).reshape(-1)
    sm, off = {}, 0
    for k, p, n in zip(_SMALL_ORDER, parts, sizes):
        sm[k] = summed[off:off + n].reshape(p.shape)
        off += n
    g_convw = lax.dynamic_slice_in_dim(sm["conv_w"], chip * nc_up, nc_up, axis=2)
    pc = pool_scale.shape[1]
    g_pscale = lax.dynamic_slice_in_dim(sm["pool_scale"], chip * pc, pc, axis=1)

    grads = [sm["norm_mix"], sm["norm_ffn"], sm["rel_bias"], g_wqkv, sm["q_gain"], sm["k_gain"], sm["sinks"], g_wo,
             g_wp, g_pscale, g_wup, g_convw, sm["conv_b"], g_wdn]
    ws = [norm_mix, norm_ffn, rel_bias, attn_w_qkv, attn_q_gain, attn_k_gain, attn_sinks, attn_w_o, pool_w, pool_scale,
          ffn_w_up, ffn_conv_w, ffn_conv_b, ffn_w_down]
    ms = [m_norm_mix, m_norm_ffn, m_rel_bias, m_attn_w_qkv, m_attn_q_gain, m_attn_k_gain, m_attn_sinks, m_attn_w_o,
          m_pool_w, m_pool_scale, m_ffn_w_up, m_ffn_conv_w, m_ffn_conv_b, m_ffn_w_down]
    vs = [v_norm_mix, v_norm_ffn, v_rel_bias, v_attn_w_qkv, v_attn_q_gain, v_attn_k_gain, v_attn_sinks, v_attn_w_o,
          v_pool_w, v_pool_scale, v_ffn_w_up, v_ffn_conv_w, v_ffn_conv_b, v_ffn_w_down]
    deltas, new_m, new_v = [], [], []
    for idx, (w, g, m, v) in enumerate(zip(ws, grads, ms, vs)):
        d, nm, nv = _adamw("adamw_%d" % idx, w, g, m, v)
        deltas.append(d), new_m.append(nm), new_v.append(nv)
    return (loss, dx.reshape(1, S, D), *grads, *deltas, *new_m, *new_v)
```

```python
import functools

import numpy as np
import jax
import jax.numpy as jnp
from jax import lax
from jax.experimental import pallas as pl
from jax.experimental.pallas import tpu as pltpu

F32 = jnp.float32
BF16 = jnp.bfloat16
MESH = pl.DeviceIdType.MESH
_ANY = pl.BlockSpec(memory_space=pl.ANY)
_HBM = pl.BlockSpec(memory_space=pltpu.HBM)
_SEM = pl.BlockSpec(memory_space=pltpu.SEMAPHORE)
_DATAFLOW = pltpu.SideEffectType.DATAFLOW_SIDE_EFFECTING

N_CHIPS = 4
HEAD_DIM = 64
GQA_GROUP = 8
WINDOW = 128
N_BUCKETS = 32
MAX_DISTANCE = 128
POOL_WINDOWS = (2, 4, 8, 16)
POOL_HALO = 16
EPS = 1e-6
NEG_INF = -1e30
LANES = 128
VMEM_LIMIT = 56 * 1024 * 1024

ADAM_LR = 0.001
ADAM_B1 = 0.9
ADAM_B2 = 0.999
ADAM_EPS = 1e-08
ADAM_WD = 0.01
ADAM_STEP = 10


def _tile(n, prefs):
    for p in prefs:
        if p <= n and n % p == 0:
            return p
    return n


def _cparams(sem):
    return pltpu.CompilerParams(dimension_semantics=sem, vmem_limit_bytes=VMEM_LIMIT)


_DN = {
    "nn": (((1,), (0,)), ((), ())),
    "nt": (((1,), (1,)), ((), ())),
    "tn": (((0,), (0,)), ((), ())),
}


def _mm(name, kind, a, b, grid, a_spec, b_spec, outs, acc_shape, extras=(), epilogue=None, after=None):
    nk = grid[2]
    n_ex, n_out = len(extras), len(outs)
    order = [] if after is None else [after]

    def body(a_ref, b_ref, *rest):
        ex_refs = rest[:n_ex]
        rest = rest[n_ex + len(order):]
        out_refs = rest[:n_out]
        acc_ref = rest[n_out] if nk > 1 else None
        part = lax.dot_general(a_ref[...], b_ref[...], _DN[kind], preferred_element_type=F32)

        def finish(val):
            vals = epilogue(val, *[r[...] for r in ex_refs]) if epilogue else (val,)
            for r, v in zip(out_refs, vals):
                r[...] = v.astype(r.dtype)

        if nk == 1:
            finish(part)
        else:
            k = pl.program_id(2)

            @pl.when(k == 0)
            def _():
                acc_ref[...] = part

            @pl.when(k > 0)
            def _():
                acc_ref[...] += part

            @pl.when(k == nk - 1)
            def _():
                finish(acc_ref[...])

    res = pl.pallas_call(
        body,
        name=name,
        grid=grid,
        in_specs=[a_spec, b_spec] + [s for _, s in extras] + [_ANY] * len(order),
        out_specs=[s for _, _, s in outs],
        out_shape=[jax.ShapeDtypeStruct(sh, dt) for sh, dt, _ in outs],
        scratch_shapes=[pltpu.VMEM(acc_shape, F32)] if nk > 1 else [],
        compiler_params=_cparams(("parallel", "parallel", "arbitrary")),
    )(a, b, *[e for e, _ in extras], *order)
    return res if n_out > 1 else res[0]


def _mm_colblk(name, a, wg, out_dtype):
    S, K = a.shape
    nb, _, nc = wg.shape
    tm = _tile(S, (512,))
    tn = _tile(nc, (1408, 1024, 640, 512, 256, 128))
    npb = nc // tn
    return _mm(
        name, "nn", a, wg, (nb * npb, S // tm, 1),
        pl.BlockSpec((tm, K), lambda p, q, k: (q, 0)),
        pl.BlockSpec((None, K, tn), lambda p, q, k: (p // npb, 0, p % npb)),
        [((S, nb * nc), out_dtype, pl.BlockSpec((tm, tn), lambda p, q, k: (q, p)))], (tm, tn))


def _mm_rowblk_res(name, a, wg, res):
    S = a.shape[0]
    nb, kc, N = wg.shape
    tm = _tile(S, (512,))
    tn = _tile(N, (2048, 1024, 512))
    tk = _tile(kc, (1408, 512, 256, 128))
    kpb = kc // tk
    o_spec = pl.BlockSpec((tm, tn), lambda p, q, k: (p, q))
    return _mm(
        name, "nn", a, wg, (S // tm, N // tn, nb * kpb),
        pl.BlockSpec((tm, tk), lambda p, q, k: (p, k)),
        pl.BlockSpec((None, tk, tn), lambda p, q, k: (k // kpb, k % kpb, q)),
        [((S, N), F32, o_spec)], (tm, tn),
        extras=[(res, o_spec)], epilogue=lambda acc, r: (r + acc,))


def _mm_nt_rowblk(name, g, wg, out_dtype, after=None):
    S, N = g.shape
    nb, kc, _ = wg.shape
    tm = _tile(S, (512,))
    tn = _tile(kc, (1408, 512, 256, 128))
    kpb = kc // tn
    return _mm(
        name, "nt", g, wg, (nb * kpb, S // tm, 1),
        pl.BlockSpec((tm, N), lambda p, q, k: (q, 0)),
        pl.BlockSpec((None, tn, N), lambda p, q, k: (p // kpb, p % kpb, 0)),
        [((S, nb * kc), out_dtype, pl.BlockSpec((tm, tn), lambda p, q, k: (q, p)))], (tm, tn), after=after)


def _mm_nt_colblk(name, g, wg):
    S = g.shape[0]
    nb, K, nc = wg.shape
    tm = _tile(S, (512,))
    tn = _tile(K, (2048, 1024))
    tk = _tile(nc, (1408, 640, 512, 256, 128))
    npb = nc // tk
    return _mm(
        name, "nt", g, wg, (S // tm, K // tn, nb * npb),
        pl.BlockSpec((tm, tk), lambda p, q, k: (p, k)),
        pl.BlockSpec((None, tn, tk), lambda p, q, k: (k // npb, q, k % npb)),
        [((S, K), F32, pl.BlockSpec((tm, tn), lambda p, q, k: (p, q)))], (tm, tn))


def _mm_tn_colblk(name, a, g, nb, split2=False):
    S, K = a.shape
    ntot = g.shape[-1] * (2 if split2 else 1)
    nc = ntot // nb
    ti = _tile(K, (1024,))
    tn = _tile(nc, (1408, 640, 512, 256, 128))
    ts = _tile(S, (1024, 512, 256))
    npb = nc // tn
    half = nb * npb // 2
    if split2:
        b_spec = pl.BlockSpec((None, ts, tn), lambda p, q, k: (q // half, k, q % half))
    else:
        b_spec = pl.BlockSpec((ts, tn), lambda p, q, k: (k, q))
    return _mm(
        name, "tn", a, g, (K // ti, nb * npb, S // ts),
        pl.BlockSpec((ts, ti), lambda p, q, k: (k, p)),
        b_spec,
        [((nb, K, nc), BF16, pl.BlockSpec((None, ti, tn), lambda p, q, k: (q // npb, p, q % npb)))], (ti, tn))


def _mm_tn_rowblk(name, a, g, nb):
    S, ktot = a.shape
    N = g.shape[1]
    kc = ktot // nb
    ti = _tile(kc, (1408, 512, 256, 128))
    tj = _tile(N, (1024,))
    ts = _tile(S, (1024, 512, 256))
    ipb = kc // ti
    return _mm(
        name, "tn", a, g, (nb * ipb, N // tj, S // ts),
        pl.BlockSpec((ts, ti), lambda p, q, k: (k, p)),
        pl.BlockSpec((ts, tj), lambda p, q, k: (k, q)),
        [((nb, kc, N), BF16, pl.BlockSpec((None, ti, tj), lambda p, q, k: (p // ipb, p % ipb, q)))], (ti, tj))


def _rms_fwd(name, x, gain):
    S, D = x.shape
    tm = _tile(S, (512,))

    def body(x_ref, g_ref, o_ref):
        xv = x_ref[...]
        r = lax.rsqrt(jnp.mean(xv * xv, axis=-1, keepdims=True) + EPS)
        o_ref[...] = (xv * r * g_ref[...]).astype(o_ref.dtype)

    return pl.pallas_call(
        body, name=name, grid=(S // tm,),
        in_specs=[pl.BlockSpec((tm, D), lambda i: (i, 0)), pl.BlockSpec((1, D), lambda i: (0, 0))],
        out_specs=pl.BlockSpec((tm, D), lambda i: (i, 0)),
        out_shape=jax.ShapeDtypeStruct((S, D), BF16),
        compiler_params=_cparams(("parallel",)),
    )(x, gain)


def _rms_bwd_math(xv, gain, dh):
    r = lax.rsqrt(jnp.mean(xv * xv, axis=-1, keepdims=True) + EPS)
    xhat = xv * r
    dxhat = dh * gain
    c = jnp.mean(dxhat * xhat, axis=-1, keepdims=True)
    return r * (dxhat - xhat * c), dh * xhat


def _rows_to_8(v):
    tm, C = v.shape
    return jnp.sum(v.reshape(tm // 8, 8, C), axis=0)


def _rms_bwd(name, x, gain, dh, dres):
    S, D = x.shape
    tm = _tile(S, (256,))
    n = S // tm

    def body(x_ref, g_ref, dh_ref, dr_ref, dx_ref, dxb_ref, dg_ref, acc_ref):
        i = pl.program_id(0)
        dxn, dgr = _rms_bwd_math(x_ref[...], g_ref[...], dh_ref[...])
        dx = dr_ref[...] + dxn
        dx_ref[...] = dx
        dxb_ref[...] = dx.astype(BF16)

        @pl.when(i == 0)
        def _():
            acc_ref[...] = jnp.zeros_like(acc_ref)

        acc_ref[...] += _rows_to_8(dgr)

        @pl.when(i == n - 1)
        def _():
            dg_ref[...] = jnp.sum(acc_ref[...], axis=0, keepdims=True)

    row = pl.BlockSpec((tm, D), lambda i: (i, 0))
    vec = pl.BlockSpec((1, D), lambda i: (0, 0))
    return pl.pallas_call(
        body, name=name, grid=(n,),
        in_specs=[row, vec, row, row],
        out_specs=[row, row, vec],
        out_shape=[jax.ShapeDtypeStruct((S, D), F32), jax.ShapeDtypeStruct((S, D), BF16),
                   jax.ShapeDtypeStruct((1, D), F32)],
        scratch_shapes=[pltpu.VMEM((8, D), F32)],
        compiler_params=_cparams(("arbitrary",)),
    )(x, gain, dh, dres)


def _loss_head(y, tgt):
    S, D = y.shape
    tm = _tile(S, (256,))
    n = S // tm

    def body(y_ref, t_ref, dy_ref, dyb_ref, l_ref):
        i = pl.program_id(0)
        e = y_ref[...] - t_ref[...]
        dy = e * (1.0 / D)
        dy_ref[...] = dy
        dyb_ref[...] = dy.astype(BF16)

        @pl.when(i == 0)
        def _():
            l_ref[...] = jnp.zeros_like(l_ref)

        sq = _rows_to_8(e * e)
        part = sq[:, 0:LANES]
        for t in range(1, D // LANES):
            part = part + sq[:, t * LANES:(t + 1) * LANES]
        l_ref[...] += part * (0.5 / D)

    row = pl.BlockSpec((tm, D), lambda i: (i, 0))
    return pl.pallas_call(
        body, name="loss_head", grid=(n,),
        in_specs=[row, row],
        out_specs=[row, row, pl.BlockSpec((8, LANES), lambda i: (0, 0))],
        out_shape=[jax.ShapeDtypeStruct((S, D), F32), jax.ShapeDtypeStruct((S, D), BF16),
                   jax.ShapeDtypeStruct((8, LANES), F32)],
        compiler_params=_cparams(("arbitrary",)),
    )(y, tgt)


def _shift_down(u, halo, k):
    row = lax.broadcasted_iota(jnp.int32, u.shape, 0)
    out = pltpu.roll(u, k, 0)
    for j in range(k):
        out = jnp.where(row == j, halo[8 - k + j:8 - k + j + 1, :], out)
    return out


def _shift_up(u, halo, k):
    tm = u.shape[0]
    row = lax.broadcasted_iota(jnp.int32, u.shape, 0)
    out = pltpu.roll(u, tm - k, 0)
    for j in range(k):
        out = jnp.where(row == tm - k + j, halo[j:j + 1, :], out)
    return out


def _conv_fwd(u, halo, cw, cb):
    um1 = _shift_down(u, halo, 1)
    um2 = _shift_down(u, halo, 2)
    return cw[0:1] * um2 + cw[1:2] * um1 + cw[2:3] * u + cb, um1, um2


def _up_act_fwd(h2, wg, cw, cb):
    S, K = h2.shape
    nb, _, nc = wg.shape
    Fh = nb * nc // 2
    tm = _tile(S, (1024, 512))
    tn = _tile(nc, (256, 128))
    npb = nc // tn
    hb = nb // 2

    def body(a_ref, bg_ref, bv_ref, cw_ref, cb_ref, u_ref, o_ref, halo_ref):
        @pl.when(pl.program_id(1) == 0)
        def _():
            halo_ref[...] = jnp.zeros_like(halo_ref)

        av = a_ref[...]
        ucs = []
        for s, b_ref in enumerate((bg_ref, bv_ref)):
            us = jnp.dot(av, b_ref[...], preferred_element_type=F32)
            u_ref[s] = us
            uc, _, _ = _conv_fwd(us, halo_ref[s], cw_ref[s], cb_ref[s])
            halo_ref[s] = us[tm - 8:, :]
            ucs.append(uc)
        gate, val = ucs
        sig = 1.0 / (1.0 + jnp.exp(-gate))
        o_ref[...] = (gate * sig * val).astype(BF16)

    return pl.pallas_call(
        body, name="up_act_fwd", grid=(Fh // tn, S // tm),
        in_specs=[pl.BlockSpec((tm, K), lambda j, i: (i, 0)),
                  pl.BlockSpec((None, K, tn), lambda j, i: (j // npb, 0, j % npb)),
                  pl.BlockSpec((None, K, tn), lambda j, i: (j // npb + hb, 0, j % npb)),
                  pl.BlockSpec((2, 3, tn), lambda j, i: (0, 0, j)),
                  pl.BlockSpec((2, 1, tn), lambda j, i: (0, 0, j))],
        out_specs=[pl.BlockSpec((2, tm, tn), lambda j, i: (0, i, j)), pl.BlockSpec((tm, tn), lambda j, i: (i, j))],
        out_shape=[jax.ShapeDtypeStruct((2, S, Fh), F32), jax.ShapeDtypeStruct((S, Fh), BF16)],
        scratch_shapes=[pltpu.VMEM((2, 8, tn), F32)],
        compiler_params=_cparams(("parallel", "arbitrary")),
    )(h2, wg, wg, cw, cb)


def _ffn_act_bwd(u, cw, cb, da):
    _, S, Fh = u.shape
    tm, tc = _tile(S, (512,)), _tile(Fh, (512, 256, 128))
    n8 = tm // 8
    blk = pl.BlockSpec((2, tm, tc), lambda j, i: (0, i, j))
    prev = pl.BlockSpec((2, 8, tc), lambda j, i: (0, jnp.maximum(i * n8 - 1, 0), j))
    cws = pl.BlockSpec((2, 3, tc), lambda j, i: (0, 0, j))
    cbs = pl.BlockSpec((2, 1, tc), lambda j, i: (0, 0, j))
    one = pl.BlockSpec((tm, tc), lambda j, i: (i, j))

    def body(u_ref, h_ref, cw_ref, cb_ref, da_ref, duc_ref, acc_ref):
        i = pl.program_id(1)
        keep = (i > 0).astype(F32)
        uc, um1, um2 = [], [], []
        for s in range(2):
            c, m1, m2 = _conv_fwd(u_ref[s], h_ref[s] * keep, cw_ref[s], cb_ref[s])
            uc.append(c), um1.append(m1), um2.append(m2)
        gate, val = uc
        dav = da_ref[...]
        sig = 1.0 / (1.0 + jnp.exp(-gate))
        dgate = dav * val * (sig * (1.0 + gate * (1.0 - sig)))
        dval = dav * (gate * sig)
        duc_ref[0] = dgate
        duc_ref[1] = dval

        @pl.when(i == 0)
        def _():
            acc_ref[...] = jnp.zeros_like(acc_ref)

        for s, d in enumerate((dgate, dval)):
            acc_ref[s, 0:1, :] += jnp.sum(d * um2[s], axis=0, keepdims=True)
            acc_ref[s, 1:2, :] += jnp.sum(d * um1[s], axis=0, keepdims=True)
            acc_ref[s, 2:3, :] += jnp.sum(d * u_ref[s], axis=0, keepdims=True)
            acc_ref[s, 3:4, :] += jnp.sum(d, axis=0, keepdims=True)

    return pl.pallas_call(
        body, name="ffn_act_bwd", grid=(Fh // tc, S // tm),
        in_specs=[blk, prev, cws, cbs, one],
        out_specs=[blk, pl.BlockSpec((2, 8, tc), lambda j, i: (0, 0, j))],
        out_shape=[jax.ShapeDtypeStruct((2, S, Fh), F32), jax.ShapeDtypeStruct((2, 8, Fh), F32)],
        compiler_params=_cparams(("parallel", "arbitrary")),
    )(u, u, cw, cb, da)


def _up_bwd_in(duc, cw, wg):
    _, S, Fh = duc.shape
    nb, K, nc = wg.shape
    tm = _tile(S, (512,))
    tk = _tile(nc, (1408, 640, 512, 256, 128))
    npb = nc // tk
    nk = nb * npb
    half = nk // 2
    n, n8 = S // tm, tm // 8

    def body(d_ref, h_ref, cw_ref, b_ref, du_ref, o_ref, acc_ref):
        i, k = pl.program_id(0), pl.program_id(1)
        d, w = d_ref[...], cw_ref[...]
        halo = h_ref[...] * (i < n - 1).astype(F32)
        du = (w[2:3] * d + w[1:2] * _shift_up(d, halo, 1) + w[0:1] * _shift_up(d, halo, 2)).astype(BF16)
        du_ref[...] = du
        part = lax.dot_general(du, b_ref[...], _DN["nt"], preferred_element_type=F32)

        @pl.when(k == 0)
        def _():
            acc_ref[...] = part

        @pl.when(k > 0)
        def _():
            acc_ref[...] += part

        @pl.when(k == nk - 1)
        def _():
            o_ref[...] = acc_ref[...]

    blk = pl.BlockSpec((None, tm, tk), lambda i, k: (k // half, i, k % half))
    return pl.pallas_call(
        body, name="up_bwd_in", grid=(n, nk),
        in_specs=[blk,
                  pl.BlockSpec((None, 8, tk), lambda i, k: (k // half, jnp.minimum((i + 1) * n8, S // 8 - 1), k % half)),
                  pl.BlockSpec((None, 3, tk), lambda i, k: (k // half, 0, k % half)),
                  pl.BlockSpec((None, K, tk), lambda i, k: (k // npb, 0, k % npb))],
        out_specs=[blk, pl.BlockSpec((tm, K), lambda i, k: (i, 0))],
        out_shape=[jax.ShapeDtypeStruct((2, S, Fh), BF16), jax.ShapeDtypeStruct((S, K), F32)],
        scratch_shapes=[pltpu.VMEM((tm, K), F32)],
        compiler_params=_cparams(("parallel", "arbitrary")),
    )(duc, duc, cw, wg)


def _pool_counts(i, tm, w):
    t = i * tm + lax.broadcasted_iota(jnp.int32, (tm, 1), 0)
    return jnp.minimum(t + 1, w).astype(F32)


def _pool_fwd(x, gain):
    S, D = x.shape
    tm = _tile(S, (256,))
    gd = D // len(POOL_WINDOWS)
    nh = tm // POOL_HALO

    def body(x_ref, xh_ref, g_ref, d_ref):
        i = pl.program_id(0)

        def norm(v):
            return v * lax.rsqrt(jnp.mean(v * v, axis=-1, keepdims=True) + EPS) * g_ref[...]

        h = norm(x_ref[...])
        hh = norm(xh_ref[...]) * (i > 0).astype(F32)
        ext = jnp.concatenate([hh, h], axis=0)
        for gi, w in enumerate(POOL_WINDOWS):
            sl = slice(gi * gd, (gi + 1) * gd)
            win = ext[:, sl]
            k = 1
            while k < w:
                win = win + pltpu.roll(win, k, 0)
                k *= 2
            mean = win[POOL_HALO:] / _pool_counts(i, tm, w)
            d_ref[:, sl] = (mean - h[:, sl]).astype(BF16)

    return pl.pallas_call(
        body, name="pool_fwd", grid=(S // tm,),
        in_specs=[pl.BlockSpec((tm, D), lambda i: (i, 0)),
                  pl.BlockSpec((POOL_HALO, D), lambda i: (jnp.maximum(i * nh - 1, 0), 0)),
                  pl.BlockSpec((1, D), lambda i: (0, 0))],
        out_specs=pl.BlockSpec((tm, D), lambda i: (i, 0)),
        out_shape=jax.ShapeDtypeStruct((S, D), BF16),
        compiler_params=_cparams(("parallel",)),
    )(x, x, gain)


def _pool_mm_fwd(d, wpg, x, scale):
    S, D = d.shape
    nb, G, kcb, gd = wpg.shape
    tm = _tile(S, (512,))
    o_spec = pl.BlockSpec((tm, gd), lambda p, q, k: (p, q))
    return _mm(
        "pool_mm_fwd", "nn", d, wpg, (S // tm, G, nb),
        pl.BlockSpec((tm, kcb), lambda p, q, k: (p, q * nb + k)),
        pl.BlockSpec((None, None, kcb, gd), lambda p, q, k: (k, q, 0, 0)),
        [((S, D), F32, o_spec), ((S, D), F32, o_spec)], (tm, gd),
        extras=[(x, o_spec), (scale, pl.BlockSpec((1, gd), lambda p, q, k: (0, q)))],
        epilogue=lambda acc, xv, sc: (acc, xv + acc * sc))


def _pool_bwd_pre(dx1, ypre, scale):
    S, D = dx1.shape
    tm = _tile(S, (256,))
    n = S // tm

    def body(dx_ref, y_ref, s_ref, dy_ref, ds_ref, acc_ref):
        i = pl.program_id(0)
        dx = dx_ref[...]
        dy_ref[...] = (dx * s_ref[...]).astype(BF16)

        @pl.when(i == 0)
        def _():
            acc_ref[...] = jnp.zeros_like(acc_ref)

        acc_ref[...] += _rows_to_8(dx * y_ref[...])

        @pl.when(i == n - 1)
        def _():
            ds_ref[...] = jnp.sum(acc_ref[...], axis=0, keepdims=True)

    row = pl.BlockSpec((tm, D), lambda i: (i, 0))
    vec = pl.BlockSpec((1, D), lambda i: (0, 0))
    return pl.pallas_call(
        body, name="pool_bwd_pre", grid=(n,),
        in_specs=[row, row, vec], out_specs=[row, vec],
        out_shape=[jax.ShapeDtypeStruct((S, D), BF16), jax.ShapeDtypeStruct((1, D), F32)],
        scratch_shapes=[pltpu.VMEM((8, D), F32)],
        compiler_params=_cparams(("arbitrary",)),
    )(dx1, ypre, scale)


def _pool_mm_bwd_in(dyp, wpg):
    S, D = dyp.shape
    nb, G, kcb, gd = wpg.shape
    tm = _tile(S, (512,))
    return _mm(
        "pool_mm_bwd_in", "nt", dyp, wpg, (S // tm, G * nb, 1),
        pl.BlockSpec((tm, gd), lambda p, q, k: (p, q // nb)),
        pl.BlockSpec((None, None, kcb, gd), lambda p, q, k: (q % nb, q // nb, 0, 0)),
        [((S, D), F32, pl.BlockSpec((tm, kcb), lambda p, q, k: (p, q)))], (tm, kcb))


def _pool_mm_bwd_w(d, dyp, nb):
    S, D = d.shape
    G = len(POOL_WINDOWS)
    gd = D // G
    kcb = gd // nb
    ts = _tile(S, (1024, 512, 256))
    return _mm(
        "pool_mm_bwd_w", "tn", d, dyp, (G * nb, 1, S // ts),
        pl.BlockSpec((ts, kcb), lambda p, q, k: (k, p)),
        pl.BlockSpec((ts, gd), lambda p, q, k: (k, p // nb)),
        [((nb, G, kcb, gd), BF16, pl.BlockSpec((None, None, kcb, gd), lambda p, q, k: (p % nb, p // nb, 0, 0)))],
        (kcb, gd))


def _pool_bwd_post(dd, x, gain, dres):
    S, D = x.shape
    tm = _tile(S, (256,))
    n = S // tm
    gd = D // len(POOL_WINDOWS)
    nh = tm // POOL_HALO

    def body(dd_ref, ddh_ref, x_ref, g_ref, dr_ref, dx_ref, dxb_ref, dg_ref, acc_ref):
        i = pl.program_id(0)
        dd = dd_ref[...]
        halo = ddh_ref[...] * (i < n - 1).astype(F32)
        parts = []
        for gi, w in enumerate(POOL_WINDOWS):
            sl = slice(gi * gd, (gi + 1) * gd)
            ext = jnp.concatenate([dd[:, sl] / _pool_counts(i, tm, w), halo[:, sl] * (1.0 / w)], axis=0)
            k = 1
            while k < w:
                ext = ext + pltpu.roll(ext, tm + POOL_HALO - k, 0)
                k *= 2
            parts.append(ext[:tm] - dd[:, sl])
        dh = jnp.concatenate(parts, axis=1)
        dxn, dgr = _rms_bwd_math(x_ref[...], g_ref[...], dh)
        dx = dr_ref[...] + dxn
        dx_ref[...] = dx
        dxb_ref[...] = dx.astype(BF16)

        @pl.when(i == 0)
        def _():
            acc_ref[...] = jnp.zeros_like(acc_ref)

        acc_ref[...] += _rows_to_8(dgr)

        @pl.when(i == n - 1)
        def _():
            dg_ref[...] = jnp.sum(acc_ref[...], axis=0, keepdims=True)

    row = pl.BlockSpec((tm, D), lambda i: (i, 0))
    vec = pl.BlockSpec((1, D), lambda i: (0, 0))
    nxt = pl.BlockSpec((POOL_HALO, D), lambda i: (jnp.minimum((i + 1) * nh, S // POOL_HALO - 1), 0))
    return pl.pallas_call(
        body, name="pool_bwd_post", grid=(n,),
        in_specs=[row, nxt, row, vec, row],
        out_specs=[row, row, vec],
        out_shape=[jax.ShapeDtypeStruct((S, D), F32), jax.ShapeDtypeStruct((S, D), BF16),
                   jax.ShapeDtypeStruct((1, D), F32)],
        scratch_shapes=[pltpu.VMEM((8, D), F32)],
        compiler_params=_cparams(("arbitrary",)),
    )(dd, dd, x, gain, dres)


def _band_tables():
    i = np.arange(WINDOW)[:, None]
    j = np.arange(2 * WINDOW)[None, :]
    n = np.maximum(WINDOW + i - j, 0)
    max_exact = N_BUCKETS // 2
    nf = np.maximum(n, 1).astype(np.float32)
    large = max_exact + (np.log(nf / max_exact) / np.log(MAX_DISTANCE / max_exact)
                         * (N_BUCKETS - max_exact)).astype(np.int32)
    large = np.minimum(large, N_BUCKETS - 1)
    buckets = np.where(n < max_exact, n, large).astype(np.int32)
    dist = WINDOW + i - j
    in_win = ((dist >= 0) & (dist < WINDOW)).astype(np.int32)
    return buckets, in_win


def _bias_expand(rel_bias):
    H = rel_bias.shape[0]
    buckets, in_win = _band_tables()

    def body(rb_ref, bk_ref, win_ref, o_ref):
        h = pl.program_id(0)
        bk = bk_ref[...]
        acc = jnp.zeros(bk.shape, F32)
        for b in range(N_BUCKETS):
            acc = jnp.where(bk == b, rb_ref[h, b], acc)
        o_ref[...] = jnp.where(win_ref[...] > 0, acc, NEG_INF)

    full = pl.BlockSpec((WINDOW, 2 * WINDOW), lambda h: (0, 0))
    return pl.pallas_call(
        body, name="bias_expand", grid=(H,),
        in_specs=[pl.BlockSpec(memory_space=pltpu.SMEM), full, full],
        out_specs=pl.BlockSpec((None, WINDOW, 2 * WINDOW), lambda h: (h, 0, 0)),
        out_shape=jax.ShapeDtypeStruct((H, WINDOW, 2 * WINDOW), F32),
        compiler_params=_cparams(("parallel",)),
    )(rel_bias, jnp.asarray(buckets), jnp.asarray(in_win))


def _bias_reduce(dbias):
    H = dbias.shape[0]
    buckets, in_win = _band_tables()

    def body(d_ref, bk_ref, win_ref, o_ref):
        bk = jnp.where(win_ref[...] > 0, bk_ref[...], -1)
        d = d_ref[...]
        lane = lax.broadcasted_iota(jnp.int32, (8, LANES), 1)
        out = jnp.zeros((8, LANES), F32)
        for b in range(N_BUCKETS):
            s = jnp.sum(jnp.where(bk == b, d, 0.0))
            out = jnp.where(lane == b, s, out)
        o_ref[...] = out

    full = pl.BlockSpec((WINDOW, 2 * WINDOW), lambda h: (0, 0))
    return pl.pallas_call(
        body, name="bias_reduce", grid=(H,),
        in_specs=[pl.BlockSpec((None, WINDOW, 2 * WINDOW), lambda h: (h, 0, 0)), full, full],
        out_specs=pl.BlockSpec((None, 8, LANES), lambda h: (h, 0, 0)),
        out_shape=jax.ShapeDtypeStruct((H, 8, LANES), F32),
        compiler_params=_cparams(("parallel",)),
    )(dbias, jnp.asarray(buckets), jnp.asarray(in_win))


def _half_rsqrt(t, lo):
    sq = t * t
    s_lo = jnp.sum(jnp.where(lo, sq, 0.0), axis=-1, keepdims=True)
    s_hi = jnp.sum(jnp.where(lo, 0.0, sq), axis=-1, keepdims=True)
    return jnp.where(lo, lax.rsqrt(s_lo * (1.0 / HEAD_DIM) + EPS), lax.rsqrt(s_hi * (1.0 / HEAD_DIM) + EPS))


def _qk_norm_fwd(qkv, qg, kg, n_q, n_kv):
    S, W = qkv.shape
    tm = _tile(S, (256,))
    qw = n_q * HEAD_DIM
    kw = n_kv * HEAD_DIM
    scale = HEAD_DIM ** -0.5

    def body(x_ref, qg_ref, kg_ref, q_ref, k_ref, v_ref):
        lo = lax.broadcasted_iota(jnp.int32, (tm, LANES), 1) < HEAD_DIM
        for t in range(qw // LANES):
            q = x_ref[:, t * LANES:(t + 1) * LANES]
            q_ref[:, t * LANES:(t + 1) * LANES] = (q * _half_rsqrt(q, lo) * qg_ref[...] * scale).astype(BF16)
        for p in range(kw // LANES):
            k = x_ref[:, qw + p * LANES:qw + (p + 1) * LANES]
            kn = k * _half_rsqrt(k, lo) * kg_ref[...]
            v = x_ref[:, qw + kw + p * LANES:qw + kw + (p + 1) * LANES]
            for src, dst in ((kn, k_ref), (v, v_ref)):
                rolled = pltpu.roll(src, HEAD_DIM, 1)
                dst[2 * p] = jnp.where(lo, src, rolled).astype(BF16)
                dst[2 * p + 1] = jnp.where(lo, rolled, src).astype(BF16)

    vec = pl.BlockSpec((1, LANES), lambda i: (0, 0))
    kv = pl.BlockSpec((n_kv, tm, LANES), lambda i: (0, i, 0))
    return pl.pallas_call(
        body, name="qk_norm_fwd", grid=(S // tm,),
        in_specs=[pl.BlockSpec((tm, W), lambda i: (i, 0)), vec, vec],
        out_specs=[pl.BlockSpec((tm, qw), lambda i: (i, 0)), kv, kv],
        out_shape=[jax.ShapeDtypeStruct((S, qw), BF16), jax.ShapeDtypeStruct((n_kv, S, LANES), BF16),
                   jax.ShapeDtypeStruct((n_kv, S, LANES), BF16)],
        compiler_params=_cparams(("parallel",)),
    )(qkv, qg, kg)


def _qk_norm_bwd(qkv, qg, kg, dq, dkc, dkp, dvc, dvp, n_q, n_kv):
    S, W = qkv.shape
    tm = WINDOW
    n = S // tm
    qw = n_q * HEAD_DIM
    kw = n_kv * HEAD_DIM
    scale = HEAD_DIM ** -0.5

    def body(x_ref, qg_ref, kg_ref, dq_ref, dkc_ref, dkp_ref, dvc_ref, dvp_ref, dx_ref, dqg_ref, dkg_ref,
             accq_ref, acck_ref):
        i = pl.program_id(0)
        lo = lax.broadcasted_iota(jnp.int32, (tm, LANES), 1) < HEAD_DIM
        has_next = (i < n - 1).astype(F32)

        @pl.when(i == 0)
        def _():
            accq_ref[...] = jnp.zeros_like(accq_ref)
            acck_ref[...] = jnp.zeros_like(acck_ref)

        def norm_bwd(t, dy, gain):
            r = _half_rsqrt(t, lo)
            xhat = t * r
            dxhat = dy * gain
            prod = dxhat * xhat
            c_lo = jnp.sum(jnp.where(lo, prod, 0.0), axis=-1, keepdims=True) * (1.0 / HEAD_DIM)
            c_hi = jnp.sum(jnp.where(lo, 0.0, prod), axis=-1, keepdims=True) * (1.0 / HEAD_DIM)
            return r * (dxhat - xhat * jnp.where(lo, c_lo, c_hi)), _rows_to_8(dy * xhat)

        for t in range(qw // LANES):
            sl = slice(t * LANES, (t + 1) * LANES)
            dt, dg = norm_bwd(x_ref[:, sl], dq_ref[:, sl] * scale, qg_ref[...])
            dx_ref[:, sl] = dt.astype(BF16)
            accq_ref[...] += dg

        def pair(cur_ref, prev_ref, p):
            folded = []
            for h in (2 * p, 2 * p + 1):
                tot = cur_ref[h] + prev_ref[h] * has_next
                folded.append(tot + pltpu.roll(tot, HEAD_DIM, 1))
            return jnp.where(lo, folded[0], folded[1])

        for p in range(kw // LANES):
            sl = slice(qw + p * LANES, qw + (p + 1) * LANES)
            dt, dg = norm_bwd(x_ref[:, sl], pair(dkc_ref, dkp_ref, p), kg_ref[...])
            dx_ref[:, sl] = dt.astype(BF16)
            acck_ref[...] += dg
            sl = slice(qw + kw + p * LANES, qw + kw + (p + 1) * LANES)
            dx_ref[:, sl] = pair(dvc_ref, dvp_ref, p).astype(BF16)

        @pl.when(i == n - 1)
        def _():
            for acc_ref, o_ref in ((accq_ref, dqg_ref), (acck_ref, dkg_ref)):
                s = jnp.sum(acc_ref[...], axis=0, keepdims=True)
                o_ref[...] = s + pltpu.roll(s, HEAD_DIM, 1)

    vec = pl.BlockSpec((1, LANES), lambda i: (0, 0))
    cur = pl.BlockSpec((n_kv, tm, LANES), lambda i: (0, i, 0))
    nxt = pl.BlockSpec((n_kv, tm, LANES), lambda i: (0, jnp.minimum(i + 1, n - 1), 0))
    return pl.pallas_call(
        body, name="qk_norm_bwd", grid=(n,),
        in_specs=[pl.BlockSpec((tm, W), lambda i: (i, 0)), vec, vec, pl.BlockSpec((tm, qw), lambda i: (i, 0)),
                  cur, nxt, cur, nxt],
        out_specs=[pl.BlockSpec((tm, W), lambda i: (i, 0)), vec, vec],
        out_shape=[jax.ShapeDtypeStruct((S, W), BF16), jax.ShapeDtypeStruct((1, LANES), F32),
                   jax.ShapeDtypeStruct((1, LANES), F32)],
        scratch_shapes=[pltpu.VMEM((8, LANES), F32), pltpu.VMEM((8, LANES), F32)],
        compiler_params=_cparams(("arbitrary",)),
    )(qkv, qg, kg, dq, dkc, dkp, dvc, dvp)


def _attn_specs(n_kv):
    gw = GQA_GROUP * HEAD_DIM
    q = pl.BlockSpec((WINDOW, gw), lambda kh, n: (n, kh))
    cur = pl.BlockSpec((None, WINDOW, LANES), lambda kh, n: (kh, n, 0))
    prev = pl.BlockSpec((None, WINDOW, LANES), lambda kh, n: (kh, jnp.maximum(n - 1, 0), 0))
    bias = pl.BlockSpec((GQA_GROUP, WINDOW, 2 * WINDOW), lambda kh, n: (kh, 0, 0))
    sink = pl.BlockSpec(memory_space=pltpu.SMEM)
    return q, cur, prev, bias, sink


def _stack_heads(x_ref, lo):
    parts = []
    for g in range(GQA_GROUP):
        t = x_ref[:, (g // 2) * LANES:(g // 2 + 1) * LANES]
        parts.append(jnp.where(lo if g % 2 == 0 else jnp.logical_not(lo), t, jnp.zeros_like(t)))
    return jnp.concatenate(parts, axis=0)


def _unstack_heads(v, lo):
    return [jnp.where(lo, v[(2 * p) * WINDOW:(2 * p + 1) * WINDOW], v[(2 * p + 1) * WINDOW:(2 * p + 2) * WINDOW])
            for p in range(GQA_GROUP // 2)]


def _attn_probs(qs, kk, bias_ref, s_ref, kh, has_prev):
    rows = GQA_GROUP * WINDOW
    s = lax.dot_general(qs, kk, _DN["nt"], preferred_element_type=F32) + bias_ref[...].reshape(rows, 2 * WINDOW)
    col = lax.broadcasted_iota(jnp.int32, (rows, 2 * WINDOW), 1)
    s = jnp.where(jnp.logical_or(has_prev, col >= WINDOW), s, NEG_INF)
    sink = jnp.concatenate([jnp.full((WINDOW, 1), s_ref[kh * GQA_GROUP + g], F32) for g in range(GQA_GROUP)], axis=0)
    m = jnp.maximum(jnp.max(s, axis=-1, keepdims=True), sink)
    p = jnp.exp(s - m)
    es = jnp.exp(sink - m)
    den = jnp.sum(p, axis=-1, keepdims=True) + es
    return p / den, es / den


def _attn_fwd(q, kcat, vcat, biasm, sinks):
    S, qw = q.shape
    n_kv = kcat.shape[0]
    qs, cur, prev, bias, sink = _attn_specs(n_kv)

    def body(q_ref, kc_ref, kp_ref, vc_ref, vp_ref, b_ref, s_ref, o_ref):
        kh, n = pl.program_id(0), pl.program_id(1)
        lo = lax.broadcasted_iota(jnp.int32, (WINDOW, LANES), 1) < HEAD_DIM
        kk = jnp.concatenate([kp_ref[...], kc_ref[...]], axis=0)
        vv = jnp.concatenate([vp_ref[...], vc_ref[...]], axis=0)
        pn, _ = _attn_probs(_stack_heads(q_ref, lo), kk, b_ref, s_ref, kh, n > 0)
        o = jnp.dot(pn.astype(BF16), vv, preferred_element_type=F32)
        for p, t in enumerate(_unstack_heads(o, lo)):
            o_ref[:, p * LANES:(p + 1) * LANES] = t.astype(BF16)

    return pl.pallas_call(
        body, name="attn_fwd", grid=(n_kv, S // WINDOW),
        in_specs=[qs, cur, prev, cur, prev, bias, sink],
        out_specs=qs,
        out_shape=jax.ShapeDtypeStruct((S, qw), BF16),
        compiler_params=_cparams(("parallel", "parallel")),
    )(q, kcat, kcat, vcat, vcat, biasm, sinks)


def _attn_bwd(q, kcat, vcat, biasm, sinks, do):
    S, qw = q.shape
    n_kv = kcat.shape[0]
    H = n_kv * GQA_GROUP
    qs, cur, prev, bias, sink = _attn_specs(n_kv)

    def body(q_ref, kc_ref, kp_ref, vc_ref, vp_ref, b_ref, s_ref, do_ref,
             dq_ref, dkc_ref, dkp_ref, dvc_ref, dvp_ref, db_ref, ds_ref):
        kh, n = pl.program_id(0), pl.program_id(1)
        lo = lax.broadcasted_iota(jnp.int32, (WINDOW, LANES), 1) < HEAD_DIM
        kk = jnp.concatenate([kp_ref[...], kc_ref[...]], axis=0)
        vv = jnp.concatenate([vp_ref[...], vc_ref[...]], axis=0)

        @pl.when(n == 0)
        def _():
            db_ref[...] = jnp.zeros_like(db_ref)
            ds_ref[...] = jnp.zeros_like(ds_ref)

        qs_ = _stack_heads(q_ref, lo)
        dos = _stack_heads(do_ref, lo)
        pn, ps = _attn_probs(qs_, kk, b_ref, s_ref, kh, n > 0)
        dp = lax.dot_general(dos, vv, _DN["nt"], preferred_element_type=F32)
        delta = jnp.sum(pn * dp, axis=-1, keepdims=True)
        ds = pn * (dp - delta)
        db_ref[...] += ds.reshape(GQA_GROUP, WINDOW, 2 * WINDOW)
        ds_ref[...] += jnp.zeros((GQA_GROUP, LANES), F32) - jnp.sum((ps * delta).reshape(GQA_GROUP, WINDOW, 1), axis=1)
        dsb = ds.astype(BF16)
        dq = jnp.dot(dsb, kk, preferred_element_type=F32)
        for p, t in enumerate(_unstack_heads(dq, lo)):
            dq_ref[:, p * LANES:(p + 1) * LANES] = t
        dk = lax.dot_general(dsb, qs_, _DN["tn"], preferred_element_type=F32)
        dv = lax.dot_general(pn.astype(BF16), dos, _DN["tn"], preferred_element_type=F32)
        dkp_ref[...] = dk[:WINDOW]
        dkc_ref[...] = dk[WINDOW:]
        dvp_ref[...] = dv[:WINDOW]
        dvc_ref[...] = dv[WINDOW:]

    part = jax.ShapeDtypeStruct((n_kv, S, LANES), F32)
    return pl.pallas_call(
        body, name="attn_bwd", grid=(n_kv, S // WINDOW),
        in_specs=[qs, cur, prev, cur, prev, bias, sink, qs],
        out_specs=[qs, cur, cur, cur, cur, bias, pl.BlockSpec((None, 8, LANES), lambda kh, n: (kh, 0, 0))],
        out_shape=[jax.ShapeDtypeStruct((S, qw), F32), part, part, part, part,
                   jax.ShapeDtypeStruct((H, WINDOW, 2 * WINDOW), F32),
                   jax.ShapeDtypeStruct((n_kv, 8, LANES), F32)],
        compiler_params=_cparams(("parallel", "arbitrary")),
    )(q, kcat, kcat, vcat, vcat, biasm, sinks, do)


def _adamw(name, w, g, m, v):
    shape = w.shape
    C = shape[-1]
    R = int(np.prod(shape[:-1]))
    tr = R
    if R * C * 4 > (1 << 20):
        tr = _tile(R, tuple(t for t in (512, 256, 128, 64, 32, 16, 8) if t * C * 4 <= (3 << 19)))
    c1 = 1.0 - ADAM_B1 ** ADAM_STEP
    c2 = 1.0 - ADAM_B2 ** ADAM_STEP

    def body(w_ref, g_ref, m_ref, v_ref, d_ref, nm_ref, nv_ref):
        gv = g_ref[...]
        nm = ADAM_B1 * m_ref[...] + (1.0 - ADAM_B1) * gv
        nv = ADAM_B2 * v_ref[...] + (1.0 - ADAM_B2) * (gv * gv)
        d_ref[...] = -ADAM_LR * ((nm / c1) / (jnp.sqrt(nv / c2) + ADAM_EPS) + ADAM_WD * w_ref[...])
        nm_ref[...] = nm
        nv_ref[...] = nv

    spec = pl.BlockSpec((tr, C), lambda i: (i, 0))
    outs = pl.pallas_call(
        body, name=name, grid=(R // tr,),
        in_specs=[spec] * 4, out_specs=[spec] * 3,
        out_shape=[jax.ShapeDtypeStruct((R, C), F32)] * 3,
        compiler_params=_cparams(("parallel",)),
    )(*[t.reshape(R, C) for t in (w, g, m, v)])
    return [o.reshape(shape) for o in outs]


def _place():
    return lax.axis_index("x"), lax.axis_index("y"), lax.axis_index("c")


def _other_chips(x, y):
    return [(1 - x, y), (x, 1 - y), (1 - x, 1 - y)]


def _swap_halves(name, grads):
    T = len(grads)

    def body(*refs):
        ins, outs = refs[:T], refs[T:2 * T]
        send, recv = refs[2 * T:]
        x, y, c = _place()

        def copy(t):
            half = grads[t].shape[1] // 2
            src = ins[t].at[:, pl.ds(pl.multiple_of((1 - c) * half, 16), half), :]
            return pltpu.make_async_remote_copy(
                src_ref=src, dst_ref=outs[t], send_sem=send.at[t], recv_sem=recv.at[t],
                device_id=(x, y, 1 - c), device_id_type=MESH)

        for t in range(T):
            copy(t).start()
        for t in range(T):
            copy(t).wait()

    return pl.pallas_call(
        body, name=name,
        in_specs=[_ANY] * T, out_specs=[_ANY] * T,
        out_shape=[jax.ShapeDtypeStruct((g.shape[0], g.shape[1] // 2, g.shape[2]), g.dtype) for g in grads],
        scratch_shapes=[pltpu.SemaphoreType.DMA((T,)), pltpu.SemaphoreType.DMA((T,))],
        compiler_params=pltpu.CompilerParams(has_side_effects=True),
    )(*grads)


def _add_half(name, g, other, c_arr):
    nb, R, C = g.shape
    half = R // 2
    tr = _tile(half, (256, 128, 64, 32, 16))
    n = half // tr

    def body(c_ref, g_ref, o_ref, s_ref):
        s_ref[...] = (g_ref[...].astype(F32) + o_ref[...].astype(F32)).astype(BF16)

    return pl.pallas_call(
        body, name=name,
        grid_spec=pltpu.PrefetchScalarGridSpec(
            num_scalar_prefetch=1, grid=(nb, n),
            in_specs=[pl.BlockSpec((None, tr, C), lambda b, i, c: (b, c[0] * n + i, 0)),
                      pl.BlockSpec((None, tr, C), lambda b, i, c: (b, i, 0))],
            out_specs=pl.BlockSpec((None, tr, C), lambda b, i, c: (b, i, 0))),
        out_shape=jax.ShapeDtypeStruct((nb, half, C), BF16),
        compiler_params=_cparams(("parallel", "parallel")),
    )(c_arr, g, other)


def _in_hbm(arrays):
    return [pltpu.with_memory_space_constraint(a, pltpu.HBM) for a in arrays]


def _chip_copy(src, dst, send, recv, chip, c):
    return pltpu.make_async_remote_copy(src_ref=src, dst_ref=dst, send_sem=send, recv_sem=recv,
                                        device_id=(chip[0], chip[1], c), device_id_type=MESH)


def _cast_slot(name, w, l, chip_arr, dtype):
    _, R, C = w.shape
    tr = _tile(R, tuple(t for t in (1024, 512, 256, 128, 64, 32, 16) if t * C * 4 <= (1 << 21)))

    def body(chip_ref, w_ref, o_ref):
        o_ref[...] = w_ref[...].astype(dtype)

    return pl.pallas_call(
        body, name=name,
        grid_spec=pltpu.PrefetchScalarGridSpec(
            num_scalar_prefetch=1, grid=(R // tr,),
            in_specs=[pl.BlockSpec((None, tr, C), lambda i, chip: (l, i, 0))],
            out_specs=pl.BlockSpec((None, tr, C), lambda i, chip: (chip[0], i, 0))),
        out_shape=jax.ShapeDtypeStruct((N_CHIPS, R, C), dtype),
        compiler_params=_cparams(("parallel",)),
    )(chip_arr, w)


def _gather_start(groups):
    sizes = [len(g) for g in groups]
    flat = [b for g in groups for b in g]
    n, G = len(flat), len(groups)

    def body(*refs):
        bufs = refs[:n]
        sems = refs[n:n + 2 * G]
        token = refs[-1]
        x, y, c = _place()
        chips = _other_chips(x, y)
        me = 2 * x + y
        k = 0
        for l in range(G):
            for t in range(sizes[l]):
                for j in range(3):
                    _chip_copy(bufs[k].at[me], bufs[k].at[me], sems[2 * l].at[3 * t + j], sems[2 * l + 1].at[3 * t + j],
                               chips[j], c).start()
                k += 1
        token[...] = jnp.zeros_like(token)

    sem_shapes = []
    for s in sizes:
        sem_shapes += [pltpu.SemaphoreType.DMA((3 * s,)), pltpu.SemaphoreType.DMA((3 * s,))]
    outs = pl.pallas_call(
        body, name="gather_start",
        out_shape=(*sem_shapes, *[pltpu.HBM(b.shape, b.dtype) for b in flat], jax.ShapeDtypeStruct((8, LANES), F32)),
        in_specs=[_HBM] * n,
        out_specs=(*[_SEM] * (2 * G), *[_HBM] * n, pl.BlockSpec(memory_space=pltpu.VMEM)),
        input_output_aliases={t: 2 * G + t for t in range(n)},
        compiler_params=pltpu.CompilerParams(has_side_effects=_DATAFLOW),
    )(*_in_hbm(flat))
    res, k = [], 2 * G
    for l in range(G):
        res.append((outs[2 * l], outs[2 * l + 1], list(outs[k:k + sizes[l]])))
        k += sizes[l]
    return res, outs[-1]


def _gather_wait(name, send, recv, bufs, after):
    T = len(bufs)

    def body(*refs):
        ins = refs[:T]
        send_ref, recv_ref = refs[T], refs[T + 1]
        x, y, c = _place()
        chips = _other_chips(x, y)
        me = 2 * x + y
        for t in range(T):
            for j in range(3):
                cp = _chip_copy(ins[t].at[me], ins[t].at[2 * chips[j][0] + chips[j][1]], send_ref.at[3 * t + j],
                                recv_ref.at[3 * t + j], chips[j], c)
                cp.wait_send()
                cp.wait_recv()

    return pl.pallas_call(
        body, name=name,
        out_shape=[pltpu.HBM(b.shape, b.dtype) for b in bufs],
        in_specs=[_HBM] * T + [_SEM, _SEM, _ANY],
        out_specs=[_HBM] * T,
        input_output_aliases={t: t for t in range(T)},
        compiler_params=pltpu.CompilerParams(has_side_effects=_DATAFLOW),
    )(*bufs, send, recv, after)


def _scatter_start(name, sums):
    T = len(sums)
    lands = [lax.empty((3,) + s.shape[1:], s.dtype) for s in sums]

    def body(*refs):
        srcs, lnds = refs[:T], refs[T:2 * T]
        send, recv = refs[2 * T], refs[2 * T + 1]
        token = refs[-1]
        x, y, c = _place()
        chips = _other_chips(x, y)
        for t in range(T):
            for j in range(3):
                _chip_copy(srcs[t].at[2 * chips[j][0] + chips[j][1]], lnds[t].at[j], send.at[3 * t + j],
                           recv.at[3 * t + j], chips[j], c).start()
        token[...] = jnp.zeros_like(token)

    both = list(sums) + lands
    outs = pl.pallas_call(
        body, name=name,
        out_shape=(pltpu.SemaphoreType.DMA((3 * T,)), pltpu.SemaphoreType.DMA((3 * T,)),
                   *[pltpu.HBM(b.shape, b.dtype) for b in both], jax.ShapeDtypeStruct((8, LANES), F32)),
        in_specs=[_HBM] * (2 * T),
        out_specs=(_SEM, _SEM, *[_HBM] * (2 * T), pl.BlockSpec(memory_space=pltpu.VMEM)),
        input_output_aliases={t: 2 + t for t in range(2 * T)},
        compiler_params=pltpu.CompilerParams(has_side_effects=_DATAFLOW),
    )(*_in_hbm(both))
    return outs[0], outs[1], list(outs[2:2 + T]), list(outs[2 + T:2 + 2 * T]), outs[-1]


def _scatter_wait(name, send, recv, sums, lands, after):
    T = len(sums)

    def body(*refs):
        srcs, lnds = refs[:T], refs[T:2 * T]
        send_ref, recv_ref = refs[2 * T], refs[2 * T + 1]
        x, y, c = _place()
        chips = _other_chips(x, y)
        for t in range(T):
            for j in range(3):
                cp = _chip_copy(srcs[t].at[2 * chips[j][0] + chips[j][1]], lnds[t].at[j], send_ref.at[3 * t + j],
                                recv_ref.at[3 * t + j], chips[j], c)
                cp.wait_send()
                cp.wait_recv()

    both = list(sums) + list(lands)
    outs = pl.pallas_call(
        body, name=name,
        out_shape=[pltpu.HBM(b.shape, b.dtype) for b in both],
        in_specs=[_HBM] * (2 * T) + [_SEM, _SEM, _ANY],
        out_specs=[_HBM] * (2 * T),
        input_output_aliases={t: t for t in range(2 * T)},
        compiler_params=pltpu.CompilerParams(has_side_effects=_DATAFLOW),
    )(*both, send, recv, after)
    return list(outs[:T]), list(outs[T:])


def _sum_parts(name, sums, land, chip_arr, c_arr):
    _, R2, C = sums.shape
    tr = _tile(R2, (256, 128, 64, 32, 16))

    def body(chip_ref, c_ref, s_ref, l_ref, o_ref):
        acc = s_ref[...].astype(F32)
        for j in range(3):
            acc = acc + l_ref[j].astype(F32)
        o_ref[...] = acc

    return pl.pallas_call(
        body, name=name,
        grid_spec=pltpu.PrefetchScalarGridSpec(
            num_scalar_prefetch=2, grid=(R2 // tr,),
            in_specs=[pl.BlockSpec((None, tr, C), lambda i, chip, c: (chip[0], i, 0)),
                      pl.BlockSpec((3, tr, C), lambda i, chip, c: (0, i, 0))],
            out_specs=pl.BlockSpec((None, tr, C), lambda i, chip, c: (c[0], i, 0))),
        out_shape=jax.ShapeDtypeStruct((2, R2, C), F32),
        compiler_params=_cparams(("parallel",)),
    )(chip_arr, c_arr, sums, land)


def _join_halves(name, bufs):
    T = len(bufs)

    def body(*refs):
        outs = refs[T:2 * T]
        send, recv = refs[2 * T:]
        x, y, c = _place()

        def copy(t, half):
            return pltpu.make_async_remote_copy(
                src_ref=outs[t].at[c], dst_ref=outs[t].at[half], send_sem=send.at[t], recv_sem=recv.at[t],
                device_id=(x, y, 1 - c), device_id_type=MESH)

        for t in range(T):
            copy(t, c).start()
        for t in range(T):
            copy(t, 1 - c).wait_recv()
        for t in range(T):
            copy(t, c).wait_send()

    return pl.pallas_call(
        body, name=name,
        in_specs=[_ANY] * T, out_specs=[_ANY] * T,
        out_shape=[jax.ShapeDtypeStruct(b.shape, b.dtype) for b in bufs],
        input_output_aliases={t: t for t in range(T)},
        scratch_shapes=[pltpu.SemaphoreType.DMA((T,)), pltpu.SemaphoreType.DMA((T,))],
        compiler_params=pltpu.CompilerParams(has_side_effects=True),
    )(*bufs)


def _rs_begin(tag, grads, c_arr):
    got = _swap_halves("rs_swap_" + tag, grads)
    sums = [_add_half("rs_add_%s_%d" % (tag, t), g, o, c_arr) for t, (g, o) in enumerate(zip(grads, got))]
    return _scatter_start("rs_scatter_start_" + tag, sums)


def _rs_finish(tag, state, shapes, chip_arr, c_arr, after=None):
    send, recv, sums, lands, token = state
    sums, lands = _scatter_wait("rs_scatter_wait_" + tag, send, recv, sums, lands, token if after is None else after)
    parts = [_sum_parts("rs_sum_%s_%d" % (tag, t), s, l, chip_arr, c_arr) for t, (s, l) in enumerate(zip(sums, lands))]
    joined = _join_halves("rs_join_" + tag, parts)
    return [j.reshape(sh[1], sh[2]) for j, sh in zip(joined, shapes)]


def _all_reduce_small(v):
    R = v.shape[0]

    def body(v_ref, o_ref, buf, send, recv):
        x, y, c = _place()
        me = 4 * x + 2 * y + c
        buf[me] = v_ref[...]

        def copy(k, slot):
            to = (x ^ ((k >> 2) & 1), y ^ ((k >> 1) & 1), c ^ (k & 1))
            return pltpu.make_async_remote_copy(
                src_ref=v_ref, dst_ref=buf.at[slot], send_sem=send.at[k - 1], recv_sem=recv.at[k - 1],
                device_id=to, device_id_type=MESH)

        for k in range(1, 8):
            copy(k, me).start()
        for k in range(1, 8):
            copy(k, me ^ k).wait_recv()
        for k in range(1, 8):
            copy(k, me).wait_send()
        acc = buf[0]
        for s in range(1, 8):
            acc = acc + buf[s]
        o_ref[...] = acc

    vm = pl.BlockSpec(memory_space=pltpu.VMEM)
    return pl.pallas_call(
        body, name="all_reduce_small",
        in_specs=[vm], out_specs=vm,
        out_shape=jax.ShapeDtypeStruct((R, LANES), F32),
        scratch_shapes=[pltpu.VMEM((8, R, LANES), F32), pltpu.SemaphoreType.DMA((7,)), pltpu.SemaphoreType.DMA((7,))],
        compiler_params=pltpu.CompilerParams(has_side_effects=True, vmem_limit_bytes=VMEM_LIMIT),
    )(v)


def _local_step(x, tgt, small, depth, nb, weight, convw, pscale, on_grads):
    S, D = x.shape
    n_q = D // HEAD_DIM
    n_kv = n_q // GQA_GROUP
    Fh = convw[0].shape[2]

    def dup(gain):
        return jnp.tile(gain, 2).reshape(1, LANES)

    biasm = _bias_expand(small["rel_bias"])
    saved = []
    for i in range(depth):
        j = i // 2
        w = {}
        st = {"x0": x, "w": w}
        gm = small["norm_mix"][i].reshape(1, D)
        if i % 2 == 0:
            h = _rms_fwd("rms_mix_fwd", x, gm)
            w["wqkv"] = weight(i, "wqkv", h)
            qkv = _mm_colblk("qkv_fwd", h, w["wqkv"], F32)
            qh, kcat, vcat = _qk_norm_fwd(qkv, dup(small["attn_q_gain"][j]), dup(small["attn_k_gain"][j]), n_q, n_kv)
            o = _attn_fwd(qh, kcat, vcat, biasm, small["attn_sinks"][j])
            w["wo"] = weight(i, "wo", o)
            x = _mm_rowblk_res("wo_fwd", o, w["wo"], x)
            st.update(h=h, qkv=qkv, qh=qh, kcat=kcat, vcat=vcat, o=o)
        else:
            d = _pool_fwd(x, gm)
            w["wp"] = weight(i, "wp", d)
            ypre, x = _pool_mm_fwd(d, w["wp"], x, pscale[j])
            st.update(d=d, ypre=ypre)
        st["x1"] = x
        h2 = _rms_fwd("rms_ffn_fwd", x, small["norm_ffn"][i].reshape(1, D))
        w["wup"] = weight(i, "wup", h2)
        cb = small["ffn_conv_b"][i].reshape(2, 1, Fh)
        u, a = _up_act_fwd(h2, w["wup"], convw[i], cb)
        w["wdn"] = weight(i, "wdn", a)
        x = _mm_rowblk_res("down_fwd", a, w["wdn"], x)
        st.update(h2=h2, u=u, a=a, cb=cb)
        saved.append(st)

    dx, dxb, loss_part = _loss_head(x, tgt)

    big = [dict() for _ in range(depth)]
    sg = {k: [None] * depth for k in ("norm_mix", "norm_ffn", "conv_w", "conv_b")}
    sg.update({k: [None] * (depth // 2) for k in ("q_gain", "k_gain", "sinks", "pool_scale")})
    dbias_tot = None
    before = None
    for i in reversed(range(depth)):
        j = i // 2
        st = saved[i]
        w = st["w"]
        da = _mm_nt_rowblk("down_bwd_in", dxb, w["wdn"], F32, after=before)
        big[i]["wdn"] = _mm_tn_rowblk("down_bwd_w", st["a"], dxb, nb)
        duc, dcwb = _ffn_act_bwd(st["u"], convw[i], st["cb"], da)
        du, dh2 = _up_bwd_in(duc, convw[i], w["wup"])
        sg["conv_w"][i] = jnp.transpose(dcwb[:, 0:3, :], (1, 0, 2)).reshape(3, 2 * Fh)
        sg["conv_b"][i] = dcwb[:, 3, :].reshape(2 * Fh)
        big[i]["wup"] = _mm_tn_colblk("up_bwd_w", st["h2"], du, nb, split2=True)
        dx, dxb, dg = _rms_bwd("rms_ffn_bwd", st["x1"], small["norm_ffn"][i].reshape(1, D), dh2, dx)
        sg["norm_ffn"][i] = dg.reshape(D)
        gm = small["norm_mix"][i].reshape(1, D)
        if i % 2 == 0:
            do = _mm_nt_rowblk("wo_bwd_in", dxb, w["wo"], BF16)
            big[i]["wo"] = _mm_tn_rowblk("wo_bwd_w", st["o"], dxb, nb)
            dq, dkc, dkp, dvc, dvp, dbias, dsink = _attn_bwd(
                st["qh"], st["kcat"], st["vcat"], biasm, small["attn_sinks"][j], do)
            dbias_tot = dbias if dbias_tot is None else dbias_tot + dbias
            sg["sinks"][j] = dsink[:, :, 0].reshape(n_q)
            dqkv, dqg, dkg = _qk_norm_bwd(st["qkv"], dup(small["attn_q_gain"][j]), dup(small["attn_k_gain"][j]),
                                          dq, dkc, dkp, dvc, dvp, n_q, n_kv)
            sg["q_gain"][j] = dqg[0, :HEAD_DIM]
            sg["k_gain"][j] = dkg[0, :HEAD_DIM]
            big[i]["wqkv"] = _mm_tn_colblk("qkv_bwd_w", st["h"], dqkv, nb)
            dh = _mm_nt_colblk("qkv_bwd_in", dqkv, w["wqkv"])
            dx, dxb, dg = _rms_bwd("rms_mix_bwd", st["x0"], gm, dh, dx)
        else:
            dyp, dsc = _pool_bwd_pre(dx, st["ypre"], pscale[j])
            sg["pool_scale"][j] = dsc.reshape(D)
            big[i]["wp"] = _pool_mm_bwd_w(st["d"], dyp, nb)
            dd = _pool_mm_bwd_in(dyp, w["wp"])
            dx, dxb, dg = _pool_bwd_post(dd, st["x0"], gm, dx)
        sg["norm_mix"][i] = dg.reshape(D)
        before = on_grads(i, big[i], dx)
    sg["rel_bias"] = _bias_reduce(dbias_tot)[:, 0, :N_BUCKETS]
    return loss_part, dx, sg


_SMALL_ORDER = ("norm_mix", "norm_ffn", "rel_bias", "q_gain", "k_gain", "sinks", "conv_b", "conv_w", "pool_scale")


def kernel(x, norm_mix, norm_ffn, rel_bias, attn_w_qkv, attn_q_gain, attn_k_gain, attn_sinks, attn_w_o, pool_w, pool_scale, ffn_w_up, ffn_conv_w, ffn_conv_b, ffn_w_down, loss_target, m_norm_mix, m_norm_ffn, m_rel_bias, m_attn_w_qkv, m_attn_q_gain, m_attn_k_gain, m_attn_sinks, m_attn_w_o, m_pool_w, m_pool_scale, m_ffn_w_up, m_ffn_conv_w, m_ffn_conv_b, m_ffn_w_down, v_norm_mix, v_norm_ffn, v_rel_bias, v_attn_w_qkv, v_attn_q_gain, v_attn_k_gain, v_attn_sinks, v_attn_w_o, v_pool_w, v_pool_scale, v_ffn_w_up, v_ffn_conv_w, v_ffn_conv_b, v_ffn_w_down):
    _, S, D = x.shape
    depth = ffn_w_up.shape[0]
    n_attn, n_pool = attn_w_qkv.shape[0], pool_w.shape[0]
    nb = N_CHIPS
    nc_up = ffn_w_up.shape[2]
    Fh = nc_up * nb // 2
    cx, cy, cc = _place()
    chip = 2 * cx + cy
    c_arr = jnp.reshape(cc, (1,)).astype(jnp.int32)
    chip_arr = jnp.reshape(chip, (1,)).astype(jnp.int32)

    pool_w3 = pool_w.reshape(n_pool, -1, pool_w.shape[-1])
    order = [("convw", ffn_conv_w.reshape(1, -1, nc_up), 0, F32), ("pscale", pool_scale[None], 0, F32)]
    for i in range(depth):
        j = i // 2
        mixer = ([("wqkv", attn_w_qkv, j), ("wo", attn_w_o, j)] if i % 2 == 0 else [("wp", pool_w3, j)])
        order += [((i, k), w, l, BF16) for k, w, l in mixer + [("wup", ffn_w_up, i), ("wdn", ffn_w_down, i)]]
    slots = [_cast_slot("slot_%d" % n, w, l, chip_arr, dt) for n, (_, w, l, dt) in enumerate(order)]
    started, token = _gather_start([[b] for b in slots])
    index = {key: n for n, (key, _, _, _) in enumerate(order)}

    def gathered(key, after):
        n = index[key]
        send, recv, bufs = started[n]
        return _gather_wait("gather_wait_%d" % n, send, recv, bufs, after)[0]

    convw_g = gathered("convw", token).reshape(nb, depth, 3, nc_up)
    pscale_g = gathered("pscale", token)
    convw = [jnp.transpose(convw_g[:, i], (1, 0, 2)).reshape(3, 2, Fh).transpose(1, 0, 2) for i in range(depth)]
    pscale = [pscale_g[:, j].reshape(1, D) for j in range(n_pool)]
    small = dict(norm_mix=norm_mix, norm_ffn=norm_ffn, rel_bias=rel_bias, attn_q_gain=attn_q_gain,
                 attn_k_gain=attn_k_gain, attn_sinks=attn_sinks, ffn_conv_b=ffn_conv_b)

    def weight(i, name, after):
        w = gathered((i, name), after)
        return w.reshape((nb,) + pool_w.shape[1:]) if name == "wp" else w

    red = {k: [None] * (n_attn if k in ("wqkv", "wo") else n_pool if k == "wp" else depth)
           for k in ("wqkv", "wo", "wp", "wup", "wdn")}
    pending = []

    def finish(after):
        i, names, shapes, state = pending.pop()
        outs = _rs_finish("%d" % i, state, shapes, chip_arr, c_arr, after)
        for k, o in zip(names, outs):
            red[k][i // 2 if k in ("wqkv", "wo", "wp") else i] = o

    def on_grads(i, big_i, dx_i):
        if pending:
            finish(dx_i)
        names = sorted(big_i)
        flat = [big_i[k].reshape(nb, -1, big_i[k].shape[-1]) for k in names]
        state = _rs_begin("%d" % i, flat, c_arr)
        pending.append((i, names, [f.shape for f in flat], state))
        return state[-1]

    loss_part, dx, sg = _local_step(x[0], loss_target[0], small, depth, nb, weight, convw, pscale, on_grads)
    finish(None)
    loss = lax.psum(jnp.sum(loss_part), ("x", "y", "c"))

    g_wqkv = jnp.stack(red["wqkv"])
    g_wo = jnp.stack(red["wo"])
    g_wp = jnp.stack(red["wp"]).reshape(pool_w.shape)
    g_wup = jnp.stack(red["wup"])
    g_wdn = jnp.stack(red["wdn"])

    parts = [jnp.stack(sg[k]) if isinstance(sg[k], list) else sg[k] for k in _SMALL_ORDER]
    sizes = [int(np.prod(p.shape)) for p in parts]
    total = sum(sizes)
    rows = -(-total // (8 * LANES)) * 8
    packed = jnp.concatenate([p.reshape(-1) for p in parts] + [jnp.zeros((rows * LANES - total,), F32)])
    summed = _all_reduce_small(packed.reshape(rows, LANES)).reshape(-1)
    sm, off = {}, 0
    for k, p, n in zip(_SMALL_ORDER, parts, sizes):
        sm[k] = summed[off:off + n].reshape(p.shape)
        off += n
    g_convw = lax.dynamic_slice_in_dim(sm["conv_w"], chip * nc_up, nc_up, axis=2)
    pc = pool_scale.shape[1]
    g_pscale = lax.dynamic_slice_in_dim(sm["pool_scale"], chip * pc, pc, axis=1)

    grads = [sm["norm_mix"], sm["norm_ffn"], sm["rel_bias"], g_wqkv, sm["q_gain"], sm["k_gain"], sm["sinks"], g_wo,
             g_wp, g_pscale, g_wup, g_convw, sm["conv_b"], g_wdn]
    ws = [norm_mix, norm_ffn, rel_bias, attn_w_qkv, attn_q_gain, attn_k_gain, attn_sinks, attn_w_o, pool_w, pool_scale,
          ffn_w_up, ffn_conv_w, ffn_conv_b, ffn_w_down]
    ms = [m_norm_mix, m_norm_ffn, m_rel_bias, m_attn_w_qkv, m_attn_q_gain, m_attn_k_gain, m_attn_sinks, m_attn_w_o,
          m_pool_w, m_pool_scale, m_ffn_w_up, m_ffn_conv_w, m_ffn_conv_b, m_ffn_w_down]
    vs = [v_norm_mix, v_norm_ffn, v_rel_bias, v_attn_w_qkv, v_attn_q_gain, v_attn_k_gain, v_attn_sinks, v_attn_w_o,
          v_pool_w, v_pool_scale, v_ffn_w_up, v_ffn_conv_w, v_ffn_conv_b, v_ffn_w_down]
    deltas, new_m, new_v = [], [], []
    for idx, (w, g, m, v) in enumerate(zip(ws, grads, ms, vs)):
        d, nm, nv = _adamw("adamw_%d" % idx, w, g, m, v)
        deltas.append(d), new_m.append(nm), new_v.append(nv)
    return (loss, dx.reshape(1, S, D), *grads, *deltas, *new_m, *new_v)
```

```python
import functools

import numpy as np
import jax
import jax.numpy as jnp
from jax import lax
from jax.experimental import pallas as pl
from jax.experimental.pallas import tpu as pltpu

F32 = jnp.float32
BF16 = jnp.bfloat16
MESH = pl.DeviceIdType.MESH
_ANY = pl.BlockSpec(memory_space=pl.ANY)
_HBM = pl.BlockSpec(memory_space=pltpu.HBM)
_SEM = pl.BlockSpec(memory_space=pltpu.SEMAPHORE)
_DATAFLOW = pltpu.SideEffectType.DATAFLOW_SIDE_EFFECTING

N_CHIPS = 4
HEAD_DIM = 64
GQA_GROUP = 8
WINDOW = 128
N_BUCKETS = 32
MAX_DISTANCE = 128
POOL_WINDOWS = (2, 4, 8, 16)
POOL_HALO = 16
EPS = 1e-6
NEG_INF = -1e30
LANES = 128
VMEM_LIMIT = 56 * 1024 * 1024

ADAM_LR = 0.001
ADAM_B1 = 0.9
ADAM_B2 = 0.999
ADAM_EPS = 1e-08
ADAM_WD = 0.01
ADAM_STEP = 10


def _tile(n, prefs):
    for p in prefs:
        if p <= n and n % p == 0:
            return p
    return n


def _cparams(sem):
    return pltpu.CompilerParams(dimension_semantics=sem, vmem_limit_bytes=VMEM_LIMIT)


_DN = {
    "nn": (((1,), (0,)), ((), ())),
    "nt": (((1,), (1,)), ((), ())),
    "tn": (((0,), (0,)), ((), ())),
}


def _mm(name, kind, a, b, grid, a_spec, b_spec, outs, acc_shape, extras=(), epilogue=None, after=None):
    nk = grid[2]
    n_ex, n_out = len(extras), len(outs)
    order = [] if after is None else [after]

    def body(a_ref, b_ref, *rest):
        ex_refs = rest[:n_ex]
        rest = rest[n_ex + len(order):]
        out_refs = rest[:n_out]
        acc_ref = rest[n_out] if nk > 1 else None
        part = lax.dot_general(a_ref[...], b_ref[...], _DN[kind], preferred_element_type=F32)

        def finish(val):
            vals = epilogue(val, *[r[...] for r in ex_refs]) if epilogue else (val,)
            for r, v in zip(out_refs, vals):
                r[...] = v.astype(r.dtype)

        if nk == 1:
            finish(part)
        else:
            k = pl.program_id(2)

            @pl.when(k == 0)
            def _():
                acc_ref[...] = part

            @pl.when(k > 0)
            def _():
                acc_ref[...] += part

            @pl.when(k == nk - 1)
            def _():
                finish(acc_ref[...])

    res = pl.pallas_call(
        body,
        name=name,
        grid=grid,
        in_specs=[a_spec, b_spec] + [s for _, s in extras] + [_ANY] * len(order),
        out_specs=[s for _, _, s in outs],
        out_shape=[jax.ShapeDtypeStruct(sh, dt) for sh, dt, _ in outs],
        scratch_shapes=[pltpu.VMEM(acc_shape, F32)] if nk > 1 else [],
        compiler_params=_cparams(("parallel", "parallel", "arbitrary")),
    )(a, b, *[e for e, _ in extras], *order)
    return res if n_out > 1 else res[0]


def _mm_colblk(name, a, wg, out_dtype):
    S, K = a.shape
    nb, _, nc = wg.shape
    tm = _tile(S, (512,))
    tn = _tile(nc, (1408, 1024, 640, 512, 256, 128))
    npb = nc // tn
    return _mm(
        name, "nn", a, wg, (nb * npb, S // tm, 1),
        pl.BlockSpec((tm, K), lambda p, q, k: (q, 0)),
        pl.BlockSpec((None, K, tn), lambda p, q, k: (p // npb, 0, p % npb)),
        [((S, nb * nc), out_dtype, pl.BlockSpec((tm, tn), lambda p, q, k: (q, p)))], (tm, tn))


def _mm_rowblk_res(name, a, wg, res):
    S = a.shape[0]
    nb, kc, N = wg.shape
    tm = _tile(S, (512,))
    tn = _tile(N, (2048, 1024, 512))
    tk = _tile(kc, (1408, 512, 256, 128))
    kpb = kc // tk
    o_spec = pl.BlockSpec((tm, tn), lambda p, q, k: (p, q))
    return _mm(
        name, "nn", a, wg, (S // tm, N // tn, nb * kpb),
        pl.BlockSpec((tm, tk), lambda p, q, k: (p, k)),
        pl.BlockSpec((None, tk, tn), lambda p, q, k: (k // kpb, k % kpb, q)),
        [((S, N), F32, o_spec)], (tm, tn),
        extras=[(res, o_spec)], epilogue=lambda acc, r: (r + acc,))


def _mm_nt_rowblk(name, g, wg, out_dtype, after=None):
    S, N = g.shape
    nb, kc, _ = wg.shape
    tm = _tile(S, (512,))
    tn = _tile(kc, (1408, 512, 256, 128))
    kpb = kc // tn
    return _mm(
        name, "nt", g, wg, (nb * kpb, S // tm, 1),
        pl.BlockSpec((tm, N), lambda p, q, k: (q, 0)),
        pl.BlockSpec((None, tn, N), lambda p, q, k: (p // kpb, p % kpb, 0)),
        [((S, nb * kc), out_dtype, pl.BlockSpec((tm, tn), lambda p, q, k: (q, p)))], (tm, tn), after=after)


def _mm_nt_colblk(name, g, wg):
    S = g.shape[0]
    nb, K, nc = wg.shape
    tm = _tile(S, (512,))
    tn = _tile(K, (2048, 1024))
    tk = _tile(nc, (1408, 640, 512, 256, 128))
    npb = nc // tk
    return _mm(
        name, "nt", g, wg, (S // tm, K // tn, nb * npb),
        pl.BlockSpec((tm, tk), lambda p, q, k: (p, k)),
        pl.BlockSpec((None, tn, tk), lambda p, q, k: (k // npb, q, k % npb)),
        [((S, K), F32, pl.BlockSpec((tm, tn), lambda p, q, k: (p, q)))], (tm, tn))


def _mm_tn_colblk(name, a, g, nb, split2=False):
    S, K = a.shape
    ntot = g.shape[-1] * (2 if split2 else 1)
    nc = ntot // nb
    ti = _tile(K, (1024,))
    tn = _tile(nc, (1408, 640, 512, 256, 128))
    ts = _tile(S, (1024, 512, 256))
    npb = nc // tn
    half = nb * npb // 2
    if split2:
        b_spec = pl.BlockSpec((None, ts, tn), lambda p, q, k: (q // half, k, q % half))
    else:
        b_spec = pl.BlockSpec((ts, tn), lambda p, q, k: (k, q))
    return _mm(
        name, "tn", a, g, (K // ti, nb * npb, S // ts),
        pl.BlockSpec((ts, ti), lambda p, q, k: (k, p)),
        b_spec,
        [((nb, K, nc), BF16, pl.BlockSpec((None, ti, tn), lambda p, q, k: (q // npb, p, q % npb)))], (ti, tn))


def _mm_tn_rowblk(name, a, g, nb):
    S, ktot = a.shape
    N = g.shape[1]
    kc = ktot // nb
    ti = _tile(kc, (1408, 512, 256, 128))
    tj = _tile(N, (1024,))
    ts = _tile(S, (1024, 512, 256))
    ipb = kc // ti
    return _mm(
        name, "tn", a, g, (nb * ipb, N // tj, S // ts),
        pl.BlockSpec((ts, ti), lambda p, q, k: (k, p)),
        pl.BlockSpec((ts, tj), lambda p, q, k: (k, q)),
        [((nb, kc, N), BF16, pl.BlockSpec((None, ti, tj), lambda p, q, k: (p // ipb, p % ipb, q)))], (ti, tj))


def _rms_fwd(name, x, gain):
    S, D = x.shape
    tm = _tile(S, (512,))

    def body(x_ref, g_ref, o_ref):
        xv = x_ref[...]
        r = lax.rsqrt(jnp.mean(xv * xv, axis=-1, keepdims=True) + EPS)
        o_ref[...] = (xv * r * g_ref[...]).astype(o_ref.dtype)

    return pl.pallas_call(
        body, name=name, grid=(S // tm,),
        in_specs=[pl.BlockSpec((tm, D), lambda i: (i, 0)), pl.BlockSpec((1, D), lambda i: (0, 0))],
        out_specs=pl.BlockSpec((tm, D), lambda i: (i, 0)),
        out_shape=jax.ShapeDtypeStruct((S, D), BF16),
        compiler_params=_cparams(("parallel",)),
    )(x, gain)


def _rms_bwd_math(xv, gain, dh):
    r = lax.rsqrt(jnp.mean(xv * xv, axis=-1, keepdims=True) + EPS)
    xhat = xv * r
    dxhat = dh * gain
    c = jnp.mean(dxhat * xhat, axis=-1, keepdims=True)
    return r * (dxhat - xhat * c), dh * xhat


def _rows_to_8(v):
    tm, C = v.shape
    return jnp.sum(v.reshape(tm // 8, 8, C), axis=0)


def _rms_bwd(name, x, gain, dh, dres, after=None):
    S, D = x.shape
    tm = _tile(S, (256,))
    n = S // tm
    order = [] if after is None else [after]

    def body(x_ref, g_ref, dh_ref, dr_ref, *rest):
        dx_ref, dxb_ref, dg_ref, acc_ref = rest[len(order):]
        i = pl.program_id(0)
        dxn, dgr = _rms_bwd_math(x_ref[...], g_ref[...], dh_ref[...])
        dx = dr_ref[...] + dxn
        dx_ref[...] = dx
        dxb_ref[...] = dx.astype(BF16)

        @pl.when(i == 0)
        def _():
            acc_ref[...] = jnp.zeros_like(acc_ref)

        acc_ref[...] += _rows_to_8(dgr)

        @pl.when(i == n - 1)
        def _():
            dg_ref[...] = jnp.sum(acc_ref[...], axis=0, keepdims=True)

    row = pl.BlockSpec((tm, D), lambda i: (i, 0))
    vec = pl.BlockSpec((1, D), lambda i: (0, 0))
    return pl.pallas_call(
        body, name=name, grid=(n,),
        in_specs=[row, vec, row, row] + [_ANY] * len(order),
        out_specs=[row, row, vec],
        out_shape=[jax.ShapeDtypeStruct((S, D), F32), jax.ShapeDtypeStruct((S, D), BF16),
                   jax.ShapeDtypeStruct((1, D), F32)],
        scratch_shapes=[pltpu.VMEM((8, D), F32)],
        compiler_params=_cparams(("arbitrary",)),
    )(x, gain, dh, dres, *order)


def _loss_head(y, tgt):
    S, D = y.shape
    tm = _tile(S, (256,))
    n = S // tm

    def body(y_ref, t_ref, dy_ref, dyb_ref, l_ref):
        i = pl.program_id(0)
        e = y_ref[...] - t_ref[...]
        dy = e * (1.0 / D)
        dy_ref[...] = dy
        dyb_ref[...] = dy.astype(BF16)

        @pl.when(i == 0)
        def _():
            l_ref[...] = jnp.zeros_like(l_ref)

        sq = _rows_to_8(e * e)
        part = sq[:, 0:LANES]
        for t in range(1, D // LANES):
            part = part + sq[:, t * LANES:(t + 1) * LANES]
        l_ref[...] += part * (0.5 / D)

    row = pl.BlockSpec((tm, D), lambda i: (i, 0))
    return pl.pallas_call(
        body, name="loss_head", grid=(n,),
        in_specs=[row, row],
        out_specs=[row, row, pl.BlockSpec((8, LANES), lambda i: (0, 0))],
        out_shape=[jax.ShapeDtypeStruct((S, D), F32), jax.ShapeDtypeStruct((S, D), BF16),
                   jax.ShapeDtypeStruct((8, LANES), F32)],
        compiler_params=_cparams(("arbitrary",)),
    )(y, tgt)


def _shift_down(u, halo, k):
    row = lax.broadcasted_iota(jnp.int32, u.shape, 0)
    out = pltpu.roll(u, k, 0)
    for j in range(k):
        out = jnp.where(row == j, halo[8 - k + j:8 - k + j + 1, :], out)
    return out


def _shift_up(u, halo, k):
    tm = u.shape[0]
    row = lax.broadcasted_iota(jnp.int32, u.shape, 0)
    out = pltpu.roll(u, tm - k, 0)
    for j in range(k):
        out = jnp.where(row == tm - k + j, halo[j:j + 1, :], out)
    return out


def _conv_fwd(u, halo, cw, cb):
    return cw[0:1] * _shift_down(u, halo, 2) + cw[1:2] * _shift_down(u, halo, 1) + cw[2:3] * u + cb


def _up_act_fwd(h2, wg, cw, cb):
    S, K = h2.shape
    nb, _, nc = wg.shape
    Fh = nb * nc // 2
    tm = _tile(S, (1024, 512))
    tn = _tile(nc, (256, 128))
    npb = nc // tn
    hb = nb // 2

    def body(a_ref, bg_ref, bv_ref, cw_ref, cb_ref, u_ref, uc_ref, o_ref, halo_ref):
        @pl.when(pl.program_id(1) == 0)
        def _():
            halo_ref[...] = jnp.zeros_like(halo_ref)

        av = a_ref[...]
        ucs = []
        for s, b_ref in enumerate((bg_ref, bv_ref)):
            us = jnp.dot(av, b_ref[...], preferred_element_type=F32)
            u_ref[s] = us
            ucs.append(_conv_fwd(us, halo_ref[s], cw_ref[s], cb_ref[s]))
            uc_ref[s] = ucs[s]
            halo_ref[s] = us[tm - 8:, :]
        gate, val = ucs
        sig = 1.0 / (1.0 + jnp.exp(-gate))
        o_ref[...] = (gate * sig * val).astype(BF16)

    both = pl.BlockSpec((2, tm, tn), lambda j, i: (0, i, j))
    return pl.pallas_call(
        body, name="up_act_fwd", grid=(Fh // tn, S // tm),
        in_specs=[pl.BlockSpec((tm, K), lambda j, i: (i, 0)),
                  pl.BlockSpec((None, K, tn), lambda j, i: (j // npb, 0, j % npb)),
                  pl.BlockSpec((None, K, tn), lambda j, i: (j // npb + hb, 0, j % npb)),
                  pl.BlockSpec((2, 3, tn), lambda j, i: (0, 0, j)),
                  pl.BlockSpec((2, 1, tn), lambda j, i: (0, 0, j))],
        out_specs=[both, both, pl.BlockSpec((tm, tn), lambda j, i: (i, j))],
        out_shape=[jax.ShapeDtypeStruct((2, S, Fh), F32), jax.ShapeDtypeStruct((2, S, Fh), F32),
                   jax.ShapeDtypeStruct((S, Fh), BF16)],
        scratch_shapes=[pltpu.VMEM((2, 8, tn), F32)],
        compiler_params=_cparams(("parallel", "arbitrary")),
    )(h2, wg, wg, cw, cb)


def _down_act_bwd(dxb, wg, uc, after=None):
    S, D = dxb.shape
    nb, kc, _ = wg.shape
    Fh = nb * kc
    tm = _tile(S, (1024, 512))
    t128 = kc // LANES
    tn = 2 * LANES
    order = [] if after is None else [after]

    def body(dx_ref, b0_ref, b1_ref, uc_ref, *rest):
        duc_ref, bcat_ref = rest[len(order):]

        @pl.when(pl.program_id(1) == 0)
        def _():
            bcat_ref[0:LANES, :] = b0_ref[...]
            bcat_ref[LANES:tn, :] = b1_ref[...]

        da = lax.dot_general(dx_ref[...], bcat_ref[...], _DN["nt"], preferred_element_type=F32)
        gate, val = uc_ref[0], uc_ref[1]
        sig = 1.0 / (1.0 + jnp.exp(-gate))
        duc_ref[0] = da * val * (sig * (1.0 + gate * (1.0 - sig)))
        duc_ref[1] = da * (gate * sig)

    both = pl.BlockSpec((2, tm, tn), lambda j, i: (0, i, j))
    return pl.pallas_call(
        body, name="down_act_bwd", grid=(Fh // tn, S // tm),
        in_specs=[pl.BlockSpec((tm, D), lambda j, i: (i, 0)),
                  pl.BlockSpec((None, LANES, D), lambda j, i: ((2 * j) // t128, (2 * j) % t128, 0)),
                  pl.BlockSpec((None, LANES, D), lambda j, i: ((2 * j + 1) // t128, (2 * j + 1) % t128, 0)),
                  both] + [_ANY] * len(order),
        out_specs=both,
        out_shape=jax.ShapeDtypeStruct((2, S, Fh), F32),
        scratch_shapes=[pltpu.VMEM((tn, D), BF16)],
        compiler_params=_cparams(("parallel", "arbitrary")),
    )(dxb, wg, wg, uc, *order)


def _up_bwd_in(duc, u, cw, wg):
    _, S, Fh = duc.shape
    nb, K, nc = wg.shape
    tm = _tile(S, (512,))
    tk = _tile(nc, (1408, 640, 512, 256, 128))
    npb = nc // tk
    nk = nb * npb
    half = nk // 2
    n, n8 = S // tm, tm // 8

    def body(d_ref, h_ref, u_ref, cw_ref, b_ref, du_ref, o_ref, cg_ref, acc_ref):
        i, k = pl.program_id(0), pl.program_id(1)
        d, w, uv = d_ref[...], cw_ref[...], u_ref[...]
        halo = h_ref[...] * (i < n - 1).astype(F32)
        d1, d2 = _shift_up(d, halo, 1), _shift_up(d, halo, 2)
        du = (w[2:3] * d + w[1:2] * d1 + w[0:1] * d2).astype(BF16)
        du_ref[...] = du

        @pl.when(i == 0)
        def _():
            cg_ref[k] = jnp.zeros((8, tk), F32)

        for r, v in enumerate((d2 * uv, d1 * uv, d * uv, d)):
            cg_ref[k, r:r + 1, :] += jnp.sum(v, axis=0, keepdims=True)
        part = lax.dot_general(du, b_ref[...], _DN["nt"], preferred_element_type=F32)

        @pl.when(k == 0)
        def _():
            acc_ref[...] = part

        @pl.when(k > 0)
        def _():
            acc_ref[...] += part

        @pl.when(k == nk - 1)
        def _():
            o_ref[...] = acc_ref[...]

    blk = pl.BlockSpec((None, tm, tk), lambda i, k: (k // half, i, k % half))
    return pl.pallas_call(
        body, name="up_bwd_in", grid=(n, nk),
        in_specs=[blk,
                  pl.BlockSpec((None, 8, tk), lambda i, k: (k // half, jnp.minimum((i + 1) * n8, S // 8 - 1), k % half)),
                  blk,
                  pl.BlockSpec((None, 3, tk), lambda i, k: (k // half, 0, k % half)),
                  pl.BlockSpec((None, K, tk), lambda i, k: (k // npb, 0, k % npb))],
        out_specs=[blk, pl.BlockSpec((tm, K), lambda i, k: (i, 0)), pl.BlockSpec((nk, 8, tk), lambda i, k: (0, 0, 0))],
        out_shape=[jax.ShapeDtypeStruct((2, S, Fh), BF16), jax.ShapeDtypeStruct((S, K), F32),
                   jax.ShapeDtypeStruct((nk, 8, tk), F32)],
        scratch_shapes=[pltpu.VMEM((tm, K), F32)],
        compiler_params=_cparams(("arbitrary", "arbitrary")),
    )(duc, duc, u, cw, wg)


def _pool_counts(i, tm, w):
    t = i * tm + lax.broadcasted_iota(jnp.int32, (tm, 1), 0)
    return jnp.minimum(t + 1, w).astype(F32)


def _pool_fwd(x, gain):
    S, D = x.shape
    tm = _tile(S, (256,))
    gd = D // len(POOL_WINDOWS)
    nh = tm // POOL_HALO

    def body(x_ref, xh_ref, g_ref, d_ref):
        i = pl.program_id(0)

        def norm(v):
            return v * lax.rsqrt(jnp.mean(v * v, axis=-1, keepdims=True) + EPS) * g_ref[...]

        h = norm(x_ref[...])
        hh = norm(xh_ref[...]) * (i > 0).astype(F32)
        ext = jnp.concatenate([hh, h], axis=0)
        for gi, w in enumerate(POOL_WINDOWS):
            sl = slice(gi * gd, (gi + 1) * gd)
            win = ext[:, sl]
            k = 1
            while k < w:
                win = win + pltpu.roll(win, k, 0)
                k *= 2
            mean = win[POOL_HALO:] / _pool_counts(i, tm, w)
            d_ref[:, sl] = (mean - h[:, sl]).astype(BF16)

    return pl.pallas_call(
        body, name="pool_fwd", grid=(S // tm,),
        in_specs=[pl.BlockSpec((tm, D), lambda i: (i, 0)),
                  pl.BlockSpec((POOL_HALO, D), lambda i: (jnp.maximum(i * nh - 1, 0), 0)),
                  pl.BlockSpec((1, D), lambda i: (0, 0))],
        out_specs=pl.BlockSpec((tm, D), lambda i: (i, 0)),
        out_shape=jax.ShapeDtypeStruct((S, D), BF16),
        compiler_params=_cparams(("parallel",)),
    )(x, x, gain)


def _pool_mm_fwd(d, wp, x, scale):
    S, D = d.shape
    G, gd, _ = wp.shape
    tm = _tile(S, (1024, 512))
    o_spec = pl.BlockSpec((tm, gd), lambda p, q, k: (p, q))
    return _mm(
        "pool_mm_fwd", "nn", d, wp, (S // tm, G, 1),
        o_spec, pl.BlockSpec((None, gd, gd), lambda p, q, k: (q, 0, 0)),
        [((S, D), F32, o_spec), ((S, D), F32, o_spec)], (tm, gd),
        extras=[(x, o_spec), (scale, pl.BlockSpec((1, gd), lambda p, q, k: (0, q)))],
        epilogue=lambda acc, xv, sc: (acc, xv + acc * sc))


def _pool_bwd_pre(dx1, ypre, scale):
    S, D = dx1.shape
    tm = _tile(S, (256,))
    n = S // tm

    def body(dx_ref, y_ref, s_ref, dy_ref, ds_ref, acc_ref):
        i = pl.program_id(0)
        dx = dx_ref[...]
        dy_ref[...] = (dx * s_ref[...]).astype(BF16)

        @pl.when(i == 0)
        def _():
            acc_ref[...] = jnp.zeros_like(acc_ref)

        acc_ref[...] += _rows_to_8(dx * y_ref[...])

        @pl.when(i == n - 1)
        def _():
            ds_ref[...] = jnp.sum(acc_ref[...], axis=0, keepdims=True)

    row = pl.BlockSpec((tm, D), lambda i: (i, 0))
    vec = pl.BlockSpec((1, D), lambda i: (0, 0))
    return pl.pallas_call(
        body, name="pool_bwd_pre", grid=(n,),
        in_specs=[row, row, vec], out_specs=[row, vec],
        out_shape=[jax.ShapeDtypeStruct((S, D), BF16), jax.ShapeDtypeStruct((1, D), F32)],
        scratch_shapes=[pltpu.VMEM((8, D), F32)],
        compiler_params=_cparams(("arbitrary",)),
    )(dx1, ypre, scale)


def _pool_mm_bwd_in(dyp, wp):
    S, D = dyp.shape
    G, gd, _ = wp.shape
    tm = _tile(S, (1024, 512))
    spec = pl.BlockSpec((tm, gd), lambda p, q, k: (p, q))
    return _mm(
        "pool_mm_bwd_in", "nt", dyp, wp, (S // tm, G, 1),
        spec, pl.BlockSpec((None, gd, gd), lambda p, q, k: (q, 0, 0)),
        [((S, D), F32, spec)], (tm, gd))


def _pool_mm_bwd_w(d, dyp):
    S, D = d.shape
    G = len(POOL_WINDOWS)
    gd = D // G
    ts = _tile(S, (1024, 512, 256))
    spec = pl.BlockSpec((ts, gd), lambda p, q, k: (k, p))
    return _mm(
        "pool_mm_bwd_w", "tn", d, dyp, (G, 1, S // ts),
        spec, spec,
        [((G, gd, gd), BF16, pl.BlockSpec((None, gd, gd), lambda p, q, k: (p, 0, 0)))], (gd, gd))


def _pool_bwd_post(dd, x, gain, dres):
    S, D = x.shape
    tm = _tile(S, (256,))
    n = S // tm
    gd = D // len(POOL_WINDOWS)
    nh = tm // POOL_HALO

    def body(dd_ref, ddh_ref, x_ref, g_ref, dr_ref, dx_ref, dxb_ref, dg_ref, acc_ref):
        i = pl.program_id(0)
        dd = dd_ref[...]
        halo = ddh_ref[...] * (i < n - 1).astype(F32)
        parts = []
        for gi, w in enumerate(POOL_WINDOWS):
            sl = slice(gi * gd, (gi + 1) * gd)
            ext = jnp.concatenate([dd[:, sl] / _pool_counts(i, tm, w), halo[:, sl] * (1.0 / w)], axis=0)
            k = 1
            while k < w:
                ext = ext + pltpu.roll(ext, tm + POOL_HALO - k, 0)
                k *= 2
            parts.append(ext[:tm] - dd[:, sl])
        dh = jnp.concatenate(parts, axis=1)
        dxn, dgr = _rms_bwd_math(x_ref[...], g_ref[...], dh)
        dx = dr_ref[...] + dxn
        dx_ref[...] = dx
        dxb_ref[...] = dx.astype(BF16)

        @pl.when(i == 0)
        def _():
            acc_ref[...] = jnp.zeros_like(acc_ref)

        acc_ref[...] += _rows_to_8(dgr)

        @pl.when(i == n - 1)
        def _():
            dg_ref[...] = jnp.sum(acc_ref[...], axis=0, keepdims=True)

    row = pl.BlockSpec((tm, D), lambda i: (i, 0))
    vec = pl.BlockSpec((1, D), lambda i: (0, 0))
    nxt = pl.BlockSpec((POOL_HALO, D), lambda i: (jnp.minimum((i + 1) * nh, S // POOL_HALO - 1), 0))
    return pl.pallas_call(
        body, name="pool_bwd_post", grid=(n,),
        in_specs=[row, nxt, row, vec, row],
        out_specs=[row, row, vec],
        out_shape=[jax.ShapeDtypeStruct((S, D), F32), jax.ShapeDtypeStruct((S, D), BF16),
                   jax.ShapeDtypeStruct((1, D), F32)],
        scratch_shapes=[pltpu.VMEM((8, D), F32)],
        compiler_params=_cparams(("arbitrary",)),
    )(dd, dd, x, gain, dres)


def _band_tables():
    i = np.arange(WINDOW)[:, None]
    j = np.arange(2 * WINDOW)[None, :]
    n = np.maximum(WINDOW + i - j, 0)
    max_exact = N_BUCKETS // 2
    nf = np.maximum(n, 1).astype(np.float32)
    large = max_exact + (np.log(nf / max_exact) / np.log(MAX_DISTANCE / max_exact)
                         * (N_BUCKETS - max_exact)).astype(np.int32)
    large = np.minimum(large, N_BUCKETS - 1)
    buckets = np.where(n < max_exact, n, large).astype(np.int32)
    dist = WINDOW + i - j
    in_win = ((dist >= 0) & (dist < WINDOW)).astype(np.int32)
    return buckets, in_win


def _bias_expand(rel_bias):
    H = rel_bias.shape[0]
    buckets, in_win = _band_tables()

    def body(rb_ref, bk_ref, win_ref, o_ref):
        h = pl.program_id(0)
        bk = bk_ref[...]
        acc = jnp.zeros(bk.shape, F32)
        for b in range(N_BUCKETS):
            acc = jnp.where(bk == b, rb_ref[h, b], acc)
        o_ref[...] = jnp.where(win_ref[...] > 0, acc, NEG_INF)

    full = pl.BlockSpec((WINDOW, 2 * WINDOW), lambda h: (0, 0))
    return pl.pallas_call(
        body, name="bias_expand", grid=(H,),
        in_specs=[pl.BlockSpec(memory_space=pltpu.SMEM), full, full],
        out_specs=pl.BlockSpec((None, WINDOW, 2 * WINDOW), lambda h: (h, 0, 0)),
        out_shape=jax.ShapeDtypeStruct((H, WINDOW, 2 * WINDOW), F32),
        compiler_params=_cparams(("parallel",)),
    )(rel_bias, jnp.asarray(buckets), jnp.asarray(in_win))


def _bias_reduce(dbias):
    H = dbias.shape[0]
    buckets, in_win = _band_tables()

    def body(d_ref, bk_ref, win_ref, o_ref):
        bk = jnp.where(win_ref[...] > 0, bk_ref[...], -1)
        d = d_ref[...]
        lane = lax.broadcasted_iota(jnp.int32, (8, LANES), 1)
        out = jnp.zeros((8, LANES), F32)
        for b in range(N_BUCKETS):
            s = jnp.sum(jnp.where(bk == b, d, 0.0))
            out = jnp.where(lane == b, s, out)
        o_ref[...] = out

    full = pl.BlockSpec((WINDOW, 2 * WINDOW), lambda h: (0, 0))
    return pl.pallas_call(
        body, name="bias_reduce", grid=(H,),
        in_specs=[pl.BlockSpec((None, WINDOW, 2 * WINDOW), lambda h: (h, 0, 0)), full, full],
        out_specs=pl.BlockSpec((None, 8, LANES), lambda h: (h, 0, 0)),
        out_shape=jax.ShapeDtypeStruct((H, 8, LANES), F32),
        compiler_params=_cparams(("parallel",)),
    )(dbias, jnp.asarray(buckets), jnp.asarray(in_win))


def _half_rsqrt(t, lo):
    sq = t * t
    s_lo = jnp.sum(jnp.where(lo, sq, 0.0), axis=-1, keepdims=True)
    s_hi = jnp.sum(jnp.where(lo, 0.0, sq), axis=-1, keepdims=True)
    return jnp.where(lo, lax.rsqrt(s_lo * (1.0 / HEAD_DIM) + EPS), lax.rsqrt(s_hi * (1.0 / HEAD_DIM) + EPS))


def _qk_norm_fwd(qkv, qg, kg, n_q, n_kv):
    S, W = qkv.shape
    tm = _tile(S, (256,))
    qw = n_q * HEAD_DIM
    kw = n_kv * HEAD_DIM
    scale = HEAD_DIM ** -0.5

    def body(x_ref, qg_ref, kg_ref, q_ref, k_ref, v_ref):
        lo = lax.broadcasted_iota(jnp.int32, (tm, LANES), 1) < HEAD_DIM
        for t in range(qw // LANES):
            q = x_ref[:, t * LANES:(t + 1) * LANES]
            q_ref[:, t * LANES:(t + 1) * LANES] = (q * _half_rsqrt(q, lo) * qg_ref[...] * scale).astype(BF16)
        for p in range(kw // LANES):
            k = x_ref[:, qw + p * LANES:qw + (p + 1) * LANES]
            kn = k * _half_rsqrt(k, lo) * kg_ref[...]
            v = x_ref[:, qw + kw + p * LANES:qw + kw + (p + 1) * LANES]
            for src, dst in ((kn, k_ref), (v, v_ref)):
                rolled = pltpu.roll(src, HEAD_DIM, 1)
                dst[2 * p] = jnp.where(lo, src, rolled).astype(BF16)
                dst[2 * p + 1] = jnp.where(lo, rolled, src).astype(BF16)

    vec = pl.BlockSpec((1, LANES), lambda i: (0, 0))
    kv = pl.BlockSpec((n_kv, tm, LANES), lambda i: (0, i, 0))
    return pl.pallas_call(
        body, name="qk_norm_fwd", grid=(S // tm,),
        in_specs=[pl.BlockSpec((tm, W), lambda i: (i, 0)), vec, vec],
        out_specs=[pl.BlockSpec((tm, qw), lambda i: (i, 0)), kv, kv],
        out_shape=[jax.ShapeDtypeStruct((S, qw), BF16), jax.ShapeDtypeStruct((n_kv, S, LANES), BF16),
                   jax.ShapeDtypeStruct((n_kv, S, LANES), BF16)],
        compiler_params=_cparams(("parallel",)),
    )(qkv, qg, kg)


def _qk_norm_bwd(qkv, qg, kg, dq, dkc, dkp, dvc, dvp, n_q, n_kv):
    S, W = qkv.shape
    tm = WINDOW
    n = S // tm
    qw = n_q * HEAD_DIM
    kw = n_kv * HEAD_DIM
    scale = HEAD_DIM ** -0.5

    def body(x_ref, qg_ref, kg_ref, dq_ref, dkc_ref, dkp_ref, dvc_ref, dvp_ref, dx_ref, dqg_ref, dkg_ref,
             accq_ref, acck_ref):
        i = pl.program_id(0)
        lo = lax.broadcasted_iota(jnp.int32, (tm, LANES), 1) < HEAD_DIM
        has_next = (i < n - 1).astype(F32)

        @pl.when(i == 0)
        def _():
            accq_ref[...] = jnp.zeros_like(accq_ref)
            acck_ref[...] = jnp.zeros_like(acck_ref)

        def norm_bwd(t, dy, gain):
            r = _half_rsqrt(t, lo)
            xhat = t * r
            dxhat = dy * gain
            prod = dxhat * xhat
            c_lo = jnp.sum(jnp.where(lo, prod, 0.0), axis=-1, keepdims=True) * (1.0 / HEAD_DIM)
            c_hi = jnp.sum(jnp.where(lo, 0.0, prod), axis=-1, keepdims=True) * (1.0 / HEAD_DIM)
            return r * (dxhat - xhat * jnp.where(lo, c_lo, c_hi)), _rows_to_8(dy * xhat)

        for t in range(qw // LANES):
            sl = slice(t * LANES, (t + 1) * LANES)
            dt, dg = norm_bwd(x_ref[:, sl], dq_ref[:, sl] * scale, qg_ref[...])
            dx_ref[:, sl] = dt.astype(BF16)
            accq_ref[...] += dg

        def pair(cur_ref, prev_ref, p):
            folded = []
            for h in (2 * p, 2 * p + 1):
                tot = cur_ref[h] + prev_ref[h] * has_next
                folded.append(tot + pltpu.roll(tot, HEAD_DIM, 1))
            return jnp.where(lo, folded[0], folded[1])

        for p in range(kw // LANES):
            sl = slice(qw + p * LANES, qw + (p + 1) * LANES)
            dt, dg = norm_bwd(x_ref[:, sl], pair(dkc_ref, dkp_ref, p), kg_ref[...])
            dx_ref[:, sl] = dt.astype(BF16)
            acck_ref[...] += dg
            sl = slice(qw + kw + p * LANES, qw + kw + (p + 1) * LANES)
            dx_ref[:, sl] = pair(dvc_ref, dvp_ref, p).astype(BF16)

        @pl.when(i == n - 1)
        def _():
            for acc_ref, o_ref in ((accq_ref, dqg_ref), (acck_ref, dkg_ref)):
                s = jnp.sum(acc_ref[...], axis=0, keepdims=True)
                o_ref[...] = s + pltpu.roll(s, HEAD_DIM, 1)

    vec = pl.BlockSpec((1, LANES), lambda i: (0, 0))
    cur = pl.BlockSpec((n_kv, tm, LANES), lambda i: (0, i, 0))
    nxt = pl.BlockSpec((n_kv, tm, LANES), lambda i: (0, jnp.minimum(i + 1, n - 1), 0))
    return pl.pallas_call(
        body, name="qk_norm_bwd", grid=(n,),
        in_specs=[pl.BlockSpec((tm, W), lambda i: (i, 0)), vec, vec, pl.BlockSpec((tm, qw), lambda i: (i, 0)),
                  cur, nxt, cur, nxt],
        out_specs=[pl.BlockSpec((tm, W), lambda i: (i, 0)), vec, vec],
        out_shape=[jax.ShapeDtypeStruct((S, W), BF16), jax.ShapeDtypeStruct((1, LANES), F32),
                   jax.ShapeDtypeStruct((1, LANES), F32)],
        scratch_shapes=[pltpu.VMEM((8, LANES), F32), pltpu.VMEM((8, LANES), F32)],
        compiler_params=_cparams(("arbitrary",)),
    )(qkv, qg, kg, dq, dkc, dkp, dvc, dvp)


def _attn_specs(n_kv):
    gw = GQA_GROUP * HEAD_DIM
    q = pl.BlockSpec((WINDOW, gw), lambda kh, n: (n, kh))
    cur = pl.BlockSpec((None, WINDOW, LANES), lambda kh, n: (kh, n, 0))
    prev = pl.BlockSpec((None, WINDOW, LANES), lambda kh, n: (kh, jnp.maximum(n - 1, 0), 0))
    bias = pl.BlockSpec((GQA_GROUP, WINDOW, 2 * WINDOW), lambda kh, n: (kh, 0, 0))
    sink = pl.BlockSpec(memory_space=pltpu.SMEM)
    return q, cur, prev, bias, sink


def _stack_heads(x_ref, lo):
    parts = []
    for g in range(GQA_GROUP):
        t = x_ref[:, (g // 2) * LANES:(g // 2 + 1) * LANES]
        parts.append(jnp.where(lo if g % 2 == 0 else jnp.logical_not(lo), t, jnp.zeros_like(t)))
    return jnp.concatenate(parts, axis=0)


def _unstack_heads(v, lo):
    return [jnp.where(lo, v[(2 * p) * WINDOW:(2 * p + 1) * WINDOW], v[(2 * p + 1) * WINDOW:(2 * p + 2) * WINDOW])
            for p in range(GQA_GROUP // 2)]


def _attn_probs(qs, kk, bias_ref, s_ref, kh, has_prev):
    rows = GQA_GROUP * WINDOW
    s = lax.dot_general(qs, kk, _DN["nt"], preferred_element_type=F32) + bias_ref[...].reshape(rows, 2 * WINDOW)
    col = lax.broadcasted_iota(jnp.int32, (rows, 2 * WINDOW), 1)
    s = jnp.where(jnp.logical_or(has_prev, col >= WINDOW), s, NEG_INF)
    sink = jnp.concatenate([jnp.full((WINDOW, 1), s_ref[kh * GQA_GROUP + g], F32) for g in range(GQA_GROUP)], axis=0)
    m = jnp.maximum(jnp.max(s, axis=-1, keepdims=True), sink)
    p = jnp.exp(s - m)
    es = jnp.exp(sink - m)
    den = jnp.sum(p, axis=-1, keepdims=True) + es
    return p / den, es / den


def _attn_fwd(q, kcat, vcat, biasm, sinks):
    S, qw = q.shape
    n_kv = kcat.shape[0]
    qs, cur, prev, bias, sink = _attn_specs(n_kv)

    def body(q_ref, kc_ref, kp_ref, vc_ref, vp_ref, b_ref, s_ref, o_ref):
        kh, n = pl.program_id(0), pl.program_id(1)
        lo = lax.broadcasted_iota(jnp.int32, (WINDOW, LANES), 1) < HEAD_DIM
        kk = jnp.concatenate([kp_ref[...], kc_ref[...]], axis=0)
        vv = jnp.concatenate([vp_ref[...], vc_ref[...]], axis=0)
        pn, _ = _attn_probs(_stack_heads(q_ref, lo), kk, b_ref, s_ref, kh, n > 0)
        o = jnp.dot(pn.astype(BF16), vv, preferred_element_type=F32)
        for p, t in enumerate(_unstack_heads(o, lo)):
            o_ref[:, p * LANES:(p + 1) * LANES] = t.astype(BF16)

    return pl.pallas_call(
        body, name="attn_fwd", grid=(n_kv, S // WINDOW),
        in_specs=[qs, cur, prev, cur, prev, bias, sink],
        out_specs=qs,
        out_shape=jax.ShapeDtypeStruct((S, qw), BF16),
        compiler_params=_cparams(("parallel", "parallel")),
    )(q, kcat, kcat, vcat, vcat, biasm, sinks)


def _attn_bwd(q, kcat, vcat, biasm, sinks, do):
    S, qw = q.shape
    n_kv = kcat.shape[0]
    H = n_kv * GQA_GROUP
    qs, cur, prev, bias, sink = _attn_specs(n_kv)

    def body(q_ref, kc_ref, kp_ref, vc_ref, vp_ref, b_ref, s_ref, do_ref,
             dq_ref, dkc_ref, dkp_ref, dvc_ref, dvp_ref, db_ref, ds_ref):
        kh, n = pl.program_id(0), pl.program_id(1)
        lo = lax.broadcasted_iota(jnp.int32, (WINDOW, LANES), 1) < HEAD_DIM
        kk = jnp.concatenate([kp_ref[...], kc_ref[...]], axis=0)
        vv = jnp.concatenate([vp_ref[...], vc_ref[...]], axis=0)

        @pl.when(n == 0)
        def _():
            db_ref[...] = jnp.zeros_like(db_ref)
            ds_ref[...] = jnp.zeros_like(ds_ref)

        qs_ = _stack_heads(q_ref, lo)
        dos = _stack_heads(do_ref, lo)
        pn, ps = _attn_probs(qs_, kk, b_ref, s_ref, kh, n > 0)
        dp = lax.dot_general(dos, vv, _DN["nt"], preferred_element_type=F32)
        delta = jnp.sum(pn * dp, axis=-1, keepdims=True)
        ds = pn * (dp - delta)
        db_ref[...] += ds.reshape(GQA_GROUP, WINDOW, 2 * WINDOW)
        ds_ref[...] += jnp.zeros((GQA_GROUP, LANES), F32) - jnp.sum((ps * delta).reshape(GQA_GROUP, WINDOW, 1), axis=1)
        dsb = ds.astype(BF16)
        dq = jnp.dot(dsb, kk, preferred_element_type=F32)
        for p, t in enumerate(_unstack_heads(dq, lo)):
            dq_ref[:, p * LANES:(p + 1) * LANES] = t
        dk = lax.dot_general(dsb, qs_, _DN["tn"], preferred_element_type=F32)
        dv = lax.dot_general(pn.astype(BF16), dos, _DN["tn"], preferred_element_type=F32)
        dkp_ref[...] = dk[:WINDOW]
        dkc_ref[...] = dk[WINDOW:]
        dvp_ref[...] = dv[:WINDOW]
        dvc_ref[...] = dv[WINDOW:]

    part = jax.ShapeDtypeStruct((n_kv, S, LANES), F32)
    return pl.pallas_call(
        body, name="attn_bwd", grid=(n_kv, S // WINDOW),
        in_specs=[qs, cur, prev, cur, prev, bias, sink, qs],
        out_specs=[qs, cur, cur, cur, cur, bias, pl.BlockSpec((None, 8, LANES), lambda kh, n: (kh, 0, 0))],
        out_shape=[jax.ShapeDtypeStruct((S, qw), F32), part, part, part, part,
                   jax.ShapeDtypeStruct((H, WINDOW, 2 * WINDOW), F32),
                   jax.ShapeDtypeStruct((n_kv, 8, LANES), F32)],
        compiler_params=_cparams(("parallel", "arbitrary")),
    )(q, kcat, kcat, vcat, vcat, biasm, sinks, do)


def _adamw(name, w, g, m, v):
    shape = w.shape
    C = shape[-1]
    R = int(np.prod(shape[:-1]))
    tr = R
    if R * C * 4 > (1 << 20):
        tr = _tile(R, tuple(t for t in (512, 256, 128, 64, 32, 16, 8) if t * C * 4 <= (3 << 19)))
    c1 = 1.0 - ADAM_B1 ** ADAM_STEP
    c2 = 1.0 - ADAM_B2 ** ADAM_STEP

    def body(w_ref, g_ref, m_ref, v_ref, d_ref, nm_ref, nv_ref):
        gv = g_ref[...]
        nm = ADAM_B1 * m_ref[...] + (1.0 - ADAM_B1) * gv
        nv = ADAM_B2 * v_ref[...] + (1.0 - ADAM_B2) * (gv * gv)
        d_ref[...] = -ADAM_LR * ((nm / c1) / (jnp.sqrt(nv / c2) + ADAM_EPS) + ADAM_WD * w_ref[...])
        nm_ref[...] = nm
        nv_ref[...] = nv

    spec = pl.BlockSpec((tr, C), lambda i: (i, 0))
    outs = pl.pallas_call(
        body, name=name, grid=(R // tr,),
        in_specs=[spec] * 4, out_specs=[spec] * 3,
        out_shape=[jax.ShapeDtypeStruct((R, C), F32)] * 3,
        compiler_params=_cparams(("parallel",)),
    )(*[t.reshape(R, C) for t in (w, g, m, v)])
    return [o.reshape(shape) for o in outs]


def _place():
    return lax.axis_index("x"), lax.axis_index("y"), lax.axis_index("c")


def _other_chips(x, y):
    return [(1 - x, y), (x, 1 - y), (1 - x, 1 - y)]


def _swap_halves(name, grads):
    T = len(grads)

    def body(*refs):
        ins, outs = refs[:T], refs[T:2 * T]
        send, recv = refs[2 * T:]
        x, y, c = _place()

        def copy(t):
            half = grads[t].shape[1] // 2
            src = ins[t].at[:, pl.ds(pl.multiple_of((1 - c) * half, 16), half), :]
            return pltpu.make_async_remote_copy(
                src_ref=src, dst_ref=outs[t], send_sem=send.at[t], recv_sem=recv.at[t],
                device_id=(x, y, 1 - c), device_id_type=MESH)

        for t in range(T):
            copy(t).start()
        for t in range(T):
            copy(t).wait()

    return pl.pallas_call(
        body, name=name,
        in_specs=[_ANY] * T, out_specs=[_ANY] * T,
        out_shape=[jax.ShapeDtypeStruct((g.shape[0], g.shape[1] // 2, g.shape[2]), g.dtype) for g in grads],
        scratch_shapes=[pltpu.SemaphoreType.DMA((T,)), pltpu.SemaphoreType.DMA((T,))],
        compiler_params=pltpu.CompilerParams(has_side_effects=True),
    )(*grads)


def _add_half(name, g, other, c_arr):
    nb, R, C = g.shape
    half = R // 2
    tr = _tile(half, (256, 128, 64, 32, 16))
    n = half // tr

    def body(c_ref, g_ref, o_ref, s_ref):
        s_ref[...] = (g_ref[...].astype(F32) + o_ref[...].astype(F32)).astype(BF16)

    return pl.pallas_call(
        body, name=name,
        grid_spec=pltpu.PrefetchScalarGridSpec(
            num_scalar_prefetch=1, grid=(nb, n),
            in_specs=[pl.BlockSpec((None, tr, C), lambda b, i, c: (b, c[0] * n + i, 0)),
                      pl.BlockSpec((None, tr, C), lambda b, i, c: (b, i, 0))],
            out_specs=pl.BlockSpec((None, tr, C), lambda b, i, c: (b, i, 0))),
        out_shape=jax.ShapeDtypeStruct((nb, half, C), BF16),
        compiler_params=_cparams(("parallel", "parallel")),
    )(c_arr, g, other)


def _in_hbm(arrays):
    return [pltpu.with_memory_space_constraint(a, pltpu.HBM) for a in arrays]


def _chip_copy(src, dst, send, recv, chip, c):
    return pltpu.make_async_remote_copy(src_ref=src, dst_ref=dst, send_sem=send, recv_sem=recv,
                                        device_id=(chip[0], chip[1], c), device_id_type=MESH)


def _cast_slot(name, w, l, chip_arr, dtype):
    _, R, C = w.shape
    tr = _tile(R, tuple(t for t in (1024, 512, 256, 128, 64, 32, 16) if t * C * 4 <= (1 << 21)))

    def body(chip_ref, w_ref, o_ref):
        o_ref[...] = w_ref[...].astype(dtype)

    return pl.pallas_call(
        body, name=name,
        grid_spec=pltpu.PrefetchScalarGridSpec(
            num_scalar_prefetch=1, grid=(R // tr,),
            in_specs=[pl.BlockSpec((None, tr, C), lambda i, chip: (l, i, 0))],
            out_specs=pl.BlockSpec((None, tr, C), lambda i, chip: (chip[0], i, 0))),
        out_shape=jax.ShapeDtypeStruct((N_CHIPS, R, C), dtype),
        compiler_params=_cparams(("parallel",)),
    )(chip_arr, w)


def _gather_start(groups):
    sizes = [len(g) for g in groups]
    flat = [b for g in groups for b in g]
    n, G = len(flat), len(groups)

    def body(*refs):
        bufs = refs[:n]
        sems = refs[n:n + 2 * G]
        token = refs[-1]
        x, y, c = _place()
        chips = _other_chips(x, y)
        me = 2 * x + y
        k = 0
        for l in range(G):
            for t in range(sizes[l]):
                for j in range(3):
                    _chip_copy(bufs[k].at[me], bufs[k].at[me], sems[2 * l].at[3 * t + j], sems[2 * l + 1].at[3 * t + j],
                               chips[j], c).start()
                k += 1
        token[...] = jnp.zeros_like(token)

    sem_shapes = []
    for s in sizes:
        sem_shapes += [pltpu.SemaphoreType.DMA((3 * s,)), pltpu.SemaphoreType.DMA((3 * s,))]
    outs = pl.pallas_call(
        body, name="gather_start",
        out_shape=(*sem_shapes, *[pltpu.HBM(b.shape, b.dtype) for b in flat], jax.ShapeDtypeStruct((8, LANES), F32)),
        in_specs=[_HBM] * n,
        out_specs=(*[_SEM] * (2 * G), *[_HBM] * n, pl.BlockSpec(memory_space=pltpu.VMEM)),
        input_output_aliases={t: 2 * G + t for t in range(n)},
        compiler_params=pltpu.CompilerParams(has_side_effects=_DATAFLOW),
    )(*_in_hbm(flat))
    res, k = [], 2 * G
    for l in range(G):
        res.append((outs[2 * l], outs[2 * l + 1], list(outs[k:k + sizes[l]])))
        k += sizes[l]
    return res, outs[-1]


def _gather_wait(name, send, recv, bufs, after):
    T = len(bufs)

    def body(*refs):
        ins = refs[:T]
        send_ref, recv_ref = refs[T], refs[T + 1]
        x, y, c = _place()
        chips = _other_chips(x, y)
        me = 2 * x + y
        for t in range(T):
            for j in range(3):
                cp = _chip_copy(ins[t].at[me], ins[t].at[2 * chips[j][0] + chips[j][1]], send_ref.at[3 * t + j],
                                recv_ref.at[3 * t + j], chips[j], c)
                cp.wait_send()
                cp.wait_recv()

    return pl.pallas_call(
        body, name=name,
        out_shape=[pltpu.HBM(b.shape, b.dtype) for b in bufs],
        in_specs=[_HBM] * T + [_SEM, _SEM, _ANY],
        out_specs=[_HBM] * T,
        input_output_aliases={t: t for t in range(T)},
        compiler_params=pltpu.CompilerParams(has_side_effects=_DATAFLOW),
    )(*bufs, send, recv, after)


def _scatter_start(name, sums):
    T = len(sums)
    lands = [lax.empty((3,) + s.shape[1:], s.dtype) for s in sums]

    def body(*refs):
        srcs, lnds = refs[:T], refs[T:2 * T]
        send, recv = refs[2 * T], refs[2 * T + 1]
        token = refs[-1]
        x, y, c = _place()
        chips = _other_chips(x, y)
        for t in range(T):
            for j in range(3):
                _chip_copy(srcs[t].at[2 * chips[j][0] + chips[j][1]], lnds[t].at[j], send.at[3 * t + j],
                           recv.at[3 * t + j], chips[j], c).start()
        token[...] = jnp.zeros_like(token)

    both = list(sums) + lands
    outs = pl.pallas_call(
        body, name=name,
        out_shape=(pltpu.SemaphoreType.DMA((3 * T,)), pltpu.SemaphoreType.DMA((3 * T,)),
                   *[pltpu.HBM(b.shape, b.dtype) for b in both], jax.ShapeDtypeStruct((8, LANES), F32)),
        in_specs=[_HBM] * (2 * T),
        out_specs=(_SEM, _SEM, *[_HBM] * (2 * T), pl.BlockSpec(memory_space=pltpu.VMEM)),
        input_output_aliases={t: 2 + t for t in range(2 * T)},
        compiler_params=pltpu.CompilerParams(has_side_effects=_DATAFLOW),
    )(*_in_hbm(both))
    return outs[0], outs[1], list(outs[2:2 + T]), list(outs[2 + T:2 + 2 * T]), outs[-1]


def _scatter_wait(name, send, recv, sums, lands, after):
    T = len(sums)

    def body(*refs):
        srcs, lnds = refs[:T], refs[T:2 * T]
        send_ref, recv_ref = refs[2 * T], refs[2 * T + 1]
        x, y, c = _place()
        chips = _other_chips(x, y)
        for t in range(T):
            for j in range(3):
                cp = _chip_copy(srcs[t].at[2 * chips[j][0] + chips[j][1]], lnds[t].at[j], send_ref.at[3 * t + j],
                                recv_ref.at[3 * t + j], chips[j], c)
                cp.wait_send()
                cp.wait_recv()

    both = list(sums) + list(lands)
    outs = pl.pallas_call(
        body, name=name,
        out_shape=[pltpu.HBM(b.shape, b.dtype) for b in both],
        in_specs=[_HBM] * (2 * T) + [_SEM, _SEM, _ANY],
        out_specs=[_HBM] * (2 * T),
        input_output_aliases={t: t for t in range(2 * T)},
        compiler_params=pltpu.CompilerParams(has_side_effects=_DATAFLOW),
    )(*both, send, recv, after)
    return list(outs[:T]), list(outs[T:])


def _sum_parts(name, sums, land, chip_arr, c_arr):
    _, R2, C = sums.shape
    tr = _tile(R2, (256, 128, 64, 32, 16))

    def body(chip_ref, c_ref, s_ref, l_ref, o_ref):
        acc = s_ref[...].astype(F32)
        for j in range(3):
            acc = acc + l_ref[j].astype(F32)
        o_ref[...] = acc

    return pl.pallas_call(
        body, name=name,
        grid_spec=pltpu.PrefetchScalarGridSpec(
            num_scalar_prefetch=2, grid=(R2 // tr,),
            in_specs=[pl.BlockSpec((None, tr, C), lambda i, chip, c: (chip[0], i, 0)),
                      pl.BlockSpec((3, tr, C), lambda i, chip, c: (0, i, 0))],
            out_specs=pl.BlockSpec((None, tr, C), lambda i, chip, c: (c[0], i, 0))),
        out_shape=jax.ShapeDtypeStruct((2, R2, C), F32),
        compiler_params=_cparams(("parallel",)),
    )(chip_arr, c_arr, sums, land)


def _join_halves(name, bufs):
    T = len(bufs)

    def body(*refs):
        outs = refs[T:2 * T]
        send, recv = refs[2 * T:]
        x, y, c = _place()

        def copy(t, half):
            return pltpu.make_async_remote_copy(
                src_ref=outs[t].at[c], dst_ref=outs[t].at[half], send_sem=send.at[t], recv_sem=recv.at[t],
                device_id=(x, y, 1 - c), device_id_type=MESH)

        for t in range(T):
            copy(t, c).start()
        for t in range(T):
            copy(t, 1 - c).wait_recv()
        for t in range(T):
            copy(t, c).wait_send()

    return pl.pallas_call(
        body, name=name,
        in_specs=[_ANY] * T, out_specs=[_ANY] * T,
        out_shape=[jax.ShapeDtypeStruct(b.shape, b.dtype) for b in bufs],
        input_output_aliases={t: t for t in range(T)},
        scratch_shapes=[pltpu.SemaphoreType.DMA((T,)), pltpu.SemaphoreType.DMA((T,))],
        compiler_params=pltpu.CompilerParams(has_side_effects=True),
    )(*bufs)


def _rs_begin(tag, grads, c_arr):
    got = _swap_halves("rs_swap_" + tag, grads)
    sums = [_add_half("rs_add_%s_%d" % (tag, t), g, o, c_arr) for t, (g, o) in enumerate(zip(grads, got))]
    return _scatter_start("rs_scatter_start_" + tag, sums)


def _rs_finish(tag, state, shapes, chip_arr, c_arr, after=None):
    send, recv, sums, lands, token = state
    sums, lands = _scatter_wait("rs_scatter_wait_" + tag, send, recv, sums, lands, token if after is None else after)
    parts = [_sum_parts("rs_sum_%s_%d" % (tag, t), s, l, chip_arr, c_arr) for t, (s, l) in enumerate(zip(sums, lands))]
    joined = _join_halves("rs_join_" + tag, parts)
    return [j.reshape(sh[1], sh[2]) for j, sh in zip(joined, shapes)]


def _all_reduce_small(v):
    R = v.shape[0]

    def body(v_ref, o_ref, buf, send, recv):
        x, y, c = _place()
        me = 4 * x + 2 * y + c
        buf[me] = v_ref[...]

        def copy(k, slot):
            to = (x ^ ((k >> 2) & 1), y ^ ((k >> 1) & 1), c ^ (k & 1))
            return pltpu.make_async_remote_copy(
                src_ref=v_ref, dst_ref=buf.at[slot], send_sem=send.at[k - 1], recv_sem=recv.at[k - 1],
                device_id=to, device_id_type=MESH)

        for k in range(1, 8):
            copy(k, me).start()
        for k in range(1, 8):
            copy(k, me ^ k).wait_recv()
        for k in range(1, 8):
            copy(k, me).wait_send()
        acc = buf[0]
        for s in range(1, 8):
            acc = acc + buf[s]
        o_ref[...] = acc

    vm = pl.BlockSpec(memory_space=pltpu.VMEM)
    return pl.pallas_call(
        body, name="all_reduce_small",
        in_specs=[vm], out_specs=vm,
        out_shape=jax.ShapeDtypeStruct((R, LANES), F32),
        scratch_shapes=[pltpu.VMEM((8, R, LANES), F32), pltpu.SemaphoreType.DMA((7,)), pltpu.SemaphoreType.DMA((7,))],
        compiler_params=pltpu.CompilerParams(has_side_effects=True, vmem_limit_bytes=VMEM_LIMIT),
    )(v)


def _local_step(x, tgt, small, depth, nb, weight, convw, pscale, on_grads):
    S, D = x.shape
    n_q = D // HEAD_DIM
    n_kv = n_q // GQA_GROUP
    Fh = convw[0].shape[2]

    def dup(gain):
        return jnp.tile(gain, 2).reshape(1, LANES)

    biasm = _bias_expand(small["rel_bias"])
    saved = []
    for i in range(depth):
        j = i // 2
        w = {}
        st = {"x0": x, "w": w}
        gm = small["norm_mix"][i].reshape(1, D)
        if i % 2 == 0:
            h = _rms_fwd("rms_mix_fwd", x, gm)
            w["wqkv"] = weight(i, "wqkv", h)
            qkv = _mm_colblk("qkv_fwd", h, w["wqkv"], F32)
            qh, kcat, vcat = _qk_norm_fwd(qkv, dup(small["attn_q_gain"][j]), dup(small["attn_k_gain"][j]), n_q, n_kv)
            o = _attn_fwd(qh, kcat, vcat, biasm, small["attn_sinks"][j])
            w["wo"] = weight(i, "wo", o)
            x = _mm_rowblk_res("wo_fwd", o, w["wo"], x)
            st.update(h=h, qkv=qkv, qh=qh, kcat=kcat, vcat=vcat, o=o)
        else:
            d = _pool_fwd(x, gm)
            w["wp"] = weight(i, "wp", d)
            ypre, x = _pool_mm_fwd(d, w["wp"], x, pscale[j])
            st.update(d=d, ypre=ypre)
        st["x1"] = x
        h2 = _rms_fwd("rms_ffn_fwd", x, small["norm_ffn"][i].reshape(1, D))
        w["wup"] = weight(i, "wup", h2)
        cb = small["ffn_conv_b"][i].reshape(2, 1, Fh)
        u, uc, a = _up_act_fwd(h2, w["wup"], convw[i], cb)
        w["wdn"] = weight(i, "wdn", a)
        x = _mm_rowblk_res("down_fwd", a, w["wdn"], x)
        st.update(h2=h2, u=u, uc=uc, a=a)
        saved.append(st)

    dx, dxb, loss_part = _loss_head(x, tgt)

    big = [dict() for _ in range(depth)]
    sg = {k: [None] * depth for k in ("norm_mix", "norm_ffn", "conv_w", "conv_b")}
    sg.update({k: [None] * (depth // 2) for k in ("q_gain", "k_gain", "sinks", "pool_scale")})
    dbias_tot = None
    before = None
    for i in reversed(range(depth)):
        j = i // 2
        st = saved[i]
        w = st["w"]
        duc = _down_act_bwd(dxb, w["wdn"], st["uc"], after=before)
        wdn_grad = _mm_tn_rowblk("down_bwd_w", st["a"], dxb, nb)
        du, dh2, dcwb = _up_bwd_in(duc, st["u"], convw[i], w["wup"])
        dcwb = jnp.transpose(dcwb, (1, 0, 2)).reshape(8, 2 * Fh)
        sg["conv_w"][i] = dcwb[0:3]
        sg["conv_b"][i] = dcwb[3]
        wup_grad = _mm_tn_colblk("up_bwd_w", st["h2"], du, nb, split2=True)
        before = on_grads(i, "ffn", dict(wdn=wdn_grad, wup=wup_grad), dx)
        dx, dxb, dg = _rms_bwd("rms_ffn_bwd", st["x1"], small["norm_ffn"][i].reshape(1, D), dh2, dx, after=before)
        sg["norm_ffn"][i] = dg.reshape(D)
        gm = small["norm_mix"][i].reshape(1, D)
        if i % 2 == 0:
            do = _mm_nt_rowblk("wo_bwd_in", dxb, w["wo"], BF16)
            big[i]["wo"] = _mm_tn_rowblk("wo_bwd_w", st["o"], dxb, nb)
            dq, dkc, dkp, dvc, dvp, dbias, dsink = _attn_bwd(
                st["qh"], st["kcat"], st["vcat"], biasm, small["attn_sinks"][j], do)
            dbias_tot = dbias if dbias_tot is None else dbias_tot + dbias
            sg["sinks"][j] = dsink[:, :, 0].reshape(n_q)
            dqkv, dqg, dkg = _qk_norm_bwd(st["qkv"], dup(small["attn_q_gain"][j]), dup(small["attn_k_gain"][j]),
                                          dq, dkc, dkp, dvc, dvp, n_q, n_kv)
            sg["q_gain"][j] = dqg[0, :HEAD_DIM]
            sg["k_gain"][j] = dkg[0, :HEAD_DIM]
            big[i]["wqkv"] = _mm_tn_colblk("qkv_bwd_w", st["h"], dqkv, nb)
            dh = _mm_nt_colblk("qkv_bwd_in", dqkv, w["wqkv"])
            dx, dxb, dg = _rms_bwd("rms_mix_bwd", st["x0"], gm, dh, dx)
        else:
            dyp, dsc = _pool_bwd_pre(dx, st["ypre"], pscale[j])
            sg["pool_scale"][j] = dsc.reshape(D)
            big[i]["wp"] = _pool_mm_bwd_w(st["d"], dyp)
            dd = _pool_mm_bwd_in(dyp, w["wp"])
            dx, dxb, dg = _pool_bwd_post(dd, st["x0"], gm, dx)
        sg["norm_mix"][i] = dg.reshape(D)
        before = on_grads(i, "mix", big[i], dx)
    sg["rel_bias"] = _bias_reduce(dbias_tot)[:, 0, :N_BUCKETS]
    return loss_part, dx, sg


_SMALL_ORDER = ("norm_mix", "norm_ffn", "rel_bias", "q_gain", "k_gain", "sinks", "conv_b", "conv_w", "pool_scale")


def kernel(x, norm_mix, norm_ffn, rel_bias, attn_w_qkv, attn_q_gain, attn_k_gain, attn_sinks, attn_w_o, pool_w, pool_scale, ffn_w_up, ffn_conv_w, ffn_conv_b, ffn_w_down, loss_target, m_norm_mix, m_norm_ffn, m_rel_bias, m_attn_w_qkv, m_attn_q_gain, m_attn_k_gain, m_attn_sinks, m_attn_w_o, m_pool_w, m_pool_scale, m_ffn_w_up, m_ffn_conv_w, m_ffn_conv_b, m_ffn_w_down, v_norm_mix, v_norm_ffn, v_rel_bias, v_attn_w_qkv, v_attn_q_gain, v_attn_k_gain, v_attn_sinks, v_attn_w_o, v_pool_w, v_pool_scale, v_ffn_w_up, v_ffn_conv_w, v_ffn_conv_b, v_ffn_w_down):
    _, S, D = x.shape
    depth = ffn_w_up.shape[0]
    n_attn, n_pool = attn_w_qkv.shape[0], pool_w.shape[0]
    nb = N_CHIPS
    nc_up = ffn_w_up.shape[2]
    Fh = nc_up * nb // 2
    cx, cy, cc = _place()
    chip = 2 * cx + cy
    c_arr = jnp.reshape(cc, (1,)).astype(jnp.int32)
    chip_arr = jnp.reshape(chip, (1,)).astype(jnp.int32)

    pool_w3 = pool_w.reshape(n_pool, -1, pool_w.shape[-1])
    order = [("convw", ffn_conv_w.reshape(1, -1, nc_up), 0, F32), ("pscale", pool_scale[None], 0, F32)]
    for i in range(depth):
        j = i // 2
        mixer = ([("wqkv", attn_w_qkv, j), ("wo", attn_w_o, j)] if i % 2 == 0 else [("wp", pool_w3, j)])
        order += [((i, k), w, l, BF16) for k, w, l in mixer + [("wup", ffn_w_up, i), ("wdn", ffn_w_down, i)]]
    slots = [_cast_slot("slot_%d" % n, w, l, chip_arr, dt) for n, (_, w, l, dt) in enumerate(order)]
    started, token = _gather_start([[b] for b in slots])
    index = {key: n for n, (key, _, _, _) in enumerate(order)}

    def gathered(key, after):
        n = index[key]
        send, recv, bufs = started[n]
        return _gather_wait("gather_wait_%d" % n, send, recv, bufs, after)[0]

    convw_g = gathered("convw", token).reshape(nb, depth, 3, nc_up)
    pscale_g = gathered("pscale", token)
    convw = [jnp.transpose(convw_g[:, i], (1, 0, 2)).reshape(3, 2, Fh).transpose(1, 0, 2) for i in range(depth)]
    pscale = [pscale_g[:, j].reshape(1, D) for j in range(n_pool)]
    small = dict(norm_mix=norm_mix, norm_ffn=norm_ffn, rel_bias=rel_bias, attn_q_gain=attn_q_gain,
                 attn_k_gain=attn_k_gain, attn_sinks=attn_sinks, ffn_conv_b=ffn_conv_b)

    G, gd = pool_w.shape[1], pool_w.shape[3]

    def weight(i, name, after):
        w = gathered((i, name), after)
        if name == "wp":
            w = jnp.transpose(w.reshape(nb, G, gd // nb, gd), (1, 0, 2, 3)).reshape(G, gd, gd)
        return w

    red = {k: [None] * (n_attn if k in ("wqkv", "wo") else n_pool if k == "wp" else depth)
           for k in ("wqkv", "wo", "wp", "wup", "wdn")}
    pending = []

    def finish(after):
        i, tag, names, shapes, state = pending.pop()
        outs = _rs_finish("%d%s" % (i, tag), state, shapes, chip_arr, c_arr, after)
        for k, o in zip(names, outs):
            red[k][i // 2 if k in ("wqkv", "wo", "wp") else i] = o

    def on_grads(i, tag, grads, dx_i):
        if pending:
            finish(dx_i)
        names = sorted(grads)
        flat = []
        for k in names:
            g = grads[k]
            if k == "wp":
                g = jnp.transpose(g.reshape(G, nb, gd // nb, gd), (1, 0, 2, 3))
            flat.append(g.reshape(nb, -1, g.shape[-1]))
        state = _rs_begin("%d%s" % (i, tag), flat, c_arr)
        pending.append((i, tag, names, [f.shape for f in flat], state))
        return state[-1]

    loss_part, dx, sg = _local_step(x[0], loss_target[0], small, depth, nb, weight, convw, pscale, on_grads)
    finish(None)
    loss = lax.psum(jnp.sum(loss_part), ("x", "y", "c"))

    g_wqkv = jnp.stack(red["wqkv"])
    g_wo = jnp.stack(red["wo"])
    g_wp = jnp.stack(red["wp"]).reshape(pool_w.shape)
    g_wup = jnp.stack(red["wup"])
    g_wdn = jnp.stack(red["wdn"])

    parts = [jnp.stack(sg[k]) if isinstance(sg[k], list) else sg[k] for k in _SMALL_ORDER]
    sizes = [int(np.prod(p.shape)) for p in parts]
    total = sum(sizes)
    rows = -(-total // (8 * LANES)) * 8
    packed = jnp.concatenate([p.reshape(-1) for p in parts] + [jnp.zeros((rows * LANES - total,), F32)])
    summed = _all_reduce_small(packed.reshape(rows, LANES)).reshape(-1)
    sm, off = {}, 0
    for k, p, n in zip(_SMALL_ORDER, parts, sizes):
        sm[k] = summed[off:off + n].reshape(p.shape)
        off += n
    g_convw = lax.dynamic_slice_in_dim(sm["conv_w"], chip * nc_up, nc_up, axis=2)
    pc = pool_scale.shape[1]
    g_pscale = lax.dynamic_slice_in_dim(sm["pool_scale"], chip * pc, pc, axis=1)

    grads = [sm["norm_mix"], sm["norm_ffn"], sm["rel_bias"], g_wqkv, sm["q_gain"], sm["k_gain"], sm["sinks"], g_wo,
             g_wp, g_pscale, g_wup, g_convw, sm["conv_b"], g_wdn]
    ws = [norm_mix, norm_ffn, rel_bias, attn_w_qkv, attn_q_gain, attn_k_gain, attn_sinks, attn_w_o, pool_w, pool_scale,
          ffn_w_up, ffn_conv_w, ffn_conv_b, ffn_w_down]
    ms = [m_norm_mix, m_norm_ffn, m_rel_bias, m_attn_w_qkv, m_attn_q_gain, m_attn_k_gain, m_attn_sinks, m_attn_w_o,
          m_pool_w, m_pool_scale, m_ffn_w_up, m_ffn_conv_w, m_ffn_conv_b, m_ffn_w_down]
    vs = [v_norm_mix, v_norm_ffn, v_rel_bias, v_attn_w_qkv, v_attn_q_gain, v_attn_k_gain, v_attn_sinks, v_attn_w_o,
          v_pool_w, v_pool_scale, v_ffn_w_up, v_ffn_conv_w, v_ffn_conv_b, v_ffn_w_down]
    deltas, new_m, new_v = [], [], []
    for idx, (w, g, m, v) in enumerate(zip(ws, grads, ms, vs)):
        d, nm, nv = _adamw("adamw_%d" % idx, w, g, m, v)
        deltas.append(d), new_m.append(nm), new_v.append(nv)
    return (loss, dx.reshape(1, S, D), *grads, *deltas, *new_m, *new_v)
```

```python
import functools

import numpy as np
import jax
import jax.numpy as jnp
from jax import lax
from jax.experimental import pallas as pl
from jax.experimental.pallas import tpu as pltpu

F32 = jnp.float32
BF16 = jnp.bfloat16
MESH = pl.DeviceIdType.MESH
_ANY = pl.BlockSpec(memory_space=pl.ANY)
_HBM = pl.BlockSpec(memory_space=pltpu.HBM)
_SEM = pl.BlockSpec(memory_space=pltpu.SEMAPHORE)
_DATAFLOW = pltpu.SideEffectType.DATAFLOW_SIDE_EFFECTING

N_CHIPS = 4
HEAD_DIM = 64
GQA_GROUP = 8
WINDOW = 128
N_BUCKETS = 32
MAX_DISTANCE = 128
POOL_WINDOWS = (2, 4, 8, 16)
POOL_HALO = 16
EPS = 1e-6
NEG_INF = -1e30
LANES = 128
VMEM_LIMIT = 56 * 1024 * 1024

ADAM_LR = 0.001
ADAM_B1 = 0.9
ADAM_B2 = 0.999
ADAM_EPS = 1e-08
ADAM_WD = 0.01
ADAM_STEP = 10


def _tile(n, prefs):
    for p in prefs:
        if p <= n and n % p == 0:
            return p
    return n


def _cparams(sem):
    return pltpu.CompilerParams(dimension_semantics=sem, vmem_limit_bytes=VMEM_LIMIT)


_DN = {
    "nn": (((1,), (0,)), ((), ())),
    "nt": (((1,), (1,)), ((), ())),
    "tn": (((0,), (0,)), ((), ())),
}


def _mm(name, kind, a, b, grid, a_spec, b_spec, outs, acc_shape, extras=(), epilogue=None, after=None):
    nk = grid[2]
    n_ex, n_out = len(extras), len(outs)
    order = [] if after is None else [after]

    def body(a_ref, b_ref, *rest):
        ex_refs = rest[:n_ex]
        rest = rest[n_ex + len(order):]
        out_refs = rest[:n_out]
        acc_ref = rest[n_out] if nk > 1 else None
        part = lax.dot_general(a_ref[...], b_ref[...], _DN[kind], preferred_element_type=F32)

        def finish(val):
            vals = epilogue(val, *[r[...] for r in ex_refs]) if epilogue else (val,)
            for r, v in zip(out_refs, vals):
                r[...] = v.astype(r.dtype)

        if nk == 1:
            finish(part)
        else:
            k = pl.program_id(2)

            @pl.when(k == 0)
            def _():
                acc_ref[...] = part

            @pl.when(k > 0)
            def _():
                acc_ref[...] += part

            @pl.when(k == nk - 1)
            def _():
                finish(acc_ref[...])

    res = pl.pallas_call(
        body,
        name=name,
        grid=grid,
        in_specs=[a_spec, b_spec] + [s for _, s in extras] + [_ANY] * len(order),
        out_specs=[s for _, _, s in outs],
        out_shape=[jax.ShapeDtypeStruct(sh, dt) for sh, dt, _ in outs],
        scratch_shapes=[pltpu.VMEM(acc_shape, F32)] if nk > 1 else [],
        compiler_params=_cparams(("parallel", "parallel", "arbitrary")),
    )(a, b, *[e for e, _ in extras], *order)
    return res if n_out > 1 else res[0]


def _mm_colblk(name, a, wg, out_dtype):
    S, K = a.shape
    nb, _, nc = wg.shape
    tm = _tile(S, (512,))
    tn = _tile(nc, (1408, 1024, 640, 512, 256, 128))
    npb = nc // tn
    return _mm(
        name, "nn", a, wg, (nb * npb, S // tm, 1),
        pl.BlockSpec((tm, K), lambda p, q, k: (q, 0)),
        pl.BlockSpec((None, K, tn), lambda p, q, k: (p // npb, 0, p % npb)),
        [((S, nb * nc), out_dtype, pl.BlockSpec((tm, tn), lambda p, q, k: (q, p)))], (tm, tn))


def _mm_rowblk_res(name, a, wg, res):
    S = a.shape[0]
    nb, kc, N = wg.shape
    tm = _tile(S, (512,))
    tn = _tile(N, (2048, 1024, 512))
    tk = _tile(kc, (1408, 512, 256, 128))
    kpb = kc // tk
    o_spec = pl.BlockSpec((tm, tn), lambda p, q, k: (p, q))
    return _mm(
        name, "nn", a, wg, (S // tm, N // tn, nb * kpb),
        pl.BlockSpec((tm, tk), lambda p, q, k: (p, k)),
        pl.BlockSpec((None, tk, tn), lambda p, q, k: (k // kpb, k % kpb, q)),
        [((S, N), F32, o_spec)], (tm, tn),
        extras=[(res, o_spec)], epilogue=lambda acc, r: (r + acc,))


def _mm_nt_rowblk(name, g, wg, out_dtype, after=None):
    S, N = g.shape
    nb, kc, _ = wg.shape
    tm = _tile(S, (512,))
    tn = _tile(kc, (1408, 512, 256, 128))
    kpb = kc // tn
    return _mm(
        name, "nt", g, wg, (nb * kpb, S // tm, 1),
        pl.BlockSpec((tm, N), lambda p, q, k: (q, 0)),
        pl.BlockSpec((None, tn, N), lambda p, q, k: (p // kpb, p % kpb, 0)),
        [((S, nb * kc), out_dtype, pl.BlockSpec((tm, tn), lambda p, q, k: (q, p)))], (tm, tn), after=after)


def _mm_nt_colblk(name, g, wg):
    S = g.shape[0]
    nb, K, nc = wg.shape
    tm = _tile(S, (512,))
    tn = _tile(K, (2048, 1024))
    tk = _tile(nc, (1408, 640, 512, 256, 128))
    npb = nc // tk
    return _mm(
        name, "nt", g, wg, (S // tm, K // tn, nb * npb),
        pl.BlockSpec((tm, tk), lambda p, q, k: (p, k)),
        pl.BlockSpec((None, tn, tk), lambda p, q, k: (k // npb, q, k % npb)),
        [((S, K), F32, pl.BlockSpec((tm, tn), lambda p, q, k: (p, q)))], (tm, tn))


def _mm_tn_colblk(name, a, g, nb, split2=False):
    S, K = a.shape
    ntot = g.shape[-1] * (2 if split2 else 1)
    nc = ntot // nb
    ti = _tile(K, (1024,))
    tn = _tile(nc, (1408, 640, 512, 256, 128))
    ts = _tile(S, (1024, 512, 256))
    npb = nc // tn
    half = nb * npb // 2
    if split2:
        b_spec = pl.BlockSpec((None, ts, tn), lambda p, q, k: (q // half, k, q % half))
    else:
        b_spec = pl.BlockSpec((ts, tn), lambda p, q, k: (k, q))
    return _mm(
        name, "tn", a, g, (K // ti, nb * npb, S // ts),
        pl.BlockSpec((ts, ti), lambda p, q, k: (k, p)),
        b_spec,
        [((nb, K, nc), BF16, pl.BlockSpec((None, ti, tn), lambda p, q, k: (q // npb, p, q % npb)))], (ti, tn))


def _mm_tn_rowblk(name, a, g, nb):
    S, ktot = a.shape
    N = g.shape[1]
    kc = ktot // nb
    ti = _tile(kc, (1408, 512, 256, 128))
    tj = _tile(N, (1024,))
    ts = _tile(S, (1024, 512, 256))
    ipb = kc // ti
    return _mm(
        name, "tn", a, g, (nb * ipb, N // tj, S // ts),
        pl.BlockSpec((ts, ti), lambda p, q, k: (k, p)),
        pl.BlockSpec((ts, tj), lambda p, q, k: (k, q)),
        [((nb, kc, N), BF16, pl.BlockSpec((None, ti, tj), lambda p, q, k: (p // ipb, p % ipb, q)))], (ti, tj))


def _rms_fwd(name, x, gain):
    S, D = x.shape
    tm = _tile(S, (512,))

    def body(x_ref, g_ref, o_ref):
        xv = x_ref[...]
        r = lax.rsqrt(jnp.mean(xv * xv, axis=-1, keepdims=True) + EPS)
        o_ref[...] = (xv * r * g_ref[...]).astype(o_ref.dtype)

    return pl.pallas_call(
        body, name=name, grid=(S // tm,),
        in_specs=[pl.BlockSpec((tm, D), lambda i: (i, 0)), pl.BlockSpec((1, D), lambda i: (0, 0))],
        out_specs=pl.BlockSpec((tm, D), lambda i: (i, 0)),
        out_shape=jax.ShapeDtypeStruct((S, D), BF16),
        compiler_params=_cparams(("parallel",)),
    )(x, gain)


def _rms_bwd_math(xv, gain, dh):
    r = lax.rsqrt(jnp.mean(xv * xv, axis=-1, keepdims=True) + EPS)
    xhat = xv * r
    dxhat = dh * gain
    c = jnp.mean(dxhat * xhat, axis=-1, keepdims=True)
    return r * (dxhat - xhat * c), dh * xhat


def _rows_to_8(v):
    tm, C = v.shape
    return jnp.sum(v.reshape(tm // 8, 8, C), axis=0)


def _rms_bwd(name, x, gain, dh, dres, after=None):
    S, D = x.shape
    tm = _tile(S, (256,))
    n = S // tm
    order = [] if after is None else [after]

    def body(x_ref, g_ref, dh_ref, dr_ref, *rest):
        dx_ref, dxb_ref, dg_ref, acc_ref = rest[len(order):]
        i = pl.program_id(0)
        dxn, dgr = _rms_bwd_math(x_ref[...], g_ref[...], dh_ref[...])
        dx = dr_ref[...] + dxn
        dx_ref[...] = dx
        dxb_ref[...] = dx.astype(BF16)

        @pl.when(i == 0)
        def _():
            acc_ref[...] = jnp.zeros_like(acc_ref)

        acc_ref[...] += _rows_to_8(dgr)

        @pl.when(i == n - 1)
        def _():
            dg_ref[...] = jnp.sum(acc_ref[...], axis=0, keepdims=True)

    row = pl.BlockSpec((tm, D), lambda i: (i, 0))
    vec = pl.BlockSpec((1, D), lambda i: (0, 0))
    return pl.pallas_call(
        body, name=name, grid=(n,),
        in_specs=[row, vec, row, row] + [_ANY] * len(order),
        out_specs=[row, row, vec],
        out_shape=[jax.ShapeDtypeStruct((S, D), F32), jax.ShapeDtypeStruct((S, D), BF16),
                   jax.ShapeDtypeStruct((1, D), F32)],
        scratch_shapes=[pltpu.VMEM((8, D), F32)],
        compiler_params=_cparams(("arbitrary",)),
    )(x, gain, dh, dres, *order)


def _loss_head(y, tgt):
    S, D = y.shape
    tm = _tile(S, (256,))
    n = S // tm

    def body(y_ref, t_ref, dy_ref, dyb_ref, l_ref):
        i = pl.program_id(0)
        e = y_ref[...] - t_ref[...]
        dy = e * (1.0 / D)
        dy_ref[...] = dy
        dyb_ref[...] = dy.astype(BF16)

        @pl.when(i == 0)
        def _():
            l_ref[...] = jnp.zeros_like(l_ref)

        sq = _rows_to_8(e * e)
        part = sq[:, 0:LANES]
        for t in range(1, D // LANES):
            part = part + sq[:, t * LANES:(t + 1) * LANES]
        l_ref[...] += part * (0.5 / D)

    row = pl.BlockSpec((tm, D), lambda i: (i, 0))
    return pl.pallas_call(
        body, name="loss_head", grid=(n,),
        in_specs=[row, row],
        out_specs=[row, row, pl.BlockSpec((8, LANES), lambda i: (0, 0))],
        out_shape=[jax.ShapeDtypeStruct((S, D), F32), jax.ShapeDtypeStruct((S, D), BF16),
                   jax.ShapeDtypeStruct((8, LANES), F32)],
        compiler_params=_cparams(("arbitrary",)),
    )(y, tgt)


def _up_act_fwd(h2, wg, cw, cb):
    S, K = h2.shape
    nb, _, nc = wg.shape
    Fh = nb * nc // 2
    tm = _tile(S, (2048, 1024, 512))
    tn = _tile(nc, (256, 128))
    sub = _tile(tm, (512,))
    ch = _tile(sub, (32,))
    npb = nc // tn
    hb = nb // 2
    ncol = Fh // tn

    def body(a_ref, bg_ref, bv_ref, cw_ref, cb_ref, u_ref, uc_ref, o_ref, buf_ref, halo_ref):
        j = pl.program_id(1)

        @pl.when(pl.program_id(0) == 0)
        def _():
            halo_ref[j] = jnp.zeros((2, 8, tn), F32)

        buf_ref[:, 0:8, :] = halo_ref[j]
        cws, cbs = [cw_ref[0], cw_ref[1]], [cb_ref[0], cb_ref[1]]
        for r in range(tm // sub):
            for s, b_ref in enumerate((bg_ref, bv_ref)):
                buf_ref[s, 8 + r * sub:8 + (r + 1) * sub, :] = jnp.dot(
                    a_ref[r * sub:(r + 1) * sub, :], b_ref[...], preferred_element_type=F32)
            for c in range(sub // ch):
                base = r * sub + c * ch
                ucs = []
                for s in range(2):
                    ext = buf_ref[s, base:base + ch + 8, :]
                    us = ext[8:]
                    uc = (cws[s][0:1] * pltpu.roll(ext, 2, 0)[8:] + cws[s][1:2] * pltpu.roll(ext, 1, 0)[8:]
                          + cws[s][2:3] * us + cbs[s])
                    u_ref[s, base:base + ch, :] = us.astype(BF16)
                    uc_ref[s, base:base + ch, :] = uc.astype(BF16)
                    ucs.append(uc)
                gate, val = ucs
                o_ref[base:base + ch, :] = (gate * (1.0 / (1.0 + jnp.exp(-gate))) * val).astype(BF16)
        halo_ref[j] = buf_ref[:, tm:tm + 8, :]

    both = pl.BlockSpec((2, tm, tn), lambda i, j: (0, i, j))
    return pl.pallas_call(
        body, name="up_act_fwd", grid=(S // tm, ncol),
        in_specs=[pl.BlockSpec((tm, K), lambda i, j: (i, 0)),
                  pl.BlockSpec((None, K, tn), lambda i, j: (j // npb, 0, j % npb)),
                  pl.BlockSpec((None, K, tn), lambda i, j: (j // npb + hb, 0, j % npb)),
                  pl.BlockSpec((2, 3, tn), lambda i, j: (0, 0, j)),
                  pl.BlockSpec((2, 1, tn), lambda i, j: (0, 0, j))],
        out_specs=[both, both, pl.BlockSpec((tm, tn), lambda i, j: (i, j))],
        out_shape=[jax.ShapeDtypeStruct((2, S, Fh), BF16), jax.ShapeDtypeStruct((2, S, Fh), BF16),
                   jax.ShapeDtypeStruct((S, Fh), BF16)],
        scratch_shapes=[pltpu.VMEM((2, tm + 8, tn), F32), pltpu.VMEM((ncol, 2, 8, tn), F32)],
        compiler_params=_cparams(("arbitrary", "arbitrary")),
    )(h2, wg, wg, cw, cb)


def _down_act_bwd(dxb, wg, uc, after=None):
    S, D = dxb.shape
    nb, kc, _ = wg.shape
    Fh = nb * kc
    tm = _tile(S, (1024, 512))
    sub = _tile(tm, (256,))
    ch = _tile(sub, (32,))
    t128 = kc // LANES
    tn = 2 * LANES
    order = [] if after is None else [after]

    def body(dx_ref, b0_ref, b1_ref, uc_ref, *rest):
        duc_ref, bcat_ref, da_ref = rest[len(order):]
        bcat_ref[0:LANES, :] = b0_ref[...]
        bcat_ref[LANES:tn, :] = b1_ref[...]
        for r in range(tm // sub):
            rows = slice(r * sub, (r + 1) * sub)
            da_ref[rows, :] = lax.dot_general(dx_ref[rows, :], bcat_ref[...], _DN["nt"], preferred_element_type=F32)
            for c in range(sub // ch):
                cs = slice(r * sub + c * ch, r * sub + (c + 1) * ch)
                da = da_ref[cs, :]
                gate, val = uc_ref[0, cs, :].astype(F32), uc_ref[1, cs, :].astype(F32)
                sig = 1.0 / (1.0 + jnp.exp(-gate))
                duc_ref[0, cs, :] = (da * val * (sig * (1.0 + gate * (1.0 - sig)))).astype(BF16)
                duc_ref[1, cs, :] = (da * (gate * sig)).astype(BF16)

    both = pl.BlockSpec((2, tm, tn), lambda i, j: (0, i, j))
    return pl.pallas_call(
        body, name="down_act_bwd", grid=(S // tm, Fh // tn),
        in_specs=[pl.BlockSpec((tm, D), lambda i, j: (i, 0)),
                  pl.BlockSpec((None, LANES, D), lambda i, j: ((2 * j) // t128, (2 * j) % t128, 0)),
                  pl.BlockSpec((None, LANES, D), lambda i, j: ((2 * j + 1) // t128, (2 * j + 1) % t128, 0)),
                  both] + [_ANY] * len(order),
        out_specs=both,
        out_shape=jax.ShapeDtypeStruct((2, S, Fh), BF16),
        scratch_shapes=[pltpu.VMEM((tn, D), BF16), pltpu.VMEM((tm, tn), F32)],
        compiler_params=_cparams(("parallel", "parallel")),
    )(dxb, wg, wg, uc, *order)


def _up_bwd_in(duc, u, cw, wg):
    _, S, Fh = duc.shape
    nb, K, nc = wg.shape
    tm = _tile(S, (1024, 512))
    tk = _tile(nc, (1408, 640, 512, 256, 128))
    sub = _tile(tm, (256,))
    ch = _tile(sub, (64,))
    hr = 16
    npb = nc // tk
    nk = nb * npb
    half = nk // 2
    n, nh = S // tm, tm // hr

    def body(d_ref, h_ref, u_ref, cw_ref, b_ref, du_ref, o_ref, cg_ref):
        i, k = pl.program_id(0), pl.program_id(1)
        keep = (i < n - 1).astype(F32)

        @pl.when(i == 0)
        def _():
            cg_ref[k] = jnp.zeros((8, tk), F32)

        @pl.when(k == 0)
        def _():
            o_ref[...] = jnp.zeros_like(o_ref)

        for r in range(tm // sub):
            for l in range(tk // LANES):
                ls = slice(l * LANES, (l + 1) * LANES)
                w = cw_ref[:, ls]
                sums = [jnp.zeros((8, LANES), F32) for _ in range(4)]
                for c in range(sub // ch):
                    base = r * sub + c * ch
                    if base + ch + hr <= tm:
                        ext = d_ref[base:base + ch + hr, ls].astype(F32)
                    else:
                        ext = jnp.concatenate([d_ref[base:base + ch, ls].astype(F32),
                                               h_ref[:, ls].astype(F32) * keep], axis=0)
                    d = ext[:ch]
                    d1 = pltpu.roll(ext, ch + hr - 1, 0)[:ch]
                    d2 = pltpu.roll(ext, ch + hr - 2, 0)[:ch]
                    du_ref[base:base + ch, ls] = (w[2:3] * d + w[1:2] * d1 + w[0:1] * d2).astype(BF16)
                    uv = u_ref[base:base + ch, ls].astype(F32)
                    for q, v in enumerate((d2 * uv, d1 * uv, d * uv, d)):
                        sums[q] = sums[q] + _rows_to_8(v)
                for q in range(4):
                    cg_ref[k, q:q + 1, ls] += jnp.sum(sums[q], axis=0, keepdims=True)
            rows = slice(r * sub, (r + 1) * sub)
            o_ref[rows, :] += lax.dot_general(du_ref[rows, :], b_ref[...], _DN["nt"], preferred_element_type=F32)

    blk = pl.BlockSpec((None, tm, tk), lambda i, k: (k // half, i, k % half))
    return pl.pallas_call(
        body, name="up_bwd_in", grid=(n, nk),
        in_specs=[blk,
                  pl.BlockSpec((None, hr, tk), lambda i, k: (k // half, jnp.minimum((i + 1) * nh, S // hr - 1), k % half)),
                  blk,
                  pl.BlockSpec((None, 3, tk), lambda i, k: (k // half, 0, k % half)),
                  pl.BlockSpec((None, K, tk), lambda i, k: (k // npb, 0, k % npb))],
        out_specs=[blk, pl.BlockSpec((tm, K), lambda i, k: (i, 0)), pl.BlockSpec((nk, 8, tk), lambda i, k: (0, 0, 0))],
        out_shape=[jax.ShapeDtypeStruct((2, S, Fh), BF16), jax.ShapeDtypeStruct((S, K), F32),
                   jax.ShapeDtypeStruct((nk, 8, tk), F32)],
        compiler_params=_cparams(("arbitrary", "arbitrary")),
    )(duc, duc, u, cw, wg)


def _pool_counts(i, tm, w):
    t = i * tm + lax.broadcasted_iota(jnp.int32, (tm, 1), 0)
    return jnp.minimum(t + 1, w).astype(F32)


def _pool_fwd(x, gain):
    S, D = x.shape
    tm = _tile(S, (256,))
    gd = D // len(POOL_WINDOWS)
    nh = tm // POOL_HALO

    def body(x_ref, xh_ref, g_ref, d_ref):
        i = pl.program_id(0)

        def norm(v):
            return v * lax.rsqrt(jnp.mean(v * v, axis=-1, keepdims=True) + EPS) * g_ref[...]

        h = norm(x_ref[...])
        hh = norm(xh_ref[...]) * (i > 0).astype(F32)
        ext = jnp.concatenate([hh, h], axis=0)
        for gi, w in enumerate(POOL_WINDOWS):
            sl = slice(gi * gd, (gi + 1) * gd)
            win = ext[:, sl]
            k = 1
            while k < w:
                win = win + pltpu.roll(win, k, 0)
                k *= 2
            mean = win[POOL_HALO:] / _pool_counts(i, tm, w)
            d_ref[:, sl] = (mean - h[:, sl]).astype(BF16)

    return pl.pallas_call(
        body, name="pool_fwd", grid=(S // tm,),
        in_specs=[pl.BlockSpec((tm, D), lambda i: (i, 0)),
                  pl.BlockSpec((POOL_HALO, D), lambda i: (jnp.maximum(i * nh - 1, 0), 0)),
                  pl.BlockSpec((1, D), lambda i: (0, 0))],
        out_specs=pl.BlockSpec((tm, D), lambda i: (i, 0)),
        out_shape=jax.ShapeDtypeStruct((S, D), BF16),
        compiler_params=_cparams(("parallel",)),
    )(x, x, gain)


def _pool_mm_fwd(d, wp, x, scale):
    S, D = d.shape
    G, gd, _ = wp.shape
    tm = _tile(S, (1024, 512))
    o_spec = pl.BlockSpec((tm, gd), lambda p, q, k: (p, q))
    return _mm(
        "pool_mm_fwd", "nn", d, wp, (S // tm, G, 1),
        o_spec, pl.BlockSpec((None, gd, gd), lambda p, q, k: (q, 0, 0)),
        [((S, D), F32, o_spec), ((S, D), F32, o_spec)], (tm, gd),
        extras=[(x, o_spec), (scale, pl.BlockSpec((1, gd), lambda p, q, k: (0, q)))],
        epilogue=lambda acc, xv, sc: (acc, xv + acc * sc))


def _pool_bwd_pre(dx1, ypre, scale):
    S, D = dx1.shape
    tm = _tile(S, (256,))
    n = S // tm

    def body(dx_ref, y_ref, s_ref, dy_ref, ds_ref, acc_ref):
        i = pl.program_id(0)
        dx = dx_ref[...]
        dy_ref[...] = (dx * s_ref[...]).astype(BF16)

        @pl.when(i == 0)
        def _():
            acc_ref[...] = jnp.zeros_like(acc_ref)

        acc_ref[...] += _rows_to_8(dx * y_ref[...])

        @pl.when(i == n - 1)
        def _():
            ds_ref[...] = jnp.sum(acc_ref[...], axis=0, keepdims=True)

    row = pl.BlockSpec((tm, D), lambda i: (i, 0))
    vec = pl.BlockSpec((1, D), lambda i: (0, 0))
    return pl.pallas_call(
        body, name="pool_bwd_pre", grid=(n,),
        in_specs=[row, row, vec], out_specs=[row, vec],
        out_shape=[jax.ShapeDtypeStruct((S, D), BF16), jax.ShapeDtypeStruct((1, D), F32)],
        scratch_shapes=[pltpu.VMEM((8, D), F32)],
        compiler_params=_cparams(("arbitrary",)),
    )(dx1, ypre, scale)


def _pool_mm_bwd_in(dyp, wp):
    S, D = dyp.shape
    G, gd, _ = wp.shape
    tm = _tile(S, (1024, 512))
    spec = pl.BlockSpec((tm, gd), lambda p, q, k: (p, q))
    return _mm(
        "pool_mm_bwd_in", "nt", dyp, wp, (S // tm, G, 1),
        spec, pl.BlockSpec((None, gd, gd), lambda p, q, k: (q, 0, 0)),
        [((S, D), F32, spec)], (tm, gd))


def _pool_mm_bwd_w(d, dyp):
    S, D = d.shape
    G = len(POOL_WINDOWS)
    gd = D // G
    ts = _tile(S, (1024, 512, 256))
    spec = pl.BlockSpec((ts, gd), lambda p, q, k: (k, p))
    return _mm(
        "pool_mm_bwd_w", "tn", d, dyp, (G, 1, S // ts),
        spec, spec,
        [((G, gd, gd), BF16, pl.BlockSpec((None, gd, gd), lambda p, q, k: (p, 0, 0)))], (gd, gd))


def _pool_bwd_post(dd, x, gain, dres):
    S, D = x.shape
    tm = _tile(S, (256,))
    n = S // tm
    gd = D // len(POOL_WINDOWS)
    nh = tm // POOL_HALO

    def body(dd_ref, ddh_ref, x_ref, g_ref, dr_ref, dx_ref, dxb_ref, dg_ref, acc_ref):
        i = pl.program_id(0)
        dd = dd_ref[...]
        halo = ddh_ref[...] * (i < n - 1).astype(F32)
        parts = []
        for gi, w in enumerate(POOL_WINDOWS):
            sl = slice(gi * gd, (gi + 1) * gd)
            ext = jnp.concatenate([dd[:, sl] / _pool_counts(i, tm, w), halo[:, sl] * (1.0 / w)], axis=0)
            k = 1
            while k < w:
                ext = ext + pltpu.roll(ext, tm + POOL_HALO - k, 0)
                k *= 2
            parts.append(ext[:tm] - dd[:, sl])
        dh = jnp.concatenate(parts, axis=1)
        dxn, dgr = _rms_bwd_math(x_ref[...], g_ref[...], dh)
        dx = dr_ref[...] + dxn
        dx_ref[...] = dx
        dxb_ref[...] = dx.astype(BF16)

        @pl.when(i == 0)
        def _():
            acc_ref[...] = jnp.zeros_like(acc_ref)

        acc_ref[...] += _rows_to_8(dgr)

        @pl.when(i == n - 1)
        def _():
            dg_ref[...] = jnp.sum(acc_ref[...], axis=0, keepdims=True)

    row = pl.BlockSpec((tm, D), lambda i: (i, 0))
    vec = pl.BlockSpec((1, D), lambda i: (0, 0))
    nxt = pl.BlockSpec((POOL_HALO, D), lambda i: (jnp.minimum((i + 1) * nh, S // POOL_HALO - 1), 0))
    return pl.pallas_call(
        body, name="pool_bwd_post", grid=(n,),
        in_specs=[row, nxt, row, vec, row],
        out_specs=[row, row, vec],
        out_shape=[jax.ShapeDtypeStruct((S, D), F32), jax.ShapeDtypeStruct((S, D), BF16),
                   jax.ShapeDtypeStruct((1, D), F32)],
        scratch_shapes=[pltpu.VMEM((8, D), F32)],
        compiler_params=_cparams(("arbitrary",)),
    )(dd, dd, x, gain, dres)


def _band_tables():
    i = np.arange(WINDOW)[:, None]
    j = np.arange(2 * WINDOW)[None, :]
    n = np.maximum(WINDOW + i - j, 0)
    max_exact = N_BUCKETS // 2
    nf = np.maximum(n, 1).astype(np.float32)
    large = max_exact + (np.log(nf / max_exact) / np.log(MAX_DISTANCE / max_exact)
                         * (N_BUCKETS - max_exact)).astype(np.int32)
    large = np.minimum(large, N_BUCKETS - 1)
    buckets = np.where(n < max_exact, n, large).astype(np.int32)
    dist = WINDOW + i - j
    in_win = ((dist >= 0) & (dist < WINDOW)).astype(np.int32)
    return buckets, in_win


def _bias_expand(rel_bias):
    H = rel_bias.shape[0]
    buckets, in_win = _band_tables()

    def body(rb_ref, bk_ref, win_ref, o_ref):
        h = pl.program_id(0)
        bk = bk_ref[...]
        acc = jnp.zeros(bk.shape, F32)
        for b in range(N_BUCKETS):
            acc = jnp.where(bk == b, rb_ref[h, b], acc)
        o_ref[...] = jnp.where(win_ref[...] > 0, acc, NEG_INF)

    full = pl.BlockSpec((WINDOW, 2 * WINDOW), lambda h: (0, 0))
    return pl.pallas_call(
        body, name="bias_expand", grid=(H,),
        in_specs=[pl.BlockSpec(memory_space=pltpu.SMEM), full, full],
        out_specs=pl.BlockSpec((None, WINDOW, 2 * WINDOW), lambda h: (h, 0, 0)),
        out_shape=jax.ShapeDtypeStruct((H, WINDOW, 2 * WINDOW), F32),
        compiler_params=_cparams(("parallel",)),
    )(rel_bias, jnp.asarray(buckets), jnp.asarray(in_win))


def _bias_reduce(dbias):
    H = dbias.shape[0]
    buckets, in_win = _band_tables()

    def body(d_ref, bk_ref, win_ref, o_ref):
        bk = jnp.where(win_ref[...] > 0, bk_ref[...], -1)
        d = d_ref[...]
        lane = lax.broadcasted_iota(jnp.int32, (8, LANES), 1)
        out = jnp.zeros((8, LANES), F32)
        for b in range(N_BUCKETS):
            s = jnp.sum(jnp.where(bk == b, d, 0.0))
            out = jnp.where(lane == b, s, out)
        o_ref[...] = out

    full = pl.BlockSpec((WINDOW, 2 * WINDOW), lambda h: (0, 0))
    return pl.pallas_call(
        body, name="bias_reduce", grid=(H,),
        in_specs=[pl.BlockSpec((None, WINDOW, 2 * WINDOW), lambda h: (h, 0, 0)), full, full],
        out_specs=pl.BlockSpec((None, 8, LANES), lambda h: (h, 0, 0)),
        out_shape=jax.ShapeDtypeStruct((H, 8, LANES), F32),
        compiler_params=_cparams(("parallel",)),
    )(dbias, jnp.asarray(buckets), jnp.asarray(in_win))


def _half_rsqrt(t, lo):
    sq = t * t
    s_lo = jnp.sum(jnp.where(lo, sq, 0.0), axis=-1, keepdims=True)
    s_hi = jnp.sum(jnp.where(lo, 0.0, sq), axis=-1, keepdims=True)
    return jnp.where(lo, lax.rsqrt(s_lo * (1.0 / HEAD_DIM) + EPS), lax.rsqrt(s_hi * (1.0 / HEAD_DIM) + EPS))


def _qk_norm_fwd(qkv, qg, kg, n_q, n_kv):
    S, W = qkv.shape
    tm = _tile(S, (256,))
    qw = n_q * HEAD_DIM
    kw = n_kv * HEAD_DIM
    scale = HEAD_DIM ** -0.5

    def body(x_ref, qg_ref, kg_ref, q_ref, k_ref, v_ref):
        lo = lax.broadcasted_iota(jnp.int32, (tm, LANES), 1) < HEAD_DIM
        for t in range(qw // LANES):
            q = x_ref[:, t * LANES:(t + 1) * LANES]
            q_ref[:, t * LANES:(t + 1) * LANES] = (q * _half_rsqrt(q, lo) * qg_ref[...] * scale).astype(BF16)
        for p in range(kw // LANES):
            k = x_ref[:, qw + p * LANES:qw + (p + 1) * LANES]
            kn = k * _half_rsqrt(k, lo) * kg_ref[...]
            v = x_ref[:, qw + kw + p * LANES:qw + kw + (p + 1) * LANES]
            for src, dst in ((kn, k_ref), (v, v_ref)):
                rolled = pltpu.roll(src, HEAD_DIM, 1)
                dst[2 * p] = jnp.where(lo, src, rolled).astype(BF16)
                dst[2 * p + 1] = jnp.where(lo, rolled, src).astype(BF16)

    vec = pl.BlockSpec((1, LANES), lambda i: (0, 0))
    kv = pl.BlockSpec((n_kv, tm, LANES), lambda i: (0, i, 0))
    return pl.pallas_call(
        body, name="qk_norm_fwd", grid=(S // tm,),
        in_specs=[pl.BlockSpec((tm, W), lambda i: (i, 0)), vec, vec],
        out_specs=[pl.BlockSpec((tm, qw), lambda i: (i, 0)), kv, kv],
        out_shape=[jax.ShapeDtypeStruct((S, qw), BF16), jax.ShapeDtypeStruct((n_kv, S, LANES), BF16),
                   jax.ShapeDtypeStruct((n_kv, S, LANES), BF16)],
        compiler_params=_cparams(("parallel",)),
    )(qkv, qg, kg)


def _qk_norm_bwd(qkv, qg, kg, dq, dkc, dkp, dvc, dvp, n_q, n_kv):
    S, W = qkv.shape
    tm = WINDOW
    n = S // tm
    qw = n_q * HEAD_DIM
    kw = n_kv * HEAD_DIM
    scale = HEAD_DIM ** -0.5

    def body(x_ref, qg_ref, kg_ref, dq_ref, dkc_ref, dkp_ref, dvc_ref, dvp_ref, dx_ref, dqg_ref, dkg_ref,
             accq_ref, acck_ref):
        i = pl.program_id(0)
        lo = lax.broadcasted_iota(jnp.int32, (tm, LANES), 1) < HEAD_DIM
        has_next = (i < n - 1).astype(F32)

        @pl.when(i == 0)
        def _():
            accq_ref[...] = jnp.zeros_like(accq_ref)
            acck_ref[...] = jnp.zeros_like(acck_ref)

        def norm_bwd(t, dy, gain):
            r = _half_rsqrt(t, lo)
            xhat = t * r
            dxhat = dy * gain
            prod = dxhat * xhat
            c_lo = jnp.sum(jnp.where(lo, prod, 0.0), axis=-1, keepdims=True) * (1.0 / HEAD_DIM)
            c_hi = jnp.sum(jnp.where(lo, 0.0, prod), axis=-1, keepdims=True) * (1.0 / HEAD_DIM)
            return r * (dxhat - xhat * jnp.where(lo, c_lo, c_hi)), _rows_to_8(dy * xhat)

        for t in range(qw // LANES):
            sl = slice(t * LANES, (t + 1) * LANES)
            dt, dg = norm_bwd(x_ref[:, sl], dq_ref[:, sl] * scale, qg_ref[...])
            dx_ref[:, sl] = dt.astype(BF16)
            accq_ref[...] += dg

        def pair(cur_ref, prev_ref, p):
            folded = []
            for h in (2 * p, 2 * p + 1):
                tot = cur_ref[h] + prev_ref[h] * has_next
                folded.append(tot + pltpu.roll(tot, HEAD_DIM, 1))
            return jnp.where(lo, folded[0], folded[1])

        for p in range(kw // LANES):
            sl = slice(qw + p * LANES, qw + (p + 1) * LANES)
            dt, dg = norm_bwd(x_ref[:, sl], pair(dkc_ref, dkp_ref, p), kg_ref[...])
            dx_ref[:, sl] = dt.astype(BF16)
            acck_ref[...] += dg
            sl = slice(qw + kw + p * LANES, qw + kw + (p + 1) * LANES)
            dx_ref[:, sl] = pair(dvc_ref, dvp_ref, p).astype(BF16)

        @pl.when(i == n - 1)
        def _():
            for acc_ref, o_ref in ((accq_ref, dqg_ref), (acck_ref, dkg_ref)):
                s = jnp.sum(acc_ref[...], axis=0, keepdims=True)
                o_ref[...] = s + pltpu.roll(s, HEAD_DIM, 1)

    vec = pl.BlockSpec((1, LANES), lambda i: (0, 0))
    cur = pl.BlockSpec((n_kv, tm, LANES), lambda i: (0, i, 0))
    nxt = pl.BlockSpec((n_kv, tm, LANES), lambda i: (0, jnp.minimum(i + 1, n - 1), 0))
    return pl.pallas_call(
        body, name="qk_norm_bwd", grid=(n,),
        in_specs=[pl.BlockSpec((tm, W), lambda i: (i, 0)), vec, vec, pl.BlockSpec((tm, qw), lambda i: (i, 0)),
                  cur, nxt, cur, nxt],
        out_specs=[pl.BlockSpec((tm, W), lambda i: (i, 0)), vec, vec],
        out_shape=[jax.ShapeDtypeStruct((S, W), BF16), jax.ShapeDtypeStruct((1, LANES), F32),
                   jax.ShapeDtypeStruct((1, LANES), F32)],
        scratch_shapes=[pltpu.VMEM((8, LANES), F32), pltpu.VMEM((8, LANES), F32)],
        compiler_params=_cparams(("arbitrary",)),
    )(qkv, qg, kg, dq, dkc, dkp, dvc, dvp)


def _attn_specs(n_kv):
    gw = GQA_GROUP * HEAD_DIM
    q = pl.BlockSpec((WINDOW, gw), lambda kh, n: (n, kh))
    cur = pl.BlockSpec((None, WINDOW, LANES), lambda kh, n: (kh, n, 0))
    prev = pl.BlockSpec((None, WINDOW, LANES), lambda kh, n: (kh, jnp.maximum(n - 1, 0), 0))
    bias = pl.BlockSpec((GQA_GROUP, WINDOW, 2 * WINDOW), lambda kh, n: (kh, 0, 0))
    sink = pl.BlockSpec(memory_space=pltpu.SMEM)
    return q, cur, prev, bias, sink


def _stack_heads(x_ref, lo):
    parts = []
    for g in range(GQA_GROUP):
        t = x_ref[:, (g // 2) * LANES:(g // 2 + 1) * LANES]
        parts.append(jnp.where(lo if g % 2 == 0 else jnp.logical_not(lo), t, jnp.zeros_like(t)))
    return jnp.concatenate(parts, axis=0)


def _unstack_heads(v, lo):
    return [jnp.where(lo, v[(2 * p) * WINDOW:(2 * p + 1) * WINDOW], v[(2 * p + 1) * WINDOW:(2 * p + 2) * WINDOW])
            for p in range(GQA_GROUP // 2)]


def _attn_probs(qs, kk, bias_ref, s_ref, kh, has_prev):
    rows = GQA_GROUP * WINDOW
    s = lax.dot_general(qs, kk, _DN["nt"], preferred_element_type=F32) + bias_ref[...].reshape(rows, 2 * WINDOW)
    col = lax.broadcasted_iota(jnp.int32, (rows, 2 * WINDOW), 1)
    s = jnp.where(jnp.logical_or(has_prev, col >= WINDOW), s, NEG_INF)
    sink = jnp.concatenate([jnp.full((WINDOW, 1), s_ref[kh * GQA_GROUP + g], F32) for g in range(GQA_GROUP)], axis=0)
    m = jnp.maximum(jnp.max(s, axis=-1, keepdims=True), sink)
    p = jnp.exp(s - m)
    es = jnp.exp(sink - m)
    den = jnp.sum(p, axis=-1, keepdims=True) + es
    return p / den, es / den


def _attn_fwd(q, kcat, vcat, biasm, sinks):
    S, qw = q.shape
    n_kv = kcat.shape[0]
    qs, cur, prev, bias, sink = _attn_specs(n_kv)

    def body(q_ref, kc_ref, kp_ref, vc_ref, vp_ref, b_ref, s_ref, o_ref):
        kh, n = pl.program_id(0), pl.program_id(1)
        lo = lax.broadcasted_iota(jnp.int32, (WINDOW, LANES), 1) < HEAD_DIM
        kk = jnp.concatenate([kp_ref[...], kc_ref[...]], axis=0)
        vv = jnp.concatenate([vp_ref[...], vc_ref[...]], axis=0)
        pn, _ = _attn_probs(_stack_heads(q_ref, lo), kk, b_ref, s_ref, kh, n > 0)
        o = jnp.dot(pn.astype(BF16), vv, preferred_element_type=F32)
        for p, t in enumerate(_unstack_heads(o, lo)):
            o_ref[:, p * LANES:(p + 1) * LANES] = t.astype(BF16)

    return pl.pallas_call(
        body, name="attn_fwd", grid=(n_kv, S // WINDOW),
        in_specs=[qs, cur, prev, cur, prev, bias, sink],
        out_specs=qs,
        out_shape=jax.ShapeDtypeStruct((S, qw), BF16),
        compiler_params=_cparams(("parallel", "parallel")),
    )(q, kcat, kcat, vcat, vcat, biasm, sinks)


def _attn_bwd(q, kcat, vcat, biasm, sinks, do):
    S, qw = q.shape
    n_kv = kcat.shape[0]
    H = n_kv * GQA_GROUP
    qs, cur, prev, bias, sink = _attn_specs(n_kv)

    def body(q_ref, kc_ref, kp_ref, vc_ref, vp_ref, b_ref, s_ref, do_ref,
             dq_ref, dkc_ref, dkp_ref, dvc_ref, dvp_ref, db_ref, ds_ref):
        kh, n = pl.program_id(0), pl.program_id(1)
        lo = lax.broadcasted_iota(jnp.int32, (WINDOW, LANES), 1) < HEAD_DIM
        kk = jnp.concatenate([kp_ref[...], kc_ref[...]], axis=0)
        vv = jnp.concatenate([vp_ref[...], vc_ref[...]], axis=0)

        @pl.when(n == 0)
        def _():
            db_ref[...] = jnp.zeros_like(db_ref)
            ds_ref[...] = jnp.zeros_like(ds_ref)

        qs_ = _stack_heads(q_ref, lo)
        dos = _stack_heads(do_ref, lo)
        pn, ps = _attn_probs(qs_, kk, b_ref, s_ref, kh, n > 0)
        dp = lax.dot_general(dos, vv, _DN["nt"], preferred_element_type=F32)
        delta = jnp.sum(pn * dp, axis=-1, keepdims=True)
        ds = pn * (dp - delta)
        db_ref[...] += ds.reshape(GQA_GROUP, WINDOW, 2 * WINDOW)
        ds_ref[...] += jnp.zeros((GQA_GROUP, LANES), F32) - jnp.sum((ps * delta).reshape(GQA_GROUP, WINDOW, 1), axis=1)
        dsb = ds.astype(BF16)
        dq = jnp.dot(dsb, kk, preferred_element_type=F32)
        for p, t in enumerate(_unstack_heads(dq, lo)):
            dq_ref[:, p * LANES:(p + 1) * LANES] = t
        dk = lax.dot_general(dsb, qs_, _DN["tn"], preferred_element_type=F32)
        dv = lax.dot_general(pn.astype(BF16), dos, _DN["tn"], preferred_element_type=F32)
        dkp_ref[...] = dk[:WINDOW]
        dkc_ref[...] = dk[WINDOW:]
        dvp_ref[...] = dv[:WINDOW]
        dvc_ref[...] = dv[WINDOW:]

    part = jax.ShapeDtypeStruct((n_kv, S, LANES), F32)
    return pl.pallas_call(
        body, name="attn_bwd", grid=(n_kv, S // WINDOW),
        in_specs=[qs, cur, prev, cur, prev, bias, sink, qs],
        out_specs=[qs, cur, cur, cur, cur, bias, pl.BlockSpec((None, 8, LANES), lambda kh, n: (kh, 0, 0))],
        out_shape=[jax.ShapeDtypeStruct((S, qw), F32), part, part, part, part,
                   jax.ShapeDtypeStruct((H, WINDOW, 2 * WINDOW), F32),
                   jax.ShapeDtypeStruct((n_kv, 8, LANES), F32)],
        compiler_params=_cparams(("parallel", "arbitrary")),
    )(q, kcat, kcat, vcat, vcat, biasm, sinks, do)


def _adamw(name, w, g, m, v):
    shape = w.shape
    C = shape[-1]
    R = int(np.prod(shape[:-1]))
    tr = R
    if R * C * 4 > (1 << 20):
        tr = _tile(R, tuple(t for t in (512, 256, 128, 64, 32, 16, 8) if t * C * 4 <= (3 << 19)))
    c1 = 1.0 - ADAM_B1 ** ADAM_STEP
    c2 = 1.0 - ADAM_B2 ** ADAM_STEP

    def body(w_ref, g_ref, m_ref, v_ref, d_ref, nm_ref, nv_ref):
        gv = g_ref[...]
        nm = ADAM_B1 * m_ref[...] + (1.0 - ADAM_B1) * gv
        nv = ADAM_B2 * v_ref[...] + (1.0 - ADAM_B2) * (gv * gv)
        d_ref[...] = -ADAM_LR * ((nm / c1) / (jnp.sqrt(nv / c2) + ADAM_EPS) + ADAM_WD * w_ref[...])
        nm_ref[...] = nm
        nv_ref[...] = nv

    spec = pl.BlockSpec((tr, C), lambda i: (i, 0))
    outs = pl.pallas_call(
        body, name=name, grid=(R // tr,),
        in_specs=[spec] * 4, out_specs=[spec] * 3,
        out_shape=[jax.ShapeDtypeStruct((R, C), F32)] * 3,
        compiler_params=_cparams(("parallel",)),
    )(*[t.reshape(R, C) for t in (w, g, m, v)])
    return [o.reshape(shape) for o in outs]


def _place():
    return lax.axis_index("x"), lax.axis_index("y"), lax.axis_index("c")


def _other_chips(x, y):
    return [(1 - x, y), (x, 1 - y), (1 - x, 1 - y)]


def _swap_halves(name, grads):
    T = len(grads)

    def body(*refs):
        ins, outs = refs[:T], refs[T:2 * T]
        send, recv = refs[2 * T:]
        x, y, c = _place()

        def copy(t):
            half = grads[t].shape[1] // 2
            src = ins[t].at[:, pl.ds(pl.multiple_of((1 - c) * half, 16), half), :]
            return pltpu.make_async_remote_copy(
                src_ref=src, dst_ref=outs[t], send_sem=send.at[t], recv_sem=recv.at[t],
                device_id=(x, y, 1 - c), device_id_type=MESH)

        for t in range(T):
            copy(t).start()
        for t in range(T):
            copy(t).wait()

    return pl.pallas_call(
        body, name=name,
        in_specs=[_ANY] * T, out_specs=[_ANY] * T,
        out_shape=[jax.ShapeDtypeStruct((g.shape[0], g.shape[1] // 2, g.shape[2]), g.dtype) for g in grads],
        scratch_shapes=[pltpu.SemaphoreType.DMA((T,)), pltpu.SemaphoreType.DMA((T,))],
        compiler_params=pltpu.CompilerParams(has_side_effects=True),
    )(*grads)


def _add_half(name, g, other, c_arr):
    nb, R, C = g.shape
    half = R // 2
    tr = _tile(half, (256, 128, 64, 32, 16))
    n = half // tr

    def body(c_ref, g_ref, o_ref, s_ref):
        s_ref[...] = (g_ref[...].astype(F32) + o_ref[...].astype(F32)).astype(BF16)

    return pl.pallas_call(
        body, name=name,
        grid_spec=pltpu.PrefetchScalarGridSpec(
            num_scalar_prefetch=1, grid=(nb, n),
            in_specs=[pl.BlockSpec((None, tr, C), lambda b, i, c: (b, c[0] * n + i, 0)),
                      pl.BlockSpec((None, tr, C), lambda b, i, c: (b, i, 0))],
            out_specs=pl.BlockSpec((None, tr, C), lambda b, i, c: (b, i, 0))),
        out_shape=jax.ShapeDtypeStruct((nb, half, C), BF16),
        compiler_params=_cparams(("parallel", "parallel")),
    )(c_arr, g, other)


def _in_hbm(arrays):
    return [pltpu.with_memory_space_constraint(a, pltpu.HBM) for a in arrays]


def _chip_copy(src, dst, send, recv, chip, c):
    return pltpu.make_async_remote_copy(src_ref=src, dst_ref=dst, send_sem=send, recv_sem=recv,
                                        device_id=(chip[0], chip[1], c), device_id_type=MESH)


def _cast_slot(name, w, l, chip_arr, dtype):
    _, R, C = w.shape
    tr = _tile(R, tuple(t for t in (1024, 512, 256, 128, 64, 32, 16) if t * C * 4 <= (1 << 21)))

    def body(chip_ref, w_ref, o_ref):
        o_ref[...] = w_ref[...].astype(dtype)

    return pl.pallas_call(
        body, name=name,
        grid_spec=pltpu.PrefetchScalarGridSpec(
            num_scalar_prefetch=1, grid=(R // tr,),
            in_specs=[pl.BlockSpec((None, tr, C), lambda i, chip: (l, i, 0))],
            out_specs=pl.BlockSpec((None, tr, C), lambda i, chip: (chip[0], i, 0))),
        out_shape=jax.ShapeDtypeStruct((N_CHIPS, R, C), dtype),
        compiler_params=_cparams(("parallel",)),
    )(chip_arr, w)


def _gather_start(groups):
    sizes = [len(g) for g in groups]
    flat = [b for g in groups for b in g]
    n, G = len(flat), len(groups)

    def body(*refs):
        bufs = refs[:n]
        sems = refs[n:n + 2 * G]
        token = refs[-1]
        x, y, c = _place()
        chips = _other_chips(x, y)
        me = 2 * x + y
        k = 0
        for l in range(G):
            for t in range(sizes[l]):
                for j in range(3):
                    _chip_copy(bufs[k].at[me], bufs[k].at[me], sems[2 * l].at[3 * t + j], sems[2 * l + 1].at[3 * t + j],
                               chips[j], c).start()
                k += 1
        token[...] = jnp.zeros_like(token)

    sem_shapes = []
    for s in sizes:
        sem_shapes += [pltpu.SemaphoreType.DMA((3 * s,)), pltpu.SemaphoreType.DMA((3 * s,))]
    outs = pl.pallas_call(
        body, name="gather_start",
        out_shape=(*sem_shapes, *[pltpu.HBM(b.shape, b.dtype) for b in flat], jax.ShapeDtypeStruct((8, LANES), F32)),
        in_specs=[_HBM] * n,
        out_specs=(*[_SEM] * (2 * G), *[_HBM] * n, pl.BlockSpec(memory_space=pltpu.VMEM)),
        input_output_aliases={t: 2 * G + t for t in range(n)},
        compiler_params=pltpu.CompilerParams(has_side_effects=_DATAFLOW),
    )(*_in_hbm(flat))
    res, k = [], 2 * G
    for l in range(G):
        res.append((outs[2 * l], outs[2 * l + 1], list(outs[k:k + sizes[l]])))
        k += sizes[l]
    return res, outs[-1]


def _gather_wait(name, send, recv, bufs, after):
    T = len(bufs)

    def body(*refs):
        ins = refs[:T]
        send_ref, recv_ref = refs[T], refs[T + 1]
        x, y, c = _place()
        chips = _other_chips(x, y)
        me = 2 * x + y
        for t in range(T):
            for j in range(3):
                cp = _chip_copy(ins[t].at[me], ins[t].at[2 * chips[j][0] + chips[j][1]], send_ref.at[3 * t + j],
                                recv_ref.at[3 * t + j], chips[j], c)
                cp.wait_send()
                cp.wait_recv()

    return pl.pallas_call(
        body, name=name,
        out_shape=[pltpu.HBM(b.shape, b.dtype) for b in bufs],
        in_specs=[_HBM] * T + [_SEM, _SEM, _ANY],
        out_specs=[_HBM] * T,
        input_output_aliases={t: t for t in range(T)},
        compiler_params=pltpu.CompilerParams(has_side_effects=_DATAFLOW),
    )(*bufs, send, recv, after)


def _scatter_start(name, sums):
    T = len(sums)
    lands = [lax.empty((3,) + s.shape[1:], s.dtype) for s in sums]

    def body(*refs):
        srcs, lnds = refs[:T], refs[T:2 * T]
        send, recv = refs[2 * T], refs[2 * T + 1]
        token = refs[-1]
        x, y, c = _place()
        chips = _other_chips(x, y)
        for t in range(T):
            for j in range(3):
                _chip_copy(srcs[t].at[2 * chips[j][0] + chips[j][1]], lnds[t].at[j], send.at[3 * t + j],
                           recv.at[3 * t + j], chips[j], c).start()
        token[...] = jnp.zeros_like(token)

    both = list(sums) + lands
    outs = pl.pallas_call(
        body, name=name,
        out_shape=(pltpu.SemaphoreType.DMA((3 * T,)), pltpu.SemaphoreType.DMA((3 * T,)),
                   *[pltpu.HBM(b.shape, b.dtype) for b in both], jax.ShapeDtypeStruct((8, LANES), F32)),
        in_specs=[_HBM] * (2 * T),
        out_specs=(_SEM, _SEM, *[_HBM] * (2 * T), pl.BlockSpec(memory_space=pltpu.VMEM)),
        input_output_aliases={t: 2 + t for t in range(2 * T)},
        compiler_params=pltpu.CompilerParams(has_side_effects=_DATAFLOW),
    )(*_in_hbm(both))
    return outs[0], outs[1], list(outs[2:2 + T]), list(outs[2 + T:2 + 2 * T]), outs[-1]


def _scatter_wait(name, send, recv, sums, lands, after):
    T = len(sums)

    def body(*refs):
        srcs, lnds = refs[:T], refs[T:2 * T]
        send_ref, recv_ref = refs[2 * T], refs[2 * T + 1]
        x, y, c = _place()
        chips = _other_chips(x, y)
        for t in range(T):
            for j in range(3):
                cp = _chip_copy(srcs[t].at[2 * chips[j][0] + chips[j][1]], lnds[t].at[j], send_ref.at[3 * t + j],
                                recv_ref.at[3 * t + j], chips[j], c)
                cp.wait_send()
                cp.wait_recv()

    both = list(sums) + list(lands)
    outs = pl.pallas_call(
        body, name=name,
        out_shape=[pltpu.HBM(b.shape, b.dtype) for b in both],
        in_specs=[_HBM] * (2 * T) + [_SEM, _SEM, _ANY],
        out_specs=[_HBM] * (2 * T),
        input_output_aliases={t: t for t in range(2 * T)},
        compiler_params=pltpu.CompilerParams(has_side_effects=_DATAFLOW),
    )(*both, send, recv, after)
    return list(outs[:T]), list(outs[T:])


def _sum_parts(name, sums, land, chip_arr, c_arr):
    _, R2, C = sums.shape
    tr = _tile(R2, (256, 128, 64, 32, 16))

    def body(chip_ref, c_ref, s_ref, l_ref, o_ref):
        acc = s_ref[...].astype(F32)
        for j in range(3):
            acc = acc + l_ref[j].astype(F32)
        o_ref[...] = acc

    return pl.pallas_call(
        body, name=name,
        grid_spec=pltpu.PrefetchScalarGridSpec(
            num_scalar_prefetch=2, grid=(R2 // tr,),
            in_specs=[pl.BlockSpec((None, tr, C), lambda i, chip, c: (chip[0], i, 0)),
                      pl.BlockSpec((3, tr, C), lambda i, chip, c: (0, i, 0))],
            out_specs=pl.BlockSpec((None, tr, C), lambda i, chip, c: (c[0], i, 0))),
        out_shape=jax.ShapeDtypeStruct((2, R2, C), F32),
        compiler_params=_cparams(("parallel",)),
    )(chip_arr, c_arr, sums, land)


def _join_halves(name, bufs):
    T = len(bufs)

    def body(*refs):
        outs = refs[T:2 * T]
        send, recv = refs[2 * T:]
        x, y, c = _place()

        def copy(t, half):
            return pltpu.make_async_remote_copy(
                src_ref=outs[t].at[c], dst_ref=outs[t].at[half], send_sem=send.at[t], recv_sem=recv.at[t],
                device_id=(x, y, 1 - c), device_id_type=MESH)

        for t in range(T):
            copy(t, c).start()
        for t in range(T):
            copy(t, 1 - c).wait_recv()
        for t in range(T):
            copy(t, c).wait_send()

    return pl.pallas_call(
        body, name=name,
        in_specs=[_ANY] * T, out_specs=[_ANY] * T,
        out_shape=[jax.ShapeDtypeStruct(b.shape, b.dtype) for b in bufs],
        input_output_aliases={t: t for t in range(T)},
        scratch_shapes=[pltpu.SemaphoreType.DMA((T,)), pltpu.SemaphoreType.DMA((T,))],
        compiler_params=pltpu.CompilerParams(has_side_effects=True),
    )(*bufs)


def _rs_begin(tag, grads, c_arr):
    got = _swap_halves("rs_swap_" + tag, grads)
    sums = [_add_half("rs_add_%s_%d" % (tag, t), g, o, c_arr) for t, (g, o) in enumerate(zip(grads, got))]
    return _scatter_start("rs_scatter_start_" + tag, sums)


def _rs_finish(tag, state, shapes, chip_arr, c_arr, after=None):
    send, recv, sums, lands, token = state
    sums, lands = _scatter_wait("rs_scatter_wait_" + tag, send, recv, sums, lands, token if after is None else after)
    parts = [_sum_parts("rs_sum_%s_%d" % (tag, t), s, l, chip_arr, c_arr) for t, (s, l) in enumerate(zip(sums, lands))]
    joined = _join_halves("rs_join_" + tag, parts)
    return [j.reshape(sh[1], sh[2]) for j, sh in zip(joined, shapes)]


def _all_reduce_small(v):
    R = v.shape[0]

    def body(v_ref, o_ref, buf, send, recv):
        x, y, c = _place()
        me = 4 * x + 2 * y + c
        buf[me] = v_ref[...]

        def copy(k, slot):
            to = (x ^ ((k >> 2) & 1), y ^ ((k >> 1) & 1), c ^ (k & 1))
            return pltpu.make_async_remote_copy(
                src_ref=v_ref, dst_ref=buf.at[slot], send_sem=send.at[k - 1], recv_sem=recv.at[k - 1],
                device_id=to, device_id_type=MESH)

        for k in range(1, 8):
            copy(k, me).start()
        for k in range(1, 8):
            copy(k, me ^ k).wait_recv()
        for k in range(1, 8):
            copy(k, me).wait_send()
        acc = buf[0]
        for s in range(1, 8):
            acc = acc + buf[s]
        o_ref[...] = acc

    vm = pl.BlockSpec(memory_space=pltpu.VMEM)
    return pl.pallas_call(
        body, name="all_reduce_small",
        in_specs=[vm], out_specs=vm,
        out_shape=jax.ShapeDtypeStruct((R, LANES), F32),
        scratch_shapes=[pltpu.VMEM((8, R, LANES), F32), pltpu.SemaphoreType.DMA((7,)), pltpu.SemaphoreType.DMA((7,))],
        compiler_params=pltpu.CompilerParams(has_side_effects=True, vmem_limit_bytes=VMEM_LIMIT),
    )(v)


def _local_step(x, tgt, small, depth, nb, weight, convw, pscale, on_grads):
    S, D = x.shape
    n_q = D // HEAD_DIM
    n_kv = n_q // GQA_GROUP
    Fh = convw[0].shape[2]

    def dup(gain):
        return jnp.tile(gain, 2).reshape(1, LANES)

    biasm = _bias_expand(small["rel_bias"])
    saved = []
    for i in range(depth):
        j = i // 2
        w = {}
        st = {"x0": x, "w": w}
        gm = small["norm_mix"][i].reshape(1, D)
        if i % 2 == 0:
            h = _rms_fwd("rms_mix_fwd", x, gm)
            w["wqkv"] = weight(i, "wqkv", h)
            qkv = _mm_colblk("qkv_fwd", h, w["wqkv"], F32)
            qh, kcat, vcat = _qk_norm_fwd(qkv, dup(small["attn_q_gain"][j]), dup(small["attn_k_gain"][j]), n_q, n_kv)
            o = _attn_fwd(qh, kcat, vcat, biasm, small["attn_sinks"][j])
            w["wo"] = weight(i, "wo", o)
            x = _mm_rowblk_res("wo_fwd", o, w["wo"], x)
            st.update(h=h, qkv=qkv, qh=qh, kcat=kcat, vcat=vcat, o=o)
        else:
            d = _pool_fwd(x, gm)
            w["wp"] = weight(i, "wp", d)
            ypre, x = _pool_mm_fwd(d, w["wp"], x, pscale[j])
            st.update(d=d, ypre=ypre)
        st["x1"] = x
        h2 = _rms_fwd("rms_ffn_fwd", x, small["norm_ffn"][i].reshape(1, D))
        w["wup"] = weight(i, "wup", h2)
        cb = small["ffn_conv_b"][i].reshape(2, 1, Fh)
        u, uc, a = _up_act_fwd(h2, w["wup"], convw[i], cb)
        w["wdn"] = weight(i, "wdn", a)
        x = _mm_rowblk_res("down_fwd", a, w["wdn"], x)
        st.update(h2=h2, u=u, uc=uc, a=a)
        saved.append(st)

    dx, dxb, loss_part = _loss_head(x, tgt)

    big = [dict() for _ in range(depth)]
    sg = {k: [None] * depth for k in ("norm_mix", "norm_ffn", "conv_w", "conv_b")}
    sg.update({k: [None] * ((depth + 1) // 2) for k in ("q_gain", "k_gain", "sinks")})
    sg["pool_scale"] = [None] * (depth // 2)
    dbias_tot = None
    before = None
    for i in reversed(range(depth)):
        j = i // 2
        st = saved[i]
        w = st["w"]
        duc = _down_act_bwd(dxb, w["wdn"], st["uc"], after=before)
        wdn_grad = _mm_tn_rowblk("down_bwd_w", st["a"], dxb, nb)
        du, dh2, dcwb = _up_bwd_in(duc, st["u"], convw[i], w["wup"])
        dcwb = jnp.transpose(dcwb, (1, 0, 2)).reshape(8, 2 * Fh)
        sg["conv_w"][i] = dcwb[0:3]
        sg["conv_b"][i] = dcwb[3]
        wup_grad = _mm_tn_colblk("up_bwd_w", st["h2"], du, nb, split2=True)
        before = on_grads(i, "ffn", dict(wdn=wdn_grad, wup=wup_grad), dx)
        dx, dxb, dg = _rms_bwd("rms_ffn_bwd", st["x1"], small["norm_ffn"][i].reshape(1, D), dh2, dx, after=before)
        sg["norm_ffn"][i] = dg.reshape(D)
        gm = small["norm_mix"][i].reshape(1, D)
        if i % 2 == 0:
            do = _mm_nt_rowblk("wo_bwd_in", dxb, w["wo"], BF16)
            big[i]["wo"] = _mm_tn_rowblk("wo_bwd_w", st["o"], dxb, nb)
            dq, dkc, dkp, dvc, dvp, dbias, dsink = _attn_bwd(
                st["qh"], st["kcat"], st["vcat"], biasm, small["attn_sinks"][j], do)
            dbias_tot = dbias if dbias_tot is None else dbias_tot + dbias
            sg["sinks"][j] = dsink[:, :, 0].reshape(n_q)
            dqkv, dqg, dkg = _qk_norm_bwd(st["qkv"], dup(small["attn_q_gain"][j]), dup(small["attn_k_gain"][j]),
                                          dq, dkc, dkp, dvc, dvp, n_q, n_kv)
            sg["q_gain"][j] = dqg[0, :HEAD_DIM]
            sg["k_gain"][j] = dkg[0, :HEAD_DIM]
            big[i]["wqkv"] = _mm_tn_colblk("qkv_bwd_w", st["h"], dqkv, nb)
            dh = _mm_nt_colblk("qkv_bwd_in", dqkv, w["wqkv"])
            dx, dxb, dg = _rms_bwd("rms_mix_bwd", st["x0"], gm, dh, dx)
        else:
            dyp, dsc = _pool_bwd_pre(dx, st["ypre"], pscale[j])
            sg["pool_scale"][j] = dsc.reshape(D)
            big[i]["wp"] = _pool_mm_bwd_w(st["d"], dyp)
            dd = _pool_mm_bwd_in(dyp, w["wp"])
            dx, dxb, dg = _pool_bwd_post(dd, st["x0"], gm, dx)
        sg["norm_mix"][i] = dg.reshape(D)
        before = on_grads(i, "mix", big[i], dx)
    sg["rel_bias"] = _bias_reduce(dbias_tot)[:, 0, :N_BUCKETS]
    return loss_part, dx, sg


_SMALL_ORDER = ("norm_mix", "norm_ffn", "rel_bias", "q_gain", "k_gain", "sinks", "conv_b", "conv_w", "pool_scale")


def kernel(x, norm_mix, norm_ffn, rel_bias, attn_w_qkv, attn_q_gain, attn_k_gain, attn_sinks, attn_w_o, pool_w, pool_scale, ffn_w_up, ffn_conv_w, ffn_conv_b, ffn_w_down, loss_target, m_norm_mix, m_norm_ffn, m_rel_bias, m_attn_w_qkv, m_attn_q_gain, m_attn_k_gain, m_attn_sinks, m_attn_w_o, m_pool_w, m_pool_scale, m_ffn_w_up, m_ffn_conv_w, m_ffn_conv_b, m_ffn_w_down, v_norm_mix, v_norm_ffn, v_rel_bias, v_attn_w_qkv, v_attn_q_gain, v_attn_k_gain, v_attn_sinks, v_attn_w_o, v_pool_w, v_pool_scale, v_ffn_w_up, v_ffn_conv_w, v_ffn_conv_b, v_ffn_w_down):
    _, S, D = x.shape
    depth = ffn_w_up.shape[0]
    n_attn, n_pool = attn_w_qkv.shape[0], pool_w.shape[0]
    nb = N_CHIPS
    nc_up = ffn_w_up.shape[2]
    Fh = nc_up * nb // 2
    cx, cy, cc = _place()
    chip = 2 * cx + cy
    c_arr = jnp.reshape(cc, (1,)).astype(jnp.int32)
    chip_arr = jnp.reshape(chip, (1,)).astype(jnp.int32)

    pool_w3 = pool_w.reshape(n_pool, -1, pool_w.shape[-1])
    order = [("convw", ffn_conv_w.reshape(1, -1, nc_up), 0, F32), ("pscale", pool_scale[None], 0, F32)]
    for i in range(depth):
        j = i // 2
        mixer = ([("wqkv", attn_w_qkv, j), ("wo", attn_w_o, j)] if i % 2 == 0 else [("wp", pool_w3, j)])
        order += [((i, k), w, l, BF16) for k, w, l in mixer + [("wup", ffn_w_up, i), ("wdn", ffn_w_down, i)]]
    slots = [_cast_slot("slot_%d" % n, w, l, chip_arr, dt) for n, (_, w, l, dt) in enumerate(order)]
    started, token = _gather_start([[b] for b in slots])
    index = {key: n for n, (key, _, _, _) in enumerate(order)}

    def gathered(key, after):
        n = index[key]
        send, recv, bufs = started[n]
        return _gather_wait("gather_wait_%d" % n, send, recv, bufs, after)[0]

    convw_g = gathered("convw", token).reshape(nb, depth, 3, nc_up)
    pscale_g = gathered("pscale", token)
    convw = [jnp.transpose(convw_g[:, i], (1, 0, 2)).reshape(3, 2, Fh).transpose(1, 0, 2) for i in range(depth)]
    pscale = [pscale_g[:, j].reshape(1, D) for j in range(n_pool)]
    small = dict(norm_mix=norm_mix, norm_ffn=norm_ffn, rel_bias=rel_bias, attn_q_gain=attn_q_gain,
                 attn_k_gain=attn_k_gain, attn_sinks=attn_sinks, ffn_conv_b=ffn_conv_b)

    G, gd = pool_w.shape[1], pool_w.shape[3]

    def weight(i, name, after):
        w = gathered((i, name), after)
        if name == "wp":
            w = jnp.transpose(w.reshape(nb, G, gd // nb, gd), (1, 0, 2, 3)).reshape(G, gd, gd)
        return w

    red = {k: [None] * (n_attn if k in ("wqkv", "wo") else n_pool if k == "wp" else depth)
           for k in ("wqkv", "wo", "wp", "wup", "wdn")}
    pending = []

    def finish(after):
        i, tag, names, shapes, state = pending.pop()
        outs = _rs_finish("%d%s" % (i, tag), state, shapes, chip_arr, c_arr, after)
        for k, o in zip(names, outs):
            red[k][i // 2 if k in ("wqkv", "wo", "wp") else i] = o

    def on_grads(i, tag, grads, dx_i):
        if pending:
            finish(dx_i)
        names = sorted(grads)
        flat = []
        for k in names:
            g = grads[k]
            if k == "wp":
                g = jnp.transpose(g.reshape(G, nb, gd // nb, gd), (1, 0, 2, 3))
            flat.append(g.reshape(nb, -1, g.shape[-1]))
        state = _rs_begin("%d%s" % (i, tag), flat, c_arr)
        pending.append((i, tag, names, [f.shape for f in flat], state))
        return state[-1]

    loss_part, dx, sg = _local_step(x[0], loss_target[0], small, depth, nb, weight, convw, pscale, on_grads)
    finish(None)
    loss = lax.psum(jnp.sum(loss_part), ("x", "y", "c"))

    g_wqkv = jnp.stack(red["wqkv"])
    g_wo = jnp.stack(red["wo"])
    g_wp = jnp.stack(red["wp"]).reshape(pool_w.shape)
    g_wup = jnp.stack(red["wup"])
    g_wdn = jnp.stack(red["wdn"])

    parts = [jnp.stack(sg[k]) if isinstance(sg[k], list) else sg[k] for k in _SMALL_ORDER]
    sizes = [int(np.prod(p.shape)) for p in parts]
    total = sum(sizes)
    rows = -(-total // (8 * LANES)) * 8
    packed = jnp.concatenate([p.reshape(-1) for p in parts] + [jnp.zeros((rows * LANES - total,), F32)])
    summed = _all_reduce_small(packed.reshape(rows, LANES)).reshape(-1)
    sm, off = {}, 0
    for k, p, n in zip(_SMALL_ORDER, parts, sizes):
        sm[k] = summed[off:off + n].reshape(p.shape)
        off += n
    g_convw = lax.dynamic_slice_in_dim(sm["conv_w"], chip * nc_up, nc_up, axis=2)
    pc = pool_scale.shape[1]
    g_pscale = lax.dynamic_slice_in_dim(sm["pool_scale"], chip * pc, pc, axis=1)

    grads = [sm["norm_mix"], sm["norm_ffn"], sm["rel_bias"], g_wqkv, sm["q_gain"], sm["k_gain"], sm["sinks"], g_wo,
             g_wp, g_pscale, g_wup, g_convw, sm["conv_b"], g_wdn]
    ws = [norm_mix, norm_ffn, rel_bias, attn_w_qkv, attn_q_gain, attn_k_gain, attn_sinks, attn_w_o, pool_w, pool_scale,
          ffn_w_up, ffn_conv_w, ffn_conv_b, ffn_w_down]
    ms = [m_norm_mix, m_norm_ffn, m_rel_bias, m_attn_w_qkv, m_attn_q_gain, m_attn_k_gain, m_attn_sinks, m_attn_w_o,
          m_pool_w, m_pool_scale, m_ffn_w_up, m_ffn_conv_w, m_ffn_conv_b, m_ffn_w_down]
    vs = [v_norm_mix, v_norm_ffn, v_rel_bias, v_attn_w_qkv, v_attn_q_gain, v_attn_k_gain, v_attn_sinks, v_attn_w_o,
          v_pool_w, v_pool_scale, v_ffn_w_up, v_ffn_conv_w, v_ffn_conv_b, v_ffn_w_down]
    deltas, new_m, new_v = [], [], []
    for idx, (w, g, m, v) in enumerate(zip(ws, grads, ms, vs)):
        d, nm, nv = _adamw("adamw_%d" % idx, w, g, m, v)
        deltas.append(d), new_m.append(nm), new_v.append(nv)
    return (loss, dx.reshape(1, S, D), *grads, *deltas, *new_m, *new_v)
```

```python
import functools

import numpy as np
import jax
import jax.numpy as jnp
from jax import lax
from jax.experimental import pallas as pl
from jax.experimental.pallas import tpu as pltpu

F32 = jnp.float32
BF16 = jnp.bfloat16
MESH = pl.DeviceIdType.MESH
_ANY = pl.BlockSpec(memory_space=pl.ANY)
_HBM = pl.BlockSpec(memory_space=pltpu.HBM)
_SEM = pl.BlockSpec(memory_space=pltpu.SEMAPHORE)
_DATAFLOW = pltpu.SideEffectType.DATAFLOW_SIDE_EFFECTING

N_CHIPS = 4
HEAD_DIM = 64
GQA_GROUP = 8
WINDOW = 128
N_BUCKETS = 32
MAX_DISTANCE = 128
POOL_WINDOWS = (2, 4, 8, 16)
POOL_HALO = 16
ATTN_BWD_KV_PER_STEP = 4
EPS = 1e-6
NEG_INF = -1e30
LANES = 128
VMEM_LIMIT = 56 * 1024 * 1024

ADAM_LR = 0.001
ADAM_B1 = 0.9
ADAM_B2 = 0.999
ADAM_EPS = 1e-08
ADAM_WD = 0.01
ADAM_STEP = 10


def _tile(n, prefs):
    for p in prefs:
        if p <= n and n % p == 0:
            return p
    return n


def _cparams(sem):
    return pltpu.CompilerParams(dimension_semantics=sem, vmem_limit_bytes=VMEM_LIMIT)


_DN = {
    "nn": (((1,), (0,)), ((), ())),
    "nt": (((1,), (1,)), ((), ())),
    "tn": (((0,), (0,)), ((), ())),
}


def _mm(name, kind, a, b, grid, a_spec, b_spec, outs, acc_shape, extras=(), epilogue=None, after=None):
    nk = grid[2]
    n_ex, n_out = len(extras), len(outs)
    order = [] if after is None else [after]

    def body(a_ref, b_ref, *rest):
        ex_refs = rest[:n_ex]
        rest = rest[n_ex + len(order):]
        out_refs = rest[:n_out]
        acc_ref = rest[n_out] if nk > 1 else None
        part = lax.dot_general(a_ref[...], b_ref[...], _DN[kind], preferred_element_type=F32)

        def finish(val):
            vals = epilogue(val, *[r[...] for r in ex_refs]) if epilogue else (val,)
            for r, v in zip(out_refs, vals):
                r[...] = v.astype(r.dtype)

        if nk == 1:
            finish(part)
        else:
            k = pl.program_id(2)

            @pl.when(k == 0)
            def _():
                acc_ref[...] = part

            @pl.when(k > 0)
            def _():
                acc_ref[...] += part

            @pl.when(k == nk - 1)
            def _():
                finish(acc_ref[...])

    res = pl.pallas_call(
        body,
        name=name,
        grid=grid,
        in_specs=[a_spec, b_spec] + [s for _, s in extras] + [_ANY] * len(order),
        out_specs=[s for _, _, s in outs],
        out_shape=[jax.ShapeDtypeStruct(sh, dt) for sh, dt, _ in outs],
        scratch_shapes=[pltpu.VMEM(acc_shape, F32)] if nk > 1 else [],
        compiler_params=_cparams(("parallel", "parallel", "arbitrary")),
    )(a, b, *[e for e, _ in extras], *order)
    return res if n_out > 1 else res[0]


def _mm_colblk(name, a, wg, out_dtype):
    S, K = a.shape
    nb, _, nc = wg.shape
    tm = _tile(S, (512,))
    tn = _tile(nc, (1408, 1024, 640, 512, 256, 128))
    npb = nc // tn
    return _mm(
        name, "nn", a, wg, (nb * npb, S // tm, 1),
        pl.BlockSpec((tm, K), lambda p, q, k: (q, 0)),
        pl.BlockSpec((None, K, tn), lambda p, q, k: (p // npb, 0, p % npb)),
        [((S, nb * nc), out_dtype, pl.BlockSpec((tm, tn), lambda p, q, k: (q, p)))], (tm, tn))


def _mm_rowblk_res(name, a, wg, res):
    S = a.shape[0]
    nb, kc, N = wg.shape
    tm = _tile(S, (512,))
    tn = _tile(N, (2048, 1024, 512))
    tk = _tile(kc, (1408, 512, 256, 128))
    kpb = kc // tk
    o_spec = pl.BlockSpec((tm, tn), lambda p, q, k: (p, q))
    return _mm(
        name, "nn", a, wg, (S // tm, N // tn, nb * kpb),
        pl.BlockSpec((tm, tk), lambda p, q, k: (p, k)),
        pl.BlockSpec((None, tk, tn), lambda p, q, k: (k // kpb, k % kpb, q)),
        [((S, N), F32, o_spec)], (tm, tn),
        extras=[(res, o_spec)], epilogue=lambda acc, r: (r + acc,))


def _mm_nt_rowblk(name, g, wg, out_dtype, after=None):
    S, N = g.shape
    nb, kc, _ = wg.shape
    tm = _tile(S, (512,))
    tn = _tile(kc, (1408, 512, 256, 128))
    kpb = kc // tn
    return _mm(
        name, "nt", g, wg, (nb * kpb, S // tm, 1),
        pl.BlockSpec((tm, N), lambda p, q, k: (q, 0)),
        pl.BlockSpec((None, tn, N), lambda p, q, k: (p // kpb, p % kpb, 0)),
        [((S, nb * kc), out_dtype, pl.BlockSpec((tm, tn), lambda p, q, k: (q, p)))], (tm, tn), after=after)


def _mm_nt_colblk(name, g, wg):
    S = g.shape[0]
    nb, K, nc = wg.shape
    tm = _tile(S, (512,))
    tn = _tile(K, (2048, 1024))
    tk = _tile(nc, (1408, 640, 512, 256, 128))
    npb = nc // tk
    return _mm(
        name, "nt", g, wg, (S // tm, K // tn, nb * npb),
        pl.BlockSpec((tm, tk), lambda p, q, k: (p, k)),
        pl.BlockSpec((None, tn, tk), lambda p, q, k: (k // npb, q, k % npb)),
        [((S, K), F32, pl.BlockSpec((tm, tn), lambda p, q, k: (p, q)))], (tm, tn))


def _mm_tn_colblk(name, a, g, nb, split2=False):
    S, K = a.shape
    ntot = g.shape[-1] * (2 if split2 else 1)
    nc = ntot // nb
    ti = _tile(K, (1024,))
    tn = _tile(nc, (1408, 640, 512, 256, 128))
    ts = _tile(S, (2048, 1024, 512, 256))
    npb = nc // tn
    half = nb * npb // 2
    if split2:
        b_spec = pl.BlockSpec((None, ts, tn), lambda p, q, k: (q // half, k, q % half))
    else:
        b_spec = pl.BlockSpec((ts, tn), lambda p, q, k: (k, q))
    return _mm(
        name, "tn", a, g, (K // ti, nb * npb, S // ts),
        pl.BlockSpec((ts, ti), lambda p, q, k: (k, p)),
        b_spec,
        [((nb, K, nc), BF16, pl.BlockSpec((None, ti, tn), lambda p, q, k: (q // npb, p, q % npb)))], (ti, tn))


def _mm_tn_rowblk(name, a, g, nb):
    S, ktot = a.shape
    N = g.shape[1]
    kc = ktot // nb
    ti = _tile(kc, (1408, 512, 256, 128))
    tj = _tile(N, (1024,))
    ts = _tile(S, (2048, 1024, 512, 256))
    ipb = kc // ti
    return _mm(
        name, "tn", a, g, (nb * ipb, N // tj, S // ts),
        pl.BlockSpec((ts, ti), lambda p, q, k: (k, p)),
        pl.BlockSpec((ts, tj), lambda p, q, k: (k, q)),
        [((nb, kc, N), BF16, pl.BlockSpec((None, ti, tj), lambda p, q, k: (p // ipb, p % ipb, q)))], (ti, tj))


def _rms_fwd(name, x, gain):
    S, D = x.shape
    tm = _tile(S, (512,))

    def body(x_ref, g_ref, o_ref):
        xv = x_ref[...]
        r = lax.rsqrt(jnp.mean(xv * xv, axis=-1, keepdims=True) + EPS)
        o_ref[...] = (xv * r * g_ref[...]).astype(o_ref.dtype)

    return pl.pallas_call(
        body, name=name, grid=(S // tm,),
        in_specs=[pl.BlockSpec((tm, D), lambda i: (i, 0)), pl.BlockSpec((1, D), lambda i: (0, 0))],
        out_specs=pl.BlockSpec((tm, D), lambda i: (i, 0)),
        out_shape=jax.ShapeDtypeStruct((S, D), BF16),
        compiler_params=_cparams(("parallel",)),
    )(x, gain)


def _rms_bwd_math(xv, gain, dh):
    r = lax.rsqrt(jnp.mean(xv * xv, axis=-1, keepdims=True) + EPS)
    xhat = xv * r
    dxhat = dh * gain
    c = jnp.mean(dxhat * xhat, axis=-1, keepdims=True)
    return r * (dxhat - xhat * c), dh * xhat


def _rows_to_8(v):
    tm, C = v.shape
    return jnp.sum(v.reshape(tm // 8, 8, C), axis=0)


def _rms_bwd(name, x, gain, dh, dres, after=None):
    S, D = x.shape
    tm = _tile(S, (256,))
    n = S // tm
    order = [] if after is None else [after]

    def body(x_ref, g_ref, dh_ref, dr_ref, *rest):
        dx_ref, dxb_ref, dg_ref, acc_ref = rest[len(order):]
        i = pl.program_id(0)
        dxn, dgr = _rms_bwd_math(x_ref[...], g_ref[...], dh_ref[...])
        dx = dr_ref[...] + dxn
        dx_ref[...] = dx
        dxb_ref[...] = dx.astype(BF16)

        @pl.when(i == 0)
        def _():
            acc_ref[...] = jnp.zeros_like(acc_ref)

        acc_ref[...] += _rows_to_8(dgr)

        @pl.when(i == n - 1)
        def _():
            dg_ref[...] = jnp.sum(acc_ref[...], axis=0, keepdims=True)

    row = pl.BlockSpec((tm, D), lambda i: (i, 0))
    vec = pl.BlockSpec((1, D), lambda i: (0, 0))
    return pl.pallas_call(
        body, name=name, grid=(n,),
        in_specs=[row, vec, row, row] + [_ANY] * len(order),
        out_specs=[row, row, vec],
        out_shape=[jax.ShapeDtypeStruct((S, D), F32), jax.ShapeDtypeStruct((S, D), BF16),
                   jax.ShapeDtypeStruct((1, D), F32)],
        scratch_shapes=[pltpu.VMEM((8, D), F32)],
        compiler_params=_cparams(("arbitrary",)),
    )(x, gain, dh, dres, *order)


def _loss_head(y, tgt):
    S, D = y.shape
    tm = _tile(S, (256,))
    n = S // tm

    def body(y_ref, t_ref, dy_ref, dyb_ref, l_ref):
        i = pl.program_id(0)
        e = y_ref[...] - t_ref[...]
        dy = e * (1.0 / D)
        dy_ref[...] = dy
        dyb_ref[...] = dy.astype(BF16)

        @pl.when(i == 0)
        def _():
            l_ref[...] = jnp.zeros_like(l_ref)

        sq = _rows_to_8(e * e)
        part = sq[:, 0:LANES]
        for t in range(1, D // LANES):
            part = part + sq[:, t * LANES:(t + 1) * LANES]
        l_ref[...] += part * (0.5 / D)

    row = pl.BlockSpec((tm, D), lambda i: (i, 0))
    return pl.pallas_call(
        body, name="loss_head", grid=(n,),
        in_specs=[row, row],
        out_specs=[row, row, pl.BlockSpec((8, LANES), lambda i: (0, 0))],
        out_shape=[jax.ShapeDtypeStruct((S, D), F32), jax.ShapeDtypeStruct((S, D), BF16),
                   jax.ShapeDtypeStruct((8, LANES), F32)],
        compiler_params=_cparams(("arbitrary",)),
    )(y, tgt)


def _up_act_fwd(h2, wg, cw, cb):
    S, K = h2.shape
    nb, _, nc = wg.shape
    Fh = nb * nc // 2
    tm = _tile(S, (2048, 1024, 512))
    tn = _tile(nc, (256, 128))
    sub = _tile(tm, (512,))
    ch = _tile(sub, (32,))
    npb = nc // tn
    hb = nb // 2
    ncol = Fh // tn

    def body(a_ref, bg_ref, bv_ref, cw_ref, cb_ref, u_ref, uc_ref, o_ref, buf_ref, halo_ref):
        j = pl.program_id(1)

        @pl.when(pl.program_id(0) == 0)
        def _():
            halo_ref[j] = jnp.zeros((2, 8, tn), F32)

        buf_ref[:, 0:8, :] = halo_ref[j]
        cws, cbs = [cw_ref[0], cw_ref[1]], [cb_ref[0], cb_ref[1]]
        for r in range(tm // sub):
            for s, b_ref in enumerate((bg_ref, bv_ref)):
                buf_ref[s, 8 + r * sub:8 + (r + 1) * sub, :] = jnp.dot(
                    a_ref[r * sub:(r + 1) * sub, :], b_ref[...], preferred_element_type=F32)
            for c in range(sub // ch):
                base = r * sub + c * ch
                ucs = []
                for s in range(2):
                    ext = buf_ref[s, base:base + ch + 8, :]
                    us = ext[8:]
                    uc = (cws[s][0:1] * pltpu.roll(ext, 2, 0)[8:] + cws[s][1:2] * pltpu.roll(ext, 1, 0)[8:]
                          + cws[s][2:3] * us + cbs[s])
                    u_ref[s, base:base + ch, :] = us.astype(BF16)
                    uc_ref[s, base:base + ch, :] = uc.astype(BF16)
                    ucs.append(uc)
                gate, val = ucs
                o_ref[base:base + ch, :] = (gate * (1.0 / (1.0 + jnp.exp(-gate))) * val).astype(BF16)
        halo_ref[j] = buf_ref[:, tm:tm + 8, :]

    both = pl.BlockSpec((2, tm, tn), lambda i, j: (0, i, j))
    return pl.pallas_call(
        body, name="up_act_fwd", grid=(S // tm, ncol),
        in_specs=[pl.BlockSpec((tm, K), lambda i, j: (i, 0)),
                  pl.BlockSpec((None, K, tn), lambda i, j: (j // npb, 0, j % npb)),
                  pl.BlockSpec((None, K, tn), lambda i, j: (j // npb + hb, 0, j % npb)),
                  pl.BlockSpec((2, 3, tn), lambda i, j: (0, 0, j)),
                  pl.BlockSpec((2, 1, tn), lambda i, j: (0, 0, j))],
        out_specs=[both, both, pl.BlockSpec((tm, tn), lambda i, j: (i, j))],
        out_shape=[jax.ShapeDtypeStruct((2, S, Fh), BF16), jax.ShapeDtypeStruct((2, S, Fh), BF16),
                   jax.ShapeDtypeStruct((S, Fh), BF16)],
        scratch_shapes=[pltpu.VMEM((2, tm + 8, tn), F32), pltpu.VMEM((ncol, 2, 8, tn), F32)],
        compiler_params=_cparams(("arbitrary", "arbitrary")),
    )(h2, wg, wg, cw, cb)


def _down_act_bwd(dxb, wg, uc, after=None):
    S, D = dxb.shape
    nb, kc, _ = wg.shape
    Fh = nb * kc
    tm = _tile(S, (1024, 512))
    sub = _tile(tm, (256,))
    ch = _tile(sub, (32,))
    t128 = kc // LANES
    tn = 2 * LANES
    order = [] if after is None else [after]

    def body(dx_ref, b0_ref, b1_ref, uc_ref, *rest):
        duc_ref, bcat_ref, da_ref = rest[len(order):]
        bcat_ref[0:LANES, :] = b0_ref[...]
        bcat_ref[LANES:tn, :] = b1_ref[...]
        for r in range(tm // sub):
            rows = slice(r * sub, (r + 1) * sub)
            da_ref[rows, :] = lax.dot_general(dx_ref[rows, :], bcat_ref[...], _DN["nt"], preferred_element_type=F32)
            for c in range(sub // ch):
                cs = slice(r * sub + c * ch, r * sub + (c + 1) * ch)
                da = da_ref[cs, :]
                gate, val = uc_ref[0, cs, :].astype(F32), uc_ref[1, cs, :].astype(F32)
                sig = 1.0 / (1.0 + jnp.exp(-gate))
                duc_ref[0, cs, :] = (da * val * (sig * (1.0 + gate * (1.0 - sig)))).astype(BF16)
                duc_ref[1, cs, :] = (da * (gate * sig)).astype(BF16)

    both = pl.BlockSpec((2, tm, tn), lambda i, j: (0, i, j))
    return pl.pallas_call(
        body, name="down_act_bwd", grid=(S // tm, Fh // tn),
        in_specs=[pl.BlockSpec((tm, D), lambda i, j: (i, 0)),
                  pl.BlockSpec((None, LANES, D), lambda i, j: ((2 * j) // t128, (2 * j) % t128, 0)),
                  pl.BlockSpec((None, LANES, D), lambda i, j: ((2 * j + 1) // t128, (2 * j + 1) % t128, 0)),
                  both] + [_ANY] * len(order),
        out_specs=both,
        out_shape=jax.ShapeDtypeStruct((2, S, Fh), BF16),
        scratch_shapes=[pltpu.VMEM((tn, D), BF16), pltpu.VMEM((tm, tn), F32)],
        compiler_params=_cparams(("parallel", "parallel")),
    )(dxb, wg, wg, uc, *order)


def _up_bwd_in(duc, u, cw, wg):
    _, S, Fh = duc.shape
    nb, K, nc = wg.shape
    tm = _tile(S, (1024, 512))
    tk = _tile(nc, (1408, 640, 512, 256, 128))
    sub = _tile(tm, (256,))
    ch = _tile(sub, (64,))
    hr = 16
    npb = nc // tk
    nk = nb * npb
    half = nk // 2
    n, nh = S // tm, tm // hr

    def body(d_ref, h_ref, u_ref, cw_ref, b_ref, du_ref, o_ref, cg_ref):
        i, k = pl.program_id(0), pl.program_id(1)
        keep = (i < n - 1).astype(F32)

        @pl.when(i == 0)
        def _():
            cg_ref[k] = jnp.zeros((8, tk), F32)

        @pl.when(k == 0)
        def _():
            o_ref[...] = jnp.zeros_like(o_ref)

        for r in range(tm // sub):
            for l in range(tk // LANES):
                ls = slice(l * LANES, (l + 1) * LANES)
                w = cw_ref[:, ls]
                sums = [jnp.zeros((8, LANES), F32) for _ in range(4)]
                for c in range(sub // ch):
                    base = r * sub + c * ch
                    if base + ch + hr <= tm:
                        ext = d_ref[base:base + ch + hr, ls].astype(F32)
                    else:
                        ext = jnp.concatenate([d_ref[base:base + ch, ls].astype(F32),
                                               h_ref[:, ls].astype(F32) * keep], axis=0)
                    d = ext[:ch]
                    d1 = pltpu.roll(ext, ch + hr - 1, 0)[:ch]
                    d2 = pltpu.roll(ext, ch + hr - 2, 0)[:ch]
                    du_ref[base:base + ch, ls] = (w[2:3] * d + w[1:2] * d1 + w[0:1] * d2).astype(BF16)
                    uv = u_ref[base:base + ch, ls].astype(F32)
                    for q, v in enumerate((d2 * uv, d1 * uv, d * uv, d)):
                        sums[q] = sums[q] + _rows_to_8(v)
                for q in range(4):
                    cg_ref[k, q:q + 1, ls] += jnp.sum(sums[q], axis=0, keepdims=True)
            rows = slice(r * sub, (r + 1) * sub)
            o_ref[rows, :] += lax.dot_general(du_ref[rows, :], b_ref[...], _DN["nt"], preferred_element_type=F32)

    blk = pl.BlockSpec((None, tm, tk), lambda i, k: (k // half, i, k % half))
    return pl.pallas_call(
        body, name="up_bwd_in", grid=(n, nk),
        in_specs=[blk,
                  pl.BlockSpec((None, hr, tk), lambda i, k: (k // half, jnp.minimum((i + 1) * nh, S // hr - 1), k % half)),
                  blk,
                  pl.BlockSpec((None, 3, tk), lambda i, k: (k // half, 0, k % half)),
                  pl.BlockSpec((None, K, tk), lambda i, k: (k // npb, 0, k % npb))],
        out_specs=[blk, pl.BlockSpec((tm, K), lambda i, k: (i, 0)), pl.BlockSpec((nk, 8, tk), lambda i, k: (0, 0, 0))],
        out_shape=[jax.ShapeDtypeStruct((2, S, Fh), BF16), jax.ShapeDtypeStruct((S, K), F32),
                   jax.ShapeDtypeStruct((nk, 8, tk), F32)],
        compiler_params=_cparams(("arbitrary", "arbitrary")),
    )(duc, duc, u, cw, wg)


def _pool_counts(i, tm, w):
    t = i * tm + lax.broadcasted_iota(jnp.int32, (tm, 1), 0)
    return jnp.minimum(t + 1, w).astype(F32)


def _pool_fwd(x, gain):
    S, D = x.shape
    tm = _tile(S, (256,))
    gd = D // len(POOL_WINDOWS)
    nh = tm // POOL_HALO

    def body(x_ref, xh_ref, g_ref, d_ref):
        i = pl.program_id(0)

        def norm(v):
            return v * lax.rsqrt(jnp.mean(v * v, axis=-1, keepdims=True) + EPS) * g_ref[...]

        h = norm(x_ref[...])
        hh = norm(xh_ref[...]) * (i > 0).astype(F32)
        ext = jnp.concatenate([hh, h], axis=0)
        for gi, w in enumerate(POOL_WINDOWS):
            sl = slice(gi * gd, (gi + 1) * gd)
            win = ext[:, sl]
            k = 1
            while k < w:
                win = win + pltpu.roll(win, k, 0)
                k *= 2
            mean = win[POOL_HALO:] / _pool_counts(i, tm, w)
            d_ref[:, sl] = (mean - h[:, sl]).astype(BF16)

    return pl.pallas_call(
        body, name="pool_fwd", grid=(S // tm,),
        in_specs=[pl.BlockSpec((tm, D), lambda i: (i, 0)),
                  pl.BlockSpec((POOL_HALO, D), lambda i: (jnp.maximum(i * nh - 1, 0), 0)),
                  pl.BlockSpec((1, D), lambda i: (0, 0))],
        out_specs=pl.BlockSpec((tm, D), lambda i: (i, 0)),
        out_shape=jax.ShapeDtypeStruct((S, D), BF16),
        compiler_params=_cparams(("parallel",)),
    )(x, x, gain)


def _pool_mm_fwd(d, wp, x, scale):
    S, D = d.shape
    G, gd, _ = wp.shape
    tm = _tile(S, (1024, 512))
    o_spec = pl.BlockSpec((tm, gd), lambda p, q, k: (p, q))
    return _mm(
        "pool_mm_fwd", "nn", d, wp, (S // tm, G, 1),
        o_spec, pl.BlockSpec((None, gd, gd), lambda p, q, k: (q, 0, 0)),
        [((S, D), F32, o_spec), ((S, D), F32, o_spec)], (tm, gd),
        extras=[(x, o_spec), (scale, pl.BlockSpec((1, gd), lambda p, q, k: (0, q)))],
        epilogue=lambda acc, xv, sc: (acc, xv + acc * sc))


def _pool_bwd_pre(dx1, ypre, scale):
    S, D = dx1.shape
    tm = _tile(S, (256,))
    n = S // tm

    def body(dx_ref, y_ref, s_ref, dy_ref, ds_ref, acc_ref):
        i = pl.program_id(0)
        dx = dx_ref[...]
        dy_ref[...] = (dx * s_ref[...]).astype(BF16)

        @pl.when(i == 0)
        def _():
            acc_ref[...] = jnp.zeros_like(acc_ref)

        acc_ref[...] += _rows_to_8(dx * y_ref[...])

        @pl.when(i == n - 1)
        def _():
            ds_ref[...] = jnp.sum(acc_ref[...], axis=0, keepdims=True)

    row = pl.BlockSpec((tm, D), lambda i: (i, 0))
    vec = pl.BlockSpec((1, D), lambda i: (0, 0))
    return pl.pallas_call(
        body, name="pool_bwd_pre", grid=(n,),
        in_specs=[row, row, vec], out_specs=[row, vec],
        out_shape=[jax.ShapeDtypeStruct((S, D), BF16), jax.ShapeDtypeStruct((1, D), F32)],
        scratch_shapes=[pltpu.VMEM((8, D), F32)],
        compiler_params=_cparams(("arbitrary",)),
    )(dx1, ypre, scale)


def _pool_mm_bwd_in(dyp, wp):
    S, D = dyp.shape
    G, gd, _ = wp.shape
    tm = _tile(S, (1024, 512))
    spec = pl.BlockSpec((tm, gd), lambda p, q, k: (p, q))
    return _mm(
        "pool_mm_bwd_in", "nt", dyp, wp, (S // tm, G, 1),
        spec, pl.BlockSpec((None, gd, gd), lambda p, q, k: (q, 0, 0)),
        [((S, D), F32, spec)], (tm, gd))


def _pool_mm_bwd_w(d, dyp):
    S, D = d.shape
    G = len(POOL_WINDOWS)
    gd = D // G
    ts = _tile(S, (1024, 512, 256))
    spec = pl.BlockSpec((ts, gd), lambda p, q, k: (k, p))
    return _mm(
        "pool_mm_bwd_w", "tn", d, dyp, (G, 1, S // ts),
        spec, spec,
        [((G, gd, gd), BF16, pl.BlockSpec((None, gd, gd), lambda p, q, k: (p, 0, 0)))], (gd, gd))


def _pool_bwd_post(dd, x, gain, dres):
    S, D = x.shape
    tm = _tile(S, (256,))
    n = S // tm
    gd = D // len(POOL_WINDOWS)
    nh = tm // POOL_HALO

    def body(dd_ref, ddh_ref, x_ref, g_ref, dr_ref, dx_ref, dxb_ref, dg_ref, acc_ref):
        i = pl.program_id(0)
        dd = dd_ref[...]
        halo = ddh_ref[...] * (i < n - 1).astype(F32)
        parts = []
        for gi, w in enumerate(POOL_WINDOWS):
            sl = slice(gi * gd, (gi + 1) * gd)
            ext = jnp.concatenate([dd[:, sl] / _pool_counts(i, tm, w), halo[:, sl] * (1.0 / w)], axis=0)
            k = 1
            while k < w:
                ext = ext + pltpu.roll(ext, tm + POOL_HALO - k, 0)
                k *= 2
            parts.append(ext[:tm] - dd[:, sl])
        dh = jnp.concatenate(parts, axis=1)
        dxn, dgr = _rms_bwd_math(x_ref[...], g_ref[...], dh)
        dx = dr_ref[...] + dxn
        dx_ref[...] = dx
        dxb_ref[...] = dx.astype(BF16)

        @pl.when(i == 0)
        def _():
            acc_ref[...] = jnp.zeros_like(acc_ref)

        acc_ref[...] += _rows_to_8(dgr)

        @pl.when(i == n - 1)
        def _():
            dg_ref[...] = jnp.sum(acc_ref[...], axis=0, keepdims=True)

    row = pl.BlockSpec((tm, D), lambda i: (i, 0))
    vec = pl.BlockSpec((1, D), lambda i: (0, 0))
    nxt = pl.BlockSpec((POOL_HALO, D), lambda i: (jnp.minimum((i + 1) * nh, S // POOL_HALO - 1), 0))
    return pl.pallas_call(
        body, name="pool_bwd_post", grid=(n,),
        in_specs=[row, nxt, row, vec, row],
        out_specs=[row, row, vec],
        out_shape=[jax.ShapeDtypeStruct((S, D), F32), jax.ShapeDtypeStruct((S, D), BF16),
                   jax.ShapeDtypeStruct((1, D), F32)],
        scratch_shapes=[pltpu.VMEM((8, D), F32)],
        compiler_params=_cparams(("arbitrary",)),
    )(dd, dd, x, gain, dres)


def _band_tables():
    i = np.arange(WINDOW)[:, None]
    j = np.arange(2 * WINDOW)[None, :]
    n = np.maximum(WINDOW + i - j, 0)
    max_exact = N_BUCKETS // 2
    nf = np.maximum(n, 1).astype(np.float32)
    large = max_exact + (np.log(nf / max_exact) / np.log(MAX_DISTANCE / max_exact)
                         * (N_BUCKETS - max_exact)).astype(np.int32)
    large = np.minimum(large, N_BUCKETS - 1)
    buckets = np.where(n < max_exact, n, large).astype(np.int32)
    dist = WINDOW + i - j
    in_win = ((dist >= 0) & (dist < WINDOW)).astype(np.int32)
    return buckets, in_win


def _bias_expand(rel_bias):
    H = rel_bias.shape[0]
    buckets, in_win = _band_tables()

    def body(rb_ref, bk_ref, win_ref, o_ref):
        h = pl.program_id(0)
        bk = bk_ref[...]
        acc = jnp.zeros(bk.shape, F32)
        for b in range(N_BUCKETS):
            acc = jnp.where(bk == b, rb_ref[h, b], acc)
        o_ref[...] = jnp.where(win_ref[...] > 0, acc, NEG_INF)

    full = pl.BlockSpec((WINDOW, 2 * WINDOW), lambda h: (0, 0))
    return pl.pallas_call(
        body, name="bias_expand", grid=(H,),
        in_specs=[pl.BlockSpec(memory_space=pltpu.SMEM), full, full],
        out_specs=pl.BlockSpec((None, WINDOW, 2 * WINDOW), lambda h: (h, 0, 0)),
        out_shape=jax.ShapeDtypeStruct((H, WINDOW, 2 * WINDOW), F32),
        compiler_params=_cparams(("parallel",)),
    )(rel_bias, jnp.asarray(buckets), jnp.asarray(in_win))


def _bias_reduce(dbias):
    H = dbias.shape[0]
    buckets, in_win = _band_tables()

    def body(d_ref, bk_ref, win_ref, o_ref):
        bk = jnp.where(win_ref[...] > 0, bk_ref[...], -1)
        d = d_ref[...]
        lane = lax.broadcasted_iota(jnp.int32, (8, LANES), 1)
        out = jnp.zeros((8, LANES), F32)
        for b in range(N_BUCKETS):
            s = jnp.sum(jnp.where(bk == b, d, 0.0))
            out = jnp.where(lane == b, s, out)
        o_ref[...] = out

    full = pl.BlockSpec((WINDOW, 2 * WINDOW), lambda h: (0, 0))
    return pl.pallas_call(
        body, name="bias_reduce", grid=(H,),
        in_specs=[pl.BlockSpec((None, WINDOW, 2 * WINDOW), lambda h: (h, 0, 0)), full, full],
        out_specs=pl.BlockSpec((None, 8, LANES), lambda h: (h, 0, 0)),
        out_shape=jax.ShapeDtypeStruct((H, 8, LANES), F32),
        compiler_params=_cparams(("parallel",)),
    )(dbias, jnp.asarray(buckets), jnp.asarray(in_win))


def _half_rsqrt(t, lo):
    sq = t * t
    s_lo = jnp.sum(jnp.where(lo, sq, 0.0), axis=-1, keepdims=True)
    s_hi = jnp.sum(jnp.where(lo, 0.0, sq), axis=-1, keepdims=True)
    return jnp.where(lo, lax.rsqrt(s_lo * (1.0 / HEAD_DIM) + EPS), lax.rsqrt(s_hi * (1.0 / HEAD_DIM) + EPS))


def _qk_norm_fwd(qkv, qg, kg, n_q, n_kv):
    S, W = qkv.shape
    tm = _tile(S, (256,))
    qw = n_q * HEAD_DIM
    kw = n_kv * HEAD_DIM
    scale = HEAD_DIM ** -0.5

    def body(x_ref, qg_ref, kg_ref, q_ref, k_ref, v_ref):
        lo = lax.broadcasted_iota(jnp.int32, (tm, LANES), 1) < HEAD_DIM
        for t in range(qw // LANES):
            q = x_ref[:, t * LANES:(t + 1) * LANES]
            q_ref[:, t * LANES:(t + 1) * LANES] = (q * _half_rsqrt(q, lo) * qg_ref[...] * scale).astype(BF16)
        for p in range(kw // LANES):
            k = x_ref[:, qw + p * LANES:qw + (p + 1) * LANES]
            kn = k * _half_rsqrt(k, lo) * kg_ref[...]
            v = x_ref[:, qw + kw + p * LANES:qw + kw + (p + 1) * LANES]
            for src, dst in ((kn, k_ref), (v, v_ref)):
                rolled = pltpu.roll(src, HEAD_DIM, 1)
                dst[2 * p] = jnp.where(lo, src, rolled).astype(BF16)
                dst[2 * p + 1] = jnp.where(lo, rolled, src).astype(BF16)

    vec = pl.BlockSpec((1, LANES), lambda i: (0, 0))
    kv = pl.BlockSpec((n_kv, tm, LANES), lambda i: (0, i, 0))
    return pl.pallas_call(
        body, name="qk_norm_fwd", grid=(S // tm,),
        in_specs=[pl.BlockSpec((tm, W), lambda i: (i, 0)), vec, vec],
        out_specs=[pl.BlockSpec((tm, qw), lambda i: (i, 0)), kv, kv],
        out_shape=[jax.ShapeDtypeStruct((S, qw), BF16), jax.ShapeDtypeStruct((n_kv, S, LANES), BF16),
                   jax.ShapeDtypeStruct((n_kv, S, LANES), BF16)],
        compiler_params=_cparams(("parallel",)),
    )(qkv, qg, kg)


def _qk_norm_bwd(qkv, qg, kg, dq, dkc, dkp, dvc, dvp, n_q, n_kv):
    S, W = qkv.shape
    tm = _tile(S, (4 * WINDOW, WINDOW))
    n = S // tm
    per = tm // WINDOW
    qw = n_q * HEAD_DIM
    kw = n_kv * HEAD_DIM
    scale = HEAD_DIM ** -0.5

    def body(x_ref, qg_ref, kg_ref, dq_ref, dkc_ref, dkp_ref, dkn_ref, dvc_ref, dvp_ref, dvn_ref,
             dx_ref, dqg_ref, dkg_ref, accq_ref, acck_ref):
        i = pl.program_id(0)
        lo = lax.broadcasted_iota(jnp.int32, (tm, LANES), 1) < HEAD_DIM
        has_next = (i < n - 1).astype(F32)

        @pl.when(i == 0)
        def _():
            accq_ref[...] = jnp.zeros_like(accq_ref)
            acck_ref[...] = jnp.zeros_like(acck_ref)

        def norm_bwd(t, dy, gain):
            r = _half_rsqrt(t, lo)
            xhat = t * r
            dxhat = dy * gain
            prod = dxhat * xhat
            c_lo = jnp.sum(jnp.where(lo, prod, 0.0), axis=-1, keepdims=True) * (1.0 / HEAD_DIM)
            c_hi = jnp.sum(jnp.where(lo, 0.0, prod), axis=-1, keepdims=True) * (1.0 / HEAD_DIM)
            return r * (dxhat - xhat * jnp.where(lo, c_lo, c_hi)), _rows_to_8(dy * xhat)

        for t in range(qw // LANES):
            sl = slice(t * LANES, (t + 1) * LANES)
            dt, dg = norm_bwd(x_ref[:, sl], dq_ref[:, sl] * scale, qg_ref[...])
            dx_ref[:, sl] = dt.astype(BF16)
            accq_ref[...] += dg

        def pair(cur_ref, prev_ref, next_ref, p):
            folded = []
            for h in (2 * p, 2 * p + 1):
                later = next_ref[h] * has_next
                if per > 1:
                    later = jnp.concatenate([prev_ref[h, WINDOW:, :], later], axis=0)
                tot = cur_ref[h] + later
                folded.append(tot + pltpu.roll(tot, HEAD_DIM, 1))
            return jnp.where(lo, folded[0], folded[1])

        for p in range(kw // LANES):
            sl = slice(qw + p * LANES, qw + (p + 1) * LANES)
            dt, dg = norm_bwd(x_ref[:, sl], pair(dkc_ref, dkp_ref, dkn_ref, p), kg_ref[...])
            dx_ref[:, sl] = dt.astype(BF16)
            acck_ref[...] += dg
            sl = slice(qw + kw + p * LANES, qw + kw + (p + 1) * LANES)
            dx_ref[:, sl] = pair(dvc_ref, dvp_ref, dvn_ref, p).astype(BF16)

        @pl.when(i == n - 1)
        def _():
            for acc_ref, o_ref in ((accq_ref, dqg_ref), (acck_ref, dkg_ref)):
                s = jnp.sum(acc_ref[...], axis=0, keepdims=True)
                o_ref[...] = s + pltpu.roll(s, HEAD_DIM, 1)

    vec = pl.BlockSpec((1, LANES), lambda i: (0, 0))
    cur = pl.BlockSpec((n_kv, tm, LANES), lambda i: (0, i, 0))
    nxt = pl.BlockSpec((n_kv, WINDOW, LANES), lambda i: (0, jnp.minimum((i + 1) * per, S // WINDOW - 1), 0))
    return pl.pallas_call(
        body, name="qk_norm_bwd", grid=(n,),
        in_specs=[pl.BlockSpec((tm, W), lambda i: (i, 0)), vec, vec, pl.BlockSpec((tm, qw), lambda i: (i, 0)),
                  cur, cur, nxt, cur, cur, nxt],
        out_specs=[pl.BlockSpec((tm, W), lambda i: (i, 0)), vec, vec],
        out_shape=[jax.ShapeDtypeStruct((S, W), BF16), jax.ShapeDtypeStruct((1, LANES), F32),
                   jax.ShapeDtypeStruct((1, LANES), F32)],
        scratch_shapes=[pltpu.VMEM((8, LANES), F32), pltpu.VMEM((8, LANES), F32)],
        compiler_params=_cparams(("arbitrary",)),
    )(qkv, qg, kg, dq, dkc, dkp, dkp, dvc, dvp, dvp)


def _attn_specs(n_kv, per_step):
    kvs = per_step if n_kv % per_step == 0 else 1
    q = pl.BlockSpec((WINDOW, kvs * GQA_GROUP * HEAD_DIM), lambda kh, n: (n, kh))
    cur = pl.BlockSpec((kvs, WINDOW, LANES), lambda kh, n: (kh, n, 0))
    prev = pl.BlockSpec((kvs, WINDOW, LANES), lambda kh, n: (kh, jnp.maximum(n - 1, 0), 0))
    bias = pl.BlockSpec((kvs * GQA_GROUP, WINDOW, 2 * WINDOW), lambda kh, n: (kh, 0, 0))
    sink = pl.BlockSpec(memory_space=pltpu.SMEM)
    return kvs, q, cur, prev, bias, sink


def _stack_heads(x_ref, lo, h):
    parts = []
    for g in range(GQA_GROUP):
        c0 = (h * GQA_GROUP // 2 + g // 2) * LANES
        t = x_ref[:, c0:c0 + LANES]
        parts.append(jnp.where(lo if g % 2 == 0 else jnp.logical_not(lo), t, jnp.zeros_like(t)))
    return jnp.concatenate(parts, axis=0)


def _unstack_heads(v, lo):
    return [jnp.where(lo, v[(2 * p) * WINDOW:(2 * p + 1) * WINDOW], v[(2 * p + 1) * WINDOW:(2 * p + 2) * WINDOW])
            for p in range(GQA_GROUP // 2)]


def _attn_probs(qs, kk, bias, s_ref, head0, has_prev):
    rows = GQA_GROUP * WINDOW
    s = lax.dot_general(qs, kk, _DN["nt"], preferred_element_type=F32) + bias.reshape(rows, 2 * WINDOW)
    col = lax.broadcasted_iota(jnp.int32, (rows, 2 * WINDOW), 1)
    s = jnp.where(jnp.logical_or(has_prev, col >= WINDOW), s, NEG_INF)
    sink = jnp.concatenate([jnp.full((WINDOW, 1), s_ref[head0 + g], F32) for g in range(GQA_GROUP)], axis=0)
    m = jnp.maximum(jnp.max(s, axis=-1, keepdims=True), sink)
    p = jnp.exp(s - m)
    es = jnp.exp(sink - m)
    den = jnp.sum(p, axis=-1, keepdims=True) + es
    return p / den, es / den


def _attn_fwd(q, kcat, vcat, biasm, sinks):
    S, qw = q.shape
    n_kv = kcat.shape[0]
    kvs, qs, cur, prev, bias, sink = _attn_specs(n_kv, 1)
    pairs = GQA_GROUP // 2

    def body(q_ref, kc_ref, kp_ref, vc_ref, vp_ref, b_ref, s_ref, o_ref):
        kh, n = pl.program_id(0), pl.program_id(1)
        lo = lax.broadcasted_iota(jnp.int32, (WINDOW, LANES), 1) < HEAD_DIM
        for h in range(kvs):
            kk = jnp.concatenate([kp_ref[h], kc_ref[h]], axis=0)
            vv = jnp.concatenate([vp_ref[h], vc_ref[h]], axis=0)
            pn, _ = _attn_probs(_stack_heads(q_ref, lo, h), kk, b_ref[h * GQA_GROUP:(h + 1) * GQA_GROUP], s_ref,
                                (kh * kvs + h) * GQA_GROUP, n > 0)
            o = jnp.dot(pn.astype(BF16), vv, preferred_element_type=F32)
            for p, t in enumerate(_unstack_heads(o, lo)):
                o_ref[:, (h * pairs + p) * LANES:(h * pairs + p + 1) * LANES] = t.astype(BF16)

    return pl.pallas_call(
        body, name="attn_fwd", grid=(n_kv // kvs, S // WINDOW),
        in_specs=[qs, cur, prev, cur, prev, bias, sink],
        out_specs=qs,
        out_shape=jax.ShapeDtypeStruct((S, qw), BF16),
        compiler_params=_cparams(("parallel", "parallel")),
    )(q, kcat, kcat, vcat, vcat, biasm, sinks)


def _attn_bwd(q, kcat, vcat, biasm, sinks, do):
    S, qw = q.shape
    n_kv = kcat.shape[0]
    H = n_kv * GQA_GROUP
    kvs, qs, cur, prev, bias, sink = _attn_specs(n_kv, ATTN_BWD_KV_PER_STEP)
    pairs = GQA_GROUP // 2

    def body(q_ref, kc_ref, kp_ref, vc_ref, vp_ref, b_ref, s_ref, do_ref,
             dq_ref, dkc_ref, dkp_ref, dvc_ref, dvp_ref, db_ref, ds_ref):
        kh, n = pl.program_id(0), pl.program_id(1)
        lo = lax.broadcasted_iota(jnp.int32, (WINDOW, LANES), 1) < HEAD_DIM

        @pl.when(n == 0)
        def _():
            db_ref[...] = jnp.zeros_like(db_ref)
            ds_ref[...] = jnp.zeros_like(ds_ref)

        for h in range(kvs):
            gs = slice(h * GQA_GROUP, (h + 1) * GQA_GROUP)
            kk = jnp.concatenate([kp_ref[h], kc_ref[h]], axis=0)
            vv = jnp.concatenate([vp_ref[h], vc_ref[h]], axis=0)
            qs_ = _stack_heads(q_ref, lo, h)
            dos = _stack_heads(do_ref, lo, h)
            pn, ps = _attn_probs(qs_, kk, b_ref[gs], s_ref, (kh * kvs + h) * GQA_GROUP, n > 0)
            dp = lax.dot_general(dos, vv, _DN["nt"], preferred_element_type=F32)
            delta = jnp.sum(pn * dp, axis=-1, keepdims=True)
            ds = pn * (dp - delta)
            db_ref[gs] += ds.reshape(GQA_GROUP, WINDOW, 2 * WINDOW)
            ds_ref[h] += (jnp.zeros((GQA_GROUP, LANES), F32)
                          - jnp.sum((ps * delta).reshape(GQA_GROUP, WINDOW, 1), axis=1))
            dsb = ds.astype(BF16)
            dq = jnp.dot(dsb, kk, preferred_element_type=F32)
            for p, t in enumerate(_unstack_heads(dq, lo)):
                dq_ref[:, (h * pairs + p) * LANES:(h * pairs + p + 1) * LANES] = t
            dk = lax.dot_general(dsb, qs_, _DN["tn"], preferred_element_type=F32)
            dv = lax.dot_general(pn.astype(BF16), dos, _DN["tn"], preferred_element_type=F32)
            dkp_ref[h] = dk[:WINDOW]
            dkc_ref[h] = dk[WINDOW:]
            dvp_ref[h] = dv[:WINDOW]
            dvc_ref[h] = dv[WINDOW:]

    part = jax.ShapeDtypeStruct((n_kv, S, LANES), F32)
    return pl.pallas_call(
        body, name="attn_bwd", grid=(n_kv // kvs, S // WINDOW),
        in_specs=[qs, cur, prev, cur, prev, bias, sink, qs],
        out_specs=[qs, cur, cur, cur, cur, bias, pl.BlockSpec((kvs, 8, LANES), lambda kh, n: (kh, 0, 0))],
        out_shape=[jax.ShapeDtypeStruct((S, qw), F32), part, part, part, part,
                   jax.ShapeDtypeStruct((H, WINDOW, 2 * WINDOW), F32),
                   jax.ShapeDtypeStruct((n_kv, 8, LANES), F32)],
        compiler_params=_cparams(("parallel", "arbitrary")),
    )(q, kcat, kcat, vcat, vcat, biasm, sinks, do)


def _adamw(name, w, g, m, v):
    shape = w.shape
    C = shape[-1]
    R = int(np.prod(shape[:-1]))
    tr = R
    if R * C * 4 > (1 << 20):
        tr = _tile(R, tuple(t for t in (512, 256, 128, 64, 32, 16, 8) if t * C * 4 <= (3 << 19)))
    c1 = 1.0 - ADAM_B1 ** ADAM_STEP
    c2 = 1.0 - ADAM_B2 ** ADAM_STEP

    def body(w_ref, g_ref, m_ref, v_ref, d_ref, nm_ref, nv_ref):
        gv = g_ref[...]
        nm = ADAM_B1 * m_ref[...] + (1.0 - ADAM_B1) * gv
        nv = ADAM_B2 * v_ref[...] + (1.0 - ADAM_B2) * (gv * gv)
        d_ref[...] = -ADAM_LR * ((nm / c1) / (jnp.sqrt(nv / c2) + ADAM_EPS) + ADAM_WD * w_ref[...])
        nm_ref[...] = nm
        nv_ref[...] = nv

    spec = pl.BlockSpec((tr, C), lambda i: (i, 0))
    outs = pl.pallas_call(
        body, name=name, grid=(R // tr,),
        in_specs=[spec] * 4, out_specs=[spec] * 3,
        out_shape=[jax.ShapeDtypeStruct((R, C), F32)] * 3,
        compiler_params=_cparams(("parallel",)),
    )(*[t.reshape(R, C) for t in (w, g, m, v)])
    return [o.reshape(shape) for o in outs]


def _place():
    return lax.axis_index("x"), lax.axis_index("y"), lax.axis_index("c")


def _other_chips(x, y):
    return [(1 - x, y), (x, 1 - y), (1 - x, 1 - y)]


def _swap_halves(name, grads):
    T = len(grads)

    def body(*refs):
        ins, outs = refs[:T], refs[T:2 * T]
        send, recv = refs[2 * T:]
        x, y, c = _place()

        def copy(t):
            half = grads[t].shape[1] // 2
            src = ins[t].at[:, pl.ds(pl.multiple_of((1 - c) * half, 16), half), :]
            return pltpu.make_async_remote_copy(
                src_ref=src, dst_ref=outs[t], send_sem=send.at[t], recv_sem=recv.at[t],
                device_id=(x, y, 1 - c), device_id_type=MESH)

        for t in range(T):
            copy(t).start()
        for t in range(T):
            copy(t).wait()

    return pl.pallas_call(
        body, name=name,
        in_specs=[_ANY] * T, out_specs=[_ANY] * T,
        out_shape=[jax.ShapeDtypeStruct((g.shape[0], g.shape[1] // 2, g.shape[2]), g.dtype) for g in grads],
        scratch_shapes=[pltpu.SemaphoreType.DMA((T,)), pltpu.SemaphoreType.DMA((T,))],
        compiler_params=pltpu.CompilerParams(has_side_effects=True),
    )(*grads)


def _add_half(name, g, other, c_arr):
    nb, R, C = g.shape
    half = R // 2
    tr = _tile(half, (256, 128, 64, 32, 16))
    n = half // tr

    def body(c_ref, g_ref, o_ref, s_ref):
        s_ref[...] = (g_ref[...].astype(F32) + o_ref[...].astype(F32)).astype(BF16)

    return pl.pallas_call(
        body, name=name,
        grid_spec=pltpu.PrefetchScalarGridSpec(
            num_scalar_prefetch=1, grid=(nb, n),
            in_specs=[pl.BlockSpec((None, tr, C), lambda b, i, c: (b, c[0] * n + i, 0)),
                      pl.BlockSpec((None, tr, C), lambda b, i, c: (b, i, 0))],
            out_specs=pl.BlockSpec((None, tr, C), lambda b, i, c: (b, i, 0))),
        out_shape=jax.ShapeDtypeStruct((nb, half, C), BF16),
        compiler_params=_cparams(("parallel", "parallel")),
    )(c_arr, g, other)


def _in_hbm(arrays):
    return [pltpu.with_memory_space_constraint(a, pltpu.HBM) for a in arrays]


def _chip_copy(src, dst, send, recv, chip, c):
    return pltpu.make_async_remote_copy(src_ref=src, dst_ref=dst, send_sem=send, recv_sem=recv,
                                        device_id=(chip[0], chip[1], c), device_id_type=MESH)


def _cast_slot(name, w, l, chip_arr, dtype):
    _, R, C = w.shape
    tr = _tile(R, tuple(t for t in (1024, 512, 256, 128, 64, 32, 16) if t * C * 4 <= (1 << 21)))

    def body(chip_ref, w_ref, o_ref):
        o_ref[...] = w_ref[...].astype(dtype)

    return pl.pallas_call(
        body, name=name,
        grid_spec=pltpu.PrefetchScalarGridSpec(
            num_scalar_prefetch=1, grid=(R // tr,),
            in_specs=[pl.BlockSpec((None, tr, C), lambda i, chip: (l, i, 0))],
            out_specs=pl.BlockSpec((None, tr, C), lambda i, chip: (chip[0], i, 0))),
        out_shape=jax.ShapeDtypeStruct((N_CHIPS, R, C), dtype),
        compiler_params=_cparams(("parallel",)),
    )(chip_arr, w)


def _gather_start(groups):
    sizes = [len(g) for g in groups]
    flat = [b for g in groups for b in g]
    n, G = len(flat), len(groups)

    def body(*refs):
        bufs = refs[:n]
        sems = refs[n:n + 2 * G]
        token = refs[-1]
        x, y, c = _place()
        chips = _other_chips(x, y)
        me = 2 * x + y
        k = 0
        for l in range(G):
            for t in range(sizes[l]):
                for j in range(3):
                    _chip_copy(bufs[k].at[me], bufs[k].at[me], sems[2 * l].at[3 * t + j], sems[2 * l + 1].at[3 * t + j],
                               chips[j], c).start()
                k += 1
        token[...] = jnp.zeros_like(token)

    sem_shapes = []
    for s in sizes:
        sem_shapes += [pltpu.SemaphoreType.DMA((3 * s,)), pltpu.SemaphoreType.DMA((3 * s,))]
    outs = pl.pallas_call(
        body, name="gather_start",
        out_shape=(*sem_shapes, *[pltpu.HBM(b.shape, b.dtype) for b in flat], jax.ShapeDtypeStruct((8, LANES), F32)),
        in_specs=[_HBM] * n,
        out_specs=(*[_SEM] * (2 * G), *[_HBM] * n, pl.BlockSpec(memory_space=pltpu.VMEM)),
        input_output_aliases={t: 2 * G + t for t in range(n)},
        compiler_params=pltpu.CompilerParams(has_side_effects=_DATAFLOW),
    )(*_in_hbm(flat))
    res, k = [], 2 * G
    for l in range(G):
        res.append((outs[2 * l], outs[2 * l + 1], list(outs[k:k + sizes[l]])))
        k += sizes[l]
    return res, outs[-1]


def _gather_wait(name, send, recv, bufs, after):
    T = len(bufs)

    def body(*refs):
        ins = refs[:T]
        send_ref, recv_ref = refs[T], refs[T + 1]
        x, y, c = _place()
        chips = _other_chips(x, y)
        me = 2 * x + y
        for t in range(T):
            for j in range(3):
                cp = _chip_copy(ins[t].at[me], ins[t].at[2 * chips[j][0] + chips[j][1]], send_ref.at[3 * t + j],
                                recv_ref.at[3 * t + j], chips[j], c)
                cp.wait_send()
                cp.wait_recv()

    return pl.pallas_call(
        body, name=name,
        out_shape=[pltpu.HBM(b.shape, b.dtype) for b in bufs],
        in_specs=[_HBM] * T + [_SEM, _SEM, _ANY],
        out_specs=[_HBM] * T,
        input_output_aliases={t: t for t in range(T)},
        compiler_params=pltpu.CompilerParams(has_side_effects=_DATAFLOW),
    )(*bufs, send, recv, after)


def _scatter_start(name, sums):
    T = len(sums)
    lands = [lax.empty((3,) + s.shape[1:], s.dtype) for s in sums]

    def body(*refs):
        srcs, lnds = refs[:T], refs[T:2 * T]
        send, recv = refs[2 * T], refs[2 * T + 1]
        token = refs[-1]
        x, y, c = _place()
        chips = _other_chips(x, y)
        for t in range(T):
            for j in range(3):
                _chip_copy(srcs[t].at[2 * chips[j][0] + chips[j][1]], lnds[t].at[j], send.at[3 * t + j],
                           recv.at[3 * t + j], chips[j], c).start()
        token[...] = jnp.zeros_like(token)

    both = list(sums) + lands
    outs = pl.pallas_call(
        body, name=name,
        out_shape=(pltpu.SemaphoreType.DMA((3 * T,)), pltpu.SemaphoreType.DMA((3 * T,)),
                   *[pltpu.HBM(b.shape, b.dtype) for b in both], jax.ShapeDtypeStruct((8, LANES), F32)),
        in_specs=[_HBM] * (2 * T),
        out_specs=(_SEM, _SEM, *[_HBM] * (2 * T), pl.BlockSpec(memory_space=pltpu.VMEM)),
        input_output_aliases={t: 2 + t for t in range(2 * T)},
        compiler_params=pltpu.CompilerParams(has_side_effects=_DATAFLOW),
    )(*_in_hbm(both))
    return outs[0], outs[1], list(outs[2:2 + T]), list(outs[2 + T:2 + 2 * T]), outs[-1]


def _scatter_wait(name, send, recv, sums, lands, after):
    T = len(sums)

    def body(*refs):
        srcs, lnds = refs[:T], refs[T:2 * T]
        send_ref, recv_ref = refs[2 * T], refs[2 * T + 1]
        x, y, c = _place()
        chips = _other_chips(x, y)
        for t in range(T):
            for j in range(3):
                cp = _chip_copy(srcs[t].at[2 * chips[j][0] + chips[j][1]], lnds[t].at[j], send_ref.at[3 * t + j],
                                recv_ref.at[3 * t + j], chips[j], c)
                cp.wait_send()
                cp.wait_recv()

    both = list(sums) + list(lands)
    outs = pl.pallas_call(
        body, name=name,
        out_shape=[pltpu.HBM(b.shape, b.dtype) for b in both],
        in_specs=[_HBM] * (2 * T) + [_SEM, _SEM, _ANY],
        out_specs=[_HBM] * (2 * T),
        input_output_aliases={t: t for t in range(2 * T)},
        compiler_params=pltpu.CompilerParams(has_side_effects=_DATAFLOW),
    )(*both, send, recv, after)
    return list(outs[:T]), list(outs[T:])


def _sum_parts(name, sums, land, chip_arr, c_arr):
    _, R2, C = sums.shape
    tr = _tile(R2, (256, 128, 64, 32, 16))

    def body(chip_ref, c_ref, s_ref, l_ref, o_ref):
        acc = s_ref[...].astype(F32)
        for j in range(3):
            acc = acc + l_ref[j].astype(F32)
        o_ref[...] = acc

    return pl.pallas_call(
        body, name=name,
        grid_spec=pltpu.PrefetchScalarGridSpec(
            num_scalar_prefetch=2, grid=(R2 // tr,),
            in_specs=[pl.BlockSpec((None, tr, C), lambda i, chip, c: (chip[0], i, 0)),
                      pl.BlockSpec((3, tr, C), lambda i, chip, c: (0, i, 0))],
            out_specs=pl.BlockSpec((None, tr, C), lambda i, chip, c: (c[0], i, 0))),
        out_shape=jax.ShapeDtypeStruct((2, R2, C), F32),
        compiler_params=_cparams(("parallel",)),
    )(chip_arr, c_arr, sums, land)


def _join_halves(name, bufs):
    T = len(bufs)

    def body(*refs):
        outs = refs[T:2 * T]
        send, recv = refs[2 * T:]
        x, y, c = _place()

        def copy(t, half):
            return pltpu.make_async_remote_copy(
                src_ref=outs[t].at[c], dst_ref=outs[t].at[half], send_sem=send.at[t], recv_sem=recv.at[t],
                device_id=(x, y, 1 - c), device_id_type=MESH)

        for t in range(T):
            copy(t, c).start()
        for t in range(T):
            copy(t, 1 - c).wait_recv()
        for t in range(T):
            copy(t, c).wait_send()

    return pl.pallas_call(
        body, name=name,
        in_specs=[_ANY] * T, out_specs=[_ANY] * T,
        out_shape=[jax.ShapeDtypeStruct(b.shape, b.dtype) for b in bufs],
        input_output_aliases={t: t for t in range(T)},
        scratch_shapes=[pltpu.SemaphoreType.DMA((T,)), pltpu.SemaphoreType.DMA((T,))],
        compiler_params=pltpu.CompilerParams(has_side_effects=True),
    )(*bufs)


def _rs_begin(tag, grads, c_arr):
    got = _swap_halves("rs_swap_" + tag, grads)
    sums = [_add_half("rs_add_%s_%d" % (tag, t), g, o, c_arr) for t, (g, o) in enumerate(zip(grads, got))]
    return _scatter_start("rs_scatter_start_" + tag, sums)


def _rs_finish(tag, state, shapes, chip_arr, c_arr, after=None):
    send, recv, sums, lands, token = state
    sums, lands = _scatter_wait("rs_scatter_wait_" + tag, send, recv, sums, lands, token if after is None else after)
    parts = [_sum_parts("rs_sum_%s_%d" % (tag, t), s, l, chip_arr, c_arr) for t, (s, l) in enumerate(zip(sums, lands))]
    joined = _join_halves("rs_join_" + tag, parts)
    return [j.reshape(sh[1], sh[2]) for j, sh in zip(joined, shapes)]


def _all_reduce_small(v):
    R = v.shape[0]

    def body(v_ref, o_ref, buf, send, recv):
        x, y, c = _place()
        me = 4 * x + 2 * y + c
        buf[me] = v_ref[...]

        def copy(k, slot):
            to = (x ^ ((k >> 2) & 1), y ^ ((k >> 1) & 1), c ^ (k & 1))
            return pltpu.make_async_remote_copy(
                src_ref=v_ref, dst_ref=buf.at[slot], send_sem=send.at[k - 1], recv_sem=recv.at[k - 1],
                device_id=to, device_id_type=MESH)

        for k in range(1, 8):
            copy(k, me).start()
        for k in range(1, 8):
            copy(k, me ^ k).wait_recv()
        for k in range(1, 8):
            copy(k, me).wait_send()
        acc = buf[0]
        for s in range(1, 8):
            acc = acc + buf[s]
        o_ref[...] = acc

    vm = pl.BlockSpec(memory_space=pltpu.VMEM)
    return pl.pallas_call(
        body, name="all_reduce_small",
        in_specs=[vm], out_specs=vm,
        out_shape=jax.ShapeDtypeStruct((R, LANES), F32),
        scratch_shapes=[pltpu.VMEM((8, R, LANES), F32), pltpu.SemaphoreType.DMA((7,)), pltpu.SemaphoreType.DMA((7,))],
        compiler_params=pltpu.CompilerParams(has_side_effects=True, vmem_limit_bytes=VMEM_LIMIT),
    )(v)


def _local_step(x, tgt, small, depth, nb, weight, convw, pscale, on_grads):
    S, D = x.shape
    n_q = D // HEAD_DIM
    n_kv = n_q // GQA_GROUP
    Fh = convw[0].shape[2]

    def dup(gain):
        return jnp.tile(gain, 2).reshape(1, LANES)

    biasm = _bias_expand(small["rel_bias"])
    saved = []
    for i in range(depth):
        j = i // 2
        w = {}
        st = {"x0": x, "w": w}
        gm = small["norm_mix"][i].reshape(1, D)
        if i % 2 == 0:
            h = _rms_fwd("rms_mix_fwd", x, gm)
            w["wqkv"] = weight(i, "wqkv", h)
            qkv = _mm_colblk("qkv_fwd", h, w["wqkv"], F32)
            qh, kcat, vcat = _qk_norm_fwd(qkv, dup(small["attn_q_gain"][j]), dup(small["attn_k_gain"][j]), n_q, n_kv)
            o = _attn_fwd(qh, kcat, vcat, biasm, small["attn_sinks"][j])
            w["wo"] = weight(i, "wo", o)
            x = _mm_rowblk_res("wo_fwd", o, w["wo"], x)
            st.update(h=h, qkv=qkv, qh=qh, kcat=kcat, vcat=vcat, o=o)
        else:
            d = _pool_fwd(x, gm)
            w["wp"] = weight(i, "wp", d)
            ypre, x = _pool_mm_fwd(d, w["wp"], x, pscale[j])
            st.update(d=d, ypre=ypre)
        st["x1"] = x
        h2 = _rms_fwd("rms_ffn_fwd", x, small["norm_ffn"][i].reshape(1, D))
        w["wup"] = weight(i, "wup", h2)
        cb = small["ffn_conv_b"][i].reshape(2, 1, Fh)
        u, uc, a = _up_act_fwd(h2, w["wup"], convw[i], cb)
        w["wdn"] = weight(i, "wdn", a)
        x = _mm_rowblk_res("down_fwd", a, w["wdn"], x)
        st.update(h2=h2, u=u, uc=uc, a=a)
        saved.append(st)

    dx, dxb, loss_part = _loss_head(x, tgt)

    big = [dict() for _ in range(depth)]
    sg = {k: [None] * depth for k in ("norm_mix", "norm_ffn", "conv_w", "conv_b")}
    sg.update({k: [None] * ((depth + 1) // 2) for k in ("q_gain", "k_gain", "sinks")})
    sg["pool_scale"] = [None] * (depth // 2)
    dbias_tot = None
    before = None
    for i in reversed(range(depth)):
        j = i // 2
        st = saved[i]
        w = st["w"]
        duc = _down_act_bwd(dxb, w["wdn"], st["uc"], after=before)
        wdn_grad = _mm_tn_rowblk("down_bwd_w", st["a"], dxb, nb)
        du, dh2, dcwb = _up_bwd_in(duc, st["u"], convw[i], w["wup"])
        dcwb = jnp.transpose(dcwb, (1, 0, 2)).reshape(8, 2 * Fh)
        sg["conv_w"][i] = dcwb[0:3]
        sg["conv_b"][i] = dcwb[3]
        wup_grad = _mm_tn_colblk("up_bwd_w", st["h2"], du, nb, split2=True)
        before = on_grads(i, "ffn", dict(wdn=wdn_grad, wup=wup_grad), dx)
        dx, dxb, dg = _rms_bwd("rms_ffn_bwd", st["x1"], small["norm_ffn"][i].reshape(1, D), dh2, dx, after=before)
        sg["norm_ffn"][i] = dg.reshape(D)
        gm = small["norm_mix"][i].reshape(1, D)
        if i % 2 == 0:
            do = _mm_nt_rowblk("wo_bwd_in", dxb, w["wo"], BF16)
            big[i]["wo"] = _mm_tn_rowblk("wo_bwd_w", st["o"], dxb, nb)
            dq, dkc, dkp, dvc, dvp, dbias, dsink = _attn_bwd(
                st["qh"], st["kcat"], st["vcat"], biasm, small["attn_sinks"][j], do)
            dbias_tot = dbias if dbias_tot is None else dbias_tot + dbias
            sg["sinks"][j] = dsink[:, :, 0].reshape(n_q)
            dqkv, dqg, dkg = _qk_norm_bwd(st["qkv"], dup(small["attn_q_gain"][j]), dup(small["attn_k_gain"][j]),
                                          dq, dkc, dkp, dvc, dvp, n_q, n_kv)
            sg["q_gain"][j] = dqg[0, :HEAD_DIM]
            sg["k_gain"][j] = dkg[0, :HEAD_DIM]
            big[i]["wqkv"] = _mm_tn_colblk("qkv_bwd_w", st["h"], dqkv, nb)
            dh = _mm_nt_colblk("qkv_bwd_in", dqkv, w["wqkv"])
            dx, dxb, dg = _rms_bwd("rms_mix_bwd", st["x0"], gm, dh, dx)
        else:
            dyp, dsc = _pool_bwd_pre(dx, st["ypre"], pscale[j])
            sg["pool_scale"][j] = dsc.reshape(D)
            big[i]["wp"] = _pool_mm_bwd_w(st["d"], dyp)
            dd = _pool_mm_bwd_in(dyp, w["wp"])
            dx, dxb, dg = _pool_bwd_post(dd, st["x0"], gm, dx)
        sg["norm_mix"][i] = dg.reshape(D)
        before = on_grads(i, "mix", big[i], dx)
    sg["rel_bias"] = _bias_reduce(dbias_tot)[:, 0, :N_BUCKETS]
    return loss_part, dx, sg


_SMALL_ORDER = ("norm_mix", "norm_ffn", "rel_bias", "q_gain", "k_gain", "sinks", "conv_b", "conv_w", "pool_scale")


def kernel(x, norm_mix, norm_ffn, rel_bias, attn_w_qkv, attn_q_gain, attn_k_gain, attn_sinks, attn_w_o, pool_w, pool_scale, ffn_w_up, ffn_conv_w, ffn_conv_b, ffn_w_down, loss_target, m_norm_mix, m_norm_ffn, m_rel_bias, m_attn_w_qkv, m_attn_q_gain, m_attn_k_gain, m_attn_sinks, m_attn_w_o, m_pool_w, m_pool_scale, m_ffn_w_up, m_ffn_conv_w, m_ffn_conv_b, m_ffn_w_down, v_norm_mix, v_norm_ffn, v_rel_bias, v_attn_w_qkv, v_attn_q_gain, v_attn_k_gain, v_attn_sinks, v_attn_w_o, v_pool_w, v_pool_scale, v_ffn_w_up, v_ffn_conv_w, v_ffn_conv_b, v_ffn_w_down):
    _, S, D = x.shape
    depth = ffn_w_up.shape[0]
    n_attn, n_pool = attn_w_qkv.shape[0], pool_w.shape[0]
    nb = N_CHIPS
    nc_up = ffn_w_up.shape[2]
    Fh = nc_up * nb // 2
    cx, cy, cc = _place()
    chip = 2 * cx + cy
    c_arr = jnp.reshape(cc, (1,)).astype(jnp.int32)
    chip_arr = jnp.reshape(chip, (1,)).astype(jnp.int32)

    pool_w3 = pool_w.reshape(n_pool, -1, pool_w.shape[-1])
    order = [("convw", ffn_conv_w.reshape(1, -1, nc_up), 0, F32), ("pscale", pool_scale[None], 0, F32)]
    for i in range(depth):
        j = i // 2
        mixer = ([("wqkv", attn_w_qkv, j), ("wo", attn_w_o, j)] if i % 2 == 0 else [("wp", pool_w3, j)])
        order += [((i, k), w, l, BF16) for k, w, l in mixer + [("wup", ffn_w_up, i), ("wdn", ffn_w_down, i)]]
    slots = [_cast_slot("slot_%d" % n, w, l, chip_arr, dt) for n, (_, w, l, dt) in enumerate(order)]
    started, token = _gather_start([[b] for b in slots])
    index = {key: n for n, (key, _, _, _) in enumerate(order)}

    def gathered(key, after):
        n = index[key]
        send, recv, bufs = started[n]
        return _gather_wait("gather_wait_%d" % n, send, recv, bufs, after)[0]

    convw_g = gathered("convw", token).reshape(nb, depth, 3, nc_up)
    pscale_g = gathered("pscale", token)
    convw = [jnp.transpose(convw_g[:, i], (1, 0, 2)).reshape(3, 2, Fh).transpose(1, 0, 2) for i in range(depth)]
    pscale = [pscale_g[:, j].reshape(1, D) for j in range(n_pool)]
    small = dict(norm_mix=norm_mix, norm_ffn=norm_ffn, rel_bias=rel_bias, attn_q_gain=attn_q_gain,
                 attn_k_gain=attn_k_gain, attn_sinks=attn_sinks, ffn_conv_b=ffn_conv_b)

    G, gd = pool_w.shape[1], pool_w.shape[3]

    def weight(i, name, after):
        w = gathered((i, name), after)
        if name == "wp":
            w = jnp.transpose(w.reshape(nb, G, gd // nb, gd), (1, 0, 2, 3)).reshape(G, gd, gd)
        return w

    red = {k: [None] * (n_attn if k in ("wqkv", "wo") else n_pool if k == "wp" else depth)
           for k in ("wqkv", "wo", "wp", "wup", "wdn")}
    pending = []

    def finish(after):
        i, tag, names, shapes, state = pending.pop()
        outs = _rs_finish("%d%s" % (i, tag), state, shapes, chip_arr, c_arr, after)
        for k, o in zip(names, outs):
            red[k][i // 2 if k in ("wqkv", "wo", "wp") else i] = o

    def on_grads(i, tag, grads, dx_i):
        if pending:
            finish(dx_i)
        names = sorted(grads)
        flat = []
        for k in names:
            g = grads[k]
            if k == "wp":
                g = jnp.transpose(g.reshape(G, nb, gd // nb, gd), (1, 0, 2, 3))
            flat.append(g.reshape(nb, -1, g.shape[-1]))
        state = _rs_begin("%d%s" % (i, tag), flat, c_arr)
        pending.append((i, tag, names, [f.shape for f in flat], state))
        return state[-1]

    loss_part, dx, sg = _local_step(x[0], loss_target[0], small, depth, nb, weight, convw, pscale, on_grads)
    finish(None)
    loss = lax.psum(jnp.sum(loss_part), ("x", "y", "c"))

    g_wqkv = jnp.stack(red["wqkv"])
    g_wo = jnp.stack(red["wo"])
    g_wp = jnp.stack(red["wp"]).reshape(pool_w.shape)
    g_wup = jnp.stack(red["wup"])
    g_wdn = jnp.stack(red["wdn"])

    parts = [jnp.stack(sg[k]) if isinstance(sg[k], list) else sg[k] for k in _SMALL_ORDER]
    sizes = [int(np.prod(p.shape)) for p in parts]
    total = sum(sizes)
    rows = -(-total // (8 * LANES)) * 8
    packed = jnp.concatenate([p.reshape(-1) for p in parts] + [jnp.zeros((rows * LANES - total,), F32)])
    summed = _all_reduce_small(packed.reshape(rows, LANES)).reshape(-1)
    sm, off = {}, 0
    for k, p, n in zip(_SMALL_ORDER, parts, sizes):
        sm[k] = summed[off:off + n].reshape(p.shape)
        off += n
    g_convw = lax.dynamic_slice_in_dim(sm["conv_w"], chip * nc_up, nc_up, axis=2)
    pc = pool_scale.shape[1]
    g_pscale = lax.dynamic_slice_in_dim(sm["pool_scale"], chip * pc, pc, axis=1)

    grads = [sm["norm_mix"], sm["norm_ffn"], sm["rel_bias"], g_wqkv, sm["q_gain"], sm["k_gain"], sm["sinks"], g_wo,
             g_wp, g_pscale, g_wup, g_convw, sm["conv_b"], g_wdn]
    ws = [norm_mix, norm_ffn, rel_bias, attn_w_qkv, attn_q_gain, attn_k_gain, attn_sinks, attn_w_o, pool_w, pool_scale,
          ffn_w_up, ffn_conv_w, ffn_conv_b, ffn_w_down]
    ms = [m_norm_mix, m_norm_ffn, m_rel_bias, m_attn_w_qkv, m_attn_q_gain, m_attn_k_gain, m_attn_sinks, m_attn_w_o,
          m_pool_w, m_pool_scale, m_ffn_w_up, m_ffn_conv_w, m_ffn_conv_b, m_ffn_w_down]
    vs = [v_norm_mix, v_norm_ffn, v_rel_bias, v_attn_w_qkv, v_attn_q_gain, v_attn_k_gain, v_attn_sinks, v_attn_w_o,
          v_pool_w, v_pool_scale, v_ffn_w_up, v_ffn_conv_w, v_ffn_conv_b, v_ffn_w_down]
    deltas, new_m, new_v = [], [], []
    for idx, (w, g, m, v) in enumerate(zip(ws, grads, ms, vs)):
        d, nm, nv = _adamw("adamw_%d" % idx, w, g, m, v)
        deltas.append(d), new_m.append(nm), new_v.append(nv)
    return (loss, dx.reshape(1, S, D), *grads, *deltas, *new_m, *new_v)
```

```python
import functools

import numpy as np
import jax
import jax.numpy as jnp
from jax import lax
from jax.experimental import pallas as pl
from jax.experimental.pallas import tpu as pltpu

F32 = jnp.float32
BF16 = jnp.bfloat16
MESH = pl.DeviceIdType.MESH
_ANY = pl.BlockSpec(memory_space=pl.ANY)
_HBM = pl.BlockSpec(memory_space=pltpu.HBM)
_SEM = pl.BlockSpec(memory_space=pltpu.SEMAPHORE)
_DATAFLOW = pltpu.SideEffectType.DATAFLOW_SIDE_EFFECTING

N_CHIPS = 4
HEAD_DIM = 64
GQA_GROUP = 8
WINDOW = 128
N_BUCKETS = 32
MAX_DISTANCE = 128
POOL_WINDOWS = (2, 4, 8, 16)
POOL_HALO = 16
ATTN_BWD_KV_PER_STEP = 4
EPS = 1e-6
NEG_INF = -1e30
LANES = 128
VMEM_LIMIT = 56 * 1024 * 1024

ADAM_LR = 0.001
ADAM_B1 = 0.9
ADAM_B2 = 0.999
ADAM_EPS = 1e-08
ADAM_WD = 0.01
ADAM_STEP = 10


def _tile(n, prefs):
    for p in prefs:
        if p <= n and n % p == 0:
            return p
    return n


def _cparams(sem):
    return pltpu.CompilerParams(dimension_semantics=sem, vmem_limit_bytes=VMEM_LIMIT)


_DN = {
    "nn": (((1,), (0,)), ((), ())),
    "nt": (((1,), (1,)), ((), ())),
    "tn": (((0,), (0,)), ((), ())),
}


def _mm(name, kind, a, b, grid, a_spec, b_spec, outs, acc_shape, extras=(), epilogue=None, after=None):
    nk = grid[2]
    n_ex, n_out = len(extras), len(outs)
    order = [] if after is None else [after]

    def body(a_ref, b_ref, *rest):
        ex_refs = rest[:n_ex]
        rest = rest[n_ex + len(order):]
        out_refs = rest[:n_out]
        acc_ref = rest[n_out] if nk > 1 else None
        part = lax.dot_general(a_ref[...], b_ref[...], _DN[kind], preferred_element_type=F32)

        def finish(val):
            vals = epilogue(val, *[r[...] for r in ex_refs]) if epilogue else (val,)
            for r, v in zip(out_refs, vals):
                r[...] = v.astype(r.dtype)

        if nk == 1:
            finish(part)
        else:
            k = pl.program_id(2)

            @pl.when(k == 0)
            def _():
                acc_ref[...] = part

            @pl.when(k > 0)
            def _():
                acc_ref[...] += part

            @pl.when(k == nk - 1)
            def _():
                finish(acc_ref[...])

    res = pl.pallas_call(
        body,
        name=name,
        grid=grid,
        in_specs=[a_spec, b_spec] + [s for _, s in extras] + [_ANY] * len(order),
        out_specs=[s for _, _, s in outs],
        out_shape=[jax.ShapeDtypeStruct(sh, dt) for sh, dt, _ in outs],
        scratch_shapes=[pltpu.VMEM(acc_shape, F32)] if nk > 1 else [],
        compiler_params=_cparams(("parallel", "parallel", "arbitrary")),
    )(a, b, *[e for e, _ in extras], *order)
    return res if n_out > 1 else res[0]


def _mm_colblk(name, a, wg, out_dtype):
    S, K = a.shape
    nb, _, nc = wg.shape
    tm = _tile(S, (1024, 512))
    tn = _tile(nc, (1408, 1024, 640, 512, 256, 128))
    npb = nc // tn
    return _mm(
        name, "nn", a, wg, (nb * npb, S // tm, 1),
        pl.BlockSpec((tm, K), lambda p, q, k: (q, 0)),
        pl.BlockSpec((None, K, tn), lambda p, q, k: (p // npb, 0, p % npb)),
        [((S, nb * nc), out_dtype, pl.BlockSpec((tm, tn), lambda p, q, k: (q, p)))], (tm, tn))


def _mm_rowblk_res(name, a, wg, res):
    S = a.shape[0]
    nb, kc, N = wg.shape
    tm = _tile(S, (512,))
    tn = _tile(N, tuple(t for t in (2048, 1024, 512, 256) if nb * kc * t * 2 <= (12 << 20)))

    def body(a_ref, b_ref, r_ref, o_ref):
        acc = r_ref[...]
        for r in range(nb):
            acc = acc + jnp.dot(a_ref[:, r * kc:(r + 1) * kc], b_ref[r], preferred_element_type=F32)
        o_ref[...] = acc

    o_spec = pl.BlockSpec((tm, tn), lambda i, j: (i, j))
    return pl.pallas_call(
        body, name=name, grid=(S // tm, N // tn),
        in_specs=[pl.BlockSpec((tm, nb * kc), lambda i, j: (i, 0)),
                  pl.BlockSpec((nb, kc, tn), lambda i, j: (0, 0, j)), o_spec],
        out_specs=o_spec,
        out_shape=jax.ShapeDtypeStruct((S, N), F32),
        compiler_params=_cparams(("parallel", "parallel")),
    )(a, wg, res)


def _mm_nt_rowblk(name, g, wg, out_dtype, after=None):
    S, N = g.shape
    nb, kc, _ = wg.shape
    tm = _tile(S, (1024, 512))
    tn = _tile(kc, (1408, 512, 256, 128))
    kpb = kc // tn
    return _mm(
        name, "nt", g, wg, (nb * kpb, S // tm, 1),
        pl.BlockSpec((tm, N), lambda p, q, k: (q, 0)),
        pl.BlockSpec((None, tn, N), lambda p, q, k: (p // kpb, p % kpb, 0)),
        [((S, nb * kc), out_dtype, pl.BlockSpec((tm, tn), lambda p, q, k: (q, p)))], (tm, tn), after=after)


def _mm_nt_colblk(name, g, wg):
    S = g.shape[0]
    nb, K, nc = wg.shape
    tm = _tile(S, (512,))

    def body(g_ref, b_ref, o_ref):
        acc = lax.dot_general(g_ref[:, 0:nc], b_ref[0], _DN["nt"], preferred_element_type=F32)
        for r in range(1, nb):
            acc = acc + lax.dot_general(g_ref[:, r * nc:(r + 1) * nc], b_ref[r], _DN["nt"],
                                        preferred_element_type=F32)
        o_ref[...] = acc

    return pl.pallas_call(
        body, name=name, grid=(S // tm,),
        in_specs=[pl.BlockSpec((tm, nb * nc), lambda i: (i, 0)), pl.BlockSpec((nb, K, nc), lambda i: (0, 0, 0))],
        out_specs=pl.BlockSpec((tm, K), lambda i: (i, 0)),
        out_shape=jax.ShapeDtypeStruct((S, K), F32),
        compiler_params=_cparams(("parallel",)),
    )(g, wg)


def _mm_tn_colblk(name, a, g, nb, split2=False):
    S, K = a.shape
    ntot = g.shape[-1] * (2 if split2 else 1)
    nc = ntot // nb
    ti = _tile(K, (1024,))
    tn = _tile(nc, (1408, 640, 512, 256, 128))
    ts = _tile(S, (2048, 1024, 512, 256))
    npb = nc // tn
    half = nb * npb // 2
    if split2:
        b_spec = pl.BlockSpec((None, ts, tn), lambda p, q, k: (q // half, k, q % half))
    else:
        b_spec = pl.BlockSpec((ts, tn), lambda p, q, k: (k, q))
    return _mm(
        name, "tn", a, g, (K // ti, nb * npb, S // ts),
        pl.BlockSpec((ts, ti), lambda p, q, k: (k, p)),
        b_spec,
        [((nb, K, nc), BF16, pl.BlockSpec((None, ti, tn), lambda p, q, k: (q // npb, p, q % npb)))], (ti, tn))


def _mm_tn_rowblk(name, a, g, nb):
    S, ktot = a.shape
    N = g.shape[1]
    kc = ktot // nb
    ti = _tile(kc, (1408, 512, 256, 128))
    tj = _tile(N, (1024,))
    ts = _tile(S, (2048, 1024, 512, 256))
    ipb = kc // ti
    return _mm(
        name, "tn", a, g, (nb * ipb, N // tj, S // ts),
        pl.BlockSpec((ts, ti), lambda p, q, k: (k, p)),
        pl.BlockSpec((ts, tj), lambda p, q, k: (k, q)),
        [((nb, kc, N), BF16, pl.BlockSpec((None, ti, tj), lambda p, q, k: (p // ipb, p % ipb, q)))], (ti, tj))


def _rms_fwd(name, x, gain):
    S, D = x.shape
    tm = _tile(S, (512,))

    def body(x_ref, g_ref, o_ref):
        xv = x_ref[...]
        r = lax.rsqrt(jnp.mean(xv * xv, axis=-1, keepdims=True) + EPS)
        o_ref[...] = (xv * r * g_ref[...]).astype(o_ref.dtype)

    return pl.pallas_call(
        body, name=name, grid=(S // tm,),
        in_specs=[pl.BlockSpec((tm, D), lambda i: (i, 0)), pl.BlockSpec((1, D), lambda i: (0, 0))],
        out_specs=pl.BlockSpec((tm, D), lambda i: (i, 0)),
        out_shape=jax.ShapeDtypeStruct((S, D), BF16),
        compiler_params=_cparams(("parallel",)),
    )(x, gain)


def _rms_bwd_math(xv, gain, dh):
    r = lax.rsqrt(jnp.mean(xv * xv, axis=-1, keepdims=True) + EPS)
    xhat = xv * r
    dxhat = dh * gain
    c = jnp.mean(dxhat * xhat, axis=-1, keepdims=True)
    return r * (dxhat - xhat * c), dh * xhat


def _rows_to_8(v):
    tm, C = v.shape
    return jnp.sum(v.reshape(tm // 8, 8, C), axis=0)


def _rms_bwd(name, x, gain, dh, dres, after=None):
    S, D = x.shape
    tm = _tile(S, (256,))
    n = S // tm
    order = [] if after is None else [after]

    def body(x_ref, g_ref, dh_ref, dr_ref, *rest):
        dx_ref, dxb_ref, dg_ref, acc_ref = rest[len(order):]
        i = pl.program_id(0)
        dxn, dgr = _rms_bwd_math(x_ref[...], g_ref[...], dh_ref[...])
        dx = dr_ref[...] + dxn
        dx_ref[...] = dx
        dxb_ref[...] = dx.astype(BF16)

        @pl.when(i == 0)
        def _():
            acc_ref[...] = jnp.zeros_like(acc_ref)

        acc_ref[...] += _rows_to_8(dgr)

        @pl.when(i == n - 1)
        def _():
            dg_ref[...] = jnp.sum(acc_ref[...], axis=0, keepdims=True)

    row = pl.BlockSpec((tm, D), lambda i: (i, 0))
    vec = pl.BlockSpec((1, D), lambda i: (0, 0))
    return pl.pallas_call(
        body, name=name, grid=(n,),
        in_specs=[row, vec, row, row] + [_ANY] * len(order),
        out_specs=[row, row, vec],
        out_shape=[jax.ShapeDtypeStruct((S, D), F32), jax.ShapeDtypeStruct((S, D), BF16),
                   jax.ShapeDtypeStruct((1, D), F32)],
        scratch_shapes=[pltpu.VMEM((8, D), F32)],
        compiler_params=_cparams(("arbitrary",)),
    )(x, gain, dh, dres, *order)


def _loss_head(y, tgt):
    S, D = y.shape
    tm = _tile(S, (256,))
    n = S // tm

    def body(y_ref, t_ref, dy_ref, dyb_ref, l_ref):
        i = pl.program_id(0)
        e = y_ref[...] - t_ref[...]
        dy = e * (1.0 / D)
        dy_ref[...] = dy
        dyb_ref[...] = dy.astype(BF16)

        @pl.when(i == 0)
        def _():
            l_ref[...] = jnp.zeros_like(l_ref)

        sq = _rows_to_8(e * e)
        part = sq[:, 0:LANES]
        for t in range(1, D // LANES):
            part = part + sq[:, t * LANES:(t + 1) * LANES]
        l_ref[...] += part * (0.5 / D)

    row = pl.BlockSpec((tm, D), lambda i: (i, 0))
    return pl.pallas_call(
        body, name="loss_head", grid=(n,),
        in_specs=[row, row],
        out_specs=[row, row, pl.BlockSpec((8, LANES), lambda i: (0, 0))],
        out_shape=[jax.ShapeDtypeStruct((S, D), F32), jax.ShapeDtypeStruct((S, D), BF16),
                   jax.ShapeDtypeStruct((8, LANES), F32)],
        compiler_params=_cparams(("arbitrary",)),
    )(y, tgt)


def _up_act_fwd(h2, wg, cw, cb):
    S, K = h2.shape
    nb, _, nc = wg.shape
    Fh = nb * nc // 2
    tm = _tile(S, (2048, 1024, 512))
    tn = _tile(nc, (256, 128))
    sub = _tile(tm, (512,))
    ch = _tile(sub, (32,))
    npb = nc // tn
    hb = nb // 2
    ncol = Fh // tn

    def body(a_ref, bg_ref, bv_ref, cw_ref, cb_ref, u_ref, uc_ref, o_ref, buf_ref, halo_ref):
        j = pl.program_id(1)

        @pl.when(pl.program_id(0) == 0)
        def _():
            halo_ref[j] = jnp.zeros((2, 8, tn), F32)

        buf_ref[:, 0:8, :] = halo_ref[j]
        cws, cbs = [cw_ref[0], cw_ref[1]], [cb_ref[0], cb_ref[1]]
        for r in range(tm // sub):
            for s, b_ref in enumerate((bg_ref, bv_ref)):
                buf_ref[s, 8 + r * sub:8 + (r + 1) * sub, :] = jnp.dot(
                    a_ref[r * sub:(r + 1) * sub, :], b_ref[...], preferred_element_type=F32)
            for c in range(sub // ch):
                base = r * sub + c * ch
                ucs = []
                for s in range(2):
                    ext = buf_ref[s, base:base + ch + 8, :]
                    us = ext[8:]
                    uc = (cws[s][0:1] * pltpu.roll(ext, 2, 0)[8:] + cws[s][1:2] * pltpu.roll(ext, 1, 0)[8:]
                          + cws[s][2:3] * us + cbs[s])
                    u_ref[s, base:base + ch, :] = us.astype(BF16)
                    uc_ref[s, base:base + ch, :] = uc.astype(BF16)
                    ucs.append(uc)
                gate, val = ucs
                o_ref[base:base + ch, :] = (gate * (1.0 / (1.0 + jnp.exp(-gate))) * val).astype(BF16)
        halo_ref[j] = buf_ref[:, tm:tm + 8, :]

    both = pl.BlockSpec((2, tm, tn), lambda i, j: (0, i, j))
    return pl.pallas_call(
        body, name="up_act_fwd", grid=(S // tm, ncol),
        in_specs=[pl.BlockSpec((tm, K), lambda i, j: (i, 0)),
                  pl.BlockSpec((None, K, tn), lambda i, j: (j // npb, 0, j % npb)),
                  pl.BlockSpec((None, K, tn), lambda i, j: (j // npb + hb, 0, j % npb)),
                  pl.BlockSpec((2, 3, tn), lambda i, j: (0, 0, j)),
                  pl.BlockSpec((2, 1, tn), lambda i, j: (0, 0, j))],
        out_specs=[both, both, pl.BlockSpec((tm, tn), lambda i, j: (i, j))],
        out_shape=[jax.ShapeDtypeStruct((2, S, Fh), BF16), jax.ShapeDtypeStruct((2, S, Fh), BF16),
                   jax.ShapeDtypeStruct((S, Fh), BF16)],
        scratch_shapes=[pltpu.VMEM((2, tm + 8, tn), F32), pltpu.VMEM((ncol, 2, 8, tn), F32)],
        compiler_params=_cparams(("arbitrary", "arbitrary")),
    )(h2, wg, wg, cw, cb)


def _down_act_bwd(dxb, wg, uc, after=None):
    S, D = dxb.shape
    nb, kc, _ = wg.shape
    Fh = nb * kc
    tm = _tile(S, (1024, 512))
    sub = _tile(tm, (256,))
    ch = _tile(sub, (32,))
    t128 = kc // LANES
    tn = 2 * LANES
    order = [] if after is None else [after]

    def body(dx_ref, b0_ref, b1_ref, uc_ref, *rest):
        duc_ref, bcat_ref, da_ref = rest[len(order):]
        bcat_ref[0:LANES, :] = b0_ref[...]
        bcat_ref[LANES:tn, :] = b1_ref[...]
        for r in range(tm // sub):
            rows = slice(r * sub, (r + 1) * sub)
            da_ref[rows, :] = lax.dot_general(dx_ref[rows, :], bcat_ref[...], _DN["nt"], preferred_element_type=F32)
            for c in range(sub // ch):
                cs = slice(r * sub + c * ch, r * sub + (c + 1) * ch)
                da = da_ref[cs, :]
                gate, val = uc_ref[0, cs, :].astype(F32), uc_ref[1, cs, :].astype(F32)
                sig = 1.0 / (1.0 + jnp.exp(-gate))
                duc_ref[0, cs, :] = (da * val * (sig * (1.0 + gate * (1.0 - sig)))).astype(BF16)
                duc_ref[1, cs, :] = (da * (gate * sig)).astype(BF16)

    both = pl.BlockSpec((2, tm, tn), lambda i, j: (0, i, j))
    return pl.pallas_call(
        body, name="down_act_bwd", grid=(S // tm, Fh // tn),
        in_specs=[pl.BlockSpec((tm, D), lambda i, j: (i, 0)),
                  pl.BlockSpec((None, LANES, D), lambda i, j: ((2 * j) // t128, (2 * j) % t128, 0)),
                  pl.BlockSpec((None, LANES, D), lambda i, j: ((2 * j + 1) // t128, (2 * j + 1) % t128, 0)),
                  both] + [_ANY] * len(order),
        out_specs=both,
        out_shape=jax.ShapeDtypeStruct((2, S, Fh), BF16),
        scratch_shapes=[pltpu.VMEM((tn, D), BF16), pltpu.VMEM((tm, tn), F32)],
        compiler_params=_cparams(("parallel", "parallel")),
    )(dxb, wg, wg, uc, *order)


def _up_bwd_in(duc, u, cw, wg):
    _, S, Fh = duc.shape
    nb, K, nc = wg.shape
    tm = _tile(S, (1024, 512))
    tk = _tile(nc, (1408, 640, 512, 256, 128))
    sub = _tile(tm, (256,))
    ch = _tile(sub, (64,))
    hr = 16
    npb = nc // tk
    nk = nb * npb
    half = nk // 2
    n, nh = S // tm, tm // hr

    def body(d_ref, h_ref, u_ref, cw_ref, b_ref, du_ref, o_ref, cg_ref):
        i, k = pl.program_id(0), pl.program_id(1)
        keep = (i < n - 1).astype(F32)

        @pl.when(i == 0)
        def _():
            cg_ref[k] = jnp.zeros((8, tk), F32)

        @pl.when(k == 0)
        def _():
            o_ref[...] = jnp.zeros_like(o_ref)

        for r in range(tm // sub):
            for l in range(tk // LANES):
                ls = slice(l * LANES, (l + 1) * LANES)
                w = cw_ref[:, ls]
                sums = [jnp.zeros((8, LANES), F32) for _ in range(4)]
                for c in range(sub // ch):
                    base = r * sub + c * ch
                    if base + ch + hr <= tm:
                        ext = d_ref[base:base + ch + hr, ls].astype(F32)
                    else:
                        ext = jnp.concatenate([d_ref[base:base + ch, ls].astype(F32),
                                               h_ref[:, ls].astype(F32) * keep], axis=0)
                    d = ext[:ch]
                    d1 = pltpu.roll(ext, ch + hr - 1, 0)[:ch]
                    d2 = pltpu.roll(ext, ch + hr - 2, 0)[:ch]
                    du_ref[base:base + ch, ls] = (w[2:3] * d + w[1:2] * d1 + w[0:1] * d2).astype(BF16)
                    uv = u_ref[base:base + ch, ls].astype(F32)
                    for q, v in enumerate((d2 * uv, d1 * uv, d * uv, d)):
                        sums[q] = sums[q] + _rows_to_8(v)
                for q in range(4):
                    cg_ref[k, q:q + 1, ls] += jnp.sum(sums[q], axis=0, keepdims=True)
            rows = slice(r * sub, (r + 1) * sub)
            o_ref[rows, :] += lax.dot_general(du_ref[rows, :], b_ref[...], _DN["nt"], preferred_element_type=F32)

    blk = pl.BlockSpec((None, tm, tk), lambda i, k: (k // half, i, k % half))
    return pl.pallas_call(
        body, name="up_bwd_in", grid=(n, nk),
        in_specs=[blk,
                  pl.BlockSpec((None, hr, tk), lambda i, k: (k // half, jnp.minimum((i + 1) * nh, S // hr - 1), k % half)),
                  blk,
                  pl.BlockSpec((None, 3, tk), lambda i, k: (k // half, 0, k % half)),
                  pl.BlockSpec((None, K, tk), lambda i, k: (k // npb, 0, k % npb))],
        out_specs=[blk, pl.BlockSpec((tm, K), lambda i, k: (i, 0)), pl.BlockSpec((nk, 8, tk), lambda i, k: (0, 0, 0))],
        out_shape=[jax.ShapeDtypeStruct((2, S, Fh), BF16), jax.ShapeDtypeStruct((S, K), F32),
                   jax.ShapeDtypeStruct((nk, 8, tk), F32)],
        compiler_params=_cparams(("arbitrary", "arbitrary")),
    )(duc, duc, u, cw, wg)


def _pool_counts(i, tm, w):
    t = i * tm + lax.broadcasted_iota(jnp.int32, (tm, 1), 0)
    return jnp.minimum(t + 1, w).astype(F32)


def _pool_fwd(x, gain):
    S, D = x.shape
    tm = _tile(S, (256,))
    gd = D // len(POOL_WINDOWS)
    nh = tm // POOL_HALO

    def body(x_ref, xh_ref, g_ref, d_ref):
        i = pl.program_id(0)

        def norm(v):
            return v * lax.rsqrt(jnp.mean(v * v, axis=-1, keepdims=True) + EPS) * g_ref[...]

        h = norm(x_ref[...])
        hh = norm(xh_ref[...]) * (i > 0).astype(F32)
        ext = jnp.concatenate([hh, h], axis=0)
        for gi, w in enumerate(POOL_WINDOWS):
            sl = slice(gi * gd, (gi + 1) * gd)
            win = ext[:, sl]
            k = 1
            while k < w:
                win = win + pltpu.roll(win, k, 0)
                k *= 2
            mean = win[POOL_HALO:] / _pool_counts(i, tm, w)
            d_ref[:, sl] = (mean - h[:, sl]).astype(BF16)

    return pl.pallas_call(
        body, name="pool_fwd", grid=(S // tm,),
        in_specs=[pl.BlockSpec((tm, D), lambda i: (i, 0)),
                  pl.BlockSpec((POOL_HALO, D), lambda i: (jnp.maximum(i * nh - 1, 0), 0)),
                  pl.BlockSpec((1, D), lambda i: (0, 0))],
        out_specs=pl.BlockSpec((tm, D), lambda i: (i, 0)),
        out_shape=jax.ShapeDtypeStruct((S, D), BF16),
        compiler_params=_cparams(("parallel",)),
    )(x, x, gain)


def _pool_mm_fwd(d, wp, x, scale):
    S, D = d.shape
    G, gd, _ = wp.shape
    tm = _tile(S, (1024, 512))
    o_spec = pl.BlockSpec((tm, gd), lambda p, q, k: (p, q))
    return _mm(
        "pool_mm_fwd", "nn", d, wp, (S // tm, G, 1),
        o_spec, pl.BlockSpec((None, gd, gd), lambda p, q, k: (q, 0, 0)),
        [((S, D), F32, o_spec), ((S, D), F32, o_spec)], (tm, gd),
        extras=[(x, o_spec), (scale, pl.BlockSpec((1, gd), lambda p, q, k: (0, q)))],
        epilogue=lambda acc, xv, sc: (acc, xv + acc * sc))


def _pool_bwd_pre(dx1, ypre, scale):
    S, D = dx1.shape
    tm = _tile(S, (256,))
    n = S // tm

    def body(dx_ref, y_ref, s_ref, dy_ref, ds_ref, acc_ref):
        i = pl.program_id(0)
        dx = dx_ref[...]
        dy_ref[...] = (dx * s_ref[...]).astype(BF16)

        @pl.when(i == 0)
        def _():
            acc_ref[...] = jnp.zeros_like(acc_ref)

        acc_ref[...] += _rows_to_8(dx * y_ref[...])

        @pl.when(i == n - 1)
        def _():
            ds_ref[...] = jnp.sum(acc_ref[...], axis=0, keepdims=True)

    row = pl.BlockSpec((tm, D), lambda i: (i, 0))
    vec = pl.BlockSpec((1, D), lambda i: (0, 0))
    return pl.pallas_call(
        body, name="pool_bwd_pre", grid=(n,),
        in_specs=[row, row, vec], out_specs=[row, vec],
        out_shape=[jax.ShapeDtypeStruct((S, D), BF16), jax.ShapeDtypeStruct((1, D), F32)],
        scratch_shapes=[pltpu.VMEM((8, D), F32)],
        compiler_params=_cparams(("arbitrary",)),
    )(dx1, ypre, scale)


def _pool_mm_bwd_in(dyp, wp):
    S, D = dyp.shape
    G, gd, _ = wp.shape
    tm = _tile(S, (1024, 512))
    spec = pl.BlockSpec((tm, gd), lambda p, q, k: (p, q))
    return _mm(
        "pool_mm_bwd_in", "nt", dyp, wp, (S // tm, G, 1),
        spec, pl.BlockSpec((None, gd, gd), lambda p, q, k: (q, 0, 0)),
        [((S, D), F32, spec)], (tm, gd))


def _pool_mm_bwd_w(d, dyp):
    S, D = d.shape
    G = len(POOL_WINDOWS)
    gd = D // G
    ts = _tile(S, (1024, 512, 256))
    spec = pl.BlockSpec((ts, gd), lambda p, q, k: (k, p))
    return _mm(
        "pool_mm_bwd_w", "tn", d, dyp, (G, 1, S // ts),
        spec, spec,
        [((G, gd, gd), BF16, pl.BlockSpec((None, gd, gd), lambda p, q, k: (p, 0, 0)))], (gd, gd))


def _pool_bwd_post(dd, x, gain, dres):
    S, D = x.shape
    tm = _tile(S, (256,))
    n = S // tm
    gd = D // len(POOL_WINDOWS)
    nh = tm // POOL_HALO

    def body(dd_ref, ddh_ref, x_ref, g_ref, dr_ref, dx_ref, dxb_ref, dg_ref, acc_ref):
        i = pl.program_id(0)
        dd = dd_ref[...]
        halo = ddh_ref[...] * (i < n - 1).astype(F32)
        parts = []
        for gi, w in enumerate(POOL_WINDOWS):
            sl = slice(gi * gd, (gi + 1) * gd)
            ext = jnp.concatenate([dd[:, sl] / _pool_counts(i, tm, w), halo[:, sl] * (1.0 / w)], axis=0)
            k = 1
            while k < w:
                ext = ext + pltpu.roll(ext, tm + POOL_HALO - k, 0)
                k *= 2
            parts.append(ext[:tm] - dd[:, sl])
        dh = jnp.concatenate(parts, axis=1)
        dxn, dgr = _rms_bwd_math(x_ref[...], g_ref[...], dh)
        dx = dr_ref[...] + dxn
        dx_ref[...] = dx
        dxb_ref[...] = dx.astype(BF16)

        @pl.when(i == 0)
        def _():
            acc_ref[...] = jnp.zeros_like(acc_ref)

        acc_ref[...] += _rows_to_8(dgr)

        @pl.when(i == n - 1)
        def _():
            dg_ref[...] = jnp.sum(acc_ref[...], axis=0, keepdims=True)

    row = pl.BlockSpec((tm, D), lambda i: (i, 0))
    vec = pl.BlockSpec((1, D), lambda i: (0, 0))
    nxt = pl.BlockSpec((POOL_HALO, D), lambda i: (jnp.minimum((i + 1) * nh, S // POOL_HALO - 1), 0))
    return pl.pallas_call(
        body, name="pool_bwd_post", grid=(n,),
        in_specs=[row, nxt, row, vec, row],
        out_specs=[row, row, vec],
        out_shape=[jax.ShapeDtypeStruct((S, D), F32), jax.ShapeDtypeStruct((S, D), BF16),
                   jax.ShapeDtypeStruct((1, D), F32)],
        scratch_shapes=[pltpu.VMEM((8, D), F32)],
        compiler_params=_cparams(("arbitrary",)),
    )(dd, dd, x, gain, dres)


def _band_tables():
    i = np.arange(WINDOW)[:, None]
    j = np.arange(2 * WINDOW)[None, :]
    n = np.maximum(WINDOW + i - j, 0)
    max_exact = N_BUCKETS // 2
    nf = np.maximum(n, 1).astype(np.float32)
    large = max_exact + (np.log(nf / max_exact) / np.log(MAX_DISTANCE / max_exact)
                         * (N_BUCKETS - max_exact)).astype(np.int32)
    large = np.minimum(large, N_BUCKETS - 1)
    buckets = np.where(n < max_exact, n, large).astype(np.int32)
    dist = WINDOW + i - j
    in_win = ((dist >= 0) & (dist < WINDOW)).astype(np.int32)
    return buckets, in_win


def _bias_expand(rel_bias):
    H = rel_bias.shape[0]
    buckets, in_win = _band_tables()

    def body(rb_ref, bk_ref, win_ref, o_ref):
        h = pl.program_id(0)
        bk = bk_ref[...]
        acc = jnp.zeros(bk.shape, F32)
        for b in range(N_BUCKETS):
            acc = jnp.where(bk == b, rb_ref[h, b], acc)
        o_ref[...] = jnp.where(win_ref[...] > 0, acc, NEG_INF)

    full = pl.BlockSpec((WINDOW, 2 * WINDOW), lambda h: (0, 0))
    return pl.pallas_call(
        body, name="bias_expand", grid=(H,),
        in_specs=[pl.BlockSpec(memory_space=pltpu.SMEM), full, full],
        out_specs=pl.BlockSpec((None, WINDOW, 2 * WINDOW), lambda h: (h, 0, 0)),
        out_shape=jax.ShapeDtypeStruct((H, WINDOW, 2 * WINDOW), F32),
        compiler_params=_cparams(("parallel",)),
    )(rel_bias, jnp.asarray(buckets), jnp.asarray(in_win))


def _bias_reduce(dbias):
    H = dbias.shape[0]
    buckets, in_win = _band_tables()

    def body(d_ref, bk_ref, win_ref, o_ref):
        bk = jnp.where(win_ref[...] > 0, bk_ref[...], -1)
        d = d_ref[...]
        lane = lax.broadcasted_iota(jnp.int32, (8, LANES), 1)
        out = jnp.zeros((8, LANES), F32)
        for b in range(N_BUCKETS):
            s = jnp.sum(jnp.where(bk == b, d, 0.0))
            out = jnp.where(lane == b, s, out)
        o_ref[...] = out

    full = pl.BlockSpec((WINDOW, 2 * WINDOW), lambda h: (0, 0))
    return pl.pallas_call(
        body, name="bias_reduce", grid=(H,),
        in_specs=[pl.BlockSpec((None, WINDOW, 2 * WINDOW), lambda h: (h, 0, 0)), full, full],
        out_specs=pl.BlockSpec((None, 8, LANES), lambda h: (h, 0, 0)),
        out_shape=jax.ShapeDtypeStruct((H, 8, LANES), F32),
        compiler_params=_cparams(("parallel",)),
    )(dbias, jnp.asarray(buckets), jnp.asarray(in_win))


def _half_rsqrt(t, lo):
    sq = t * t
    s_lo = jnp.sum(jnp.where(lo, sq, 0.0), axis=-1, keepdims=True)
    s_hi = jnp.sum(jnp.where(lo, 0.0, sq), axis=-1, keepdims=True)
    return jnp.where(lo, lax.rsqrt(s_lo * (1.0 / HEAD_DIM) + EPS), lax.rsqrt(s_hi * (1.0 / HEAD_DIM) + EPS))


def _qk_norm_fwd(qkv, qg, kg, n_q, n_kv):
    S, W = qkv.shape
    tm = _tile(S, (256,))
    qw = n_q * HEAD_DIM
    kw = n_kv * HEAD_DIM
    scale = HEAD_DIM ** -0.5

    def body(x_ref, qg_ref, kg_ref, q_ref, k_ref, v_ref):
        lo = lax.broadcasted_iota(jnp.int32, (tm, LANES), 1) < HEAD_DIM
        for t in range(qw // LANES):
            q = x_ref[:, t * LANES:(t + 1) * LANES]
            q_ref[:, t * LANES:(t + 1) * LANES] = (q * _half_rsqrt(q, lo) * qg_ref[...] * scale).astype(BF16)
        for p in range(kw // LANES):
            k = x_ref[:, qw + p * LANES:qw + (p + 1) * LANES]
            kn = k * _half_rsqrt(k, lo) * kg_ref[...]
            v = x_ref[:, qw + kw + p * LANES:qw + kw + (p + 1) * LANES]
            for src, dst in ((kn, k_ref), (v, v_ref)):
                rolled = pltpu.roll(src, HEAD_DIM, 1)
                dst[2 * p] = jnp.where(lo, src, rolled).astype(BF16)
                dst[2 * p + 1] = jnp.where(lo, rolled, src).astype(BF16)

    vec = pl.BlockSpec((1, LANES), lambda i: (0, 0))
    kv = pl.BlockSpec((n_kv, tm, LANES), lambda i: (0, i, 0))
    return pl.pallas_call(
        body, name="qk_norm_fwd", grid=(S // tm,),
        in_specs=[pl.BlockSpec((tm, W), lambda i: (i, 0)), vec, vec],
        out_specs=[pl.BlockSpec((tm, qw), lambda i: (i, 0)), kv, kv],
        out_shape=[jax.ShapeDtypeStruct((S, qw), BF16), jax.ShapeDtypeStruct((n_kv, S, LANES), BF16),
                   jax.ShapeDtypeStruct((n_kv, S, LANES), BF16)],
        compiler_params=_cparams(("parallel",)),
    )(qkv, qg, kg)


def _qk_norm_bwd(qkv, qg, kg, dq, dkc, dkp, dvc, dvp, n_q, n_kv):
    S, W = qkv.shape
    tm = _tile(S, (4 * WINDOW, WINDOW))
    n = S // tm
    per = tm // WINDOW
    qw = n_q * HEAD_DIM
    kw = n_kv * HEAD_DIM
    scale = HEAD_DIM ** -0.5

    def body(x_ref, qg_ref, kg_ref, dq_ref, dkc_ref, dkp_ref, dkn_ref, dvc_ref, dvp_ref, dvn_ref,
             dx_ref, dqg_ref, dkg_ref, accq_ref, acck_ref):
        i = pl.program_id(0)
        lo = lax.broadcasted_iota(jnp.int32, (tm, LANES), 1) < HEAD_DIM
        has_next = (i < n - 1).astype(F32)

        @pl.when(i == 0)
        def _():
            accq_ref[...] = jnp.zeros_like(accq_ref)
            acck_ref[...] = jnp.zeros_like(acck_ref)

        def norm_bwd(t, dy, gain):
            r = _half_rsqrt(t, lo)
            xhat = t * r
            dxhat = dy * gain
            prod = dxhat * xhat
            c_lo = jnp.sum(jnp.where(lo, prod, 0.0), axis=-1, keepdims=True) * (1.0 / HEAD_DIM)
            c_hi = jnp.sum(jnp.where(lo, 0.0, prod), axis=-1, keepdims=True) * (1.0 / HEAD_DIM)
            return r * (dxhat - xhat * jnp.where(lo, c_lo, c_hi)), _rows_to_8(dy * xhat)

        for t in range(qw // LANES):
            sl = slice(t * LANES, (t + 1) * LANES)
            dt, dg = norm_bwd(x_ref[:, sl], dq_ref[:, sl] * scale, qg_ref[...])
            dx_ref[:, sl] = dt.astype(BF16)
            accq_ref[...] += dg

        def pair(cur_ref, prev_ref, next_ref, p):
            folded = []
            for h in (2 * p, 2 * p + 1):
                later = next_ref[h] * has_next
                if per > 1:
                    later = jnp.concatenate([prev_ref[h, WINDOW:, :], later], axis=0)
                tot = cur_ref[h] + later
                folded.append(tot + pltpu.roll(tot, HEAD_DIM, 1))
            return jnp.where(lo, folded[0], folded[1])

        for p in range(kw // LANES):
            sl = slice(qw + p * LANES, qw + (p + 1) * LANES)
            dt, dg = norm_bwd(x_ref[:, sl], pair(dkc_ref, dkp_ref, dkn_ref, p), kg_ref[...])
            dx_ref[:, sl] = dt.astype(BF16)
            acck_ref[...] += dg
            sl = slice(qw + kw + p * LANES, qw + kw + (p + 1) * LANES)
            dx_ref[:, sl] = pair(dvc_ref, dvp_ref, dvn_ref, p).astype(BF16)

        @pl.when(i == n - 1)
        def _():
            for acc_ref, o_ref in ((accq_ref, dqg_ref), (acck_ref, dkg_ref)):
                s = jnp.sum(acc_ref[...], axis=0, keepdims=True)
                o_ref[...] = s + pltpu.roll(s, HEAD_DIM, 1)

    vec = pl.BlockSpec((1, LANES), lambda i: (0, 0))
    cur = pl.BlockSpec((n_kv, tm, LANES), lambda i: (0, i, 0))
    nxt = pl.BlockSpec((n_kv, WINDOW, LANES), lambda i: (0, jnp.minimum((i + 1) * per, S // WINDOW - 1), 0))
    return pl.pallas_call(
        body, name="qk_norm_bwd", grid=(n,),
        in_specs=[pl.BlockSpec((tm, W), lambda i: (i, 0)), vec, vec, pl.BlockSpec((tm, qw), lambda i: (i, 0)),
                  cur, cur, nxt, cur, cur, nxt],
        out_specs=[pl.BlockSpec((tm, W), lambda i: (i, 0)), vec, vec],
        out_shape=[jax.ShapeDtypeStruct((S, W), BF16), jax.ShapeDtypeStruct((1, LANES), F32),
                   jax.ShapeDtypeStruct((1, LANES), F32)],
        scratch_shapes=[pltpu.VMEM((8, LANES), F32), pltpu.VMEM((8, LANES), F32)],
        compiler_params=_cparams(("arbitrary",)),
    )(qkv, qg, kg, dq, dkc, dkp, dkp, dvc, dvp, dvp)


def _attn_specs(n_kv, per_step):
    kvs = per_step if n_kv % per_step == 0 else 1
    q = pl.BlockSpec((WINDOW, kvs * GQA_GROUP * HEAD_DIM), lambda kh, n: (n, kh))
    cur = pl.BlockSpec((kvs, WINDOW, LANES), lambda kh, n: (kh, n, 0))
    prev = pl.BlockSpec((kvs, WINDOW, LANES), lambda kh, n: (kh, jnp.maximum(n - 1, 0), 0))
    bias = pl.BlockSpec((kvs * GQA_GROUP, WINDOW, 2 * WINDOW), lambda kh, n: (kh, 0, 0))
    sink = pl.BlockSpec(memory_space=pltpu.SMEM)
    return kvs, q, cur, prev, bias, sink


def _stack_heads(x_ref, lo, h):
    parts = []
    for g in range(GQA_GROUP):
        c0 = (h * GQA_GROUP // 2 + g // 2) * LANES
        t = x_ref[:, c0:c0 + LANES]
        parts.append(jnp.where(lo if g % 2 == 0 else jnp.logical_not(lo), t, jnp.zeros_like(t)))
    return jnp.concatenate(parts, axis=0)


def _unstack_heads(v, lo):
    return [jnp.where(lo, v[(2 * p) * WINDOW:(2 * p + 1) * WINDOW], v[(2 * p + 1) * WINDOW:(2 * p + 2) * WINDOW])
            for p in range(GQA_GROUP // 2)]


def _attn_probs(qs, kk, bias, s_ref, head0, has_prev):
    rows = GQA_GROUP * WINDOW
    s = lax.dot_general(qs, kk, _DN["nt"], preferred_element_type=F32) + bias.reshape(rows, 2 * WINDOW)
    col = lax.broadcasted_iota(jnp.int32, (rows, 2 * WINDOW), 1)
    s = jnp.where(jnp.logical_or(has_prev, col >= WINDOW), s, NEG_INF)
    sink = jnp.concatenate([jnp.full((WINDOW, 1), s_ref[head0 + g], F32) for g in range(GQA_GROUP)], axis=0)
    m = jnp.maximum(jnp.max(s, axis=-1, keepdims=True), sink)
    p = jnp.exp(s - m)
    es = jnp.exp(sink - m)
    den = jnp.sum(p, axis=-1, keepdims=True) + es
    return p / den, es / den


def _attn_fwd(q, kcat, vcat, biasm, sinks):
    S, qw = q.shape
    n_kv = kcat.shape[0]
    kvs, qs, cur, prev, bias, sink = _attn_specs(n_kv, 1)
    pairs = GQA_GROUP // 2

    def body(q_ref, kc_ref, kp_ref, vc_ref, vp_ref, b_ref, s_ref, o_ref):
        kh, n = pl.program_id(0), pl.program_id(1)
        lo = lax.broadcasted_iota(jnp.int32, (WINDOW, LANES), 1) < HEAD_DIM
        for h in range(kvs):
            kk = jnp.concatenate([kp_ref[h], kc_ref[h]], axis=0)
            vv = jnp.concatenate([vp_ref[h], vc_ref[h]], axis=0)
            pn, _ = _attn_probs(_stack_heads(q_ref, lo, h), kk, b_ref[h * GQA_GROUP:(h + 1) * GQA_GROUP], s_ref,
                                (kh * kvs + h) * GQA_GROUP, n > 0)
            o = jnp.dot(pn.astype(BF16), vv, preferred_element_type=F32)
            for p, t in enumerate(_unstack_heads(o, lo)):
                o_ref[:, (h * pairs + p) * LANES:(h * pairs + p + 1) * LANES] = t.astype(BF16)

    return pl.pallas_call(
        body, name="attn_fwd", grid=(n_kv // kvs, S // WINDOW),
        in_specs=[qs, cur, prev, cur, prev, bias, sink],
        out_specs=qs,
        out_shape=jax.ShapeDtypeStruct((S, qw), BF16),
        compiler_params=_cparams(("parallel", "parallel")),
    )(q, kcat, kcat, vcat, vcat, biasm, sinks)


def _attn_bwd(q, kcat, vcat, biasm, sinks, do):
    S, qw = q.shape
    n_kv = kcat.shape[0]
    H = n_kv * GQA_GROUP
    kvs, qs, cur, prev, bias, sink = _attn_specs(n_kv, ATTN_BWD_KV_PER_STEP)
    pairs = GQA_GROUP // 2

    def body(q_ref, kc_ref, kp_ref, vc_ref, vp_ref, b_ref, s_ref, do_ref,
             dq_ref, dkc_ref, dkp_ref, dvc_ref, dvp_ref, db_ref, ds_ref):
        kh, n = pl.program_id(0), pl.program_id(1)
        lo = lax.broadcasted_iota(jnp.int32, (WINDOW, LANES), 1) < HEAD_DIM

        @pl.when(n == 0)
        def _():
            db_ref[...] = jnp.zeros_like(db_ref)
            ds_ref[...] = jnp.zeros_like(ds_ref)

        for h in range(kvs):
            gs = slice(h * GQA_GROUP, (h + 1) * GQA_GROUP)
            kk = jnp.concatenate([kp_ref[h], kc_ref[h]], axis=0)
            vv = jnp.concatenate([vp_ref[h], vc_ref[h]], axis=0)
            qs_ = _stack_heads(q_ref, lo, h)
            dos = _stack_heads(do_ref, lo, h)
            pn, ps = _attn_probs(qs_, kk, b_ref[gs], s_ref, (kh * kvs + h) * GQA_GROUP, n > 0)
            dp = lax.dot_general(dos, vv, _DN["nt"], preferred_element_type=F32)
            delta = jnp.sum(pn * dp, axis=-1, keepdims=True)
            ds = pn * (dp - delta)
            db_ref[gs] += ds.reshape(GQA_GROUP, WINDOW, 2 * WINDOW)
            ds_ref[h] += (jnp.zeros((GQA_GROUP, LANES), F32)
                          - jnp.sum((ps * delta).reshape(GQA_GROUP, WINDOW, 1), axis=1))
            dsb = ds.astype(BF16)
            dq = jnp.dot(dsb, kk, preferred_element_type=F32)
            for p, t in enumerate(_unstack_heads(dq, lo)):
                dq_ref[:, (h * pairs + p) * LANES:(h * pairs + p + 1) * LANES] = t
            dk = lax.dot_general(dsb, qs_, _DN["tn"], preferred_element_type=F32)
            dv = lax.dot_general(pn.astype(BF16), dos, _DN["tn"], preferred_element_type=F32)
            dkp_ref[h] = dk[:WINDOW]
            dkc_ref[h] = dk[WINDOW:]
            dvp_ref[h] = dv[:WINDOW]
            dvc_ref[h] = dv[WINDOW:]

    part = jax.ShapeDtypeStruct((n_kv, S, LANES), F32)
    return pl.pallas_call(
        body, name="attn_bwd", grid=(n_kv // kvs, S // WINDOW),
        in_specs=[qs, cur, prev, cur, prev, bias, sink, qs],
        out_specs=[qs, cur, cur, cur, cur, bias, pl.BlockSpec((kvs, 8, LANES), lambda kh, n: (kh, 0, 0))],
        out_shape=[jax.ShapeDtypeStruct((S, qw), F32), part, part, part, part,
                   jax.ShapeDtypeStruct((H, WINDOW, 2 * WINDOW), F32),
                   jax.ShapeDtypeStruct((n_kv, 8, LANES), F32)],
        compiler_params=_cparams(("parallel", "arbitrary")),
    )(q, kcat, kcat, vcat, vcat, biasm, sinks, do)


def _adamw(name, w, g, m, v):
    shape = w.shape
    C = shape[-1]
    R = int(np.prod(shape[:-1]))
    tr = R
    if R * C * 4 > (1 << 20):
        tr = _tile(R, tuple(t for t in (512, 256, 128, 64, 32, 16, 8) if t * C * 4 <= (3 << 19)))
    c1 = 1.0 - ADAM_B1 ** ADAM_STEP
    c2 = 1.0 - ADAM_B2 ** ADAM_STEP

    def body(w_ref, g_ref, m_ref, v_ref, d_ref, nm_ref, nv_ref):
        gv = g_ref[...]
        nm = ADAM_B1 * m_ref[...] + (1.0 - ADAM_B1) * gv
        nv = ADAM_B2 * v_ref[...] + (1.0 - ADAM_B2) * (gv * gv)
        d_ref[...] = -ADAM_LR * ((nm / c1) / (jnp.sqrt(nv / c2) + ADAM_EPS) + ADAM_WD * w_ref[...])
        nm_ref[...] = nm
        nv_ref[...] = nv

    spec = pl.BlockSpec((tr, C), lambda i: (i, 0))
    outs = pl.pallas_call(
        body, name=name, grid=(R // tr,),
        in_specs=[spec] * 4, out_specs=[spec] * 3,
        out_shape=[jax.ShapeDtypeStruct((R, C), F32)] * 3,
        compiler_params=_cparams(("parallel",)),
    )(*[t.reshape(R, C) for t in (w, g, m, v)])
    return [o.reshape(shape) for o in outs]


def _place():
    return lax.axis_index("x"), lax.axis_index("y"), lax.axis_index("c")


def _other_chips(x, y):
    return [(1 - x, y), (x, 1 - y), (1 - x, 1 - y)]


def _swap_halves(name, grads):
    T = len(grads)

    def body(*refs):
        ins, outs = refs[:T], refs[T:2 * T]
        send, recv = refs[2 * T:]
        x, y, c = _place()

        def copy(t):
            half = grads[t].shape[1] // 2
            src = ins[t].at[:, pl.ds(pl.multiple_of((1 - c) * half, 16), half), :]
            return pltpu.make_async_remote_copy(
                src_ref=src, dst_ref=outs[t], send_sem=send.at[t], recv_sem=recv.at[t],
                device_id=(x, y, 1 - c), device_id_type=MESH)

        for t in range(T):
            copy(t).start()
        for t in range(T):
            copy(t).wait()

    return pl.pallas_call(
        body, name=name,
        in_specs=[_ANY] * T, out_specs=[_ANY] * T,
        out_shape=[jax.ShapeDtypeStruct((g.shape[0], g.shape[1] // 2, g.shape[2]), g.dtype) for g in grads],
        scratch_shapes=[pltpu.SemaphoreType.DMA((T,)), pltpu.SemaphoreType.DMA((T,))],
        compiler_params=pltpu.CompilerParams(has_side_effects=True),
    )(*grads)


def _add_half(name, g, other, c_arr):
    nb, R, C = g.shape
    half = R // 2
    tr = _tile(half, (256, 128, 64, 32, 16))
    n = half // tr

    def body(c_ref, g_ref, o_ref, s_ref):
        s_ref[...] = (g_ref[...].astype(F32) + o_ref[...].astype(F32)).astype(BF16)

    return pl.pallas_call(
        body, name=name,
        grid_spec=pltpu.PrefetchScalarGridSpec(
            num_scalar_prefetch=1, grid=(nb, n),
            in_specs=[pl.BlockSpec((None, tr, C), lambda b, i, c: (b, c[0] * n + i, 0)),
                      pl.BlockSpec((None, tr, C), lambda b, i, c: (b, i, 0))],
            out_specs=pl.BlockSpec((None, tr, C), lambda b, i, c: (b, i, 0))),
        out_shape=jax.ShapeDtypeStruct((nb, half, C), BF16),
        compiler_params=_cparams(("parallel", "parallel")),
    )(c_arr, g, other)


def _in_hbm(arrays):
    return [pltpu.with_memory_space_constraint(a, pltpu.HBM) for a in arrays]


def _chip_copy(src, dst, send, recv, chip, c):
    return pltpu.make_async_remote_copy(src_ref=src, dst_ref=dst, send_sem=send, recv_sem=recv,
                                        device_id=(chip[0], chip[1], c), device_id_type=MESH)


def _cast_slot(name, w, l, chip_arr, dtype):
    _, R, C = w.shape
    tr = _tile(R, tuple(t for t in (1024, 512, 256, 128, 64, 32, 16) if t * C * 4 <= (1 << 21)))

    def body(chip_ref, w_ref, o_ref):
        o_ref[...] = w_ref[...].astype(dtype)

    return pl.pallas_call(
        body, name=name,
        grid_spec=pltpu.PrefetchScalarGridSpec(
            num_scalar_prefetch=1, grid=(R // tr,),
            in_specs=[pl.BlockSpec((None, tr, C), lambda i, chip: (l, i, 0))],
            out_specs=pl.BlockSpec((None, tr, C), lambda i, chip: (chip[0], i, 0))),
        out_shape=jax.ShapeDtypeStruct((N_CHIPS, R, C), dtype),
        compiler_params=_cparams(("parallel",)),
    )(chip_arr, w)


def _gather_start(groups):
    sizes = [len(g) for g in groups]
    flat = [b for g in groups for b in g]
    n, G = len(flat), len(groups)

    def body(*refs):
        bufs = refs[:n]
        sems = refs[n:n + 2 * G]
        token = refs[-1]
        x, y, c = _place()
        chips = _other_chips(x, y)
        me = 2 * x + y
        k = 0
        for l in range(G):
            for t in range(sizes[l]):
                for j in range(3):
                    _chip_copy(bufs[k].at[me], bufs[k].at[me], sems[2 * l].at[3 * t + j], sems[2 * l + 1].at[3 * t + j],
                               chips[j], c).start()
                k += 1
        token[...] = jnp.zeros_like(token)

    sem_shapes = []
    for s in sizes:
        sem_shapes += [pltpu.SemaphoreType.DMA((3 * s,)), pltpu.SemaphoreType.DMA((3 * s,))]
    outs = pl.pallas_call(
        body, name="gather_start",
        out_shape=(*sem_shapes, *[pltpu.HBM(b.shape, b.dtype) for b in flat], jax.ShapeDtypeStruct((8, LANES), F32)),
        in_specs=[_HBM] * n,
        out_specs=(*[_SEM] * (2 * G), *[_HBM] * n, pl.BlockSpec(memory_space=pltpu.VMEM)),
        input_output_aliases={t: 2 * G + t for t in range(n)},
        compiler_params=pltpu.CompilerParams(has_side_effects=_DATAFLOW),
    )(*_in_hbm(flat))
    res, k = [], 2 * G
    for l in range(G):
        res.append((outs[2 * l], outs[2 * l + 1], list(outs[k:k + sizes[l]])))
        k += sizes[l]
    return res, outs[-1]


def _gather_wait(name, send, recv, bufs, after):
    T = len(bufs)

    def body(*refs):
        ins = refs[:T]
        send_ref, recv_ref = refs[T], refs[T + 1]
        x, y, c = _place()
        chips = _other_chips(x, y)
        me = 2 * x + y
        for t in range(T):
            for j in range(3):
                cp = _chip_copy(ins[t].at[me], ins[t].at[2 * chips[j][0] + chips[j][1]], send_ref.at[3 * t + j],
                                recv_ref.at[3 * t + j], chips[j], c)
                cp.wait_send()
                cp.wait_recv()

    return pl.pallas_call(
        body, name=name,
        out_shape=[pltpu.HBM(b.shape, b.dtype) for b in bufs],
        in_specs=[_HBM] * T + [_SEM, _SEM, _ANY],
        out_specs=[_HBM] * T,
        input_output_aliases={t: t for t in range(T)},
        compiler_params=pltpu.CompilerParams(has_side_effects=_DATAFLOW),
    )(*bufs, send, recv, after)


def _scatter_start(name, sums):
    T = len(sums)
    lands = [lax.empty((3,) + s.shape[1:], s.dtype) for s in sums]

    def body(*refs):
        srcs, lnds = refs[:T], refs[T:2 * T]
        send, recv = refs[2 * T], refs[2 * T + 1]
        token = refs[-1]
        x, y, c = _place()
        chips = _other_chips(x, y)
        for t in range(T):
            for j in range(3):
                _chip_copy(srcs[t].at[2 * chips[j][0] + chips[j][1]], lnds[t].at[j], send.at[3 * t + j],
                           recv.at[3 * t + j], chips[j], c).start()
        token[...] = jnp.zeros_like(token)

    both = list(sums) + lands
    outs = pl.pallas_call(
        body, name=name,
        out_shape=(pltpu.SemaphoreType.DMA((3 * T,)), pltpu.SemaphoreType.DMA((3 * T,)),
                   *[pltpu.HBM(b.shape, b.dtype) for b in both], jax.ShapeDtypeStruct((8, LANES), F32)),
        in_specs=[_HBM] * (2 * T),
        out_specs=(_SEM, _SEM, *[_HBM] * (2 * T), pl.BlockSpec(memory_space=pltpu.VMEM)),
        input_output_aliases={t: 2 + t for t in range(2 * T)},
        compiler_params=pltpu.CompilerParams(has_side_effects=_DATAFLOW),
    )(*_in_hbm(both))
    return outs[0], outs[1], list(outs[2:2 + T]), list(outs[2 + T:2 + 2 * T]), outs[-1]


def _scatter_wait(name, send, recv, sums, lands, after):
    T = len(sums)

    def body(*refs):
        srcs, lnds = refs[:T], refs[T:2 * T]
        send_ref, recv_ref = refs[2 * T], refs[2 * T + 1]
        x, y, c = _place()
        chips = _other_chips(x, y)
        for t in range(T):
            for j in range(3):
                cp = _chip_copy(srcs[t].at[2 * chips[j][0] + chips[j][1]], lnds[t].at[j], send_ref.at[3 * t + j],
                                recv_ref.at[3 * t + j], chips[j], c)
                cp.wait_send()
                cp.wait_recv()

    both = list(sums) + list(lands)
    outs = pl.pallas_call(
        body, name=name,
        out_shape=[pltpu.HBM(b.shape, b.dtype) for b in both],
        in_specs=[_HBM] * (2 * T) + [_SEM, _SEM, _ANY],
        out_specs=[_HBM] * (2 * T),
        input_output_aliases={t: t for t in range(2 * T)},
        compiler_params=pltpu.CompilerParams(has_side_effects=_DATAFLOW),
    )(*both, send, recv, after)
    return list(outs[:T]), list(outs[T:])


def _sum_parts(name, sums, land, chip_arr, c_arr):
    _, R2, C = sums.shape
    tr = _tile(R2, (256, 128, 64, 32, 16))

    def body(chip_ref, c_ref, s_ref, l_ref, o_ref):
        acc = s_ref[...].astype(F32)
        for j in range(3):
            acc = acc + l_ref[j].astype(F32)
        o_ref[...] = acc

    return pl.pallas_call(
        body, name=name,
        grid_spec=pltpu.PrefetchScalarGridSpec(
            num_scalar_prefetch=2, grid=(R2 // tr,),
            in_specs=[pl.BlockSpec((None, tr, C), lambda i, chip, c: (chip[0], i, 0)),
                      pl.BlockSpec((3, tr, C), lambda i, chip, c: (0, i, 0))],
            out_specs=pl.BlockSpec((None, tr, C), lambda i, chip, c: (c[0], i, 0))),
        out_shape=jax.ShapeDtypeStruct((2, R2, C), F32),
        compiler_params=_cparams(("parallel",)),
    )(chip_arr, c_arr, sums, land)


def _join_halves(name, bufs):
    T = len(bufs)

    def body(*refs):
        outs = refs[T:2 * T]
        send, recv = refs[2 * T:]
        x, y, c = _place()

        def copy(t, half):
            return pltpu.make_async_remote_copy(
                src_ref=outs[t].at[c], dst_ref=outs[t].at[half], send_sem=send.at[t], recv_sem=recv.at[t],
                device_id=(x, y, 1 - c), device_id_type=MESH)

        for t in range(T):
            copy(t, c).start()
        for t in range(T):
            copy(t, 1 - c).wait_recv()
        for t in range(T):
            copy(t, c).wait_send()

    return pl.pallas_call(
        body, name=name,
        in_specs=[_ANY] * T, out_specs=[_ANY] * T,
        out_shape=[jax.ShapeDtypeStruct(b.shape, b.dtype) for b in bufs],
        input_output_aliases={t: t for t in range(T)},
        scratch_shapes=[pltpu.SemaphoreType.DMA((T,)), pltpu.SemaphoreType.DMA((T,))],
        compiler_params=pltpu.CompilerParams(has_side_effects=True),
    )(*bufs)


def _rs_begin(tag, grads, c_arr):
    got = _swap_halves("rs_swap_" + tag, grads)
    sums = [_add_half("rs_add_%s_%d" % (tag, t), g, o, c_arr) for t, (g, o) in enumerate(zip(grads, got))]
    return _scatter_start("rs_scatter_start_" + tag, sums)


def _rs_finish(tag, state, shapes, chip_arr, c_arr, after=None):
    send, recv, sums, lands, token = state
    sums, lands = _scatter_wait("rs_scatter_wait_" + tag, send, recv, sums, lands, token if after is None else after)
    parts = [_sum_parts("rs_sum_%s_%d" % (tag, t), s, l, chip_arr, c_arr) for t, (s, l) in enumerate(zip(sums, lands))]
    joined = _join_halves("rs_join_" + tag, parts)
    return [j.reshape(sh[1], sh[2]) for j, sh in zip(joined, shapes)]


def _all_reduce_small(v):
    R = v.shape[0]

    def body(v_ref, o_ref, buf, send, recv):
        x, y, c = _place()
        me = 4 * x + 2 * y + c
        buf[me] = v_ref[...]

        def copy(k, slot):
            to = (x ^ ((k >> 2) & 1), y ^ ((k >> 1) & 1), c ^ (k & 1))
            return pltpu.make_async_remote_copy(
                src_ref=v_ref, dst_ref=buf.at[slot], send_sem=send.at[k - 1], recv_sem=recv.at[k - 1],
                device_id=to, device_id_type=MESH)

        for k in range(1, 8):
            copy(k, me).start()
        for k in range(1, 8):
            copy(k, me ^ k).wait_recv()
        for k in range(1, 8):
            copy(k, me).wait_send()
        acc = buf[0]
        for s in range(1, 8):
            acc = acc + buf[s]
        o_ref[...] = acc

    vm = pl.BlockSpec(memory_space=pltpu.VMEM)
    return pl.pallas_call(
        body, name="all_reduce_small",
        in_specs=[vm], out_specs=vm,
        out_shape=jax.ShapeDtypeStruct((R, LANES), F32),
        scratch_shapes=[pltpu.VMEM((8, R, LANES), F32), pltpu.SemaphoreType.DMA((7,)), pltpu.SemaphoreType.DMA((7,))],
        compiler_params=pltpu.CompilerParams(has_side_effects=True, vmem_limit_bytes=VMEM_LIMIT),
    )(v)


def _local_step(x, tgt, small, depth, nb, weight, convw, pscale, on_grads):
    S, D = x.shape
    n_q = D // HEAD_DIM
    n_kv = n_q // GQA_GROUP
    Fh = convw[0].shape[2]

    def dup(gain):
        return jnp.tile(gain, 2).reshape(1, LANES)

    biasm = _bias_expand(small["rel_bias"])
    saved = []
    for i in range(depth):
        j = i // 2
        w = {}
        st = {"x0": x, "w": w}
        gm = small["norm_mix"][i].reshape(1, D)
        if i % 2 == 0:
            h = _rms_fwd("rms_mix_fwd", x, gm)
            w["wqkv"] = weight(i, "wqkv", h)
            qkv = _mm_colblk("qkv_fwd", h, w["wqkv"], F32)
            qh, kcat, vcat = _qk_norm_fwd(qkv, dup(small["attn_q_gain"][j]), dup(small["attn_k_gain"][j]), n_q, n_kv)
            o = _attn_fwd(qh, kcat, vcat, biasm, small["attn_sinks"][j])
            w["wo"] = weight(i, "wo", o)
            x = _mm_rowblk_res("wo_fwd", o, w["wo"], x)
            st.update(h=h, qkv=qkv, qh=qh, kcat=kcat, vcat=vcat, o=o)
        else:
            d = _pool_fwd(x, gm)
            w["wp"] = weight(i, "wp", d)
            ypre, x = _pool_mm_fwd(d, w["wp"], x, pscale[j])
            st.update(d=d, ypre=ypre)
        st["x1"] = x
        h2 = _rms_fwd("rms_ffn_fwd", x, small["norm_ffn"][i].reshape(1, D))
        w["wup"] = weight(i, "wup", h2)
        cb = small["ffn_conv_b"][i].reshape(2, 1, Fh)
        u, uc, a = _up_act_fwd(h2, w["wup"], convw[i], cb)
        w["wdn"] = weight(i, "wdn", a)
        x = _mm_rowblk_res("down_fwd", a, w["wdn"], x)
        st.update(h2=h2, u=u, uc=uc, a=a)
        saved.append(st)

    dx, dxb, loss_part = _loss_head(x, tgt)

    big = [dict() for _ in range(depth)]
    sg = {k: [None] * depth for k in ("norm_mix", "norm_ffn", "conv_w", "conv_b")}
    sg.update({k: [None] * ((depth + 1) // 2) for k in ("q_gain", "k_gain", "sinks")})
    sg["pool_scale"] = [None] * (depth // 2)
    dbias_tot = None
    before = None
    for i in reversed(range(depth)):
        j = i // 2
        st = saved[i]
        w = st["w"]
        duc = _down_act_bwd(dxb, w["wdn"], st["uc"], after=before)
        wdn_grad = _mm_tn_rowblk("down_bwd_w", st["a"], dxb, nb)
        du, dh2, dcwb = _up_bwd_in(duc, st["u"], convw[i], w["wup"])
        dcwb = jnp.transpose(dcwb, (1, 0, 2)).reshape(8, 2 * Fh)
        sg["conv_w"][i] = dcwb[0:3]
        sg["conv_b"][i] = dcwb[3]
        wup_grad = _mm_tn_colblk("up_bwd_w", st["h2"], du, nb, split2=True)
        before = on_grads(i, "ffn", dict(wdn=wdn_grad, wup=wup_grad), dx)
        dx, dxb, dg = _rms_bwd("rms_ffn_bwd", st["x1"], small["norm_ffn"][i].reshape(1, D), dh2, dx, after=before)
        sg["norm_ffn"][i] = dg.reshape(D)
        gm = small["norm_mix"][i].reshape(1, D)
        if i % 2 == 0:
            do = _mm_nt_rowblk("wo_bwd_in", dxb, w["wo"], BF16)
            big[i]["wo"] = _mm_tn_rowblk("wo_bwd_w", st["o"], dxb, nb)
            dq, dkc, dkp, dvc, dvp, dbias, dsink = _attn_bwd(
                st["qh"], st["kcat"], st["vcat"], biasm, small["attn_sinks"][j], do)
            dbias_tot = dbias if dbias_tot is None else dbias_tot + dbias
            sg["sinks"][j] = dsink[:, :, 0].reshape(n_q)
            dqkv, dqg, dkg = _qk_norm_bwd(st["qkv"], dup(small["attn_q_gain"][j]), dup(small["attn_k_gain"][j]),
                                          dq, dkc, dkp, dvc, dvp, n_q, n_kv)
            sg["q_gain"][j] = dqg[0, :HEAD_DIM]
            sg["k_gain"][j] = dkg[0, :HEAD_DIM]
            big[i]["wqkv"] = _mm_tn_colblk("qkv_bwd_w", st["h"], dqkv, nb)
            dh = _mm_nt_colblk("qkv_bwd_in", dqkv, w["wqkv"])
            dx, dxb, dg = _rms_bwd("rms_mix_bwd", st["x0"], gm, dh, dx)
        else:
            dyp, dsc = _pool_bwd_pre(dx, st["ypre"], pscale[j])
            sg["pool_scale"][j] = dsc.reshape(D)
            big[i]["wp"] = _pool_mm_bwd_w(st["d"], dyp)
            dd = _pool_mm_bwd_in(dyp, w["wp"])
            dx, dxb, dg = _pool_bwd_post(dd, st["x0"], gm, dx)
        sg["norm_mix"][i] = dg.reshape(D)
        before = on_grads(i, "mix", big[i], dx)
    sg["rel_bias"] = _bias_reduce(dbias_tot)[:, 0, :N_BUCKETS]
    return loss_part, dx, sg


_SMALL_ORDER = ("norm_mix", "norm_ffn", "rel_bias", "q_gain", "k_gain", "sinks", "conv_b", "conv_w", "pool_scale")


def kernel(x, norm_mix, norm_ffn, rel_bias, attn_w_qkv, attn_q_gain, attn_k_gain, attn_sinks, attn_w_o, pool_w, pool_scale, ffn_w_up, ffn_conv_w, ffn_conv_b, ffn_w_down, loss_target, m_norm_mix, m_norm_ffn, m_rel_bias, m_attn_w_qkv, m_attn_q_gain, m_attn_k_gain, m_attn_sinks, m_attn_w_o, m_pool_w, m_pool_scale, m_ffn_w_up, m_ffn_conv_w, m_ffn_conv_b, m_ffn_w_down, v_norm_mix, v_norm_ffn, v_rel_bias, v_attn_w_qkv, v_attn_q_gain, v_attn_k_gain, v_attn_sinks, v_attn_w_o, v_pool_w, v_pool_scale, v_ffn_w_up, v_ffn_conv_w, v_ffn_conv_b, v_ffn_w_down):
    _, S, D = x.shape
    depth = ffn_w_up.shape[0]
    n_attn, n_pool = attn_w_qkv.shape[0], pool_w.shape[0]
    nb = N_CHIPS
    nc_up = ffn_w_up.shape[2]
    Fh = nc_up * nb // 2
    cx, cy, cc = _place()
    chip = 2 * cx + cy
    c_arr = jnp.reshape(cc, (1,)).astype(jnp.int32)
    chip_arr = jnp.reshape(chip, (1,)).astype(jnp.int32)

    pool_w3 = pool_w.reshape(n_pool, -1, pool_w.shape[-1])
    order = [("convw", ffn_conv_w.reshape(1, -1, nc_up), 0, F32), ("pscale", pool_scale[None], 0, F32)]
    for i in range(depth):
        j = i // 2
        mixer = ([("wqkv", attn_w_qkv, j), ("wo", attn_w_o, j)] if i % 2 == 0 else [("wp", pool_w3, j)])
        order += [((i, k), w, l, BF16) for k, w, l in mixer + [("wup", ffn_w_up, i), ("wdn", ffn_w_down, i)]]
    slots = [_cast_slot("slot_%d" % n, w, l, chip_arr, dt) for n, (_, w, l, dt) in enumerate(order)]
    started, token = _gather_start([[b] for b in slots])
    index = {key: n for n, (key, _, _, _) in enumerate(order)}

    def gathered(key, after):
        n = index[key]
        send, recv, bufs = started[n]
        return _gather_wait("gather_wait_%d" % n, send, recv, bufs, after)[0]

    convw_g = gathered("convw", token).reshape(nb, depth, 3, nc_up)
    pscale_g = gathered("pscale", token)
    convw = [jnp.transpose(convw_g[:, i], (1, 0, 2)).reshape(3, 2, Fh).transpose(1, 0, 2) for i in range(depth)]
    pscale = [pscale_g[:, j].reshape(1, D) for j in range(n_pool)]
    small = dict(norm_mix=norm_mix, norm_ffn=norm_ffn, rel_bias=rel_bias, attn_q_gain=attn_q_gain,
                 attn_k_gain=attn_k_gain, attn_sinks=attn_sinks, ffn_conv_b=ffn_conv_b)

    G, gd = pool_w.shape[1], pool_w.shape[3]

    def weight(i, name, after):
        w = gathered((i, name), after)
        if name == "wp":
            w = jnp.transpose(w.reshape(nb, G, gd // nb, gd), (1, 0, 2, 3)).reshape(G, gd, gd)
        return w

    red = {k: [None] * (n_attn if k in ("wqkv", "wo") else n_pool if k == "wp" else depth)
           for k in ("wqkv", "wo", "wp", "wup", "wdn")}
    pending = []

    def finish(after):
        i, tag, names, shapes, state = pending.pop()
        outs = _rs_finish("%d%s" % (i, tag), state, shapes, chip_arr, c_arr, after)
        for k, o in zip(names, outs):
            red[k][i // 2 if k in ("wqkv", "wo", "wp") else i] = o

    def on_grads(i, tag, grads, dx_i):
        if pending:
            finish(dx_i)
        names = sorted(grads)
        flat = []
        for k in names:
            g = grads[k]
            if k == "wp":
                g = jnp.transpose(g.reshape(G, nb, gd // nb, gd), (1, 0, 2, 3))
            flat.append(g.reshape(nb, -1, g.shape[-1]))
        state = _rs_begin("%d%s" % (i, tag), flat, c_arr)
        pending.append((i, tag, names, [f.shape for f in flat], state))
        return state[-1]

    loss_part, dx, sg = _local_step(x[0], loss_target[0], small, depth, nb, weight, convw, pscale, on_grads)
    finish(None)
    loss = lax.psum(jnp.sum(loss_part), ("x", "y", "c"))

    g_wqkv = jnp.stack(red["wqkv"])
    g_wo = jnp.stack(red["wo"])
    g_wp = jnp.stack(red["wp"]).reshape(pool_w.shape)
    g_wup = jnp.stack(red["wup"])
    g_wdn = jnp.stack(red["wdn"])

    parts = [jnp.stack(sg[k]) if isinstance(sg[k], list) else sg[k] for k in _SMALL_ORDER]
    sizes = [int(np.prod(p.shape)) for p in parts]
    total = sum(sizes)
    rows = -(-total // (8 * LANES)) * 8
    packed = jnp.concatenate([p.reshape(-1) for p in parts] + [jnp.zeros((rows * LANES - total,), F32)])
    summed = _all_reduce_small(packed.reshape(rows, LANES)).reshape(-1)
    sm, off = {}, 0
    for k, p, n in zip(_SMALL_ORDER, parts, sizes):
        sm[k] = summed[off:off + n].reshape(p.shape)
        off += n
    g_convw = lax.dynamic_slice_in_dim(sm["conv_w"], chip * nc_up, nc_up, axis=2)
    pc = pool_scale.shape[1]
    g_pscale = lax.dynamic_slice_in_dim(sm["pool_scale"], chip * pc, pc, axis=1)

    grads = [sm["norm_mix"], sm["norm_ffn"], sm["rel_bias"], g_wqkv, sm["q_gain"], sm["k_gain"], sm["sinks"], g_wo,
             g_wp, g_pscale, g_wup, g_convw, sm["conv_b"], g_wdn]
    ws = [norm_mix, norm_ffn, rel_bias, attn_w_qkv, attn_q_gain, attn_k_gain, attn_sinks, attn_w_o, pool_w, pool_scale,
          ffn_w_up, ffn_conv_w, ffn_conv_b, ffn_w_down]
    ms = [m_norm_mix, m_norm_ffn, m_rel_bias, m_attn_w_qkv, m_attn_q_gain, m_attn_k_gain, m_attn_sinks, m_attn_w_o,
          m_pool_w, m_pool_scale, m_ffn_w_up, m_ffn_conv_w, m_ffn_conv_b, m_ffn_w_down]
    vs = [v_norm_mix, v_norm_ffn, v_rel_bias, v_attn_w_qkv, v_attn_q_gain, v_attn_k_gain, v_attn_sinks, v_attn_w_o,
          v_pool_w, v_pool_scale, v_ffn_w_up, v_ffn_conv_w, v_ffn_conv_b, v_ffn_w_down]
    deltas, new_m, new_v = [], [], []
    for idx, (w, g, m, v) in enumerate(zip(ws, grads, ms, vs)):
        d, nm, nv = _adamw("adamw_%d" % idx, w, g, m, v)
        deltas.append(d), new_m.append(nm), new_v.append(nv)
    return (loss, dx.reshape(1, S, D), *grads, *deltas, *new_m, *new_v)
```

```python
import functools

import numpy as np
import jax
import jax.numpy as jnp
from jax import lax
from jax.experimental import pallas as pl
from jax.experimental.pallas import tpu as pltpu

F32 = jnp.float32
BF16 = jnp.bfloat16
MESH = pl.DeviceIdType.MESH
_ANY = pl.BlockSpec(memory_space=pl.ANY)
_HBM = pl.BlockSpec(memory_space=pltpu.HBM)
_SEM = pl.BlockSpec(memory_space=pltpu.SEMAPHORE)
_DATAFLOW = pltpu.SideEffectType.DATAFLOW_SIDE_EFFECTING

N_CHIPS = 4
HEAD_DIM = 64
GQA_GROUP = 8
WINDOW = 128
N_BUCKETS = 32
MAX_DISTANCE = 128
POOL_WINDOWS = (2, 4, 8, 16)
POOL_HALO = 16
ATTN_BWD_KV_PER_STEP = 4
EPS = 1e-6
NEG_INF = -1e30
LANES = 128
VMEM_LIMIT = 56 * 1024 * 1024

ADAM_LR = 0.001
ADAM_B1 = 0.9
ADAM_B2 = 0.999
ADAM_EPS = 1e-08
ADAM_WD = 0.01
ADAM_STEP = 10


def _tile(n, prefs):
    for p in prefs:
        if p <= n and n % p == 0:
            return p
    return n


def _cparams(sem):
    return pltpu.CompilerParams(dimension_semantics=sem, vmem_limit_bytes=VMEM_LIMIT)


_DN = {
    "nn": (((1,), (0,)), ((), ())),
    "nt": (((1,), (1,)), ((), ())),
    "tn": (((0,), (0,)), ((), ())),
}


def _mm(name, kind, a, b, grid, a_spec, b_spec, outs, acc_shape, extras=(), epilogue=None, after=None):
    nk = grid[2]
    n_ex, n_out = len(extras), len(outs)
    order = [] if after is None else [after]

    def body(a_ref, b_ref, *rest):
        ex_refs = rest[:n_ex]
        rest = rest[n_ex + len(order):]
        out_refs = rest[:n_out]
        acc_ref = rest[n_out] if nk > 1 else None
        part = lax.dot_general(a_ref[...], b_ref[...], _DN[kind], preferred_element_type=F32)

        def finish(val):
            vals = epilogue(val, *[r[...] for r in ex_refs]) if epilogue else (val,)
            for r, v in zip(out_refs, vals):
                r[...] = v.astype(r.dtype)

        if nk == 1:
            finish(part)
        else:
            k = pl.program_id(2)

            @pl.when(k == 0)
            def _():
                acc_ref[...] = part

            @pl.when(k > 0)
            def _():
                acc_ref[...] += part

            @pl.when(k == nk - 1)
            def _():
                finish(acc_ref[...])

    res = pl.pallas_call(
        body,
        name=name,
        grid=grid,
        in_specs=[a_spec, b_spec] + [s for _, s in extras] + [_ANY] * len(order),
        out_specs=[s for _, _, s in outs],
        out_shape=[jax.ShapeDtypeStruct(sh, dt) for sh, dt, _ in outs],
        scratch_shapes=[pltpu.VMEM(acc_shape, F32)] if nk > 1 else [],
        compiler_params=_cparams(("parallel", "parallel", "arbitrary")),
    )(a, b, *[e for e, _ in extras], *order)
    return res if n_out > 1 else res[0]


def _mm_colblk(name, a, wg, out_dtype):
    S, K = a.shape
    nb, _, nc = wg.shape
    tm = _tile(S, (1024, 512))
    tn = _tile(nc, (1408, 1024, 640, 512, 256, 128))
    npb = nc // tn
    return _mm(
        name, "nn", a, wg, (nb * npb, S // tm, 1),
        pl.BlockSpec((tm, K), lambda p, q, k: (q, 0)),
        pl.BlockSpec((None, K, tn), lambda p, q, k: (p // npb, 0, p % npb)),
        [((S, nb * nc), out_dtype, pl.BlockSpec((tm, tn), lambda p, q, k: (q, p)))], (tm, tn))


def _mm_rowblk_res(name, a, wg, res):
    S = a.shape[0]
    nb, kc, N = wg.shape
    tm = _tile(S, (1024, 512))
    ktot = nb * kc

    def vmem(t):
        return 2 * 2 * (tm * ktot + ktot * t) + 4 * 4 * tm * t

    tn = _tile(N, tuple(t for t in (2048, 1024, 512, 256) if vmem(t) <= (44 << 20)))

    def body(a_ref, b_ref, r_ref, o_ref):
        acc = r_ref[...]
        for r in range(nb):
            acc = acc + jnp.dot(a_ref[:, r * kc:(r + 1) * kc], b_ref[r], preferred_element_type=F32)
        o_ref[...] = acc

    o_spec = pl.BlockSpec((tm, tn), lambda i, j: (i, j))
    return pl.pallas_call(
        body, name=name, grid=(S // tm, N // tn),
        in_specs=[pl.BlockSpec((tm, nb * kc), lambda i, j: (i, 0)),
                  pl.BlockSpec((nb, kc, tn), lambda i, j: (0, 0, j)), o_spec],
        out_specs=o_spec,
        out_shape=jax.ShapeDtypeStruct((S, N), F32),
        compiler_params=_cparams(("parallel", "parallel")),
    )(a, wg, res)


def _mm_nt_rowblk(name, g, wg, out_dtype, after=None):
    S, N = g.shape
    nb, kc, _ = wg.shape
    tm = _tile(S, (1024, 512))
    tn = _tile(kc, (1408, 512, 256, 128))
    kpb = kc // tn
    return _mm(
        name, "nt", g, wg, (nb * kpb, S // tm, 1),
        pl.BlockSpec((tm, N), lambda p, q, k: (q, 0)),
        pl.BlockSpec((None, tn, N), lambda p, q, k: (p // kpb, p % kpb, 0)),
        [((S, nb * kc), out_dtype, pl.BlockSpec((tm, tn), lambda p, q, k: (q, p)))], (tm, tn), after=after)


def _mm_nt_colblk(name, g, wg):
    S = g.shape[0]
    nb, K, nc = wg.shape
    tm = _tile(S, (512,))

    def body(g_ref, b_ref, o_ref):
        acc = lax.dot_general(g_ref[:, 0:nc], b_ref[0], _DN["nt"], preferred_element_type=F32)
        for r in range(1, nb):
            acc = acc + lax.dot_general(g_ref[:, r * nc:(r + 1) * nc], b_ref[r], _DN["nt"],
                                        preferred_element_type=F32)
        o_ref[...] = acc

    return pl.pallas_call(
        body, name=name, grid=(S // tm,),
        in_specs=[pl.BlockSpec((tm, nb * nc), lambda i: (i, 0)), pl.BlockSpec((nb, K, nc), lambda i: (0, 0, 0))],
        out_specs=pl.BlockSpec((tm, K), lambda i: (i, 0)),
        out_shape=jax.ShapeDtypeStruct((S, K), F32),
        compiler_params=_cparams(("parallel",)),
    )(g, wg)


def _mm_tn_colblk(name, a, g, nb, split2=False):
    S, K = a.shape
    ntot = g.shape[-1] * (2 if split2 else 1)
    nc = ntot // nb
    ti = _tile(K, (1024,))
    tn = _tile(nc, (1408, 640, 512, 256, 128))
    ts = _tile(S, (2048, 1024, 512, 256))
    npb = nc // tn
    half = nb * npb // 2
    if split2:
        b_spec = pl.BlockSpec((None, ts, tn), lambda p, q, k: (q // half, k, q % half))
    else:
        b_spec = pl.BlockSpec((ts, tn), lambda p, q, k: (k, q))
    return _mm(
        name, "tn", a, g, (K // ti, nb * npb, S // ts),
        pl.BlockSpec((ts, ti), lambda p, q, k: (k, p)),
        b_spec,
        [((nb, K, nc), BF16, pl.BlockSpec((None, ti, tn), lambda p, q, k: (q // npb, p, q % npb)))], (ti, tn))


def _mm_tn_rowblk(name, a, g, nb):
    S, ktot = a.shape
    N = g.shape[1]
    kc = ktot // nb
    ti = _tile(kc, (1408, 512, 256, 128))
    tj = _tile(N, (1024,))
    ts = _tile(S, (2048, 1024, 512, 256))
    ipb = kc // ti
    return _mm(
        name, "tn", a, g, (nb * ipb, N // tj, S // ts),
        pl.BlockSpec((ts, ti), lambda p, q, k: (k, p)),
        pl.BlockSpec((ts, tj), lambda p, q, k: (k, q)),
        [((nb, kc, N), BF16, pl.BlockSpec((None, ti, tj), lambda p, q, k: (p // ipb, p % ipb, q)))], (ti, tj))


def _rms_fwd(name, x, gain):
    S, D = x.shape
    tm = _tile(S, (512,))

    def body(x_ref, g_ref, o_ref):
        xv = x_ref[...]
        r = lax.rsqrt(jnp.mean(xv * xv, axis=-1, keepdims=True) + EPS)
        o_ref[...] = (xv * r * g_ref[...]).astype(o_ref.dtype)

    return pl.pallas_call(
        body, name=name, grid=(S // tm,),
        in_specs=[pl.BlockSpec((tm, D), lambda i: (i, 0)), pl.BlockSpec((1, D), lambda i: (0, 0))],
        out_specs=pl.BlockSpec((tm, D), lambda i: (i, 0)),
        out_shape=jax.ShapeDtypeStruct((S, D), BF16),
        compiler_params=_cparams(("parallel",)),
    )(x, gain)


def _rms_bwd_math(xv, gain, dh):
    r = lax.rsqrt(jnp.mean(xv * xv, axis=-1, keepdims=True) + EPS)
    xhat = xv * r
    dxhat = dh * gain
    c = jnp.mean(dxhat * xhat, axis=-1, keepdims=True)
    return r * (dxhat - xhat * c), dh * xhat


def _rows_to_8(v):
    tm, C = v.shape
    return jnp.sum(v.reshape(tm // 8, 8, C), axis=0)


def _rms_bwd(name, x, gain, dh, dres, after=None):
    S, D = x.shape
    tm = _tile(S, (256,))
    n = S // tm
    order = [] if after is None else [after]

    def body(x_ref, g_ref, dh_ref, dr_ref, *rest):
        dx_ref, dxb_ref, dg_ref, acc_ref = rest[len(order):]
        i = pl.program_id(0)
        dxn, dgr = _rms_bwd_math(x_ref[...], g_ref[...], dh_ref[...])
        dx = dr_ref[...] + dxn
        dx_ref[...] = dx
        dxb_ref[...] = dx.astype(BF16)

        @pl.when(i == 0)
        def _():
            acc_ref[...] = jnp.zeros_like(acc_ref)

        acc_ref[...] += _rows_to_8(dgr)

        @pl.when(i == n - 1)
        def _():
            dg_ref[...] = jnp.sum(acc_ref[...], axis=0, keepdims=True)

    row = pl.BlockSpec((tm, D), lambda i: (i, 0))
    vec = pl.BlockSpec((1, D), lambda i: (0, 0))
    return pl.pallas_call(
        body, name=name, grid=(n,),
        in_specs=[row, vec, row, row] + [_ANY] * len(order),
        out_specs=[row, row, vec],
        out_shape=[jax.ShapeDtypeStruct((S, D), F32), jax.ShapeDtypeStruct((S, D), BF16),
                   jax.ShapeDtypeStruct((1, D), F32)],
        scratch_shapes=[pltpu.VMEM((8, D), F32)],
        compiler_params=_cparams(("arbitrary",)),
    )(x, gain, dh, dres, *order)


def _loss_head(y, tgt):
    S, D = y.shape
    tm = _tile(S, (256,))
    n = S // tm

    def body(y_ref, t_ref, dy_ref, dyb_ref, l_ref):
        i = pl.program_id(0)
        e = y_ref[...] - t_ref[...]
        dy = e * (1.0 / D)
        dy_ref[...] = dy
        dyb_ref[...] = dy.astype(BF16)

        @pl.when(i == 0)
        def _():
            l_ref[...] = jnp.zeros_like(l_ref)

        sq = _rows_to_8(e * e)
        part = sq[:, 0:LANES]
        for t in range(1, D // LANES):
            part = part + sq[:, t * LANES:(t + 1) * LANES]
        l_ref[...] += part * (0.5 / D)

    row = pl.BlockSpec((tm, D), lambda i: (i, 0))
    return pl.pallas_call(
        body, name="loss_head", grid=(n,),
        in_specs=[row, row],
        out_specs=[row, row, pl.BlockSpec((8, LANES), lambda i: (0, 0))],
        out_shape=[jax.ShapeDtypeStruct((S, D), F32), jax.ShapeDtypeStruct((S, D), BF16),
                   jax.ShapeDtypeStruct((8, LANES), F32)],
        compiler_params=_cparams(("arbitrary",)),
    )(y, tgt)


def _up_act_fwd(h2, wg, cw, cb):
    S, K = h2.shape
    nb, _, nc = wg.shape
    Fh = nb * nc // 2
    tm = _tile(S, (2048, 1024, 512))
    tn = _tile(nc, (256, 128))
    sub = _tile(tm, (512,))
    ch = _tile(sub, (32,))
    npb = nc // tn
    hb = nb // 2
    ncol = Fh // tn

    def body(a_ref, bg_ref, bv_ref, cw_ref, cb_ref, u_ref, uc_ref, o_ref, buf_ref, halo_ref):
        j = pl.program_id(1)

        @pl.when(pl.program_id(0) == 0)
        def _():
            halo_ref[j] = jnp.zeros((2, 8, tn), F32)

        buf_ref[:, 0:8, :] = halo_ref[j]
        cws, cbs = [cw_ref[0], cw_ref[1]], [cb_ref[0], cb_ref[1]]
        for r in range(tm // sub):
            for s, b_ref in enumerate((bg_ref, bv_ref)):
                buf_ref[s, 8 + r * sub:8 + (r + 1) * sub, :] = jnp.dot(
                    a_ref[r * sub:(r + 1) * sub, :], b_ref[...], preferred_element_type=F32)
            for c in range(sub // ch):
                base = r * sub + c * ch
                ucs = []
                for s in range(2):
                    ext = buf_ref[s, base:base + ch + 8, :]
                    us = ext[8:]
                    uc = (cws[s][0:1] * pltpu.roll(ext, 2, 0)[8:] + cws[s][1:2] * pltpu.roll(ext, 1, 0)[8:]
                          + cws[s][2:3] * us + cbs[s])
                    u_ref[s, base:base + ch, :] = us.astype(BF16)
                    uc_ref[s, base:base + ch, :] = uc.astype(BF16)
                    ucs.append(uc)
                gate, val = ucs
                o_ref[base:base + ch, :] = (gate * (1.0 / (1.0 + jnp.exp(-gate))) * val).astype(BF16)
        halo_ref[j] = buf_ref[:, tm:tm + 8, :]

    both = pl.BlockSpec((2, tm, tn), lambda i, j: (0, i, j))
    return pl.pallas_call(
        body, name="up_act_fwd", grid=(S // tm, ncol),
        in_specs=[pl.BlockSpec((tm, K), lambda i, j: (i, 0)),
                  pl.BlockSpec((None, K, tn), lambda i, j: (j // npb, 0, j % npb)),
                  pl.BlockSpec((None, K, tn), lambda i, j: (j // npb + hb, 0, j % npb)),
                  pl.BlockSpec((2, 3, tn), lambda i, j: (0, 0, j)),
                  pl.BlockSpec((2, 1, tn), lambda i, j: (0, 0, j))],
        out_specs=[both, both, pl.BlockSpec((tm, tn), lambda i, j: (i, j))],
        out_shape=[jax.ShapeDtypeStruct((2, S, Fh), BF16), jax.ShapeDtypeStruct((2, S, Fh), BF16),
                   jax.ShapeDtypeStruct((S, Fh), BF16)],
        scratch_shapes=[pltpu.VMEM((2, tm + 8, tn), F32), pltpu.VMEM((ncol, 2, 8, tn), F32)],
        compiler_params=_cparams(("arbitrary", "arbitrary")),
    )(h2, wg, wg, cw, cb)


def _down_act_bwd(dxb, wg, uc, after=None):
    S, D = dxb.shape
    nb, kc, _ = wg.shape
    Fh = nb * kc
    tm = _tile(S, (1024, 512))
    sub = _tile(tm, (256,))
    ch = _tile(sub, (32,))
    t128 = kc // LANES
    tn = 2 * LANES
    order = [] if after is None else [after]

    def body(dx_ref, b0_ref, b1_ref, uc_ref, *rest):
        duc_ref, bcat_ref, da_ref = rest[len(order):]
        bcat_ref[0:LANES, :] = b0_ref[...]
        bcat_ref[LANES:tn, :] = b1_ref[...]
        for r in range(tm // sub):
            rows = slice(r * sub, (r + 1) * sub)
            da_ref[rows, :] = lax.dot_general(dx_ref[rows, :], bcat_ref[...], _DN["nt"], preferred_element_type=F32)
            for c in range(sub // ch):
                cs = slice(r * sub + c * ch, r * sub + (c + 1) * ch)
                da = da_ref[cs, :]
                gate, val = uc_ref[0, cs, :].astype(F32), uc_ref[1, cs, :].astype(F32)
                sig = 1.0 / (1.0 + jnp.exp(-gate))
                duc_ref[0, cs, :] = (da * val * (sig * (1.0 + gate * (1.0 - sig)))).astype(BF16)
                duc_ref[1, cs, :] = (da * (gate * sig)).astype(BF16)

    both = pl.BlockSpec((2, tm, tn), lambda i, j: (0, i, j))
    return pl.pallas_call(
        body, name="down_act_bwd", grid=(S // tm, Fh // tn),
        in_specs=[pl.BlockSpec((tm, D), lambda i, j: (i, 0)),
                  pl.BlockSpec((None, LANES, D), lambda i, j: ((2 * j) // t128, (2 * j) % t128, 0)),
                  pl.BlockSpec((None, LANES, D), lambda i, j: ((2 * j + 1) // t128, (2 * j + 1) % t128, 0)),
                  both] + [_ANY] * len(order),
        out_specs=both,
        out_shape=jax.ShapeDtypeStruct((2, S, Fh), BF16),
        scratch_shapes=[pltpu.VMEM((tn, D), BF16), pltpu.VMEM((tm, tn), F32)],
        compiler_params=_cparams(("parallel", "parallel")),
    )(dxb, wg, wg, uc, *order)


def _up_bwd_in(duc, u, cw, wg):
    _, S, Fh = duc.shape
    nb, K, nc = wg.shape
    tm = _tile(S, (1024, 512))
    tk = _tile(nc, (1408, 640, 512, 256, 128))
    sub = _tile(tm, (256,))
    ch = _tile(sub, (64,))
    hr = 16
    npb = nc // tk
    nk = nb * npb
    half = nk // 2
    n, nh = S // tm, tm // hr

    def body(d_ref, h_ref, u_ref, cw_ref, b_ref, du_ref, o_ref, cg_ref):
        i, k = pl.program_id(0), pl.program_id(1)
        keep = (i < n - 1).astype(F32)

        @pl.when(i == 0)
        def _():
            cg_ref[k] = jnp.zeros((8, tk), F32)

        @pl.when(k == 0)
        def _():
            o_ref[...] = jnp.zeros_like(o_ref)

        for r in range(tm // sub):
            for l in range(tk // LANES):
                ls = slice(l * LANES, (l + 1) * LANES)
                w = cw_ref[:, ls]
                sums = [jnp.zeros((8, LANES), F32) for _ in range(4)]
                for c in range(sub // ch):
                    base = r * sub + c * ch
                    if base + ch + hr <= tm:
                        ext = d_ref[base:base + ch + hr, ls].astype(F32)
                    else:
                        ext = jnp.concatenate([d_ref[base:base + ch, ls].astype(F32),
                                               h_ref[:, ls].astype(F32) * keep], axis=0)
                    d = ext[:ch]
                    d1 = pltpu.roll(ext, ch + hr - 1, 0)[:ch]
                    d2 = pltpu.roll(ext, ch + hr - 2, 0)[:ch]
                    du_ref[base:base + ch, ls] = (w[2:3] * d + w[1:2] * d1 + w[0:1] * d2).astype(BF16)
                    uv = u_ref[base:base + ch, ls].astype(F32)
                    for q, v in enumerate((d2 * uv, d1 * uv, d * uv, d)):
                        sums[q] = sums[q] + _rows_to_8(v)
                for q in range(4):
                    cg_ref[k, q:q + 1, ls] += jnp.sum(sums[q], axis=0, keepdims=True)
            rows = slice(r * sub, (r + 1) * sub)
            o_ref[rows, :] += lax.dot_general(du_ref[rows, :], b_ref[...], _DN["nt"], preferred_element_type=F32)

    blk = pl.BlockSpec((None, tm, tk), lambda i, k: (k // half, i, k % half))
    return pl.pallas_call(
        body, name="up_bwd_in", grid=(n, nk),
        in_specs=[blk,
                  pl.BlockSpec((None, hr, tk), lambda i, k: (k // half, jnp.minimum((i + 1) * nh, S // hr - 1), k % half)),
                  blk,
                  pl.BlockSpec((None, 3, tk), lambda i, k: (k // half, 0, k % half)),
                  pl.BlockSpec((None, K, tk), lambda i, k: (k // npb, 0, k % npb))],
        out_specs=[blk, pl.BlockSpec((tm, K), lambda i, k: (i, 0)), pl.BlockSpec((nk, 8, tk), lambda i, k: (0, 0, 0))],
        out_shape=[jax.ShapeDtypeStruct((2, S, Fh), BF16), jax.ShapeDtypeStruct((S, K), F32),
                   jax.ShapeDtypeStruct((nk, 8, tk), F32)],
        compiler_params=_cparams(("arbitrary", "arbitrary")),
    )(duc, duc, u, cw, wg)


def _pool_counts(i, tm, w):
    t = i * tm + lax.broadcasted_iota(jnp.int32, (tm, 1), 0)
    return jnp.minimum(t + 1, w).astype(F32)


def _pool_fwd(x, gain):
    S, D = x.shape
    tm = _tile(S, (256,))
    gd = D // len(POOL_WINDOWS)
    nh = tm // POOL_HALO

    def body(x_ref, xh_ref, g_ref, d_ref):
        i = pl.program_id(0)

        def norm(v):
            return v * lax.rsqrt(jnp.mean(v * v, axis=-1, keepdims=True) + EPS) * g_ref[...]

        h = norm(x_ref[...])
        hh = norm(xh_ref[...]) * (i > 0).astype(F32)
        ext = jnp.concatenate([hh, h], axis=0)
        for gi, w in enumerate(POOL_WINDOWS):
            sl = slice(gi * gd, (gi + 1) * gd)
            win = ext[:, sl]
            k = 1
            while k < w:
                win = win + pltpu.roll(win, k, 0)
                k *= 2
            mean = win[POOL_HALO:] / _pool_counts(i, tm, w)
            d_ref[:, sl] = (mean - h[:, sl]).astype(BF16)

    return pl.pallas_call(
        body, name="pool_fwd", grid=(S // tm,),
        in_specs=[pl.BlockSpec((tm, D), lambda i: (i, 0)),
                  pl.BlockSpec((POOL_HALO, D), lambda i: (jnp.maximum(i * nh - 1, 0), 0)),
                  pl.BlockSpec((1, D), lambda i: (0, 0))],
        out_specs=pl.BlockSpec((tm, D), lambda i: (i, 0)),
        out_shape=jax.ShapeDtypeStruct((S, D), BF16),
        compiler_params=_cparams(("parallel",)),
    )(x, x, gain)


def _pool_mm_fwd(d, wp, x, scale):
    S, D = d.shape
    G, gd, _ = wp.shape
    tm = _tile(S, (1024, 512))
    o_spec = pl.BlockSpec((tm, gd), lambda p, q, k: (p, q))
    return _mm(
        "pool_mm_fwd", "nn", d, wp, (S // tm, G, 1),
        o_spec, pl.BlockSpec((None, gd, gd), lambda p, q, k: (q, 0, 0)),
        [((S, D), F32, o_spec), ((S, D), F32, o_spec)], (tm, gd),
        extras=[(x, o_spec), (scale, pl.BlockSpec((1, gd), lambda p, q, k: (0, q)))],
        epilogue=lambda acc, xv, sc: (acc, xv + acc * sc))


def _pool_bwd_pre(dx1, ypre, scale):
    S, D = dx1.shape
    tm = _tile(S, (256,))
    n = S // tm

    def body(dx_ref, y_ref, s_ref, dy_ref, ds_ref, acc_ref):
        i = pl.program_id(0)
        dx = dx_ref[...]
        dy_ref[...] = (dx * s_ref[...]).astype(BF16)

        @pl.when(i == 0)
        def _():
            acc_ref[...] = jnp.zeros_like(acc_ref)

        acc_ref[...] += _rows_to_8(dx * y_ref[...])

        @pl.when(i == n - 1)
        def _():
            ds_ref[...] = jnp.sum(acc_ref[...], axis=0, keepdims=True)

    row = pl.BlockSpec((tm, D), lambda i: (i, 0))
    vec = pl.BlockSpec((1, D), lambda i: (0, 0))
    return pl.pallas_call(
        body, name="pool_bwd_pre", grid=(n,),
        in_specs=[row, row, vec], out_specs=[row, vec],
        out_shape=[jax.ShapeDtypeStruct((S, D), BF16), jax.ShapeDtypeStruct((1, D), F32)],
        scratch_shapes=[pltpu.VMEM((8, D), F32)],
        compiler_params=_cparams(("arbitrary",)),
    )(dx1, ypre, scale)


def _pool_mm_bwd_in(dyp, wp):
    S, D = dyp.shape
    G, gd, _ = wp.shape
    tm = _tile(S, (1024, 512))
    spec = pl.BlockSpec((tm, gd), lambda p, q, k: (p, q))
    return _mm(
        "pool_mm_bwd_in", "nt", dyp, wp, (S // tm, G, 1),
        spec, pl.BlockSpec((None, gd, gd), lambda p, q, k: (q, 0, 0)),
        [((S, D), F32, spec)], (tm, gd))


def _pool_mm_bwd_w(d, dyp):
    S, D = d.shape
    G = len(POOL_WINDOWS)
    gd = D // G
    ts = _tile(S, (1024, 512, 256))
    spec = pl.BlockSpec((ts, gd), lambda p, q, k: (k, p))
    return _mm(
        "pool_mm_bwd_w", "tn", d, dyp, (G, 1, S // ts),
        spec, spec,
        [((G, gd, gd), BF16, pl.BlockSpec((None, gd, gd), lambda p, q, k: (p, 0, 0)))], (gd, gd))


def _pool_bwd_post(dd, x, gain, dres):
    S, D = x.shape
    tm = _tile(S, (256,))
    n = S // tm
    gd = D // len(POOL_WINDOWS)
    nh = tm // POOL_HALO

    def body(dd_ref, ddh_ref, x_ref, g_ref, dr_ref, dx_ref, dxb_ref, dg_ref, acc_ref):
        i = pl.program_id(0)
        dd = dd_ref[...]
        halo = ddh_ref[...] * (i < n - 1).astype(F32)
        parts = []
        for gi, w in enumerate(POOL_WINDOWS):
            sl = slice(gi * gd, (gi + 1) * gd)
            ext = jnp.concatenate([dd[:, sl] / _pool_counts(i, tm, w), halo[:, sl] * (1.0 / w)], axis=0)
            k = 1
            while k < w:
                ext = ext + pltpu.roll(ext, tm + POOL_HALO - k, 0)
                k *= 2
            parts.append(ext[:tm] - dd[:, sl])
        dh = jnp.concatenate(parts, axis=1)
        dxn, dgr = _rms_bwd_math(x_ref[...], g_ref[...], dh)
        dx = dr_ref[...] + dxn
        dx_ref[...] = dx
        dxb_ref[...] = dx.astype(BF16)

        @pl.when(i == 0)
        def _():
            acc_ref[...] = jnp.zeros_like(acc_ref)

        acc_ref[...] += _rows_to_8(dgr)

        @pl.when(i == n - 1)
        def _():
            dg_ref[...] = jnp.sum(acc_ref[...], axis=0, keepdims=True)

    row = pl.BlockSpec((tm, D), lambda i: (i, 0))
    vec = pl.BlockSpec((1, D), lambda i: (0, 0))
    nxt = pl.BlockSpec((POOL_HALO, D), lambda i: (jnp.minimum((i + 1) * nh, S // POOL_HALO - 1), 0))
    return pl.pallas_call(
        body, name="pool_bwd_post", grid=(n,),
        in_specs=[row, nxt, row, vec, row],
        out_specs=[row, row, vec],
        out_shape=[jax.ShapeDtypeStruct((S, D), F32), jax.ShapeDtypeStruct((S, D), BF16),
                   jax.ShapeDtypeStruct((1, D), F32)],
        scratch_shapes=[pltpu.VMEM((8, D), F32)],
        compiler_params=_cparams(("arbitrary",)),
    )(dd, dd, x, gain, dres)


def _band_tables():
    i = np.arange(WINDOW)[:, None]
    j = np.arange(2 * WINDOW)[None, :]
    n = np.maximum(WINDOW + i - j, 0)
    max_exact = N_BUCKETS // 2
    nf = np.maximum(n, 1).astype(np.float32)
    large = max_exact + (np.log(nf / max_exact) / np.log(MAX_DISTANCE / max_exact)
                         * (N_BUCKETS - max_exact)).astype(np.int32)
    large = np.minimum(large, N_BUCKETS - 1)
    buckets = np.where(n < max_exact, n, large).astype(np.int32)
    dist = WINDOW + i - j
    in_win = ((dist >= 0) & (dist < WINDOW)).astype(np.int32)
    return buckets, in_win


def _bias_expand(rel_bias):
    H = rel_bias.shape[0]
    buckets, in_win = _band_tables()

    def body(rb_ref, bk_ref, win_ref, o_ref):
        h = pl.program_id(0)
        bk = bk_ref[...]
        acc = jnp.zeros(bk.shape, F32)
        for b in range(N_BUCKETS):
            acc = jnp.where(bk == b, rb_ref[h, b], acc)
        o_ref[...] = jnp.where(win_ref[...] > 0, acc, NEG_INF)

    full = pl.BlockSpec((WINDOW, 2 * WINDOW), lambda h: (0, 0))
    return pl.pallas_call(
        body, name="bias_expand", grid=(H,),
        in_specs=[pl.BlockSpec(memory_space=pltpu.SMEM), full, full],
        out_specs=pl.BlockSpec((None, WINDOW, 2 * WINDOW), lambda h: (h, 0, 0)),
        out_shape=jax.ShapeDtypeStruct((H, WINDOW, 2 * WINDOW), F32),
        compiler_params=_cparams(("parallel",)),
    )(rel_bias, jnp.asarray(buckets), jnp.asarray(in_win))


def _bias_reduce(dbias):
    H = dbias.shape[0]
    buckets, in_win = _band_tables()

    def body(d_ref, bk_ref, win_ref, o_ref):
        bk = jnp.where(win_ref[...] > 0, bk_ref[...], -1)
        d = d_ref[...]
        lane = lax.broadcasted_iota(jnp.int32, (8, LANES), 1)
        out = jnp.zeros((8, LANES), F32)
        for b in range(N_BUCKETS):
            s = jnp.sum(jnp.where(bk == b, d, 0.0))
            out = jnp.where(lane == b, s, out)
        o_ref[...] = out

    full = pl.BlockSpec((WINDOW, 2 * WINDOW), lambda h: (0, 0))
    return pl.pallas_call(
        body, name="bias_reduce", grid=(H,),
        in_specs=[pl.BlockSpec((None, WINDOW, 2 * WINDOW), lambda h: (h, 0, 0)), full, full],
        out_specs=pl.BlockSpec((None, 8, LANES), lambda h: (h, 0, 0)),
        out_shape=jax.ShapeDtypeStruct((H, 8, LANES), F32),
        compiler_params=_cparams(("parallel",)),
    )(dbias, jnp.asarray(buckets), jnp.asarray(in_win))


def _half_rsqrt(t, lo):
    sq = t * t
    s_lo = jnp.sum(jnp.where(lo, sq, 0.0), axis=-1, keepdims=True)
    s_hi = jnp.sum(jnp.where(lo, 0.0, sq), axis=-1, keepdims=True)
    return jnp.where(lo, lax.rsqrt(s_lo * (1.0 / HEAD_DIM) + EPS), lax.rsqrt(s_hi * (1.0 / HEAD_DIM) + EPS))


def _qk_norm_fwd(qkv, qg, kg, n_q, n_kv):
    S, W = qkv.shape
    tm = _tile(S, (256,))
    qw = n_q * HEAD_DIM
    kw = n_kv * HEAD_DIM
    scale = HEAD_DIM ** -0.5

    def body(x_ref, qg_ref, kg_ref, q_ref, k_ref, v_ref):
        lo = lax.broadcasted_iota(jnp.int32, (tm, LANES), 1) < HEAD_DIM
        for t in range(qw // LANES):
            q = x_ref[:, t * LANES:(t + 1) * LANES]
            q_ref[:, t * LANES:(t + 1) * LANES] = (q * _half_rsqrt(q, lo) * qg_ref[...] * scale).astype(BF16)
        for p in range(kw // LANES):
            k = x_ref[:, qw + p * LANES:qw + (p + 1) * LANES]
            kn = k * _half_rsqrt(k, lo) * kg_ref[...]
            v = x_ref[:, qw + kw + p * LANES:qw + kw + (p + 1) * LANES]
            for src, dst in ((kn, k_ref), (v, v_ref)):
                rolled = pltpu.roll(src, HEAD_DIM, 1)
                dst[2 * p] = jnp.where(lo, src, rolled).astype(BF16)
                dst[2 * p + 1] = jnp.where(lo, rolled, src).astype(BF16)

    vec = pl.BlockSpec((1, LANES), lambda i: (0, 0))
    kv = pl.BlockSpec((n_kv, tm, LANES), lambda i: (0, i, 0))
    return pl.pallas_call(
        body, name="qk_norm_fwd", grid=(S // tm,),
        in_specs=[pl.BlockSpec((tm, W), lambda i: (i, 0)), vec, vec],
        out_specs=[pl.BlockSpec((tm, qw), lambda i: (i, 0)), kv, kv],
        out_shape=[jax.ShapeDtypeStruct((S, qw), BF16), jax.ShapeDtypeStruct((n_kv, S, LANES), BF16),
                   jax.ShapeDtypeStruct((n_kv, S, LANES), BF16)],
        compiler_params=_cparams(("parallel",)),
    )(qkv, qg, kg)


def _qk_norm_bwd(qkv, qg, kg, dq, dkc, dkp, dvc, dvp, n_q, n_kv):
    S, W = qkv.shape
    tm = _tile(S, (4 * WINDOW, WINDOW))
    n = S // tm
    per = tm // WINDOW
    qw = n_q * HEAD_DIM
    kw = n_kv * HEAD_DIM
    scale = HEAD_DIM ** -0.5

    def body(x_ref, qg_ref, kg_ref, dq_ref, dkc_ref, dkp_ref, dkn_ref, dvc_ref, dvp_ref, dvn_ref,
             dx_ref, dqg_ref, dkg_ref, accq_ref, acck_ref):
        i = pl.program_id(0)
        lo = lax.broadcasted_iota(jnp.int32, (tm, LANES), 1) < HEAD_DIM
        has_next = (i < n - 1).astype(F32)

        @pl.when(i == 0)
        def _():
            accq_ref[...] = jnp.zeros_like(accq_ref)
            acck_ref[...] = jnp.zeros_like(acck_ref)

        def norm_bwd(t, dy, gain):
            r = _half_rsqrt(t, lo)
            xhat = t * r
            dxhat = dy * gain
            prod = dxhat * xhat
            c_lo = jnp.sum(jnp.where(lo, prod, 0.0), axis=-1, keepdims=True) * (1.0 / HEAD_DIM)
            c_hi = jnp.sum(jnp.where(lo, 0.0, prod), axis=-1, keepdims=True) * (1.0 / HEAD_DIM)
            return r * (dxhat - xhat * jnp.where(lo, c_lo, c_hi)), _rows_to_8(dy * xhat)

        for t in range(qw // LANES):
            sl = slice(t * LANES, (t + 1) * LANES)
            dt, dg = norm_bwd(x_ref[:, sl], dq_ref[:, sl] * scale, qg_ref[...])
            dx_ref[:, sl] = dt.astype(BF16)
            accq_ref[...] += dg

        def pair(cur_ref, prev_ref, next_ref, p):
            folded = []
            for h in (2 * p, 2 * p + 1):
                later = next_ref[h] * has_next
                if per > 1:
                    later = jnp.concatenate([prev_ref[h, WINDOW:, :], later], axis=0)
                tot = cur_ref[h] + later
                folded.append(tot + pltpu.roll(tot, HEAD_DIM, 1))
            return jnp.where(lo, folded[0], folded[1])

        for p in range(kw // LANES):
            sl = slice(qw + p * LANES, qw + (p + 1) * LANES)
            dt, dg = norm_bwd(x_ref[:, sl], pair(dkc_ref, dkp_ref, dkn_ref, p), kg_ref[...])
            dx_ref[:, sl] = dt.astype(BF16)
            acck_ref[...] += dg
            sl = slice(qw + kw + p * LANES, qw + kw + (p + 1) * LANES)
            dx_ref[:, sl] = pair(dvc_ref, dvp_ref, dvn_ref, p).astype(BF16)

        @pl.when(i == n - 1)
        def _():
            for acc_ref, o_ref in ((accq_ref, dqg_ref), (acck_ref, dkg_ref)):
                s = jnp.sum(acc_ref[...], axis=0, keepdims=True)
                o_ref[...] = s + pltpu.roll(s, HEAD_DIM, 1)

    vec = pl.BlockSpec((1, LANES), lambda i: (0, 0))
    cur = pl.BlockSpec((n_kv, tm, LANES), lambda i: (0, i, 0))
    nxt = pl.BlockSpec((n_kv, WINDOW, LANES), lambda i: (0, jnp.minimum((i + 1) * per, S // WINDOW - 1), 0))
    return pl.pallas_call(
        body, name="qk_norm_bwd", grid=(n,),
        in_specs=[pl.BlockSpec((tm, W), lambda i: (i, 0)), vec, vec, pl.BlockSpec((tm, qw), lambda i: (i, 0)),
                  cur, cur, nxt, cur, cur, nxt],
        out_specs=[pl.BlockSpec((tm, W), lambda i: (i, 0)), vec, vec],
        out_shape=[jax.ShapeDtypeStruct((S, W), BF16), jax.ShapeDtypeStruct((1, LANES), F32),
                   jax.ShapeDtypeStruct((1, LANES), F32)],
        scratch_shapes=[pltpu.VMEM((8, LANES), F32), pltpu.VMEM((8, LANES), F32)],
        compiler_params=_cparams(("arbitrary",)),
    )(qkv, qg, kg, dq, dkc, dkp, dkp, dvc, dvp, dvp)


def _attn_specs(n_kv, per_step):
    kvs = per_step if n_kv % per_step == 0 else 1
    q = pl.BlockSpec((WINDOW, kvs * GQA_GROUP * HEAD_DIM), lambda kh, n: (n, kh))
    cur = pl.BlockSpec((kvs, WINDOW, LANES), lambda kh, n: (kh, n, 0))
    prev = pl.BlockSpec((kvs, WINDOW, LANES), lambda kh, n: (kh, jnp.maximum(n - 1, 0), 0))
    bias = pl.BlockSpec((kvs * GQA_GROUP, WINDOW, 2 * WINDOW), lambda kh, n: (kh, 0, 0))
    sink = pl.BlockSpec(memory_space=pltpu.SMEM)
    return kvs, q, cur, prev, bias, sink


def _stack_heads(x_ref, lo, h):
    parts = []
    for g in range(GQA_GROUP):
        c0 = (h * GQA_GROUP // 2 + g // 2) * LANES
        t = x_ref[:, c0:c0 + LANES]
        parts.append(jnp.where(lo if g % 2 == 0 else jnp.logical_not(lo), t, jnp.zeros_like(t)))
    return jnp.concatenate(parts, axis=0)


def _unstack_heads(v, lo):
    return [jnp.where(lo, v[(2 * p) * WINDOW:(2 * p + 1) * WINDOW], v[(2 * p + 1) * WINDOW:(2 * p + 2) * WINDOW])
            for p in range(GQA_GROUP // 2)]


def _attn_probs(qs, kk, bias, s_ref, head0, has_prev):
    rows = GQA_GROUP * WINDOW
    s = lax.dot_general(qs, kk, _DN["nt"], preferred_element_type=F32) + bias.reshape(rows, 2 * WINDOW)
    col = lax.broadcasted_iota(jnp.int32, (rows, 2 * WINDOW), 1)
    s = jnp.where(jnp.logical_or(has_prev, col >= WINDOW), s, NEG_INF)
    sink = jnp.concatenate([jnp.full((WINDOW, 1), s_ref[head0 + g], F32) for g in range(GQA_GROUP)], axis=0)
    m = jnp.maximum(jnp.max(s, axis=-1, keepdims=True), sink)
    p = jnp.exp(s - m)
    es = jnp.exp(sink - m)
    den = jnp.sum(p, axis=-1, keepdims=True) + es
    return p / den, es / den


def _attn_fwd(q, kcat, vcat, biasm, sinks):
    S, qw = q.shape
    n_kv = kcat.shape[0]
    kvs, qs, cur, prev, bias, sink = _attn_specs(n_kv, 4)
    pairs = GQA_GROUP // 2

    def body(q_ref, kc_ref, kp_ref, vc_ref, vp_ref, b_ref, s_ref, o_ref):
        kh, n = pl.program_id(0), pl.program_id(1)
        lo = lax.broadcasted_iota(jnp.int32, (WINDOW, LANES), 1) < HEAD_DIM
        for h in range(kvs):
            kk = jnp.concatenate([kp_ref[h], kc_ref[h]], axis=0)
            vv = jnp.concatenate([vp_ref[h], vc_ref[h]], axis=0)
            pn, _ = _attn_probs(_stack_heads(q_ref, lo, h), kk, b_ref[h * GQA_GROUP:(h + 1) * GQA_GROUP], s_ref,
                                (kh * kvs + h) * GQA_GROUP, n > 0)
            o = jnp.dot(pn.astype(BF16), vv, preferred_element_type=F32)
            for p, t in enumerate(_unstack_heads(o, lo)):
                o_ref[:, (h * pairs + p) * LANES:(h * pairs + p + 1) * LANES] = t.astype(BF16)

    return pl.pallas_call(
        body, name="attn_fwd", grid=(n_kv // kvs, S // WINDOW),
        in_specs=[qs, cur, prev, cur, prev, bias, sink],
        out_specs=qs,
        out_shape=jax.ShapeDtypeStruct((S, qw), BF16),
        compiler_params=_cparams(("parallel", "parallel")),
    )(q, kcat, kcat, vcat, vcat, biasm, sinks)


def _attn_bwd(q, kcat, vcat, biasm, sinks, do):
    S, qw = q.shape
    n_kv = kcat.shape[0]
    H = n_kv * GQA_GROUP
    kvs, qs, cur, prev, bias, sink = _attn_specs(n_kv, ATTN_BWD_KV_PER_STEP)
    pairs = GQA_GROUP // 2

    def body(q_ref, kc_ref, kp_ref, vc_ref, vp_ref, b_ref, s_ref, do_ref,
             dq_ref, dkc_ref, dkp_ref, dvc_ref, dvp_ref, db_ref, ds_ref):
        kh, n = pl.program_id(0), pl.program_id(1)
        lo = lax.broadcasted_iota(jnp.int32, (WINDOW, LANES), 1) < HEAD_DIM

        @pl.when(n == 0)
        def _():
            db_ref[...] = jnp.zeros_like(db_ref)
            ds_ref[...] = jnp.zeros_like(ds_ref)

        for h in range(kvs):
            gs = slice(h * GQA_GROUP, (h + 1) * GQA_GROUP)
            kk = jnp.concatenate([kp_ref[h], kc_ref[h]], axis=0)
            vv = jnp.concatenate([vp_ref[h], vc_ref[h]], axis=0)
            qs_ = _stack_heads(q_ref, lo, h)
            dos = _stack_heads(do_ref, lo, h)
            pn, ps = _attn_probs(qs_, kk, b_ref[gs], s_ref, (kh * kvs + h) * GQA_GROUP, n > 0)
            dp = lax.dot_general(dos, vv, _DN["nt"], preferred_element_type=F32)
            delta = jnp.sum(pn * dp, axis=-1, keepdims=True)
            ds = pn * (dp - delta)
            db_ref[gs] += ds.reshape(GQA_GROUP, WINDOW, 2 * WINDOW)
            ds_ref[h] += (jnp.zeros((GQA_GROUP, LANES), F32)
                          - jnp.sum((ps * delta).reshape(GQA_GROUP, WINDOW, 1), axis=1))
            dsb = ds.astype(BF16)
            dq = jnp.dot(dsb, kk, preferred_element_type=F32)
            for p, t in enumerate(_unstack_heads(dq, lo)):
                dq_ref[:, (h * pairs + p) * LANES:(h * pairs + p + 1) * LANES] = t
            dk = lax.dot_general(dsb, qs_, _DN["tn"], preferred_element_type=F32)
            dv = lax.dot_general(pn.astype(BF16), dos, _DN["tn"], preferred_element_type=F32)
            dkp_ref[h] = dk[:WINDOW]
            dkc_ref[h] = dk[WINDOW:]
            dvp_ref[h] = dv[:WINDOW]
            dvc_ref[h] = dv[WINDOW:]

    part = jax.ShapeDtypeStruct((n_kv, S, LANES), F32)
    return pl.pallas_call(
        body, name="attn_bwd", grid=(n_kv // kvs, S // WINDOW),
        in_specs=[qs, cur, prev, cur, prev, bias, sink, qs],
        out_specs=[qs, cur, cur, cur, cur, bias, pl.BlockSpec((kvs, 8, LANES), lambda kh, n: (kh, 0, 0))],
        out_shape=[jax.ShapeDtypeStruct((S, qw), F32), part, part, part, part,
                   jax.ShapeDtypeStruct((H, WINDOW, 2 * WINDOW), F32),
                   jax.ShapeDtypeStruct((n_kv, 8, LANES), F32)],
        compiler_params=_cparams(("parallel", "arbitrary")),
    )(q, kcat, kcat, vcat, vcat, biasm, sinks, do)


def _adamw(name, w, g, m, v):
    shape = w.shape
    C = shape[-1]
    R = int(np.prod(shape[:-1]))
    tr = R
    if R * C * 4 > (1 << 20):
        tr = _tile(R, tuple(t for t in (512, 256, 128, 64, 32, 16, 8) if t * C * 4 <= (3 << 19)))
    c1 = 1.0 - ADAM_B1 ** ADAM_STEP
    c2 = 1.0 - ADAM_B2 ** ADAM_STEP

    def body(w_ref, g_ref, m_ref, v_ref, d_ref, nm_ref, nv_ref):
        gv = g_ref[...]
        nm = ADAM_B1 * m_ref[...] + (1.0 - ADAM_B1) * gv
        nv = ADAM_B2 * v_ref[...] + (1.0 - ADAM_B2) * (gv * gv)
        d_ref[...] = -ADAM_LR * ((nm / c1) / (jnp.sqrt(nv / c2) + ADAM_EPS) + ADAM_WD * w_ref[...])
        nm_ref[...] = nm
        nv_ref[...] = nv

    spec = pl.BlockSpec((tr, C), lambda i: (i, 0))
    outs = pl.pallas_call(
        body, name=name, grid=(R // tr,),
        in_specs=[spec] * 4, out_specs=[spec] * 3,
        out_shape=[jax.ShapeDtypeStruct((R, C), F32)] * 3,
        compiler_params=_cparams(("parallel",)),
    )(*[t.reshape(R, C) for t in (w, g, m, v)])
    return [o.reshape(shape) for o in outs]


def _place():
    return lax.axis_index("x"), lax.axis_index("y"), lax.axis_index("c")


def _other_chips(x, y):
    return [(1 - x, y), (x, 1 - y), (1 - x, 1 - y)]


def _swap_halves(name, grads):
    T = len(grads)

    def body(*refs):
        ins, outs = refs[:T], refs[T:2 * T]
        send, recv = refs[2 * T:]
        x, y, c = _place()

        def copy(t):
            half = grads[t].shape[1] // 2
            src = ins[t].at[:, pl.ds(pl.multiple_of((1 - c) * half, 16), half), :]
            return pltpu.make_async_remote_copy(
                src_ref=src, dst_ref=outs[t], send_sem=send.at[t], recv_sem=recv.at[t],
                device_id=(x, y, 1 - c), device_id_type=MESH)

        for t in range(T):
            copy(t).start()
        for t in range(T):
            copy(t).wait()

    return pl.pallas_call(
        body, name=name,
        in_specs=[_ANY] * T, out_specs=[_ANY] * T,
        out_shape=[jax.ShapeDtypeStruct((g.shape[0], g.shape[1] // 2, g.shape[2]), g.dtype) for g in grads],
        scratch_shapes=[pltpu.SemaphoreType.DMA((T,)), pltpu.SemaphoreType.DMA((T,))],
        compiler_params=pltpu.CompilerParams(has_side_effects=True),
    )(*grads)


def _add_half(name, g, other, c_arr):
    nb, R, C = g.shape
    half = R // 2
    tr = _tile(half, (256, 128, 64, 32, 16))
    n = half // tr

    def body(c_ref, g_ref, o_ref, s_ref):
        s_ref[...] = (g_ref[...].astype(F32) + o_ref[...].astype(F32)).astype(BF16)

    return pl.pallas_call(
        body, name=name,
        grid_spec=pltpu.PrefetchScalarGridSpec(
            num_scalar_prefetch=1, grid=(nb, n),
            in_specs=[pl.BlockSpec((None, tr, C), lambda b, i, c: (b, c[0] * n + i, 0)),
                      pl.BlockSpec((None, tr, C), lambda b, i, c: (b, i, 0))],
            out_specs=pl.BlockSpec((None, tr, C), lambda b, i, c: (b, i, 0))),
        out_shape=jax.ShapeDtypeStruct((nb, half, C), BF16),
        compiler_params=_cparams(("parallel", "parallel")),
    )(c_arr, g, other)


def _in_hbm(arrays):
    return [pltpu.with_memory_space_constraint(a, pltpu.HBM) for a in arrays]


def _chip_copy(src, dst, send, recv, chip, c):
    return pltpu.make_async_remote_copy(src_ref=src, dst_ref=dst, send_sem=send, recv_sem=recv,
                                        device_id=(chip[0], chip[1], c), device_id_type=MESH)


def _cast_slot(name, w, l, chip_arr, dtype):
    _, R, C = w.shape
    tr = _tile(R, tuple(t for t in (1024, 512, 256, 128, 64, 32, 16) if t * C * 4 <= (1 << 21)))

    def body(chip_ref, w_ref, o_ref):
        o_ref[...] = w_ref[...].astype(dtype)

    return pl.pallas_call(
        body, name=name,
        grid_spec=pltpu.PrefetchScalarGridSpec(
            num_scalar_prefetch=1, grid=(R // tr,),
            in_specs=[pl.BlockSpec((None, tr, C), lambda i, chip: (l, i, 0))],
            out_specs=pl.BlockSpec((None, tr, C), lambda i, chip: (chip[0], i, 0))),
        out_shape=jax.ShapeDtypeStruct((N_CHIPS, R, C), dtype),
        compiler_params=_cparams(("parallel",)),
    )(chip_arr, w)


def _gather_start(name, groups):
    sizes = [len(g) for g in groups]
    flat = [b for g in groups for b in g]
    n, G = len(flat), len(groups)

    def body(*refs):
        bufs = refs[:n]
        sems = refs[n:n + 2 * G]
        token = refs[-1]
        x, y, c = _place()
        chips = _other_chips(x, y)
        me = 2 * x + y
        k = 0
        for l in range(G):
            for t in range(sizes[l]):
                for j in range(3):
                    _chip_copy(bufs[k].at[me], bufs[k].at[me], sems[2 * l].at[3 * t + j], sems[2 * l + 1].at[3 * t + j],
                               chips[j], c).start()
                k += 1
        token[...] = jnp.zeros_like(token)

    sem_shapes = []
    for s in sizes:
        sem_shapes += [pltpu.SemaphoreType.DMA((3 * s,)), pltpu.SemaphoreType.DMA((3 * s,))]
    outs = pl.pallas_call(
        body, name=name,
        out_shape=(*sem_shapes, *[pltpu.HBM(b.shape, b.dtype) for b in flat], jax.ShapeDtypeStruct((8, LANES), F32)),
        in_specs=[_HBM] * n,
        out_specs=(*[_SEM] * (2 * G), *[_HBM] * n, pl.BlockSpec(memory_space=pltpu.VMEM)),
        input_output_aliases={t: 2 * G + t for t in range(n)},
        compiler_params=pltpu.CompilerParams(has_side_effects=_DATAFLOW),
    )(*_in_hbm(flat))
    res, k = [], 2 * G
    for l in range(G):
        res.append((outs[2 * l], outs[2 * l + 1], list(outs[k:k + sizes[l]])))
        k += sizes[l]
    return res, outs[-1]


def _gather_wait(name, send, recv, bufs, after):
    T = len(bufs)

    def body(*refs):
        ins = refs[:T]
        send_ref, recv_ref = refs[T], refs[T + 1]
        x, y, c = _place()
        chips = _other_chips(x, y)
        me = 2 * x + y
        for t in range(T):
            for j in range(3):
                cp = _chip_copy(ins[t].at[me], ins[t].at[2 * chips[j][0] + chips[j][1]], send_ref.at[3 * t + j],
                                recv_ref.at[3 * t + j], chips[j], c)
                cp.wait_send()
                cp.wait_recv()

    return pl.pallas_call(
        body, name=name,
        out_shape=[pltpu.HBM(b.shape, b.dtype) for b in bufs],
        in_specs=[_HBM] * T + [_SEM, _SEM, _ANY],
        out_specs=[_HBM] * T,
        input_output_aliases={t: t for t in range(T)},
        compiler_params=pltpu.CompilerParams(has_side_effects=_DATAFLOW),
    )(*bufs, send, recv, after)


def _scatter_start(name, sums):
    T = len(sums)
    lands = [lax.empty((3,) + s.shape[1:], s.dtype) for s in sums]

    def body(*refs):
        srcs, lnds = refs[:T], refs[T:2 * T]
        send, recv = refs[2 * T], refs[2 * T + 1]
        token = refs[-1]
        x, y, c = _place()
        chips = _other_chips(x, y)
        for t in range(T):
            for j in range(3):
                _chip_copy(srcs[t].at[2 * chips[j][0] + chips[j][1]], lnds[t].at[j], send.at[3 * t + j],
                           recv.at[3 * t + j], chips[j], c).start()
        token[...] = jnp.zeros_like(token)

    both = list(sums) + lands
    outs = pl.pallas_call(
        body, name=name,
        out_shape=(pltpu.SemaphoreType.DMA((3 * T,)), pltpu.SemaphoreType.DMA((3 * T,)),
                   *[pltpu.HBM(b.shape, b.dtype) for b in both], jax.ShapeDtypeStruct((8, LANES), F32)),
        in_specs=[_HBM] * (2 * T),
        out_specs=(_SEM, _SEM, *[_HBM] * (2 * T), pl.BlockSpec(memory_space=pltpu.VMEM)),
        input_output_aliases={t: 2 + t for t in range(2 * T)},
        compiler_params=pltpu.CompilerParams(has_side_effects=_DATAFLOW),
    )(*_in_hbm(both))
    return outs[0], outs[1], list(outs[2:2 + T]), list(outs[2 + T:2 + 2 * T]), outs[-1]


def _scatter_wait(name, send, recv, sums, lands, after):
    T = len(sums)

    def body(*refs):
        srcs, lnds = refs[:T], refs[T:2 * T]
        send_ref, recv_ref = refs[2 * T], refs[2 * T + 1]
        x, y, c = _place()
        chips = _other_chips(x, y)
        for t in range(T):
            for j in range(3):
                cp = _chip_copy(srcs[t].at[2 * chips[j][0] + chips[j][1]], lnds[t].at[j], send_ref.at[3 * t + j],
                                recv_ref.at[3 * t + j], chips[j], c)
                cp.wait_send()
                cp.wait_recv()

    both = list(sums) + list(lands)
    outs = pl.pallas_call(
        body, name=name,
        out_shape=[pltpu.HBM(b.shape, b.dtype) for b in both],
        in_specs=[_HBM] * (2 * T) + [_SEM, _SEM, _ANY],
        out_specs=[_HBM] * (2 * T),
        input_output_aliases={t: t for t in range(2 * T)},
        compiler_params=pltpu.CompilerParams(has_side_effects=_DATAFLOW),
    )(*both, send, recv, after)
    return list(outs[:T]), list(outs[T:])


def _sum_parts(name, sums, land, chip_arr, c_arr):
    _, R2, C = sums.shape
    tr = _tile(R2, (256, 128, 64, 32, 16))

    def body(chip_ref, c_ref, s_ref, l_ref, o_ref):
        acc = s_ref[...].astype(F32)
        for j in range(3):
            acc = acc + l_ref[j].astype(F32)
        o_ref[...] = acc

    return pl.pallas_call(
        body, name=name,
        grid_spec=pltpu.PrefetchScalarGridSpec(
            num_scalar_prefetch=2, grid=(R2 // tr,),
            in_specs=[pl.BlockSpec((None, tr, C), lambda i, chip, c: (chip[0], i, 0)),
                      pl.BlockSpec((3, tr, C), lambda i, chip, c: (0, i, 0))],
            out_specs=pl.BlockSpec((None, tr, C), lambda i, chip, c: (c[0], i, 0))),
        out_shape=jax.ShapeDtypeStruct((2, R2, C), F32),
        compiler_params=_cparams(("parallel",)),
    )(chip_arr, c_arr, sums, land)


def _join_halves(name, bufs):
    T = len(bufs)

    def body(*refs):
        outs = refs[T:2 * T]
        send, recv = refs[2 * T:]
        x, y, c = _place()

        def copy(t, half):
            return pltpu.make_async_remote_copy(
                src_ref=outs[t].at[c], dst_ref=outs[t].at[half], send_sem=send.at[t], recv_sem=recv.at[t],
                device_id=(x, y, 1 - c), device_id_type=MESH)

        for t in range(T):
            copy(t, c).start()
        for t in range(T):
            copy(t, 1 - c).wait_recv()
        for t in range(T):
            copy(t, c).wait_send()

    return pl.pallas_call(
        body, name=name,
        in_specs=[_ANY] * T, out_specs=[_ANY] * T,
        out_shape=[jax.ShapeDtypeStruct(b.shape, b.dtype) for b in bufs],
        input_output_aliases={t: t for t in range(T)},
        scratch_shapes=[pltpu.SemaphoreType.DMA((T,)), pltpu.SemaphoreType.DMA((T,))],
        compiler_params=pltpu.CompilerParams(has_side_effects=True),
    )(*bufs)


def _rs_begin(tag, grads, c_arr):
    got = _swap_halves("rs_swap_" + tag, grads)
    sums = [_add_half("rs_add_%s_%d" % (tag, t), g, o, c_arr) for t, (g, o) in enumerate(zip(grads, got))]
    return _scatter_start("rs_scatter_start_" + tag, sums)


def _rs_finish(tag, state, shapes, chip_arr, c_arr, after=None):
    send, recv, sums, lands, token = state
    sums, lands = _scatter_wait("rs_scatter_wait_" + tag, send, recv, sums, lands, token if after is None else after)
    parts = [_sum_parts("rs_sum_%s_%d" % (tag, t), s, l, chip_arr, c_arr) for t, (s, l) in enumerate(zip(sums, lands))]
    joined = _join_halves("rs_join_" + tag, parts)
    return [j.reshape(sh[1], sh[2]) for j, sh in zip(joined, shapes)]


def _all_reduce_small(v):
    R = v.shape[0]

    def body(v_ref, o_ref, buf, send, recv):
        x, y, c = _place()
        me = 4 * x + 2 * y + c
        buf[me] = v_ref[...]

        def copy(k, slot):
            to = (x ^ ((k >> 2) & 1), y ^ ((k >> 1) & 1), c ^ (k & 1))
            return pltpu.make_async_remote_copy(
                src_ref=v_ref, dst_ref=buf.at[slot], send_sem=send.at[k - 1], recv_sem=recv.at[k - 1],
                device_id=to, device_id_type=MESH)

        for k in range(1, 8):
            copy(k, me).start()
        for k in range(1, 8):
            copy(k, me ^ k).wait_recv()
        for k in range(1, 8):
            copy(k, me).wait_send()
        acc = buf[0]
        for s in range(1, 8):
            acc = acc + buf[s]
        o_ref[...] = acc

    vm = pl.BlockSpec(memory_space=pltpu.VMEM)
    return pl.pallas_call(
        body, name="all_reduce_small",
        in_specs=[vm], out_specs=vm,
        out_shape=jax.ShapeDtypeStruct((R, LANES), F32),
        scratch_shapes=[pltpu.VMEM((8, R, LANES), F32), pltpu.SemaphoreType.DMA((7,)), pltpu.SemaphoreType.DMA((7,))],
        compiler_params=pltpu.CompilerParams(has_side_effects=True, vmem_limit_bytes=VMEM_LIMIT),
    )(v)


def _local_step(x, tgt, small, depth, nb, weight, convw, pscale, on_grads):
    S, D = x.shape
    n_q = D // HEAD_DIM
    n_kv = n_q // GQA_GROUP
    Fh = convw[0].shape[2]

    def dup(gain):
        return jnp.tile(gain, 2).reshape(1, LANES)

    biasm = _bias_expand(small["rel_bias"])
    saved = []
    for i in range(depth):
        j = i // 2
        w = {}
        st = {"x0": x, "w": w}
        gm = small["norm_mix"][i].reshape(1, D)
        if i % 2 == 0:
            h = _rms_fwd("rms_mix_fwd", x, gm)
            w["wqkv"] = weight(i, "wqkv", h)
            qkv = _mm_colblk("qkv_fwd", h, w["wqkv"], F32)
            qh, kcat, vcat = _qk_norm_fwd(qkv, dup(small["attn_q_gain"][j]), dup(small["attn_k_gain"][j]), n_q, n_kv)
            o = _attn_fwd(qh, kcat, vcat, biasm, small["attn_sinks"][j])
            w["wo"] = weight(i, "wo", o)
            x = _mm_rowblk_res("wo_fwd", o, w["wo"], x)
            st.update(h=h, qkv=qkv, qh=qh, kcat=kcat, vcat=vcat, o=o)
        else:
            d = _pool_fwd(x, gm)
            w["wp"] = weight(i, "wp", d)
            ypre, x = _pool_mm_fwd(d, w["wp"], x, pscale[j])
            st.update(d=d, ypre=ypre)
        st["x1"] = x
        h2 = _rms_fwd("rms_ffn_fwd", x, small["norm_ffn"][i].reshape(1, D))
        w["wup"] = weight(i, "wup", h2)
        cb = small["ffn_conv_b"][i].reshape(2, 1, Fh)
        u, uc, a = _up_act_fwd(h2, w["wup"], convw[i], cb)
        w["wdn"] = weight(i, "wdn", a)
        x = _mm_rowblk_res("down_fwd", a, w["wdn"], x)
        st.update(h2=h2, u=u, uc=uc, a=a)
        saved.append(st)

    dx, dxb, loss_part = _loss_head(x, tgt)

    big = [dict() for _ in range(depth)]
    sg = {k: [None] * depth for k in ("norm_mix", "norm_ffn", "conv_w", "conv_b")}
    sg.update({k: [None] * ((depth + 1) // 2) for k in ("q_gain", "k_gain", "sinks")})
    sg["pool_scale"] = [None] * (depth // 2)
    dbias_tot = None
    before = None
    for i in reversed(range(depth)):
        j = i // 2
        st = saved[i]
        w = st["w"]
        duc = _down_act_bwd(dxb, w["wdn"], st["uc"], after=before)
        wdn_grad = _mm_tn_rowblk("down_bwd_w", st["a"], dxb, nb)
        du, dh2, dcwb = _up_bwd_in(duc, st["u"], convw[i], w["wup"])
        dcwb = jnp.transpose(dcwb, (1, 0, 2)).reshape(8, 2 * Fh)
        sg["conv_w"][i] = dcwb[0:3]
        sg["conv_b"][i] = dcwb[3]
        wup_grad = _mm_tn_colblk("up_bwd_w", st["h2"], du, nb, split2=True)
        before = on_grads(i, "ffn", dict(wdn=wdn_grad, wup=wup_grad), dx)
        dx, dxb, dg = _rms_bwd("rms_ffn_bwd", st["x1"], small["norm_ffn"][i].reshape(1, D), dh2, dx, after=before)
        sg["norm_ffn"][i] = dg.reshape(D)
        gm = small["norm_mix"][i].reshape(1, D)
        if i % 2 == 0:
            do = _mm_nt_rowblk("wo_bwd_in", dxb, w["wo"], BF16)
            big[i]["wo"] = _mm_tn_rowblk("wo_bwd_w", st["o"], dxb, nb)
            dq, dkc, dkp, dvc, dvp, dbias, dsink = _attn_bwd(
                st["qh"], st["kcat"], st["vcat"], biasm, small["attn_sinks"][j], do)
            dbias_tot = dbias if dbias_tot is None else dbias_tot + dbias
            sg["sinks"][j] = dsink[:, :, 0].reshape(n_q)
            dqkv, dqg, dkg = _qk_norm_bwd(st["qkv"], dup(small["attn_q_gain"][j]), dup(small["attn_k_gain"][j]),
                                          dq, dkc, dkp, dvc, dvp, n_q, n_kv)
            sg["q_gain"][j] = dqg[0, :HEAD_DIM]
            sg["k_gain"][j] = dkg[0, :HEAD_DIM]
            big[i]["wqkv"] = _mm_tn_colblk("qkv_bwd_w", st["h"], dqkv, nb)
            dh = _mm_nt_colblk("qkv_bwd_in", dqkv, w["wqkv"])
            dx, dxb, dg = _rms_bwd("rms_mix_bwd", st["x0"], gm, dh, dx)
        else:
            dyp, dsc = _pool_bwd_pre(dx, st["ypre"], pscale[j])
            sg["pool_scale"][j] = dsc.reshape(D)
            big[i]["wp"] = _pool_mm_bwd_w(st["d"], dyp)
            dd = _pool_mm_bwd_in(dyp, w["wp"])
            dx, dxb, dg = _pool_bwd_post(dd, st["x0"], gm, dx)
        sg["norm_mix"][i] = dg.reshape(D)
        before = on_grads(i, "mix", big[i], dx)
    sg["rel_bias"] = _bias_reduce(dbias_tot)[:, 0, :N_BUCKETS]
    return loss_part, dx, sg


_SMALL_ORDER = ("norm_mix", "norm_ffn", "rel_bias", "q_gain", "k_gain", "sinks", "conv_b", "conv_w", "pool_scale")


def kernel(x, norm_mix, norm_ffn, rel_bias, attn_w_qkv, attn_q_gain, attn_k_gain, attn_sinks, attn_w_o, pool_w, pool_scale, ffn_w_up, ffn_conv_w, ffn_conv_b, ffn_w_down, loss_target, m_norm_mix, m_norm_ffn, m_rel_bias, m_attn_w_qkv, m_attn_q_gain, m_attn_k_gain, m_attn_sinks, m_attn_w_o, m_pool_w, m_pool_scale, m_ffn_w_up, m_ffn_conv_w, m_ffn_conv_b, m_ffn_w_down, v_norm_mix, v_norm_ffn, v_rel_bias, v_attn_w_qkv, v_attn_q_gain, v_attn_k_gain, v_attn_sinks, v_attn_w_o, v_pool_w, v_pool_scale, v_ffn_w_up, v_ffn_conv_w, v_ffn_conv_b, v_ffn_w_down):
    _, S, D = x.shape
    depth = ffn_w_up.shape[0]
    n_attn, n_pool = attn_w_qkv.shape[0], pool_w.shape[0]
    nb = N_CHIPS
    nc_up = ffn_w_up.shape[2]
    Fh = nc_up * nb // 2
    cx, cy, cc = _place()
    chip = 2 * cx + cy
    c_arr = jnp.reshape(cc, (1,)).astype(jnp.int32)
    chip_arr = jnp.reshape(chip, (1,)).astype(jnp.int32)

    pool_w3 = pool_w.reshape(n_pool, -1, pool_w.shape[-1])
    order = [("convw", ffn_conv_w.reshape(1, -1, nc_up), 0, F32), ("pscale", pool_scale[None], 0, F32)]
    for i in range(depth):
        j = i // 2
        mixer = ([("wqkv", attn_w_qkv, j), ("wo", attn_w_o, j)] if i % 2 == 0 else [("wp", pool_w3, j)])
        order += [((i, k), w, l, BF16) for k, w, l in mixer + [("wup", ffn_w_up, i), ("wdn", ffn_w_down, i)]]
    first = 4
    started, tokens = [], []
    for part, lo, hi in (("first", 0, first), ("rest", first, len(order))):
        slots = [_cast_slot("slot_%d" % n, w, l, chip_arr, dt) for n, (_, w, l, dt) in enumerate(order[lo:hi], lo)]
        st, token = _gather_start("gather_start_" + part, [[b] for b in slots])
        started += st
        tokens.append(token)
    index = {key: n for n, (key, _, _, _) in enumerate(order)}

    def gathered(key, after):
        n = index[key]
        send, recv, bufs = started[n]
        return _gather_wait("gather_wait_%d" % n, send, recv, bufs, tokens[1] if n == 2 else after)[0]

    convw_g = gathered("convw", tokens[0]).reshape(nb, depth, 3, nc_up)
    pscale_g = gathered("pscale", tokens[0])
    convw = [jnp.transpose(convw_g[:, i], (1, 0, 2)).reshape(3, 2, Fh).transpose(1, 0, 2) for i in range(depth)]
    pscale = [pscale_g[:, j].reshape(1, D) for j in range(n_pool)]
    small = dict(norm_mix=norm_mix, norm_ffn=norm_ffn, rel_bias=rel_bias, attn_q_gain=attn_q_gain,
                 attn_k_gain=attn_k_gain, attn_sinks=attn_sinks, ffn_conv_b=ffn_conv_b)

    G, gd = pool_w.shape[1], pool_w.shape[3]

    def weight(i, name, after):
        w = gathered((i, name), after)
        if name == "wp":
            w = jnp.transpose(w.reshape(nb, G, gd // nb, gd), (1, 0, 2, 3)).reshape(G, gd, gd)
        return w

    red = {k: [None] * (n_attn if k in ("wqkv", "wo") else n_pool if k == "wp" else depth)
           for k in ("wqkv", "wo", "wp", "wup", "wdn")}
    pending = []

    def finish(after):
        i, tag, names, shapes, state = pending.pop()
        outs = _rs_finish("%d%s" % (i, tag), state, shapes, chip_arr, c_arr, after)
        for k, o in zip(names, outs):
            red[k][i // 2 if k in ("wqkv", "wo", "wp") else i] = o

    def on_grads(i, tag, grads, dx_i):
        if pending:
            finish(dx_i)
        names = sorted(grads)
        flat = []
        for k in names:
            g = grads[k]
            if k == "wp":
                g = jnp.transpose(g.reshape(G, nb, gd // nb, gd), (1, 0, 2, 3))
            flat.append(g.reshape(nb, -1, g.shape[-1]))
        state = _rs_begin("%d%s" % (i, tag), flat, c_arr)
        pending.append((i, tag, names, [f.shape for f in flat], state))
        return state[-1]

    loss_part, dx, sg = _local_step(x[0], loss_target[0], small, depth, nb, weight, convw, pscale, on_grads)
    finish(None)
    loss = lax.psum(jnp.sum(loss_part), ("x", "y", "c"))

    g_wqkv = jnp.stack(red["wqkv"])
    g_wo = jnp.stack(red["wo"])
    g_wp = jnp.stack(red["wp"]).reshape(pool_w.shape)
    g_wup = jnp.stack(red["wup"])
    g_wdn = jnp.stack(red["wdn"])

    parts = [jnp.stack(sg[k]) if isinstance(sg[k], list) else sg[k] for k in _SMALL_ORDER]
    sizes = [int(np.prod(p.shape)) for p in parts]
    total = sum(sizes)
    rows = -(-total // (8 * LANES)) * 8
    packed = jnp.concatenate([p.reshape(-1) for p in parts] + [jnp.zeros((rows * LANES - total,), F32)])
    summed = _all_reduce_small(packed.reshape(rows, LANES)).reshape(-1)
    sm, off = {}, 0
    for k, p, n in zip(_SMALL_ORDER, parts, sizes):
        sm[k] = summed[off:off + n].reshape(p.shape)
        off += n
    g_convw = lax.dynamic_slice_in_dim(sm["conv_w"], chip * nc_up, nc_up, axis=2)
    pc = pool_scale.shape[1]
    g_pscale = lax.dynamic_slice_in_dim(sm["pool_scale"], chip * pc, pc, axis=1)

    grads = [sm["norm_mix"], sm["norm_ffn"], sm["rel_bias"], g_wqkv, sm["q_gain"], sm["k_gain"], sm["sinks"], g_wo,
             g_wp, g_pscale, g_wup, g_convw, sm["conv_b"], g_wdn]
    ws = [norm_mix, norm_ffn, rel_bias, attn_w_qkv, attn_q_gain, attn_k_gain, attn_sinks, attn_w_o, pool_w, pool_scale,
          ffn_w_up, ffn_conv_w, ffn_conv_b, ffn_w_down]
    ms = [m_norm_mix, m_norm_ffn, m_rel_bias, m_attn_w_qkv, m_attn_q_gain, m_attn_k_gain, m_attn_sinks, m_attn_w_o,
          m_pool_w, m_pool_scale, m_ffn_w_up, m_ffn_conv_w, m_ffn_conv_b, m_ffn_w_down]
    vs = [v_norm_mix, v_norm_ffn, v_rel_bias, v_attn_w_qkv, v_attn_q_gain, v_attn_k_gain, v_attn_sinks, v_attn_w_o,
          v_pool_w, v_pool_scale, v_ffn_w_up, v_ffn_conv_w, v_ffn_conv_b, v_ffn_w_down]
    deltas, new_m, new_v = [], [], []
    for idx, (w, g, m, v) in enumerate(zip(ws, grads, ms, vs)):
        d, nm, nv = _adamw("adamw_%d" % idx, w, g, m, v)
        deltas.append(d), new_m.append(nm), new_v.append(nv)
    return (loss, dx.reshape(1, S, D), *grads, *deltas, *new_m, *new_v)
```

```python
import functools

import numpy as np
import jax
import jax.numpy as jnp
from jax import lax
from jax.experimental import pallas as pl
from jax.experimental.pallas import tpu as pltpu

F32 = jnp.float32
BF16 = jnp.bfloat16
MESH = pl.DeviceIdType.MESH
_ANY = pl.BlockSpec(memory_space=pl.ANY)
_HBM = pl.BlockSpec(memory_space=pltpu.HBM)
_SEM = pl.BlockSpec(memory_space=pltpu.SEMAPHORE)
_DATAFLOW = pltpu.SideEffectType.DATAFLOW_SIDE_EFFECTING

N_CHIPS = 4
HEAD_DIM = 64
GQA_GROUP = 8
WINDOW = 128
N_BUCKETS = 32
MAX_DISTANCE = 128
POOL_WINDOWS = (2, 4, 8, 16)
POOL_HALO = 16
ATTN_BWD_KV_PER_STEP = 4
EPS = 1e-6
NEG_INF = -1e30
LANES = 128
VMEM_LIMIT = 56 * 1024 * 1024

ADAM_LR = 0.001
ADAM_B1 = 0.9
ADAM_B2 = 0.999
ADAM_EPS = 1e-08
ADAM_WD = 0.01
ADAM_STEP = 10


def _tile(n, prefs):
    for p in prefs:
        if p <= n and n % p == 0:
            return p
    return n


def _cparams(sem):
    return pltpu.CompilerParams(dimension_semantics=sem, vmem_limit_bytes=VMEM_LIMIT)


_DN = {
    "nn": (((1,), (0,)), ((), ())),
    "nt": (((1,), (1,)), ((), ())),
    "tn": (((0,), (0,)), ((), ())),
}


def _mm(name, kind, a, b, grid, a_spec, b_spec, outs, acc_shape, extras=(), epilogue=None, after=None):
    nk = grid[2]
    n_ex, n_out = len(extras), len(outs)
    order = list(after or [])

    def body(a_ref, b_ref, *rest):
        ex_refs = rest[:n_ex]
        rest = rest[n_ex + len(order):]
        out_refs = rest[:n_out]
        acc_ref = rest[n_out] if nk > 1 else None
        part = lax.dot_general(a_ref[...], b_ref[...], _DN[kind], preferred_element_type=F32)

        def finish(val):
            vals = epilogue(val, *[r[...] for r in ex_refs]) if epilogue else (val,)
            for r, v in zip(out_refs, vals):
                r[...] = v.astype(r.dtype)

        if nk == 1:
            finish(part)
        else:
            k = pl.program_id(2)

            @pl.when(k == 0)
            def _():
                acc_ref[...] = part

            @pl.when(k > 0)
            def _():
                acc_ref[...] += part

            @pl.when(k == nk - 1)
            def _():
                finish(acc_ref[...])

    res = pl.pallas_call(
        body,
        name=name,
        grid=grid,
        in_specs=[a_spec, b_spec] + [s for _, s in extras] + [_ANY] * len(order),
        out_specs=[s for _, _, s in outs],
        out_shape=[jax.ShapeDtypeStruct(sh, dt) for sh, dt, _ in outs],
        scratch_shapes=[pltpu.VMEM(acc_shape, F32)] if nk > 1 else [],
        compiler_params=_cparams(("parallel", "parallel", "arbitrary")),
    )(a, b, *[e for e, _ in extras], *order)
    return res if n_out > 1 else res[0]


def _mm_colblk(name, a, wg, out_dtype):
    S, K = a.shape
    nb, _, nc = wg.shape
    tm = _tile(S, (1024, 512))
    tn = _tile(nc, (1408, 1024, 640, 512, 256, 128))
    npb = nc // tn
    return _mm(
        name, "nn", a, wg, (nb * npb, S // tm, 1),
        pl.BlockSpec((tm, K), lambda p, q, k: (q, 0)),
        pl.BlockSpec((None, K, tn), lambda p, q, k: (p // npb, 0, p % npb)),
        [((S, nb * nc), out_dtype, pl.BlockSpec((tm, tn), lambda p, q, k: (q, p)))], (tm, tn))


def _mm_rowblk_res(name, a, wg, res):
    S = a.shape[0]
    nb, kc, N = wg.shape
    tm = _tile(S, (1024, 512))
    ktot = nb * kc

    def vmem(t):
        return 2 * 2 * (tm * ktot + ktot * t) + 4 * 4 * tm * t

    tn = _tile(N, tuple(t for t in (2048, 1024, 512, 256) if vmem(t) <= (44 << 20)))

    def body(a_ref, b_ref, r_ref, o_ref):
        acc = r_ref[...]
        for r in range(nb):
            acc = acc + jnp.dot(a_ref[:, r * kc:(r + 1) * kc], b_ref[r], preferred_element_type=F32)
        o_ref[...] = acc

    o_spec = pl.BlockSpec((tm, tn), lambda i, j: (i, j))
    return pl.pallas_call(
        body, name=name, grid=(S // tm, N // tn),
        in_specs=[pl.BlockSpec((tm, nb * kc), lambda i, j: (i, 0)),
                  pl.BlockSpec((nb, kc, tn), lambda i, j: (0, 0, j)), o_spec],
        out_specs=o_spec,
        out_shape=jax.ShapeDtypeStruct((S, N), F32),
        compiler_params=_cparams(("parallel", "parallel")),
    )(a, wg, res)


def _mm_nt_rowblk(name, g, wg, out_dtype, after=None):
    S, N = g.shape
    nb, kc, _ = wg.shape
    tm = _tile(S, (1024, 512))
    tn = _tile(kc, (1408, 512, 256, 128))
    kpb = kc // tn
    return _mm(
        name, "nt", g, wg, (nb * kpb, S // tm, 1),
        pl.BlockSpec((tm, N), lambda p, q, k: (q, 0)),
        pl.BlockSpec((None, tn, N), lambda p, q, k: (p // kpb, p % kpb, 0)),
        [((S, nb * kc), out_dtype, pl.BlockSpec((tm, tn), lambda p, q, k: (q, p)))], (tm, tn), after=after)


def _mm_nt_colblk(name, g, wg):
    S = g.shape[0]
    nb, K, nc = wg.shape
    tm = _tile(S, (512,))

    def body(g_ref, b_ref, o_ref):
        acc = lax.dot_general(g_ref[:, 0:nc], b_ref[0], _DN["nt"], preferred_element_type=F32)
        for r in range(1, nb):
            acc = acc + lax.dot_general(g_ref[:, r * nc:(r + 1) * nc], b_ref[r], _DN["nt"],
                                        preferred_element_type=F32)
        o_ref[...] = acc

    return pl.pallas_call(
        body, name=name, grid=(S // tm,),
        in_specs=[pl.BlockSpec((tm, nb * nc), lambda i: (i, 0)), pl.BlockSpec((nb, K, nc), lambda i: (0, 0, 0))],
        out_specs=pl.BlockSpec((tm, K), lambda i: (i, 0)),
        out_shape=jax.ShapeDtypeStruct((S, K), F32),
        compiler_params=_cparams(("parallel",)),
    )(g, wg)


def _mm_tn_colblk(name, a, g, nb, split2=False):
    S, K = a.shape
    ntot = g.shape[-1] * (2 if split2 else 1)
    nc = ntot // nb
    ti = _tile(K, (1024,))
    tn = _tile(nc, (1408, 640, 512, 256, 128))
    ts = _tile(S, (2048, 1024, 512, 256))
    npb = nc // tn
    half = nb * npb // 2
    if split2:
        b_spec = pl.BlockSpec((None, ts, tn), lambda p, q, k: (q // half, k, q % half))
    else:
        b_spec = pl.BlockSpec((ts, tn), lambda p, q, k: (k, q))
    return _mm(
        name, "tn", a, g, (K // ti, nb * npb, S // ts),
        pl.BlockSpec((ts, ti), lambda p, q, k: (k, p)),
        b_spec,
        [((nb, K, nc), BF16, pl.BlockSpec((None, ti, tn), lambda p, q, k: (q // npb, p, q % npb)))], (ti, tn))


def _mm_tn_rowblk(name, a, g, nb):
    S, ktot = a.shape
    N = g.shape[1]
    kc = ktot // nb
    ti = _tile(kc, (1408, 512, 256, 128))
    tj = _tile(N, (1024,))
    ts = _tile(S, (2048, 1024, 512, 256))
    ipb = kc // ti
    return _mm(
        name, "tn", a, g, (nb * ipb, N // tj, S // ts),
        pl.BlockSpec((ts, ti), lambda p, q, k: (k, p)),
        pl.BlockSpec((ts, tj), lambda p, q, k: (k, q)),
        [((nb, kc, N), BF16, pl.BlockSpec((None, ti, tj), lambda p, q, k: (p // ipb, p % ipb, q)))], (ti, tj))


def _rms_fwd(name, x, gain):
    S, D = x.shape
    tm = _tile(S, (512,))

    def body(x_ref, g_ref, o_ref):
        xv = x_ref[...]
        r = lax.rsqrt(jnp.mean(xv * xv, axis=-1, keepdims=True) + EPS)
        o_ref[...] = (xv * r * g_ref[...]).astype(o_ref.dtype)

    return pl.pallas_call(
        body, name=name, grid=(S // tm,),
        in_specs=[pl.BlockSpec((tm, D), lambda i: (i, 0)), pl.BlockSpec((1, D), lambda i: (0, 0))],
        out_specs=pl.BlockSpec((tm, D), lambda i: (i, 0)),
        out_shape=jax.ShapeDtypeStruct((S, D), BF16),
        compiler_params=_cparams(("parallel",)),
    )(x, gain)


def _rms_bwd_math(xv, gain, dh):
    r = lax.rsqrt(jnp.mean(xv * xv, axis=-1, keepdims=True) + EPS)
    xhat = xv * r
    dxhat = dh * gain
    c = jnp.mean(dxhat * xhat, axis=-1, keepdims=True)
    return r * (dxhat - xhat * c), dh * xhat


def _rows_to_8(v):
    tm, C = v.shape
    return jnp.sum(v.reshape(tm // 8, 8, C), axis=0)


def _rms_bwd(name, x, gain, dh, dres, after=None):
    S, D = x.shape
    tm = _tile(S, (256,))
    n = S // tm
    order = list(after or [])

    def body(x_ref, g_ref, dh_ref, dr_ref, *rest):
        dx_ref, dxb_ref, dg_ref, acc_ref = rest[len(order):]
        i = pl.program_id(0)
        dxn, dgr = _rms_bwd_math(x_ref[...], g_ref[...], dh_ref[...])
        dx = dr_ref[...] + dxn
        dx_ref[...] = dx
        dxb_ref[...] = dx.astype(BF16)

        @pl.when(i == 0)
        def _():
            acc_ref[...] = jnp.zeros_like(acc_ref)

        acc_ref[...] += _rows_to_8(dgr)

        @pl.when(i == n - 1)
        def _():
            dg_ref[...] = jnp.sum(acc_ref[...], axis=0, keepdims=True)

    row = pl.BlockSpec((tm, D), lambda i: (i, 0))
    vec = pl.BlockSpec((1, D), lambda i: (0, 0))
    return pl.pallas_call(
        body, name=name, grid=(n,),
        in_specs=[row, vec, row, row] + [_ANY] * len(order),
        out_specs=[row, row, vec],
        out_shape=[jax.ShapeDtypeStruct((S, D), F32), jax.ShapeDtypeStruct((S, D), BF16),
                   jax.ShapeDtypeStruct((1, D), F32)],
        scratch_shapes=[pltpu.VMEM((8, D), F32)],
        compiler_params=_cparams(("arbitrary",)),
    )(x, gain, dh, dres, *order)


def _loss_head(y, tgt):
    S, D = y.shape
    tm = _tile(S, (256,))
    n = S // tm

    def body(y_ref, t_ref, dy_ref, dyb_ref, l_ref):
        i = pl.program_id(0)
        e = y_ref[...] - t_ref[...]
        dy = e * (1.0 / D)
        dy_ref[...] = dy
        dyb_ref[...] = dy.astype(BF16)

        @pl.when(i == 0)
        def _():
            l_ref[...] = jnp.zeros_like(l_ref)

        sq = _rows_to_8(e * e)
        part = sq[:, 0:LANES]
        for t in range(1, D // LANES):
            part = part + sq[:, t * LANES:(t + 1) * LANES]
        l_ref[...] += part * (0.5 / D)

    row = pl.BlockSpec((tm, D), lambda i: (i, 0))
    return pl.pallas_call(
        body, name="loss_head", grid=(n,),
        in_specs=[row, row],
        out_specs=[row, row, pl.BlockSpec((8, LANES), lambda i: (0, 0))],
        out_shape=[jax.ShapeDtypeStruct((S, D), F32), jax.ShapeDtypeStruct((S, D), BF16),
                   jax.ShapeDtypeStruct((8, LANES), F32)],
        compiler_params=_cparams(("arbitrary",)),
    )(y, tgt)


def _up_act_fwd(h2, wg, cw, cb):
    S, K = h2.shape
    nb, _, nc = wg.shape
    Fh = nb * nc // 2
    tm = _tile(S, (2048, 1024, 512))
    tn = _tile(nc, (256, 128))
    sub = _tile(tm, (512,))
    ch = _tile(sub, (32,))
    npb = nc // tn
    hb = nb // 2
    ncol = Fh // tn

    def body(a_ref, bg_ref, bv_ref, cw_ref, cb_ref, u_ref, uc_ref, o_ref, buf_ref, halo_ref):
        j = pl.program_id(1)

        @pl.when(pl.program_id(0) == 0)
        def _():
            halo_ref[j] = jnp.zeros((2, 8, tn), F32)

        buf_ref[:, 0:8, :] = halo_ref[j]
        cws, cbs = [cw_ref[0], cw_ref[1]], [cb_ref[0], cb_ref[1]]
        for r in range(tm // sub):
            for s, b_ref in enumerate((bg_ref, bv_ref)):
                buf_ref[s, 8 + r * sub:8 + (r + 1) * sub, :] = jnp.dot(
                    a_ref[r * sub:(r + 1) * sub, :], b_ref[...], preferred_element_type=F32)
            for c in range(sub // ch):
                base = r * sub + c * ch
                ucs = []
                for s in range(2):
                    ext = buf_ref[s, base:base + ch + 8, :]
                    us = ext[8:]
                    uc = (cws[s][0:1] * pltpu.roll(ext, 2, 0)[8:] + cws[s][1:2] * pltpu.roll(ext, 1, 0)[8:]
                          + cws[s][2:3] * us + cbs[s])
                    u_ref[s, base:base + ch, :] = us.astype(BF16)
                    uc_ref[s, base:base + ch, :] = uc.astype(BF16)
                    ucs.append(uc)
                gate, val = ucs
                o_ref[base:base + ch, :] = (gate * (1.0 / (1.0 + jnp.exp(-gate))) * val).astype(BF16)
        halo_ref[j] = buf_ref[:, tm:tm + 8, :]

    both = pl.BlockSpec((2, tm, tn), lambda i, j: (0, i, j))
    return pl.pallas_call(
        body, name="up_act_fwd", grid=(S // tm, ncol),
        in_specs=[pl.BlockSpec((tm, K), lambda i, j: (i, 0)),
                  pl.BlockSpec((None, K, tn), lambda i, j: (j // npb, 0, j % npb)),
                  pl.BlockSpec((None, K, tn), lambda i, j: (j // npb + hb, 0, j % npb)),
                  pl.BlockSpec((2, 3, tn), lambda i, j: (0, 0, j)),
                  pl.BlockSpec((2, 1, tn), lambda i, j: (0, 0, j))],
        out_specs=[both, both, pl.BlockSpec((tm, tn), lambda i, j: (i, j))],
        out_shape=[jax.ShapeDtypeStruct((2, S, Fh), BF16), jax.ShapeDtypeStruct((2, S, Fh), BF16),
                   jax.ShapeDtypeStruct((S, Fh), BF16)],
        scratch_shapes=[pltpu.VMEM((2, tm + 8, tn), F32), pltpu.VMEM((ncol, 2, 8, tn), F32)],
        compiler_params=_cparams(("arbitrary", "arbitrary")),
    )(h2, wg, wg, cw, cb)


def _down_act_bwd(dxb, wg, uc, after=None):
    S, D = dxb.shape
    nb, kc, _ = wg.shape
    Fh = nb * kc
    tm = _tile(S, (1024, 512))
    sub = _tile(tm, (256,))
    ch = _tile(sub, (32,))
    t128 = kc // LANES
    tn = 2 * LANES
    order = list(after or [])

    def body(dx_ref, b0_ref, b1_ref, uc_ref, *rest):
        duc_ref, bcat_ref, da_ref = rest[len(order):]
        bcat_ref[0:LANES, :] = b0_ref[...]
        bcat_ref[LANES:tn, :] = b1_ref[...]
        for r in range(tm // sub):
            rows = slice(r * sub, (r + 1) * sub)
            da_ref[rows, :] = lax.dot_general(dx_ref[rows, :], bcat_ref[...], _DN["nt"], preferred_element_type=F32)
            for c in range(sub // ch):
                cs = slice(r * sub + c * ch, r * sub + (c + 1) * ch)
                da = da_ref[cs, :]
                gate, val = uc_ref[0, cs, :].astype(F32), uc_ref[1, cs, :].astype(F32)
                sig = 1.0 / (1.0 + jnp.exp(-gate))
                duc_ref[0, cs, :] = (da * val * (sig * (1.0 + gate * (1.0 - sig)))).astype(BF16)
                duc_ref[1, cs, :] = (da * (gate * sig)).astype(BF16)

    both = pl.BlockSpec((2, tm, tn), lambda i, j: (0, i, j))
    return pl.pallas_call(
        body, name="down_act_bwd", grid=(S // tm, Fh // tn),
        in_specs=[pl.BlockSpec((tm, D), lambda i, j: (i, 0)),
                  pl.BlockSpec((None, LANES, D), lambda i, j: ((2 * j) // t128, (2 * j) % t128, 0)),
                  pl.BlockSpec((None, LANES, D), lambda i, j: ((2 * j + 1) // t128, (2 * j + 1) % t128, 0)),
                  both] + [_ANY] * len(order),
        out_specs=both,
        out_shape=jax.ShapeDtypeStruct((2, S, Fh), BF16),
        scratch_shapes=[pltpu.VMEM((tn, D), BF16), pltpu.VMEM((tm, tn), F32)],
        compiler_params=_cparams(("parallel", "parallel")),
    )(dxb, wg, wg, uc, *order)


def _up_bwd_in(duc, u, cw, wg):
    _, S, Fh = duc.shape
    nb, K, nc = wg.shape
    tm = _tile(S, (1024, 512))
    tk = _tile(nc, (1408, 640, 512, 256, 128))
    sub = _tile(tm, (256,))
    ch = _tile(sub, (64,))
    hr = 16
    npb = nc // tk
    nk = nb * npb
    half = nk // 2
    n, nh = S // tm, tm // hr

    def body(d_ref, h_ref, u_ref, cw_ref, b_ref, du_ref, o_ref, cg_ref):
        i, k = pl.program_id(0), pl.program_id(1)
        keep = (i < n - 1).astype(F32)

        @pl.when(i == 0)
        def _():
            cg_ref[k] = jnp.zeros((8, tk), F32)

        @pl.when(k == 0)
        def _():
            o_ref[...] = jnp.zeros_like(o_ref)

        for r in range(tm // sub):
            for l in range(tk // LANES):
                ls = slice(l * LANES, (l + 1) * LANES)
                w = cw_ref[:, ls]
                sums = [jnp.zeros((8, LANES), F32) for _ in range(4)]
                for c in range(sub // ch):
                    base = r * sub + c * ch
                    if base + ch + hr <= tm:
                        ext = d_ref[base:base + ch + hr, ls].astype(F32)
                    else:
                        ext = jnp.concatenate([d_ref[base:base + ch, ls].astype(F32),
                                               h_ref[:, ls].astype(F32) * keep], axis=0)
                    d = ext[:ch]
                    d1 = pltpu.roll(ext, ch + hr - 1, 0)[:ch]
                    d2 = pltpu.roll(ext, ch + hr - 2, 0)[:ch]
                    du_ref[base:base + ch, ls] = (w[2:3] * d + w[1:2] * d1 + w[0:1] * d2).astype(BF16)
                    uv = u_ref[base:base + ch, ls].astype(F32)
                    for q, v in enumerate((d2 * uv, d1 * uv, d * uv, d)):
                        sums[q] = sums[q] + _rows_to_8(v)
                for q in range(4):
                    cg_ref[k, q:q + 1, ls] += jnp.sum(sums[q], axis=0, keepdims=True)
            rows = slice(r * sub, (r + 1) * sub)
            o_ref[rows, :] += lax.dot_general(du_ref[rows, :], b_ref[...], _DN["nt"], preferred_element_type=F32)

    blk = pl.BlockSpec((None, tm, tk), lambda i, k: (k // half, i, k % half))
    return pl.pallas_call(
        body, name="up_bwd_in", grid=(n, nk),
        in_specs=[blk,
                  pl.BlockSpec((None, hr, tk), lambda i, k: (k // half, jnp.minimum((i + 1) * nh, S // hr - 1), k % half)),
                  blk,
                  pl.BlockSpec((None, 3, tk), lambda i, k: (k // half, 0, k % half)),
                  pl.BlockSpec((None, K, tk), lambda i, k: (k // npb, 0, k % npb))],
        out_specs=[blk, pl.BlockSpec((tm, K), lambda i, k: (i, 0)), pl.BlockSpec((nk, 8, tk), lambda i, k: (0, 0, 0))],
        out_shape=[jax.ShapeDtypeStruct((2, S, Fh), BF16), jax.ShapeDtypeStruct((S, K), F32),
                   jax.ShapeDtypeStruct((nk, 8, tk), F32)],
        compiler_params=_cparams(("arbitrary", "arbitrary")),
    )(duc, duc, u, cw, wg)


def _pool_counts(i, tm, w):
    t = i * tm + lax.broadcasted_iota(jnp.int32, (tm, 1), 0)
    return jnp.minimum(t + 1, w).astype(F32)


def _pool_fwd(x, gain):
    S, D = x.shape
    tm = _tile(S, (256,))
    gd = D // len(POOL_WINDOWS)
    nh = tm // POOL_HALO

    def body(x_ref, xh_ref, g_ref, d_ref):
        i = pl.program_id(0)

        def norm(v):
            return v * lax.rsqrt(jnp.mean(v * v, axis=-1, keepdims=True) + EPS) * g_ref[...]

        h = norm(x_ref[...])
        hh = norm(xh_ref[...]) * (i > 0).astype(F32)
        ext = jnp.concatenate([hh, h], axis=0)
        for gi, w in enumerate(POOL_WINDOWS):
            sl = slice(gi * gd, (gi + 1) * gd)
            win = ext[:, sl]
            k = 1
            while k < w:
                win = win + pltpu.roll(win, k, 0)
                k *= 2
            mean = win[POOL_HALO:] / _pool_counts(i, tm, w)
            d_ref[:, sl] = (mean - h[:, sl]).astype(BF16)

    return pl.pallas_call(
        body, name="pool_fwd", grid=(S // tm,),
        in_specs=[pl.BlockSpec((tm, D), lambda i: (i, 0)),
                  pl.BlockSpec((POOL_HALO, D), lambda i: (jnp.maximum(i * nh - 1, 0), 0)),
                  pl.BlockSpec((1, D), lambda i: (0, 0))],
        out_specs=pl.BlockSpec((tm, D), lambda i: (i, 0)),
        out_shape=jax.ShapeDtypeStruct((S, D), BF16),
        compiler_params=_cparams(("parallel",)),
    )(x, x, gain)


def _pool_mm_fwd(d, wp, x, scale):
    S, D = d.shape
    G, gd, _ = wp.shape
    tm = _tile(S, (1024, 512))
    o_spec = pl.BlockSpec((tm, gd), lambda p, q, k: (p, q))
    return _mm(
        "pool_mm_fwd", "nn", d, wp, (S // tm, G, 1),
        o_spec, pl.BlockSpec((None, gd, gd), lambda p, q, k: (q, 0, 0)),
        [((S, D), F32, o_spec), ((S, D), F32, o_spec)], (tm, gd),
        extras=[(x, o_spec), (scale, pl.BlockSpec((1, gd), lambda p, q, k: (0, q)))],
        epilogue=lambda acc, xv, sc: (acc, xv + acc * sc))


def _pool_bwd_pre(dx1, ypre, scale):
    S, D = dx1.shape
    tm = _tile(S, (256,))
    n = S // tm

    def body(dx_ref, y_ref, s_ref, dy_ref, ds_ref, acc_ref):
        i = pl.program_id(0)
        dx = dx_ref[...]
        dy_ref[...] = (dx * s_ref[...]).astype(BF16)

        @pl.when(i == 0)
        def _():
            acc_ref[...] = jnp.zeros_like(acc_ref)

        acc_ref[...] += _rows_to_8(dx * y_ref[...])

        @pl.when(i == n - 1)
        def _():
            ds_ref[...] = jnp.sum(acc_ref[...], axis=0, keepdims=True)

    row = pl.BlockSpec((tm, D), lambda i: (i, 0))
    vec = pl.BlockSpec((1, D), lambda i: (0, 0))
    return pl.pallas_call(
        body, name="pool_bwd_pre", grid=(n,),
        in_specs=[row, row, vec], out_specs=[row, vec],
        out_shape=[jax.ShapeDtypeStruct((S, D), BF16), jax.ShapeDtypeStruct((1, D), F32)],
        scratch_shapes=[pltpu.VMEM((8, D), F32)],
        compiler_params=_cparams(("arbitrary",)),
    )(dx1, ypre, scale)


def _pool_mm_bwd_in(dyp, wp):
    S, D = dyp.shape
    G, gd, _ = wp.shape
    tm = _tile(S, (1024, 512))
    spec = pl.BlockSpec((tm, gd), lambda p, q, k: (p, q))
    return _mm(
        "pool_mm_bwd_in", "nt", dyp, wp, (S // tm, G, 1),
        spec, pl.BlockSpec((None, gd, gd), lambda p, q, k: (q, 0, 0)),
        [((S, D), F32, spec)], (tm, gd))


def _pool_mm_bwd_w(d, dyp):
    S, D = d.shape
    G = len(POOL_WINDOWS)
    gd = D // G
    ts = _tile(S, (1024, 512, 256))
    spec = pl.BlockSpec((ts, gd), lambda p, q, k: (k, p))
    return _mm(
        "pool_mm_bwd_w", "tn", d, dyp, (G, 1, S // ts),
        spec, spec,
        [((G, gd, gd), BF16, pl.BlockSpec((None, gd, gd), lambda p, q, k: (p, 0, 0)))], (gd, gd))


def _pool_bwd_post(dd, x, gain, dres):
    S, D = x.shape
    tm = _tile(S, (256,))
    n = S // tm
    gd = D // len(POOL_WINDOWS)
    nh = tm // POOL_HALO

    def body(dd_ref, ddh_ref, x_ref, g_ref, dr_ref, dx_ref, dxb_ref, dg_ref, acc_ref):
        i = pl.program_id(0)
        dd = dd_ref[...]
        halo = ddh_ref[...] * (i < n - 1).astype(F32)
        parts = []
        for gi, w in enumerate(POOL_WINDOWS):
            sl = slice(gi * gd, (gi + 1) * gd)
            ext = jnp.concatenate([dd[:, sl] / _pool_counts(i, tm, w), halo[:, sl] * (1.0 / w)], axis=0)
            k = 1
            while k < w:
                ext = ext + pltpu.roll(ext, tm + POOL_HALO - k, 0)
                k *= 2
            parts.append(ext[:tm] - dd[:, sl])
        dh = jnp.concatenate(parts, axis=1)
        dxn, dgr = _rms_bwd_math(x_ref[...], g_ref[...], dh)
        dx = dr_ref[...] + dxn
        dx_ref[...] = dx
        dxb_ref[...] = dx.astype(BF16)

        @pl.when(i == 0)
        def _():
            acc_ref[...] = jnp.zeros_like(acc_ref)

        acc_ref[...] += _rows_to_8(dgr)

        @pl.when(i == n - 1)
        def _():
            dg_ref[...] = jnp.sum(acc_ref[...], axis=0, keepdims=True)

    row = pl.BlockSpec((tm, D), lambda i: (i, 0))
    vec = pl.BlockSpec((1, D), lambda i: (0, 0))
    nxt = pl.BlockSpec((POOL_HALO, D), lambda i: (jnp.minimum((i + 1) * nh, S // POOL_HALO - 1), 0))
    return pl.pallas_call(
        body, name="pool_bwd_post", grid=(n,),
        in_specs=[row, nxt, row, vec, row],
        out_specs=[row, row, vec],
        out_shape=[jax.ShapeDtypeStruct((S, D), F32), jax.ShapeDtypeStruct((S, D), BF16),
                   jax.ShapeDtypeStruct((1, D), F32)],
        scratch_shapes=[pltpu.VMEM((8, D), F32)],
        compiler_params=_cparams(("arbitrary",)),
    )(dd, dd, x, gain, dres)


def _band_tables():
    i = np.arange(WINDOW)[:, None]
    j = np.arange(2 * WINDOW)[None, :]
    n = np.maximum(WINDOW + i - j, 0)
    max_exact = N_BUCKETS // 2
    nf = np.maximum(n, 1).astype(np.float32)
    large = max_exact + (np.log(nf / max_exact) / np.log(MAX_DISTANCE / max_exact)
                         * (N_BUCKETS - max_exact)).astype(np.int32)
    large = np.minimum(large, N_BUCKETS - 1)
    buckets = np.where(n < max_exact, n, large).astype(np.int32)
    dist = WINDOW + i - j
    in_win = ((dist >= 0) & (dist < WINDOW)).astype(np.int32)
    return buckets, in_win


def _bias_expand(rel_bias):
    H = rel_bias.shape[0]
    buckets, in_win = _band_tables()

    def body(rb_ref, bk_ref, win_ref, o_ref):
        h = pl.program_id(0)
        bk = bk_ref[...]
        acc = jnp.zeros(bk.shape, F32)
        for b in range(N_BUCKETS):
            acc = jnp.where(bk == b, rb_ref[h, b], acc)
        o_ref[...] = jnp.where(win_ref[...] > 0, acc, NEG_INF)

    full = pl.BlockSpec((WINDOW, 2 * WINDOW), lambda h: (0, 0))
    return pl.pallas_call(
        body, name="bias_expand", grid=(H,),
        in_specs=[pl.BlockSpec(memory_space=pltpu.SMEM), full, full],
        out_specs=pl.BlockSpec((None, WINDOW, 2 * WINDOW), lambda h: (h, 0, 0)),
        out_shape=jax.ShapeDtypeStruct((H, WINDOW, 2 * WINDOW), F32),
        compiler_params=_cparams(("parallel",)),
    )(rel_bias, jnp.asarray(buckets), jnp.asarray(in_win))


def _bias_reduce(dbias):
    H = dbias.shape[0]
    buckets, in_win = _band_tables()

    def body(d_ref, bk_ref, win_ref, o_ref):
        bk = jnp.where(win_ref[...] > 0, bk_ref[...], -1)
        d = d_ref[...]
        lane = lax.broadcasted_iota(jnp.int32, (8, LANES), 1)
        out = jnp.zeros((8, LANES), F32)
        for b in range(N_BUCKETS):
            s = jnp.sum(jnp.where(bk == b, d, 0.0))
            out = jnp.where(lane == b, s, out)
        o_ref[...] = out

    full = pl.BlockSpec((WINDOW, 2 * WINDOW), lambda h: (0, 0))
    return pl.pallas_call(
        body, name="bias_reduce", grid=(H,),
        in_specs=[pl.BlockSpec((None, WINDOW, 2 * WINDOW), lambda h: (h, 0, 0)), full, full],
        out_specs=pl.BlockSpec((None, 8, LANES), lambda h: (h, 0, 0)),
        out_shape=jax.ShapeDtypeStruct((H, 8, LANES), F32),
        compiler_params=_cparams(("parallel",)),
    )(dbias, jnp.asarray(buckets), jnp.asarray(in_win))


def _half_rsqrt(t, lo):
    sq = t * t
    s_lo = jnp.sum(jnp.where(lo, sq, 0.0), axis=-1, keepdims=True)
    s_hi = jnp.sum(jnp.where(lo, 0.0, sq), axis=-1, keepdims=True)
    return jnp.where(lo, lax.rsqrt(s_lo * (1.0 / HEAD_DIM) + EPS), lax.rsqrt(s_hi * (1.0 / HEAD_DIM) + EPS))


def _qk_norm_fwd(qkv, qg, kg, n_q, n_kv):
    S, W = qkv.shape
    tm = _tile(S, (256,))
    qw = n_q * HEAD_DIM
    kw = n_kv * HEAD_DIM
    scale = HEAD_DIM ** -0.5

    def body(x_ref, qg_ref, kg_ref, q_ref, k_ref, v_ref):
        lo = lax.broadcasted_iota(jnp.int32, (tm, LANES), 1) < HEAD_DIM
        for t in range(qw // LANES):
            q = x_ref[:, t * LANES:(t + 1) * LANES]
            q_ref[:, t * LANES:(t + 1) * LANES] = (q * _half_rsqrt(q, lo) * qg_ref[...] * scale).astype(BF16)
        for p in range(kw // LANES):
            k = x_ref[:, qw + p * LANES:qw + (p + 1) * LANES]
            kn = k * _half_rsqrt(k, lo) * kg_ref[...]
            v = x_ref[:, qw + kw + p * LANES:qw + kw + (p + 1) * LANES]
            for src, dst in ((kn, k_ref), (v, v_ref)):
                rolled = pltpu.roll(src, HEAD_DIM, 1)
                dst[2 * p] = jnp.where(lo, src, rolled).astype(BF16)
                dst[2 * p + 1] = jnp.where(lo, rolled, src).astype(BF16)

    vec = pl.BlockSpec((1, LANES), lambda i: (0, 0))
    kv = pl.BlockSpec((n_kv, tm, LANES), lambda i: (0, i, 0))
    return pl.pallas_call(
        body, name="qk_norm_fwd", grid=(S // tm,),
        in_specs=[pl.BlockSpec((tm, W), lambda i: (i, 0)), vec, vec],
        out_specs=[pl.BlockSpec((tm, qw), lambda i: (i, 0)), kv, kv],
        out_shape=[jax.ShapeDtypeStruct((S, qw), BF16), jax.ShapeDtypeStruct((n_kv, S, LANES), BF16),
                   jax.ShapeDtypeStruct((n_kv, S, LANES), BF16)],
        compiler_params=_cparams(("parallel",)),
    )(qkv, qg, kg)


def _qk_norm_bwd(qkv, qg, kg, dq, dkc, dkp, dvc, dvp, n_q, n_kv):
    S, W = qkv.shape
    tm = _tile(S, (4 * WINDOW, WINDOW))
    n = S // tm
    per = tm // WINDOW
    qw = n_q * HEAD_DIM
    kw = n_kv * HEAD_DIM
    scale = HEAD_DIM ** -0.5

    def body(x_ref, qg_ref, kg_ref, dq_ref, dkc_ref, dkp_ref, dkn_ref, dvc_ref, dvp_ref, dvn_ref,
             dx_ref, dqg_ref, dkg_ref, accq_ref, acck_ref):
        i = pl.program_id(0)
        lo = lax.broadcasted_iota(jnp.int32, (tm, LANES), 1) < HEAD_DIM
        has_next = (i < n - 1).astype(F32)

        @pl.when(i == 0)
        def _():
            accq_ref[...] = jnp.zeros_like(accq_ref)
            acck_ref[...] = jnp.zeros_like(acck_ref)

        def norm_bwd(t, dy, gain):
            r = _half_rsqrt(t, lo)
            xhat = t * r
            dxhat = dy * gain
            prod = dxhat * xhat
            c_lo = jnp.sum(jnp.where(lo, prod, 0.0), axis=-1, keepdims=True) * (1.0 / HEAD_DIM)
            c_hi = jnp.sum(jnp.where(lo, 0.0, prod), axis=-1, keepdims=True) * (1.0 / HEAD_DIM)
            return r * (dxhat - xhat * jnp.where(lo, c_lo, c_hi)), _rows_to_8(dy * xhat)

        for t in range(qw // LANES):
            sl = slice(t * LANES, (t + 1) * LANES)
            dt, dg = norm_bwd(x_ref[:, sl], dq_ref[:, sl] * scale, qg_ref[...])
            dx_ref[:, sl] = dt.astype(BF16)
            accq_ref[...] += dg

        def pair(cur_ref, prev_ref, next_ref, p):
            folded = []
            for h in (2 * p, 2 * p + 1):
                later = next_ref[h] * has_next
                if per > 1:
                    later = jnp.concatenate([prev_ref[h, WINDOW:, :], later], axis=0)
                tot = cur_ref[h] + later
                folded.append(tot + pltpu.roll(tot, HEAD_DIM, 1))
            return jnp.where(lo, folded[0], folded[1])

        for p in range(kw // LANES):
            sl = slice(qw + p * LANES, qw + (p + 1) * LANES)
            dt, dg = norm_bwd(x_ref[:, sl], pair(dkc_ref, dkp_ref, dkn_ref, p), kg_ref[...])
            dx_ref[:, sl] = dt.astype(BF16)
            acck_ref[...] += dg
            sl = slice(qw + kw + p * LANES, qw + kw + (p + 1) * LANES)
            dx_ref[:, sl] = pair(dvc_ref, dvp_ref, dvn_ref, p).astype(BF16)

        @pl.when(i == n - 1)
        def _():
            for acc_ref, o_ref in ((accq_ref, dqg_ref), (acck_ref, dkg_ref)):
                s = jnp.sum(acc_ref[...], axis=0, keepdims=True)
                o_ref[...] = s + pltpu.roll(s, HEAD_DIM, 1)

    vec = pl.BlockSpec((1, LANES), lambda i: (0, 0))
    cur = pl.BlockSpec((n_kv, tm, LANES), lambda i: (0, i, 0))
    nxt = pl.BlockSpec((n_kv, WINDOW, LANES), lambda i: (0, jnp.minimum((i + 1) * per, S // WINDOW - 1), 0))
    return pl.pallas_call(
        body, name="qk_norm_bwd", grid=(n,),
        in_specs=[pl.BlockSpec((tm, W), lambda i: (i, 0)), vec, vec, pl.BlockSpec((tm, qw), lambda i: (i, 0)),
                  cur, cur, nxt, cur, cur, nxt],
        out_specs=[pl.BlockSpec((tm, W), lambda i: (i, 0)), vec, vec],
        out_shape=[jax.ShapeDtypeStruct((S, W), BF16), jax.ShapeDtypeStruct((1, LANES), F32),
                   jax.ShapeDtypeStruct((1, LANES), F32)],
        scratch_shapes=[pltpu.VMEM((8, LANES), F32), pltpu.VMEM((8, LANES), F32)],
        compiler_params=_cparams(("arbitrary",)),
    )(qkv, qg, kg, dq, dkc, dkp, dkp, dvc, dvp, dvp)


def _attn_specs(n_kv, per_step):
    kvs = per_step if n_kv % per_step == 0 else 1
    q = pl.BlockSpec((WINDOW, kvs * GQA_GROUP * HEAD_DIM), lambda kh, n: (n, kh))
    cur = pl.BlockSpec((kvs, WINDOW, LANES), lambda kh, n: (kh, n, 0))
    prev = pl.BlockSpec((kvs, WINDOW, LANES), lambda kh, n: (kh, jnp.maximum(n - 1, 0), 0))
    bias = pl.BlockSpec((kvs * GQA_GROUP, WINDOW, 2 * WINDOW), lambda kh, n: (kh, 0, 0))
    sink = pl.BlockSpec(memory_space=pltpu.SMEM)
    return kvs, q, cur, prev, bias, sink


def _stack_heads(x_ref, lo, h):
    parts = []
    for g in range(GQA_GROUP):
        c0 = (h * GQA_GROUP // 2 + g // 2) * LANES
        t = x_ref[:, c0:c0 + LANES]
        parts.append(jnp.where(lo if g % 2 == 0 else jnp.logical_not(lo), t, jnp.zeros_like(t)))
    return jnp.concatenate(parts, axis=0)


def _unstack_heads(v, lo):
    return [jnp.where(lo, v[(2 * p) * WINDOW:(2 * p + 1) * WINDOW], v[(2 * p + 1) * WINDOW:(2 * p + 2) * WINDOW])
            for p in range(GQA_GROUP // 2)]


def _attn_probs(qs, kk, bias, s_ref, head0, has_prev):
    rows = GQA_GROUP * WINDOW
    s = lax.dot_general(qs, kk, _DN["nt"], preferred_element_type=F32) + bias.reshape(rows, 2 * WINDOW)
    col = lax.broadcasted_iota(jnp.int32, (rows, 2 * WINDOW), 1)
    s = jnp.where(jnp.logical_or(has_prev, col >= WINDOW), s, NEG_INF)
    sink = jnp.concatenate([jnp.full((WINDOW, 1), s_ref[head0 + g], F32) for g in range(GQA_GROUP)], axis=0)
    m = jnp.maximum(jnp.max(s, axis=-1, keepdims=True), sink)
    p = jnp.exp(s - m)
    es = jnp.exp(sink - m)
    den = jnp.sum(p, axis=-1, keepdims=True) + es
    return p / den, es / den


def _attn_fwd(q, kcat, vcat, biasm, sinks):
    S, qw = q.shape
    n_kv = kcat.shape[0]
    kvs, qs, cur, prev, bias, sink = _attn_specs(n_kv, 4)
    pairs = GQA_GROUP // 2

    def body(q_ref, kc_ref, kp_ref, vc_ref, vp_ref, b_ref, s_ref, o_ref):
        kh, n = pl.program_id(0), pl.program_id(1)
        lo = lax.broadcasted_iota(jnp.int32, (WINDOW, LANES), 1) < HEAD_DIM
        for h in range(kvs):
            kk = jnp.concatenate([kp_ref[h], kc_ref[h]], axis=0)
            vv = jnp.concatenate([vp_ref[h], vc_ref[h]], axis=0)
            pn, _ = _attn_probs(_stack_heads(q_ref, lo, h), kk, b_ref[h * GQA_GROUP:(h + 1) * GQA_GROUP], s_ref,
                                (kh * kvs + h) * GQA_GROUP, n > 0)
            o = jnp.dot(pn.astype(BF16), vv, preferred_element_type=F32)
            for p, t in enumerate(_unstack_heads(o, lo)):
                o_ref[:, (h * pairs + p) * LANES:(h * pairs + p + 1) * LANES] = t.astype(BF16)

    return pl.pallas_call(
        body, name="attn_fwd", grid=(n_kv // kvs, S // WINDOW),
        in_specs=[qs, cur, prev, cur, prev, bias, sink],
        out_specs=qs,
        out_shape=jax.ShapeDtypeStruct((S, qw), BF16),
        compiler_params=_cparams(("parallel", "parallel")),
    )(q, kcat, kcat, vcat, vcat, biasm, sinks)


def _attn_bwd(q, kcat, vcat, biasm, sinks, do):
    S, qw = q.shape
    n_kv = kcat.shape[0]
    H = n_kv * GQA_GROUP
    kvs, qs, cur, prev, bias, sink = _attn_specs(n_kv, ATTN_BWD_KV_PER_STEP)
    pairs = GQA_GROUP // 2

    def body(q_ref, kc_ref, kp_ref, vc_ref, vp_ref, b_ref, s_ref, do_ref,
             dq_ref, dkc_ref, dkp_ref, dvc_ref, dvp_ref, db_ref, ds_ref):
        kh, n = pl.program_id(0), pl.program_id(1)
        lo = lax.broadcasted_iota(jnp.int32, (WINDOW, LANES), 1) < HEAD_DIM

        @pl.when(n == 0)
        def _():
            db_ref[...] = jnp.zeros_like(db_ref)
            ds_ref[...] = jnp.zeros_like(ds_ref)

        for h in range(kvs):
            gs = slice(h * GQA_GROUP, (h + 1) * GQA_GROUP)
            kk = jnp.concatenate([kp_ref[h], kc_ref[h]], axis=0)
            vv = jnp.concatenate([vp_ref[h], vc_ref[h]], axis=0)
            qs_ = _stack_heads(q_ref, lo, h)
            dos = _stack_heads(do_ref, lo, h)
            pn, ps = _attn_probs(qs_, kk, b_ref[gs], s_ref, (kh * kvs + h) * GQA_GROUP, n > 0)
            dp = lax.dot_general(dos, vv, _DN["nt"], preferred_element_type=F32)
            delta = jnp.sum(pn * dp, axis=-1, keepdims=True)
            ds = pn * (dp - delta)
            db_ref[gs] += ds.reshape(GQA_GROUP, WINDOW, 2 * WINDOW)
            ds_ref[h] += (jnp.zeros((GQA_GROUP, LANES), F32)
                          - jnp.sum((ps * delta).reshape(GQA_GROUP, WINDOW, 1), axis=1))
            dsb = ds.astype(BF16)
            dq = jnp.dot(dsb, kk, preferred_element_type=F32)
            for p, t in enumerate(_unstack_heads(dq, lo)):
                dq_ref[:, (h * pairs + p) * LANES:(h * pairs + p + 1) * LANES] = t
            dk = lax.dot_general(dsb, qs_, _DN["tn"], preferred_element_type=F32)
            dv = lax.dot_general(pn.astype(BF16), dos, _DN["tn"], preferred_element_type=F32)
            dkp_ref[h] = dk[:WINDOW]
            dkc_ref[h] = dk[WINDOW:]
            dvp_ref[h] = dv[:WINDOW]
            dvc_ref[h] = dv[WINDOW:]

    part = jax.ShapeDtypeStruct((n_kv, S, LANES), F32)
    return pl.pallas_call(
        body, name="attn_bwd", grid=(n_kv // kvs, S // WINDOW),
        in_specs=[qs, cur, prev, cur, prev, bias, sink, qs],
        out_specs=[qs, cur, cur, cur, cur, bias, pl.BlockSpec((kvs, 8, LANES), lambda kh, n: (kh, 0, 0))],
        out_shape=[jax.ShapeDtypeStruct((S, qw), F32), part, part, part, part,
                   jax.ShapeDtypeStruct((H, WINDOW, 2 * WINDOW), F32),
                   jax.ShapeDtypeStruct((n_kv, 8, LANES), F32)],
        compiler_params=_cparams(("parallel", "arbitrary")),
    )(q, kcat, kcat, vcat, vcat, biasm, sinks, do)


def _adamw(name, w, g, m, v):
    shape = w.shape
    C = shape[-1]
    R = int(np.prod(shape[:-1]))
    tr = R
    if R * C * 4 > (1 << 20):
        tr = _tile(R, tuple(t for t in (512, 256, 128, 64, 32, 16, 8) if t * C * 4 <= (3 << 19)))
    c1 = 1.0 - ADAM_B1 ** ADAM_STEP
    c2 = 1.0 - ADAM_B2 ** ADAM_STEP

    def body(w_ref, g_ref, m_ref, v_ref, d_ref, nm_ref, nv_ref):
        gv = g_ref[...]
        nm = ADAM_B1 * m_ref[...] + (1.0 - ADAM_B1) * gv
        nv = ADAM_B2 * v_ref[...] + (1.0 - ADAM_B2) * (gv * gv)
        d_ref[...] = -ADAM_LR * ((nm / c1) / (jnp.sqrt(nv / c2) + ADAM_EPS) + ADAM_WD * w_ref[...])
        nm_ref[...] = nm
        nv_ref[...] = nv

    spec = pl.BlockSpec((tr, C), lambda i: (i, 0))
    outs = pl.pallas_call(
        body, name=name, grid=(R // tr,),
        in_specs=[spec] * 4, out_specs=[spec] * 3,
        out_shape=[jax.ShapeDtypeStruct((R, C), F32)] * 3,
        compiler_params=_cparams(("parallel",)),
    )(*[t.reshape(R, C) for t in (w, g, m, v)])
    return [o.reshape(shape) for o in outs]


def _place():
    return lax.axis_index("x"), lax.axis_index("y"), lax.axis_index("c")


def _other_chips(x, y):
    return [(1 - x, y), (x, 1 - y), (1 - x, 1 - y)]


def _add_half(name, g, other, c_arr):
    nb, R, C = g.shape
    half = R // 2
    tr = _tile(half, (256, 128, 64, 32, 16))
    n = half // tr

    def body(c_ref, g_ref, o_ref, s_ref):
        s_ref[...] = (g_ref[...].astype(F32) + o_ref[...].astype(F32)).astype(BF16)

    return pl.pallas_call(
        body, name=name,
        grid_spec=pltpu.PrefetchScalarGridSpec(
            num_scalar_prefetch=1, grid=(nb, n),
            in_specs=[pl.BlockSpec((None, tr, C), lambda b, i, c: (b, c[0] * n + i, 0)),
                      pl.BlockSpec((None, tr, C), lambda b, i, c: (b, i, 0))],
            out_specs=pl.BlockSpec((None, tr, C), lambda b, i, c: (b, i, 0))),
        out_shape=jax.ShapeDtypeStruct((nb, half, C), BF16),
        compiler_params=_cparams(("parallel", "parallel")),
    )(c_arr, g, other)


def _in_hbm(arrays):
    return [pltpu.with_memory_space_constraint(a, pltpu.HBM) for a in arrays]


def _chip_copy(src, dst, send, recv, chip, c):
    return pltpu.make_async_remote_copy(src_ref=src, dst_ref=dst, send_sem=send, recv_sem=recv,
                                        device_id=(chip[0], chip[1], c), device_id_type=MESH)


def _cast_slot(name, w, l, chip_arr, dtype):
    _, R, C = w.shape
    tr = _tile(R, tuple(t for t in (1024, 512, 256, 128, 64, 32, 16) if t * C * 4 <= (1 << 21)))

    def body(chip_ref, w_ref, o_ref):
        o_ref[...] = w_ref[...].astype(dtype)

    return pl.pallas_call(
        body, name=name,
        grid_spec=pltpu.PrefetchScalarGridSpec(
            num_scalar_prefetch=1, grid=(R // tr,),
            in_specs=[pl.BlockSpec((None, tr, C), lambda i, chip: (l, i, 0))],
            out_specs=pl.BlockSpec((None, tr, C), lambda i, chip: (chip[0], i, 0))),
        out_shape=jax.ShapeDtypeStruct((N_CHIPS, R, C), dtype),
        compiler_params=_cparams(("parallel",)),
    )(chip_arr, w)


def _gather_start(name, groups):
    sizes = [len(g) for g in groups]
    flat = [b for g in groups for b in g]
    n, G = len(flat), len(groups)

    def body(*refs):
        bufs = refs[:n]
        sems = refs[n:n + 2 * G]
        token = refs[-1]
        x, y, c = _place()
        chips = _other_chips(x, y)
        me = 2 * x + y
        k = 0
        for l in range(G):
            for t in range(sizes[l]):
                for j in range(3):
                    _chip_copy(bufs[k].at[me], bufs[k].at[me], sems[2 * l].at[3 * t + j], sems[2 * l + 1].at[3 * t + j],
                               chips[j], c).start()
                k += 1
        token[...] = jnp.zeros_like(token)

    sem_shapes = []
    for s in sizes:
        sem_shapes += [pltpu.SemaphoreType.DMA((3 * s,)), pltpu.SemaphoreType.DMA((3 * s,))]
    outs = pl.pallas_call(
        body, name=name,
        out_shape=(*sem_shapes, *[pltpu.HBM(b.shape, b.dtype) for b in flat], jax.ShapeDtypeStruct((8, LANES), F32)),
        in_specs=[_HBM] * n,
        out_specs=(*[_SEM] * (2 * G), *[_HBM] * n, pl.BlockSpec(memory_space=pltpu.VMEM)),
        input_output_aliases={t: 2 * G + t for t in range(n)},
        compiler_params=pltpu.CompilerParams(has_side_effects=_DATAFLOW),
    )(*_in_hbm(flat))
    res, k = [], 2 * G
    for l in range(G):
        res.append((outs[2 * l], outs[2 * l + 1], list(outs[k:k + sizes[l]])))
        k += sizes[l]
    return res, outs[-1]


def _gather_wait(name, send, recv, bufs, after):
    T = len(bufs)

    def body(*refs):
        ins = refs[:T]
        send_ref, recv_ref = refs[T], refs[T + 1]
        x, y, c = _place()
        chips = _other_chips(x, y)
        me = 2 * x + y
        for t in range(T):
            for j in range(3):
                cp = _chip_copy(ins[t].at[me], ins[t].at[2 * chips[j][0] + chips[j][1]], send_ref.at[3 * t + j],
                                recv_ref.at[3 * t + j], chips[j], c)
                cp.wait_send()
                cp.wait_recv()

    return pl.pallas_call(
        body, name=name,
        out_shape=[pltpu.HBM(b.shape, b.dtype) for b in bufs],
        in_specs=[_HBM] * T + [_SEM, _SEM, _ANY],
        out_specs=[_HBM] * T,
        input_output_aliases={t: t for t in range(T)},
        compiler_params=pltpu.CompilerParams(has_side_effects=_DATAFLOW),
    )(*bufs, send, recv, after)


def _split_start(name, bufs, n_copies, plan):
    n = len(bufs)

    def body(*refs):
        send, recv, token = refs[n], refs[n + 1], refs[-1]
        for s, (src, dst, dev) in enumerate(plan(refs[:n], False)):
            pltpu.make_async_remote_copy(src_ref=src, dst_ref=dst, send_sem=send.at[s], recv_sem=recv.at[s],
                                         device_id=dev, device_id_type=MESH).start()
        token[...] = jnp.zeros_like(token)

    outs = pl.pallas_call(
        body, name=name,
        out_shape=(pltpu.SemaphoreType.DMA((n_copies,)), pltpu.SemaphoreType.DMA((n_copies,)),
                   *[pltpu.HBM(b.shape, b.dtype) for b in bufs], jax.ShapeDtypeStruct((8, LANES), F32)),
        in_specs=[_HBM] * n,
        out_specs=(_SEM, _SEM, *[_HBM] * n, pl.BlockSpec(memory_space=pltpu.VMEM)),
        input_output_aliases={t: 2 + t for t in range(n)},
        compiler_params=pltpu.CompilerParams(has_side_effects=_DATAFLOW),
    )(*_in_hbm(bufs))
    return outs[0], outs[1], list(outs[2:2 + n]), outs[-1]


def _split_wait(name, send, recv, bufs, plan, after):
    n = len(bufs)

    def body(*refs):
        send_ref, recv_ref = refs[n], refs[n + 1]
        for s, (src, dst, dev) in enumerate(plan(refs[:n], True)):
            cp = pltpu.make_async_remote_copy(src_ref=src, dst_ref=dst, send_sem=send_ref.at[s], recv_sem=recv_ref.at[s],
                                              device_id=dev, device_id_type=MESH)
            cp.wait_send()
            cp.wait_recv()

    return list(pl.pallas_call(
        body, name=name,
        out_shape=[pltpu.HBM(b.shape, b.dtype) for b in bufs],
        in_specs=[_HBM] * n + [_SEM, _SEM] + [_ANY] * len(after),
        out_specs=[_HBM] * n,
        input_output_aliases={t: t for t in range(n)},
        compiler_params=pltpu.CompilerParams(has_side_effects=_DATAFLOW),
    )(*bufs, send, recv, *after))


def _swap_plan(shapes):
    T = len(shapes)

    def plan(refs, waiting):
        x, y, c = _place()
        out = []
        for t in range(T):
            half = shapes[t][1] // 2
            src = refs[t].at[:, pl.ds(pl.multiple_of((1 - c) * half, 16), half), :]
            out.append((src, refs[T + t], (x, y, 1 - c)))
        return out

    return plan


def _scatter_plan(T):
    def plan(refs, waiting):
        x, y, c = _place()
        chips = _other_chips(x, y)
        return [(refs[t].at[2 * chips[j][0] + chips[j][1]], refs[T + t].at[j], (chips[j][0], chips[j][1], c))
                for t in range(T) for j in range(3)]

    return plan


def _join_plan(T):
    def plan(refs, waiting):
        x, y, c = _place()
        return [(refs[t].at[c], refs[t].at[1 - c if waiting else c], (x, y, 1 - c)) for t in range(T)]

    return plan


def _sum_parts(name, sums, land, chip_arr, c_arr):
    _, R2, C = sums.shape
    tr = _tile(R2, (256, 128, 64, 32, 16))

    def body(chip_ref, c_ref, s_ref, l_ref, o_ref):
        acc = s_ref[...].astype(F32)
        for j in range(3):
            acc = acc + l_ref[j].astype(F32)
        o_ref[...] = acc

    return pl.pallas_call(
        body, name=name,
        grid_spec=pltpu.PrefetchScalarGridSpec(
            num_scalar_prefetch=2, grid=(R2 // tr,),
            in_specs=[pl.BlockSpec((None, tr, C), lambda i, chip, c: (chip[0], i, 0)),
                      pl.BlockSpec((3, tr, C), lambda i, chip, c: (0, i, 0))],
            out_specs=pl.BlockSpec((None, tr, C), lambda i, chip, c: (c[0], i, 0))),
        out_shape=jax.ShapeDtypeStruct((2, R2, C), F32),
        compiler_params=_cparams(("parallel",)),
    )(chip_arr, c_arr, sums, land)


class _Reduction:
    def __init__(self, tag, grads, chip_arr, c_arr):
        self.tag, self.T, self.shapes = tag, len(grads), [g.shape for g in grads]
        self.chip_arr, self.c_arr = chip_arr, c_arr
        lands = [lax.empty((g.shape[0], g.shape[1] // 2, g.shape[2]), g.dtype) for g in grads]
        self.plan = _swap_plan(self.shapes)
        self.send, self.recv, self.bufs, self.token = _split_start("rs_swap_start_" + tag, list(grads) + lands, self.T,
                                                                   self.plan)
        self.stage, self.result = 0, None

    def advance(self, after):
        T, tag = self.T, self.tag
        bufs = _split_wait("rs_wait%d_%s" % (self.stage, tag), self.send, self.recv, self.bufs, self.plan,
                           list(after) or [self.token])
        if self.stage == 0:
            sums = [_add_half("rs_add_%s_%d" % (tag, t), bufs[t], bufs[T + t], self.c_arr) for t in range(T)]
            lands = [lax.empty((3,) + s.shape[1:], s.dtype) for s in sums]
            self.plan = _scatter_plan(T)
            self.send, self.recv, self.bufs, self.token = _split_start("rs_scatter_start_" + tag, sums + lands, 3 * T,
                                                                       self.plan)
        elif self.stage == 1:
            parts = [_sum_parts("rs_sum_%s_%d" % (tag, t), bufs[t], bufs[T + t], self.chip_arr, self.c_arr)
                     for t in range(T)]
            self.plan = _join_plan(T)
            self.send, self.recv, self.bufs, self.token = _split_start("rs_join_start_" + tag, parts, T, self.plan)
        else:
            self.result = [b.reshape(sh[1], sh[2]) for b, sh in zip(bufs, self.shapes)]
            self.token = None
        self.stage += 1
        return self.token


def _all_reduce_small(v):
    R = v.shape[0]

    def body(v_ref, o_ref, buf, send, recv):
        x, y, c = _place()
        me = 4 * x + 2 * y + c
        buf[me] = v_ref[...]

        def copy(k, slot):
            to = (x ^ ((k >> 2) & 1), y ^ ((k >> 1) & 1), c ^ (k & 1))
            return pltpu.make_async_remote_copy(
                src_ref=v_ref, dst_ref=buf.at[slot], send_sem=send.at[k - 1], recv_sem=recv.at[k - 1],
                device_id=to, device_id_type=MESH)

        for k in range(1, 8):
            copy(k, me).start()
        for k in range(1, 8):
            copy(k, me ^ k).wait_recv()
        for k in range(1, 8):
            copy(k, me).wait_send()
        acc = buf[0]
        for s in range(1, 8):
            acc = acc + buf[s]
        o_ref[...] = acc

    vm = pl.BlockSpec(memory_space=pltpu.VMEM)
    return pl.pallas_call(
        body, name="all_reduce_small",
        in_specs=[vm], out_specs=vm,
        out_shape=jax.ShapeDtypeStruct((R, LANES), F32),
        scratch_shapes=[pltpu.VMEM((8, R, LANES), F32), pltpu.SemaphoreType.DMA((7,)), pltpu.SemaphoreType.DMA((7,))],
        compiler_params=pltpu.CompilerParams(has_side_effects=True, vmem_limit_bytes=VMEM_LIMIT),
    )(v)


def _local_step(x, tgt, small, depth, nb, weight, convw, pscale, on_grads):
    S, D = x.shape
    n_q = D // HEAD_DIM
    n_kv = n_q // GQA_GROUP
    Fh = convw[0].shape[2]

    def dup(gain):
        return jnp.tile(gain, 2).reshape(1, LANES)

    biasm = _bias_expand(small["rel_bias"])
    saved = []
    for i in range(depth):
        j = i // 2
        w = {}
        st = {"x0": x, "w": w}
        gm = small["norm_mix"][i].reshape(1, D)
        if i % 2 == 0:
            h = _rms_fwd("rms_mix_fwd", x, gm)
            w["wqkv"] = weight(i, "wqkv", h)
            qkv = _mm_colblk("qkv_fwd", h, w["wqkv"], F32)
            qh, kcat, vcat = _qk_norm_fwd(qkv, dup(small["attn_q_gain"][j]), dup(small["attn_k_gain"][j]), n_q, n_kv)
            o = _attn_fwd(qh, kcat, vcat, biasm, small["attn_sinks"][j])
            w["wo"] = weight(i, "wo", o)
            x = _mm_rowblk_res("wo_fwd", o, w["wo"], x)
            st.update(h=h, qkv=qkv, qh=qh, kcat=kcat, vcat=vcat, o=o)
        else:
            d = _pool_fwd(x, gm)
            w["wp"] = weight(i, "wp", d)
            ypre, x = _pool_mm_fwd(d, w["wp"], x, pscale[j])
            st.update(d=d, ypre=ypre)
        st["x1"] = x
        h2 = _rms_fwd("rms_ffn_fwd", x, small["norm_ffn"][i].reshape(1, D))
        w["wup"] = weight(i, "wup", h2)
        cb = small["ffn_conv_b"][i].reshape(2, 1, Fh)
        u, uc, a = _up_act_fwd(h2, w["wup"], convw[i], cb)
        w["wdn"] = weight(i, "wdn", a)
        x = _mm_rowblk_res("down_fwd", a, w["wdn"], x)
        st.update(h2=h2, u=u, uc=uc, a=a)
        saved.append(st)

    dx, dxb, loss_part = _loss_head(x, tgt)

    big = [dict() for _ in range(depth)]
    sg = {k: [None] * depth for k in ("norm_mix", "norm_ffn", "conv_w", "conv_b")}
    sg.update({k: [None] * ((depth + 1) // 2) for k in ("q_gain", "k_gain", "sinks")})
    sg["pool_scale"] = [None] * (depth // 2)
    dbias_tot = None
    before = None
    for i in reversed(range(depth)):
        j = i // 2
        st = saved[i]
        w = st["w"]
        duc = _down_act_bwd(dxb, w["wdn"], st["uc"], after=before)
        wdn_grad = _mm_tn_rowblk("down_bwd_w", st["a"], dxb, nb)
        du, dh2, dcwb = _up_bwd_in(duc, st["u"], convw[i], w["wup"])
        dcwb = jnp.transpose(dcwb, (1, 0, 2)).reshape(8, 2 * Fh)
        sg["conv_w"][i] = dcwb[0:3]
        sg["conv_b"][i] = dcwb[3]
        wup_grad = _mm_tn_colblk("up_bwd_w", st["h2"], du, nb, split2=True)
        before = on_grads(i, "ffn", dict(wdn=wdn_grad, wup=wup_grad), dx)
        dx, dxb, dg = _rms_bwd("rms_ffn_bwd", st["x1"], small["norm_ffn"][i].reshape(1, D), dh2, dx, after=before)
        sg["norm_ffn"][i] = dg.reshape(D)
        gm = small["norm_mix"][i].reshape(1, D)
        if i % 2 == 0:
            do = _mm_nt_rowblk("wo_bwd_in", dxb, w["wo"], BF16)
            big[i]["wo"] = _mm_tn_rowblk("wo_bwd_w", st["o"], dxb, nb)
            dq, dkc, dkp, dvc, dvp, dbias, dsink = _attn_bwd(
                st["qh"], st["kcat"], st["vcat"], biasm, small["attn_sinks"][j], do)
            dbias_tot = dbias if dbias_tot is None else dbias_tot + dbias
            sg["sinks"][j] = dsink[:, :, 0].reshape(n_q)
            dqkv, dqg, dkg = _qk_norm_bwd(st["qkv"], dup(small["attn_q_gain"][j]), dup(small["attn_k_gain"][j]),
                                          dq, dkc, dkp, dvc, dvp, n_q, n_kv)
            sg["q_gain"][j] = dqg[0, :HEAD_DIM]
            sg["k_gain"][j] = dkg[0, :HEAD_DIM]
            big[i]["wqkv"] = _mm_tn_colblk("qkv_bwd_w", st["h"], dqkv, nb)
            dh = _mm_nt_colblk("qkv_bwd_in", dqkv, w["wqkv"])
            dx, dxb, dg = _rms_bwd("rms_mix_bwd", st["x0"], gm, dh, dx)
        else:
            dyp, dsc = _pool_bwd_pre(dx, st["ypre"], pscale[j])
            sg["pool_scale"][j] = dsc.reshape(D)
            big[i]["wp"] = _pool_mm_bwd_w(st["d"], dyp)
            dd = _pool_mm_bwd_in(dyp, w["wp"])
            dx, dxb, dg = _pool_bwd_post(dd, st["x0"], gm, dx)
        sg["norm_mix"][i] = dg.reshape(D)
        before = on_grads(i, "mix", big[i], dx)
    sg["rel_bias"] = _bias_reduce(dbias_tot)[:, 0, :N_BUCKETS]
    return loss_part, dx, sg


_SMALL_ORDER = ("norm_mix", "norm_ffn", "rel_bias", "q_gain", "k_gain", "sinks", "conv_b", "conv_w", "pool_scale")


def kernel(x, norm_mix, norm_ffn, rel_bias, attn_w_qkv, attn_q_gain, attn_k_gain, attn_sinks, attn_w_o, pool_w, pool_scale, ffn_w_up, ffn_conv_w, ffn_conv_b, ffn_w_down, loss_target, m_norm_mix, m_norm_ffn, m_rel_bias, m_attn_w_qkv, m_attn_q_gain, m_attn_k_gain, m_attn_sinks, m_attn_w_o, m_pool_w, m_pool_scale, m_ffn_w_up, m_ffn_conv_w, m_ffn_conv_b, m_ffn_w_down, v_norm_mix, v_norm_ffn, v_rel_bias, v_attn_w_qkv, v_attn_q_gain, v_attn_k_gain, v_attn_sinks, v_attn_w_o, v_pool_w, v_pool_scale, v_ffn_w_up, v_ffn_conv_w, v_ffn_conv_b, v_ffn_w_down):
    _, S, D = x.shape
    depth = ffn_w_up.shape[0]
    n_attn, n_pool = attn_w_qkv.shape[0], pool_w.shape[0]
    nb = N_CHIPS
    nc_up = ffn_w_up.shape[2]
    Fh = nc_up * nb // 2
    cx, cy, cc = _place()
    chip = 2 * cx + cy
    c_arr = jnp.reshape(cc, (1,)).astype(jnp.int32)
    chip_arr = jnp.reshape(chip, (1,)).astype(jnp.int32)

    pool_w3 = pool_w.reshape(n_pool, -1, pool_w.shape[-1])
    order = [("convw", ffn_conv_w.reshape(1, -1, nc_up), 0, F32), ("pscale", pool_scale[None], 0, F32)]
    for i in range(depth):
        j = i // 2
        mixer = ([("wqkv", attn_w_qkv, j), ("wo", attn_w_o, j)] if i % 2 == 0 else [("wp", pool_w3, j)])
        order += [((i, k), w, l, BF16) for k, w, l in mixer + [("wup", ffn_w_up, i), ("wdn", ffn_w_down, i)]]
    first = 4
    started, tokens = [], []
    for part, lo, hi in (("first", 0, first), ("rest", first, len(order))):
        slots = [_cast_slot("slot_%d" % n, w, l, chip_arr, dt) for n, (_, w, l, dt) in enumerate(order[lo:hi], lo)]
        st, token = _gather_start("gather_start_" + part, [[b] for b in slots])
        started += st
        tokens.append(token)
    index = {key: n for n, (key, _, _, _) in enumerate(order)}

    def gathered(key, after):
        n = index[key]
        send, recv, bufs = started[n]
        return _gather_wait("gather_wait_%d" % n, send, recv, bufs, tokens[1] if n == 2 else after)[0]

    convw_g = gathered("convw", tokens[0]).reshape(nb, depth, 3, nc_up)
    pscale_g = gathered("pscale", tokens[0])
    convw = [jnp.transpose(convw_g[:, i], (1, 0, 2)).reshape(3, 2, Fh).transpose(1, 0, 2) for i in range(depth)]
    pscale = [pscale_g[:, j].reshape(1, D) for j in range(n_pool)]
    small = dict(norm_mix=norm_mix, norm_ffn=norm_ffn, rel_bias=rel_bias, attn_q_gain=attn_q_gain,
                 attn_k_gain=attn_k_gain, attn_sinks=attn_sinks, ffn_conv_b=ffn_conv_b)

    G, gd = pool_w.shape[1], pool_w.shape[3]

    def weight(i, name, after):
        w = gathered((i, name), after)
        if name == "wp":
            w = jnp.transpose(w.reshape(nb, G, gd // nb, gd), (1, 0, 2, 3)).reshape(G, gd, gd)
        return w

    red = {k: [None] * (n_attn if k in ("wqkv", "wo") else n_pool if k == "wp" else depth)
           for k in ("wqkv", "wo", "wp", "wup", "wdn")}
    pending = []

    def advance_all(after):
        tokens = []
        for r in list(pending):
            token = r.advance(after)
            if token is None:
                pending.remove(r)
                for k, o in zip(r.names, r.result):
                    red[k][r.layer // 2 if k in ("wqkv", "wo", "wp") else r.layer] = o
            else:
                tokens.append(token)
        return tokens

    def on_grads(i, tag, grads, dx_i):
        tokens = advance_all([dx_i])
        names = sorted(grads)
        flat = []
        for k in names:
            g = grads[k]
            if k == "wp":
                g = jnp.transpose(g.reshape(G, nb, gd // nb, gd), (1, 0, 2, 3))
            flat.append(g.reshape(nb, -1, g.shape[-1]))
        r = _Reduction("%d%s" % (i, tag), flat, chip_arr, c_arr)
        r.names, r.layer = names, i
        pending.append(r)
        return tokens + [r.token]

    loss_part, dx, sg = _local_step(x[0], loss_target[0], small, depth, nb, weight, convw, pscale, on_grads)
    while pending:
        advance_all([])
    loss = lax.psum(jnp.sum(loss_part), ("x", "y", "c"))

    g_wqkv = jnp.stack(red["wqkv"])
    g_wo = jnp.stack(red["wo"])
    g_wp = jnp.stack(red["wp"]).reshape(pool_w.shape)
    g_wup = jnp.stack(red["wup"])
    g_wdn = jnp.stack(red["wdn"])

    parts = [jnp.stack(sg[k]) if isinstance(sg[k], list) else sg[k] for k in _SMALL_ORDER]
    sizes = [int(np.prod(p.shape)) for p in parts]
    total = sum(sizes)
    rows = -(-total // (8 * LANES)) * 8
    packed = jnp.concatenate([p.reshape(-1) for p in parts] + [jnp.zeros((rows * LANES - total,), F32)])
    summed = _all_reduce_small(packed.reshape(rows, LANES)).reshape(-1)
    sm, off = {}, 0
    for k, p, n in zip(_SMALL_ORDER, parts, sizes):
        sm[k] = summed[off:off + n].reshape(p.shape)
        off += n
    g_convw = lax.dynamic_slice_in_dim(sm["conv_w"], chip * nc_up, nc_up, axis=2)
    pc = pool_scale.shape[1]
    g_pscale = lax.dynamic_slice_in_dim(sm["pool_scale"], chip * pc, pc, axis=1)

    grads = [sm["norm_mix"], sm["norm_ffn"], sm["rel_bias"], g_wqkv, sm["q_gain"], sm["k_gain"], sm["sinks"], g_wo,
             g_wp, g_pscale, g_wup, g_convw, sm["conv_b"], g_wdn]
    ws = [norm_mix, norm_ffn, rel_bias, attn_w_qkv, attn_q_gain, attn_k_gain, attn_sinks, attn_w_o, pool_w, pool_scale,
          ffn_w_up, ffn_conv_w, ffn_conv_b, ffn_w_down]
    ms = [m_norm_mix, m_norm_ffn, m_rel_bias, m_attn_w_qkv, m_attn_q_gain, m_attn_k_gain, m_attn_sinks, m_attn_w_o,
          m_pool_w, m_pool_scale, m_ffn_w_up, m_ffn_conv_w, m_ffn_conv_b, m_ffn_w_down]
    vs = [v_norm_mix, v_norm_ffn, v_rel_bias, v_attn_w_qkv, v_attn_q_gain, v_attn_k_gain, v_attn_sinks, v_attn_w_o,
          v_pool_w, v_pool_scale, v_ffn_w_up, v_ffn_conv_w, v_ffn_conv_b, v_ffn_w_down]
    deltas, new_m, new_v = [], [], []
    for idx, (w, g, m, v) in enumerate(zip(ws, grads, ms, vs)):
        d, nm, nv = _adamw("adamw_%d" % idx, w, g, m, v)
        deltas.append(d), new_m.append(nm), new_v.append(nv)
    return (loss, dx.reshape(1, S, D), *grads, *deltas, *new_m, *new_v)
```

```python
import functools

import numpy as np
import jax
import jax.numpy as jnp
from jax import lax
from jax.experimental import pallas as pl
from jax.experimental.pallas import tpu as pltpu

F32 = jnp.float32
BF16 = jnp.bfloat16
MESH = pl.DeviceIdType.MESH
_ANY = pl.BlockSpec(memory_space=pl.ANY)
_HBM = pl.BlockSpec(memory_space=pltpu.HBM)
_SEM = pl.BlockSpec(memory_space=pltpu.SEMAPHORE)
_DATAFLOW = pltpu.SideEffectType.DATAFLOW_SIDE_EFFECTING

N_CHIPS = 4
HEAD_DIM = 64
GQA_GROUP = 8
WINDOW = 128
N_BUCKETS = 32
MAX_DISTANCE = 128
POOL_WINDOWS = (2, 4, 8, 16)
POOL_HALO = 16
ATTN_BWD_KV_PER_STEP = 4
EPS = 1e-6
NEG_INF = -1e30
LANES = 128
VMEM_LIMIT = 56 * 1024 * 1024

ADAM_LR = 0.001
ADAM_B1 = 0.9
ADAM_B2 = 0.999
ADAM_EPS = 1e-08
ADAM_WD = 0.01
ADAM_STEP = 10


def _tile(n, prefs):
    for p in prefs:
        if p <= n and n % p == 0:
            return p
    return n


def _cparams(sem):
    return pltpu.CompilerParams(dimension_semantics=sem, vmem_limit_bytes=VMEM_LIMIT)


_DN = {
    "nn": (((1,), (0,)), ((), ())),
    "nt": (((1,), (1,)), ((), ())),
    "tn": (((0,), (0,)), ((), ())),
}


def _mm(name, kind, a, b, grid, a_spec, b_spec, outs, acc_shape, extras=(), epilogue=None, after=None):
    nk = grid[2]
    n_ex, n_out = len(extras), len(outs)
    order = list(after or [])

    def body(a_ref, b_ref, *rest):
        ex_refs = rest[:n_ex]
        rest = rest[n_ex + len(order):]
        out_refs = rest[:n_out]
        acc_ref = rest[n_out] if nk > 1 else None
        part = lax.dot_general(a_ref[...], b_ref[...], _DN[kind], preferred_element_type=F32)

        def finish(val):
            vals = epilogue(val, *[r[...] for r in ex_refs]) if epilogue else (val,)
            for r, v in zip(out_refs, vals):
                r[...] = v.astype(r.dtype)

        if nk == 1:
            finish(part)
        else:
            k = pl.program_id(2)

            @pl.when(k == 0)
            def _():
                acc_ref[...] = part

            @pl.when(k > 0)
            def _():
                acc_ref[...] += part

            @pl.when(k == nk - 1)
            def _():
                finish(acc_ref[...])

    res = pl.pallas_call(
        body,
        name=name,
        grid=grid,
        in_specs=[a_spec, b_spec] + [s for _, s in extras] + [_ANY] * len(order),
        out_specs=[s for _, _, s in outs],
        out_shape=[jax.ShapeDtypeStruct(sh, dt) for sh, dt, _ in outs],
        scratch_shapes=[pltpu.VMEM(acc_shape, F32)] if nk > 1 else [],
        compiler_params=_cparams(("parallel", "parallel", "arbitrary")),
    )(a, b, *[e for e, _ in extras], *order)
    return res if n_out > 1 else res[0]


def _mm_colblk(name, a, wg, out_dtype):
    S, K = a.shape
    nb, _, nc = wg.shape
    tm = _tile(S, (1024, 512))
    tn = _tile(nc, (1408, 1024, 640, 512, 256, 128))
    npb = nc // tn
    return _mm(
        name, "nn", a, wg, (nb * npb, S // tm, 1),
        pl.BlockSpec((tm, K), lambda p, q, k: (q, 0)),
        pl.BlockSpec((None, K, tn), lambda p, q, k: (p // npb, 0, p % npb)),
        [((S, nb * nc), out_dtype, pl.BlockSpec((tm, tn), lambda p, q, k: (q, p)))], (tm, tn))


def _mm_rowblk_res(name, a, wg, res):
    S = a.shape[0]
    nb, kc, N = wg.shape
    tm = _tile(S, (1024, 512))
    ktot = nb * kc

    def vmem(t):
        return 2 * 2 * (tm * ktot + ktot * t) + 4 * 4 * tm * t

    tn = _tile(N, tuple(t for t in (2048, 1024, 512, 256) if vmem(t) <= (44 << 20)))

    def body(a_ref, b_ref, r_ref, o_ref):
        acc = r_ref[...]
        for r in range(nb):
            acc = acc + jnp.dot(a_ref[:, r * kc:(r + 1) * kc], b_ref[r], preferred_element_type=F32)
        o_ref[...] = acc

    o_spec = pl.BlockSpec((tm, tn), lambda i, j: (i, j))
    return pl.pallas_call(
        body, name=name, grid=(S // tm, N // tn),
        in_specs=[pl.BlockSpec((tm, nb * kc), lambda i, j: (i, 0)),
                  pl.BlockSpec((nb, kc, tn), lambda i, j: (0, 0, j)), o_spec],
        out_specs=o_spec,
        out_shape=jax.ShapeDtypeStruct((S, N), F32),
        compiler_params=_cparams(("parallel", "parallel")),
    )(a, wg, res)


def _mm_nt_rowblk(name, g, wg, out_dtype, after=None):
    S, N = g.shape
    nb, kc, _ = wg.shape
    tm = _tile(S, (1024, 512))
    tn = _tile(kc, (1408, 512, 256, 128))
    kpb = kc // tn
    return _mm(
        name, "nt", g, wg, (nb * kpb, S // tm, 1),
        pl.BlockSpec((tm, N), lambda p, q, k: (q, 0)),
        pl.BlockSpec((None, tn, N), lambda p, q, k: (p // kpb, p % kpb, 0)),
        [((S, nb * kc), out_dtype, pl.BlockSpec((tm, tn), lambda p, q, k: (q, p)))], (tm, tn), after=after)


def _mm_nt_colblk(name, g, wg):
    S = g.shape[0]
    nb, K, nc = wg.shape
    tm = _tile(S, (512,))

    def body(g_ref, b_ref, o_ref):
        acc = lax.dot_general(g_ref[:, 0:nc], b_ref[0], _DN["nt"], preferred_element_type=F32)
        for r in range(1, nb):
            acc = acc + lax.dot_general(g_ref[:, r * nc:(r + 1) * nc], b_ref[r], _DN["nt"],
                                        preferred_element_type=F32)
        o_ref[...] = acc

    return pl.pallas_call(
        body, name=name, grid=(S // tm,),
        in_specs=[pl.BlockSpec((tm, nb * nc), lambda i: (i, 0)), pl.BlockSpec((nb, K, nc), lambda i: (0, 0, 0))],
        out_specs=pl.BlockSpec((tm, K), lambda i: (i, 0)),
        out_shape=jax.ShapeDtypeStruct((S, K), F32),
        compiler_params=_cparams(("parallel",)),
    )(g, wg)


def _mm_tn_colblk(name, a, g, nb, split2=False):
    S, K = a.shape
    ntot = g.shape[-1] * (2 if split2 else 1)
    nc = ntot // nb
    ti = _tile(K, (1024,))
    tn = _tile(nc, (1408, 640, 512, 256, 128))
    ts = _tile(S, (2048, 1024, 512, 256))
    npb = nc // tn
    half = nb * npb // 2
    if split2:
        b_spec = pl.BlockSpec((None, ts, tn), lambda p, q, k: (q // half, k, q % half))
    else:
        b_spec = pl.BlockSpec((ts, tn), lambda p, q, k: (k, q))
    return _mm(
        name, "tn", a, g, (K // ti, nb * npb, S // ts),
        pl.BlockSpec((ts, ti), lambda p, q, k: (k, p)),
        b_spec,
        [((nb, K, nc), BF16, pl.BlockSpec((None, ti, tn), lambda p, q, k: (q // npb, p, q % npb)))], (ti, tn))


def _mm_tn_rowblk(name, a, g, nb):
    S, ktot = a.shape
    N = g.shape[1]
    kc = ktot // nb
    ti = _tile(kc, (1408, 512, 256, 128))
    tj = _tile(N, (1024,))
    ts = _tile(S, (2048, 1024, 512, 256))
    ipb = kc // ti
    return _mm(
        name, "tn", a, g, (nb * ipb, N // tj, S // ts),
        pl.BlockSpec((ts, ti), lambda p, q, k: (k, p)),
        pl.BlockSpec((ts, tj), lambda p, q, k: (k, q)),
        [((nb, kc, N), BF16, pl.BlockSpec((None, ti, tj), lambda p, q, k: (p // ipb, p % ipb, q)))], (ti, tj))


def _rms_fwd(name, x, gain):
    S, D = x.shape
    tm = _tile(S, (512,))

    def body(x_ref, g_ref, o_ref):
        xv = x_ref[...]
        r = lax.rsqrt(jnp.mean(xv * xv, axis=-1, keepdims=True) + EPS)
        o_ref[...] = (xv * r * g_ref[...]).astype(o_ref.dtype)

    return pl.pallas_call(
        body, name=name, grid=(S // tm,),
        in_specs=[pl.BlockSpec((tm, D), lambda i: (i, 0)), pl.BlockSpec((1, D), lambda i: (0, 0))],
        out_specs=pl.BlockSpec((tm, D), lambda i: (i, 0)),
        out_shape=jax.ShapeDtypeStruct((S, D), BF16),
        compiler_params=_cparams(("parallel",)),
    )(x, gain)


def _rms_bwd_math(xv, gain, dh):
    r = lax.rsqrt(jnp.mean(xv * xv, axis=-1, keepdims=True) + EPS)
    xhat = xv * r
    dxhat = dh * gain
    c = jnp.mean(dxhat * xhat, axis=-1, keepdims=True)
    return r * (dxhat - xhat * c), dh * xhat


def _rows_to_8(v):
    tm, C = v.shape
    return jnp.sum(v.reshape(tm // 8, 8, C), axis=0)


def _rms_bwd(name, x, gain, dh, dres, after=None):
    S, D = x.shape
    tm = _tile(S, (256,))
    n = S // tm
    order = list(after or [])

    def body(x_ref, g_ref, dh_ref, dr_ref, *rest):
        dx_ref, dxb_ref, dg_ref, acc_ref = rest[len(order):]
        i = pl.program_id(0)
        dxn, dgr = _rms_bwd_math(x_ref[...], g_ref[...], dh_ref[...])
        dx = dr_ref[...] + dxn
        dx_ref[...] = dx
        dxb_ref[...] = dx.astype(BF16)

        @pl.when(i == 0)
        def _():
            acc_ref[...] = jnp.zeros_like(acc_ref)

        acc_ref[...] += _rows_to_8(dgr)

        @pl.when(i == n - 1)
        def _():
            dg_ref[...] = jnp.sum(acc_ref[...], axis=0, keepdims=True)

    row = pl.BlockSpec((tm, D), lambda i: (i, 0))
    vec = pl.BlockSpec((1, D), lambda i: (0, 0))
    return pl.pallas_call(
        body, name=name, grid=(n,),
        in_specs=[row, vec, row, row] + [_ANY] * len(order),
        out_specs=[row, row, vec],
        out_shape=[jax.ShapeDtypeStruct((S, D), F32), jax.ShapeDtypeStruct((S, D), BF16),
                   jax.ShapeDtypeStruct((1, D), F32)],
        scratch_shapes=[pltpu.VMEM((8, D), F32)],
        compiler_params=_cparams(("arbitrary",)),
    )(x, gain, dh, dres, *order)


def _loss_head(y, tgt):
    S, D = y.shape
    tm = _tile(S, (256,))
    n = S // tm

    def body(y_ref, t_ref, dy_ref, dyb_ref, l_ref):
        i = pl.program_id(0)
        e = y_ref[...] - t_ref[...]
        dy = e * (1.0 / D)
        dy_ref[...] = dy
        dyb_ref[...] = dy.astype(BF16)

        @pl.when(i == 0)
        def _():
            l_ref[...] = jnp.zeros_like(l_ref)

        sq = _rows_to_8(e * e)
        part = sq[:, 0:LANES]
        for t in range(1, D // LANES):
            part = part + sq[:, t * LANES:(t + 1) * LANES]
        l_ref[...] += part * (0.5 / D)

    row = pl.BlockSpec((tm, D), lambda i: (i, 0))
    return pl.pallas_call(
        body, name="loss_head", grid=(n,),
        in_specs=[row, row],
        out_specs=[row, row, pl.BlockSpec((8, LANES), lambda i: (0, 0))],
        out_shape=[jax.ShapeDtypeStruct((S, D), F32), jax.ShapeDtypeStruct((S, D), BF16),
                   jax.ShapeDtypeStruct((8, LANES), F32)],
        compiler_params=_cparams(("arbitrary",)),
    )(y, tgt)


def _up_act_fwd(h2, wg, cw, cb):
    S, K = h2.shape
    nb, _, nc = wg.shape
    Fh = nb * nc // 2
    tm = _tile(S, (2048, 1024, 512))
    tn = _tile(nc, (256, 128))
    sub = _tile(tm, (512,))
    ch = _tile(sub, (32,))
    npb = nc // tn
    hb = nb // 2
    ncol = Fh // tn

    def body(a_ref, bg_ref, bv_ref, cw_ref, cb_ref, u_ref, uc_ref, o_ref, buf_ref, halo_ref):
        j = pl.program_id(1)

        @pl.when(pl.program_id(0) == 0)
        def _():
            halo_ref[j] = jnp.zeros((2, 8, tn), F32)

        buf_ref[:, 0:8, :] = halo_ref[j]
        cws, cbs = [cw_ref[0], cw_ref[1]], [cb_ref[0], cb_ref[1]]
        for r in range(tm // sub):
            for s, b_ref in enumerate((bg_ref, bv_ref)):
                buf_ref[s, 8 + r * sub:8 + (r + 1) * sub, :] = jnp.dot(
                    a_ref[r * sub:(r + 1) * sub, :], b_ref[...], preferred_element_type=F32)
            for c in range(sub // ch):
                base = r * sub + c * ch
                ucs = []
                for s in range(2):
                    ext = buf_ref[s, base:base + ch + 8, :]
                    us = ext[8:]
                    uc = (cws[s][0:1] * pltpu.roll(ext, 2, 0)[8:] + cws[s][1:2] * pltpu.roll(ext, 1, 0)[8:]
                          + cws[s][2:3] * us + cbs[s])
                    u_ref[s, base:base + ch, :] = us.astype(BF16)
                    uc_ref[s, base:base + ch, :] = uc.astype(BF16)
                    ucs.append(uc)
                gate, val = ucs
                o_ref[base:base + ch, :] = (gate * (1.0 / (1.0 + jnp.exp(-gate))) * val).astype(BF16)
        halo_ref[j] = buf_ref[:, tm:tm + 8, :]

    both = pl.BlockSpec((2, tm, tn), lambda i, j: (0, i, j))
    return pl.pallas_call(
        body, name="up_act_fwd", grid=(S // tm, ncol),
        in_specs=[pl.BlockSpec((tm, K), lambda i, j: (i, 0)),
                  pl.BlockSpec((None, K, tn), lambda i, j: (j // npb, 0, j % npb)),
                  pl.BlockSpec((None, K, tn), lambda i, j: (j // npb + hb, 0, j % npb)),
                  pl.BlockSpec((2, 3, tn), lambda i, j: (0, 0, j)),
                  pl.BlockSpec((2, 1, tn), lambda i, j: (0, 0, j))],
        out_specs=[both, both, pl.BlockSpec((tm, tn), lambda i, j: (i, j))],
        out_shape=[jax.ShapeDtypeStruct((2, S, Fh), BF16), jax.ShapeDtypeStruct((2, S, Fh), BF16),
                   jax.ShapeDtypeStruct((S, Fh), BF16)],
        scratch_shapes=[pltpu.VMEM((2, tm + 8, tn), F32), pltpu.VMEM((ncol, 2, 8, tn), F32)],
        compiler_params=_cparams(("arbitrary", "arbitrary")),
    )(h2, wg, wg, cw, cb)


def _down_act_bwd(dxb, wg, uc, after=None):
    S, D = dxb.shape
    nb, kc, _ = wg.shape
    Fh = nb * kc
    tm = _tile(S, (1024, 512))
    sub = _tile(tm, (256,))
    ch = _tile(sub, (32,))
    t128 = kc // LANES
    tn = 2 * LANES
    order = list(after or [])

    def body(dx_ref, b0_ref, b1_ref, uc_ref, *rest):
        duc_ref, bcat_ref, da_ref = rest[len(order):]
        bcat_ref[0:LANES, :] = b0_ref[...]
        bcat_ref[LANES:tn, :] = b1_ref[...]
        for r in range(tm // sub):
            rows = slice(r * sub, (r + 1) * sub)
            da_ref[rows, :] = lax.dot_general(dx_ref[rows, :], bcat_ref[...], _DN["nt"], preferred_element_type=F32)
            for c in range(sub // ch):
                cs = slice(r * sub + c * ch, r * sub + (c + 1) * ch)
                da = da_ref[cs, :]
                gate, val = uc_ref[0, cs, :].astype(F32), uc_ref[1, cs, :].astype(F32)
                sig = 1.0 / (1.0 + jnp.exp(-gate))
                duc_ref[0, cs, :] = (da * val * (sig * (1.0 + gate * (1.0 - sig)))).astype(BF16)
                duc_ref[1, cs, :] = (da * (gate * sig)).astype(BF16)

    both = pl.BlockSpec((2, tm, tn), lambda i, j: (0, i, j))
    return pl.pallas_call(
        body, name="down_act_bwd", grid=(S // tm, Fh // tn),
        in_specs=[pl.BlockSpec((tm, D), lambda i, j: (i, 0)),
                  pl.BlockSpec((None, LANES, D), lambda i, j: ((2 * j) // t128, (2 * j) % t128, 0)),
                  pl.BlockSpec((None, LANES, D), lambda i, j: ((2 * j + 1) // t128, (2 * j + 1) % t128, 0)),
                  both] + [_ANY] * len(order),
        out_specs=both,
        out_shape=jax.ShapeDtypeStruct((2, S, Fh), BF16),
        scratch_shapes=[pltpu.VMEM((tn, D), BF16), pltpu.VMEM((tm, tn), F32)],
        compiler_params=_cparams(("parallel", "parallel")),
    )(dxb, wg, wg, uc, *order)


def _up_bwd_in(duc, u, cw, wg):
    _, S, Fh = duc.shape
    nb, K, nc = wg.shape
    tm = _tile(S, (1024, 512))
    tk = _tile(nc, (1408, 640, 512, 256, 128))
    sub = _tile(tm, (256,))
    ch = _tile(sub, (64,))
    hr = 16
    npb = nc // tk
    nk = nb * npb
    half = nk // 2
    n, nh = S // tm, tm // hr

    def body(d_ref, h_ref, u_ref, cw_ref, b_ref, du_ref, o_ref, cg_ref):
        i, k = pl.program_id(0), pl.program_id(1)
        keep = (i < n - 1).astype(F32)

        @pl.when(i == 0)
        def _():
            cg_ref[k] = jnp.zeros((8, tk), F32)

        @pl.when(k == 0)
        def _():
            o_ref[...] = jnp.zeros_like(o_ref)

        for r in range(tm // sub):
            for l in range(tk // LANES):
                ls = slice(l * LANES, (l + 1) * LANES)
                w = cw_ref[:, ls]
                sums = [jnp.zeros((8, LANES), F32) for _ in range(4)]
                for c in range(sub // ch):
                    base = r * sub + c * ch
                    if base + ch + hr <= tm:
                        ext = d_ref[base:base + ch + hr, ls].astype(F32)
                    else:
                        ext = jnp.concatenate([d_ref[base:base + ch, ls].astype(F32),
                                               h_ref[:, ls].astype(F32) * keep], axis=0)
                    d = ext[:ch]
                    d1 = pltpu.roll(ext, ch + hr - 1, 0)[:ch]
                    d2 = pltpu.roll(ext, ch + hr - 2, 0)[:ch]
                    du_ref[base:base + ch, ls] = (w[2:3] * d + w[1:2] * d1 + w[0:1] * d2).astype(BF16)
                    uv = u_ref[base:base + ch, ls].astype(F32)
                    for q, v in enumerate((d2 * uv, d1 * uv, d * uv, d)):
                        sums[q] = sums[q] + _rows_to_8(v)
                for q in range(4):
                    cg_ref[k, q:q + 1, ls] += jnp.sum(sums[q], axis=0, keepdims=True)
            rows = slice(r * sub, (r + 1) * sub)
            o_ref[rows, :] += lax.dot_general(du_ref[rows, :], b_ref[...], _DN["nt"], preferred_element_type=F32)

    blk = pl.BlockSpec((None, tm, tk), lambda i, k: (k // half, i, k % half))
    return pl.pallas_call(
        body, name="up_bwd_in", grid=(n, nk),
        in_specs=[blk,
                  pl.BlockSpec((None, hr, tk), lambda i, k: (k // half, jnp.minimum((i + 1) * nh, S // hr - 1), k % half)),
                  blk,
                  pl.BlockSpec((None, 3, tk), lambda i, k: (k // half, 0, k % half)),
                  pl.BlockSpec((None, K, tk), lambda i, k: (k // npb, 0, k % npb))],
        out_specs=[blk, pl.BlockSpec((tm, K), lambda i, k: (i, 0)), pl.BlockSpec((nk, 8, tk), lambda i, k: (0, 0, 0))],
        out_shape=[jax.ShapeDtypeStruct((2, S, Fh), BF16), jax.ShapeDtypeStruct((S, K), F32),
                   jax.ShapeDtypeStruct((nk, 8, tk), F32)],
        compiler_params=_cparams(("arbitrary", "arbitrary")),
    )(duc, duc, u, cw, wg)


def _pool_counts(i, tm, w):
    t = i * tm + lax.broadcasted_iota(jnp.int32, (tm, 1), 0)
    return jnp.minimum(t + 1, w).astype(F32)


def _pool_fwd(x, gain):
    S, D = x.shape
    tm = _tile(S, (256,))
    gd = D // len(POOL_WINDOWS)
    nh = tm // POOL_HALO

    def body(x_ref, xh_ref, g_ref, d_ref):
        i = pl.program_id(0)

        def norm(v):
            return v * lax.rsqrt(jnp.mean(v * v, axis=-1, keepdims=True) + EPS) * g_ref[...]

        h = norm(x_ref[...])
        hh = norm(xh_ref[...]) * (i > 0).astype(F32)
        ext = jnp.concatenate([hh, h], axis=0)
        for gi, w in enumerate(POOL_WINDOWS):
            sl = slice(gi * gd, (gi + 1) * gd)
            win = ext[:, sl]
            k = 1
            while k < w:
                win = win + pltpu.roll(win, k, 0)
                k *= 2
            mean = win[POOL_HALO:] / _pool_counts(i, tm, w)
            d_ref[:, sl] = (mean - h[:, sl]).astype(BF16)

    return pl.pallas_call(
        body, name="pool_fwd", grid=(S // tm,),
        in_specs=[pl.BlockSpec((tm, D), lambda i: (i, 0)),
                  pl.BlockSpec((POOL_HALO, D), lambda i: (jnp.maximum(i * nh - 1, 0), 0)),
                  pl.BlockSpec((1, D), lambda i: (0, 0))],
        out_specs=pl.BlockSpec((tm, D), lambda i: (i, 0)),
        out_shape=jax.ShapeDtypeStruct((S, D), BF16),
        compiler_params=_cparams(("parallel",)),
    )(x, x, gain)


def _pool_mm_fwd(d, wp, x, scale):
    S, D = d.shape
    G, gd, _ = wp.shape
    tm = _tile(S, (1024, 512))
    o_spec = pl.BlockSpec((tm, gd), lambda p, q, k: (p, q))
    return _mm(
        "pool_mm_fwd", "nn", d, wp, (S // tm, G, 1),
        o_spec, pl.BlockSpec((None, gd, gd), lambda p, q, k: (q, 0, 0)),
        [((S, D), F32, o_spec), ((S, D), F32, o_spec)], (tm, gd),
        extras=[(x, o_spec), (scale, pl.BlockSpec((1, gd), lambda p, q, k: (0, q)))],
        epilogue=lambda acc, xv, sc: (acc, xv + acc * sc))


def _pool_bwd_pre(dx1, ypre, scale):
    S, D = dx1.shape
    tm = _tile(S, (256,))
    n = S // tm

    def body(dx_ref, y_ref, s_ref, dy_ref, ds_ref, acc_ref):
        i = pl.program_id(0)
        dx = dx_ref[...]
        dy_ref[...] = (dx * s_ref[...]).astype(BF16)

        @pl.when(i == 0)
        def _():
            acc_ref[...] = jnp.zeros_like(acc_ref)

        acc_ref[...] += _rows_to_8(dx * y_ref[...])

        @pl.when(i == n - 1)
        def _():
            ds_ref[...] = jnp.sum(acc_ref[...], axis=0, keepdims=True)

    row = pl.BlockSpec((tm, D), lambda i: (i, 0))
    vec = pl.BlockSpec((1, D), lambda i: (0, 0))
    return pl.pallas_call(
        body, name="pool_bwd_pre", grid=(n,),
        in_specs=[row, row, vec], out_specs=[row, vec],
        out_shape=[jax.ShapeDtypeStruct((S, D), BF16), jax.ShapeDtypeStruct((1, D), F32)],
        scratch_shapes=[pltpu.VMEM((8, D), F32)],
        compiler_params=_cparams(("arbitrary",)),
    )(dx1, ypre, scale)


def _pool_mm_bwd_in(dyp, wp):
    S, D = dyp.shape
    G, gd, _ = wp.shape
    tm = _tile(S, (1024, 512))
    spec = pl.BlockSpec((tm, gd), lambda p, q, k: (p, q))
    return _mm(
        "pool_mm_bwd_in", "nt", dyp, wp, (S // tm, G, 1),
        spec, pl.BlockSpec((None, gd, gd), lambda p, q, k: (q, 0, 0)),
        [((S, D), F32, spec)], (tm, gd))


def _pool_mm_bwd_w(d, dyp):
    S, D = d.shape
    G = len(POOL_WINDOWS)
    gd = D // G
    ts = _tile(S, (1024, 512, 256))
    spec = pl.BlockSpec((ts, gd), lambda p, q, k: (k, p))
    return _mm(
        "pool_mm_bwd_w", "tn", d, dyp, (G, 1, S // ts),
        spec, spec,
        [((G, gd, gd), BF16, pl.BlockSpec((None, gd, gd), lambda p, q, k: (p, 0, 0)))], (gd, gd))


def _pool_bwd_post(dd, x, gain, dres):
    S, D = x.shape
    tm = _tile(S, (256,))
    n = S // tm
    gd = D // len(POOL_WINDOWS)
    nh = tm // POOL_HALO

    def body(dd_ref, ddh_ref, x_ref, g_ref, dr_ref, dx_ref, dxb_ref, dg_ref, acc_ref):
        i = pl.program_id(0)
        dd = dd_ref[...]
        halo = ddh_ref[...] * (i < n - 1).astype(F32)
        parts = []
        for gi, w in enumerate(POOL_WINDOWS):
            sl = slice(gi * gd, (gi + 1) * gd)
            ext = jnp.concatenate([dd[:, sl] / _pool_counts(i, tm, w), halo[:, sl] * (1.0 / w)], axis=0)
            k = 1
            while k < w:
                ext = ext + pltpu.roll(ext, tm + POOL_HALO - k, 0)
                k *= 2
            parts.append(ext[:tm] - dd[:, sl])
        dh = jnp.concatenate(parts, axis=1)
        dxn, dgr = _rms_bwd_math(x_ref[...], g_ref[...], dh)
        dx = dr_ref[...] + dxn
        dx_ref[...] = dx
        dxb_ref[...] = dx.astype(BF16)

        @pl.when(i == 0)
        def _():
            acc_ref[...] = jnp.zeros_like(acc_ref)

        acc_ref[...] += _rows_to_8(dgr)

        @pl.when(i == n - 1)
        def _():
            dg_ref[...] = jnp.sum(acc_ref[...], axis=0, keepdims=True)

    row = pl.BlockSpec((tm, D), lambda i: (i, 0))
    vec = pl.BlockSpec((1, D), lambda i: (0, 0))
    nxt = pl.BlockSpec((POOL_HALO, D), lambda i: (jnp.minimum((i + 1) * nh, S // POOL_HALO - 1), 0))
    return pl.pallas_call(
        body, name="pool_bwd_post", grid=(n,),
        in_specs=[row, nxt, row, vec, row],
        out_specs=[row, row, vec],
        out_shape=[jax.ShapeDtypeStruct((S, D), F32), jax.ShapeDtypeStruct((S, D), BF16),
                   jax.ShapeDtypeStruct((1, D), F32)],
        scratch_shapes=[pltpu.VMEM((8, D), F32)],
        compiler_params=_cparams(("arbitrary",)),
    )(dd, dd, x, gain, dres)


def _band_tables():
    i = np.arange(WINDOW)[:, None]
    j = np.arange(2 * WINDOW)[None, :]
    n = np.maximum(WINDOW + i - j, 0)
    max_exact = N_BUCKETS // 2
    nf = np.maximum(n, 1).astype(np.float32)
    large = max_exact + (np.log(nf / max_exact) / np.log(MAX_DISTANCE / max_exact)
                         * (N_BUCKETS - max_exact)).astype(np.int32)
    large = np.minimum(large, N_BUCKETS - 1)
    buckets = np.where(n < max_exact, n, large).astype(np.int32)
    dist = WINDOW + i - j
    in_win = ((dist >= 0) & (dist < WINDOW)).astype(np.int32)
    return buckets, in_win


def _bias_expand(rel_bias):
    H = rel_bias.shape[0]
    buckets, in_win = _band_tables()

    def body(rb_ref, bk_ref, win_ref, o_ref):
        h = pl.program_id(0)
        bk = bk_ref[...]
        acc = jnp.zeros(bk.shape, F32)
        for b in range(N_BUCKETS):
            acc = jnp.where(bk == b, rb_ref[h, b], acc)
        o_ref[...] = jnp.where(win_ref[...] > 0, acc, NEG_INF)

    full = pl.BlockSpec((WINDOW, 2 * WINDOW), lambda h: (0, 0))
    return pl.pallas_call(
        body, name="bias_expand", grid=(H,),
        in_specs=[pl.BlockSpec(memory_space=pltpu.SMEM), full, full],
        out_specs=pl.BlockSpec((None, WINDOW, 2 * WINDOW), lambda h: (h, 0, 0)),
        out_shape=jax.ShapeDtypeStruct((H, WINDOW, 2 * WINDOW), F32),
        compiler_params=_cparams(("parallel",)),
    )(rel_bias, jnp.asarray(buckets), jnp.asarray(in_win))


def _bias_reduce(dbias):
    H = dbias.shape[0]
    buckets, in_win = _band_tables()

    def body(d_ref, bk_ref, win_ref, o_ref):
        bk = jnp.where(win_ref[...] > 0, bk_ref[...], -1)
        d = d_ref[...]
        lane = lax.broadcasted_iota(jnp.int32, (8, LANES), 1)
        out = jnp.zeros((8, LANES), F32)
        for b in range(N_BUCKETS):
            s = jnp.sum(jnp.where(bk == b, d, 0.0))
            out = jnp.where(lane == b, s, out)
        o_ref[...] = out

    full = pl.BlockSpec((WINDOW, 2 * WINDOW), lambda h: (0, 0))
    return pl.pallas_call(
        body, name="bias_reduce", grid=(H,),
        in_specs=[pl.BlockSpec((None, WINDOW, 2 * WINDOW), lambda h: (h, 0, 0)), full, full],
        out_specs=pl.BlockSpec((None, 8, LANES), lambda h: (h, 0, 0)),
        out_shape=jax.ShapeDtypeStruct((H, 8, LANES), F32),
        compiler_params=_cparams(("parallel",)),
    )(dbias, jnp.asarray(buckets), jnp.asarray(in_win))


def _half_rsqrt(t, lo):
    sq = t * t
    s_lo = jnp.sum(jnp.where(lo, sq, 0.0), axis=-1, keepdims=True)
    s_hi = jnp.sum(jnp.where(lo, 0.0, sq), axis=-1, keepdims=True)
    return jnp.where(lo, lax.rsqrt(s_lo * (1.0 / HEAD_DIM) + EPS), lax.rsqrt(s_hi * (1.0 / HEAD_DIM) + EPS))


def _qk_norm_fwd(qkv, qg, kg, n_q, n_kv):
    S, W = qkv.shape
    tm = _tile(S, (256,))
    qw = n_q * HEAD_DIM
    kw = n_kv * HEAD_DIM
    scale = HEAD_DIM ** -0.5

    def body(x_ref, qg_ref, kg_ref, q_ref, k_ref, v_ref):
        lo = lax.broadcasted_iota(jnp.int32, (tm, LANES), 1) < HEAD_DIM
        for t in range(qw // LANES):
            q = x_ref[:, t * LANES:(t + 1) * LANES]
            q_ref[:, t * LANES:(t + 1) * LANES] = (q * _half_rsqrt(q, lo) * qg_ref[...] * scale).astype(BF16)
        for p in range(kw // LANES):
            k = x_ref[:, qw + p * LANES:qw + (p + 1) * LANES]
            kn = k * _half_rsqrt(k, lo) * kg_ref[...]
            v = x_ref[:, qw + kw + p * LANES:qw + kw + (p + 1) * LANES]
            for src, dst in ((kn, k_ref), (v, v_ref)):
                rolled = pltpu.roll(src, HEAD_DIM, 1)
                dst[2 * p] = jnp.where(lo, src, rolled).astype(BF16)
                dst[2 * p + 1] = jnp.where(lo, rolled, src).astype(BF16)

    vec = pl.BlockSpec((1, LANES), lambda i: (0, 0))
    kv = pl.BlockSpec((n_kv, tm, LANES), lambda i: (0, i, 0))
    return pl.pallas_call(
        body, name="qk_norm_fwd", grid=(S // tm,),
        in_specs=[pl.BlockSpec((tm, W), lambda i: (i, 0)), vec, vec],
        out_specs=[pl.BlockSpec((tm, qw), lambda i: (i, 0)), kv, kv],
        out_shape=[jax.ShapeDtypeStruct((S, qw), BF16), jax.ShapeDtypeStruct((n_kv, S, LANES), BF16),
                   jax.ShapeDtypeStruct((n_kv, S, LANES), BF16)],
        compiler_params=_cparams(("parallel",)),
    )(qkv, qg, kg)


def _qk_norm_bwd(qkv, qg, kg, dq, dkc, dkp, dvc, dvp, n_q, n_kv):
    S, W = qkv.shape
    tm = _tile(S, (4 * WINDOW, WINDOW))
    n = S // tm
    per = tm // WINDOW
    qw = n_q * HEAD_DIM
    kw = n_kv * HEAD_DIM
    scale = HEAD_DIM ** -0.5

    def body(x_ref, qg_ref, kg_ref, dq_ref, dkc_ref, dkp_ref, dkn_ref, dvc_ref, dvp_ref, dvn_ref,
             dx_ref, dqg_ref, dkg_ref, accq_ref, acck_ref):
        i = pl.program_id(0)
        lo = lax.broadcasted_iota(jnp.int32, (tm, LANES), 1) < HEAD_DIM
        has_next = (i < n - 1).astype(F32)

        @pl.when(i == 0)
        def _():
            accq_ref[...] = jnp.zeros_like(accq_ref)
            acck_ref[...] = jnp.zeros_like(acck_ref)

        def norm_bwd(t, dy, gain):
            r = _half_rsqrt(t, lo)
            xhat = t * r
            dxhat = dy * gain
            prod = dxhat * xhat
            c_lo = jnp.sum(jnp.where(lo, prod, 0.0), axis=-1, keepdims=True) * (1.0 / HEAD_DIM)
            c_hi = jnp.sum(jnp.where(lo, 0.0, prod), axis=-1, keepdims=True) * (1.0 / HEAD_DIM)
            return r * (dxhat - xhat * jnp.where(lo, c_lo, c_hi)), _rows_to_8(dy * xhat)

        for t in range(qw // LANES):
            sl = slice(t * LANES, (t + 1) * LANES)
            dt, dg = norm_bwd(x_ref[:, sl], dq_ref[:, sl] * scale, qg_ref[...])
            dx_ref[:, sl] = dt.astype(BF16)
            accq_ref[...] += dg

        def pair(cur_ref, prev_ref, next_ref, p):
            folded = []
            for h in (2 * p, 2 * p + 1):
                later = next_ref[h] * has_next
                if per > 1:
                    later = jnp.concatenate([prev_ref[h, WINDOW:, :], later], axis=0)
                tot = cur_ref[h] + later
                folded.append(tot + pltpu.roll(tot, HEAD_DIM, 1))
            return jnp.where(lo, folded[0], folded[1])

        for p in range(kw // LANES):
            sl = slice(qw + p * LANES, qw + (p + 1) * LANES)
            dt, dg = norm_bwd(x_ref[:, sl], pair(dkc_ref, dkp_ref, dkn_ref, p), kg_ref[...])
            dx_ref[:, sl] = dt.astype(BF16)
            acck_ref[...] += dg
            sl = slice(qw + kw + p * LANES, qw + kw + (p + 1) * LANES)
            dx_ref[:, sl] = pair(dvc_ref, dvp_ref, dvn_ref, p).astype(BF16)

        @pl.when(i == n - 1)
        def _():
            for acc_ref, o_ref in ((accq_ref, dqg_ref), (acck_ref, dkg_ref)):
                s = jnp.sum(acc_ref[...], axis=0, keepdims=True)
                o_ref[...] = s + pltpu.roll(s, HEAD_DIM, 1)

    vec = pl.BlockSpec((1, LANES), lambda i: (0, 0))
    cur = pl.BlockSpec((n_kv, tm, LANES), lambda i: (0, i, 0))
    nxt = pl.BlockSpec((n_kv, WINDOW, LANES), lambda i: (0, jnp.minimum((i + 1) * per, S // WINDOW - 1), 0))
    return pl.pallas_call(
        body, name="qk_norm_bwd", grid=(n,),
        in_specs=[pl.BlockSpec((tm, W), lambda i: (i, 0)), vec, vec, pl.BlockSpec((tm, qw), lambda i: (i, 0)),
                  cur, cur, nxt, cur, cur, nxt],
        out_specs=[pl.BlockSpec((tm, W), lambda i: (i, 0)), vec, vec],
        out_shape=[jax.ShapeDtypeStruct((S, W), BF16), jax.ShapeDtypeStruct((1, LANES), F32),
                   jax.ShapeDtypeStruct((1, LANES), F32)],
        scratch_shapes=[pltpu.VMEM((8, LANES), F32), pltpu.VMEM((8, LANES), F32)],
        compiler_params=_cparams(("arbitrary",)),
    )(qkv, qg, kg, dq, dkc, dkp, dkp, dvc, dvp, dvp)


def _attn_specs(n_kv, per_step):
    kvs = per_step if n_kv % per_step == 0 else 1
    q = pl.BlockSpec((WINDOW, kvs * GQA_GROUP * HEAD_DIM), lambda kh, n: (n, kh))
    cur = pl.BlockSpec((kvs, WINDOW, LANES), lambda kh, n: (kh, n, 0))
    prev = pl.BlockSpec((kvs, WINDOW, LANES), lambda kh, n: (kh, jnp.maximum(n - 1, 0), 0))
    bias = pl.BlockSpec((kvs * GQA_GROUP, WINDOW, 2 * WINDOW), lambda kh, n: (kh, 0, 0))
    sink = pl.BlockSpec(memory_space=pltpu.SMEM)
    return kvs, q, cur, prev, bias, sink


def _stack_heads(x_ref, lo, h):
    parts = []
    for g in range(GQA_GROUP):
        c0 = (h * GQA_GROUP // 2 + g // 2) * LANES
        t = x_ref[:, c0:c0 + LANES]
        parts.append(jnp.where(lo if g % 2 == 0 else jnp.logical_not(lo), t, jnp.zeros_like(t)))
    return jnp.concatenate(parts, axis=0)


def _unstack_heads(v, lo):
    return [jnp.where(lo, v[(2 * p) * WINDOW:(2 * p + 1) * WINDOW], v[(2 * p + 1) * WINDOW:(2 * p + 2) * WINDOW])
            for p in range(GQA_GROUP // 2)]


def _attn_probs(qs, kk, bias, s_ref, head0, has_prev):
    rows = GQA_GROUP * WINDOW
    s = lax.dot_general(qs, kk, _DN["nt"], preferred_element_type=F32) + bias.reshape(rows, 2 * WINDOW)
    col = lax.broadcasted_iota(jnp.int32, (rows, 2 * WINDOW), 1)
    s = jnp.where(jnp.logical_or(has_prev, col >= WINDOW), s, NEG_INF)
    sink = jnp.concatenate([jnp.full((WINDOW, 1), s_ref[head0 + g], F32) for g in range(GQA_GROUP)], axis=0)
    m = jnp.maximum(jnp.max(s, axis=-1, keepdims=True), sink)
    p = jnp.exp(s - m)
    es = jnp.exp(sink - m)
    den = jnp.sum(p, axis=-1, keepdims=True) + es
    return p / den, es / den


def _attn_fwd(q, kcat, vcat, biasm, sinks):
    S, qw = q.shape
    n_kv = kcat.shape[0]
    kvs, qs, cur, prev, bias, sink = _attn_specs(n_kv, 4)
    pairs = GQA_GROUP // 2

    def body(q_ref, kc_ref, kp_ref, vc_ref, vp_ref, b_ref, s_ref, o_ref):
        kh, n = pl.program_id(0), pl.program_id(1)
        lo = lax.broadcasted_iota(jnp.int32, (WINDOW, LANES), 1) < HEAD_DIM
        for h in range(kvs):
            kk = jnp.concatenate([kp_ref[h], kc_ref[h]], axis=0)
            vv = jnp.concatenate([vp_ref[h], vc_ref[h]], axis=0)
            pn, _ = _attn_probs(_stack_heads(q_ref, lo, h), kk, b_ref[h * GQA_GROUP:(h + 1) * GQA_GROUP], s_ref,
                                (kh * kvs + h) * GQA_GROUP, n > 0)
            o = jnp.dot(pn.astype(BF16), vv, preferred_element_type=F32)
            for p, t in enumerate(_unstack_heads(o, lo)):
                o_ref[:, (h * pairs + p) * LANES:(h * pairs + p + 1) * LANES] = t.astype(BF16)

    return pl.pallas_call(
        body, name="attn_fwd", grid=(n_kv // kvs, S // WINDOW),
        in_specs=[qs, cur, prev, cur, prev, bias, sink],
        out_specs=qs,
        out_shape=jax.ShapeDtypeStruct((S, qw), BF16),
        compiler_params=_cparams(("parallel", "parallel")),
    )(q, kcat, kcat, vcat, vcat, biasm, sinks)


def _attn_bwd(q, kcat, vcat, biasm, sinks, do):
    S, qw = q.shape
    n_kv = kcat.shape[0]
    H = n_kv * GQA_GROUP
    kvs, qs, cur, prev, bias, sink = _attn_specs(n_kv, ATTN_BWD_KV_PER_STEP)
    pairs = GQA_GROUP // 2

    def body(q_ref, kc_ref, kp_ref, vc_ref, vp_ref, b_ref, s_ref, do_ref,
             dq_ref, dkc_ref, dkp_ref, dvc_ref, dvp_ref, db_ref, ds_ref):
        kh, n = pl.program_id(0), pl.program_id(1)
        lo = lax.broadcasted_iota(jnp.int32, (WINDOW, LANES), 1) < HEAD_DIM

        @pl.when(n == 0)
        def _():
            db_ref[...] = jnp.zeros_like(db_ref)
            ds_ref[...] = jnp.zeros_like(ds_ref)

        for h in range(kvs):
            gs = slice(h * GQA_GROUP, (h + 1) * GQA_GROUP)
            kk = jnp.concatenate([kp_ref[h], kc_ref[h]], axis=0)
            vv = jnp.concatenate([vp_ref[h], vc_ref[h]], axis=0)
            qs_ = _stack_heads(q_ref, lo, h)
            dos = _stack_heads(do_ref, lo, h)
            pn, ps = _attn_probs(qs_, kk, b_ref[gs], s_ref, (kh * kvs + h) * GQA_GROUP, n > 0)
            dp = lax.dot_general(dos, vv, _DN["nt"], preferred_element_type=F32)
            delta = jnp.sum(pn * dp, axis=-1, keepdims=True)
            ds = pn * (dp - delta)
            db_ref[gs] += ds.reshape(GQA_GROUP, WINDOW, 2 * WINDOW)
            ds_ref[h] += (jnp.zeros((GQA_GROUP, LANES), F32)
                          - jnp.sum((ps * delta).reshape(GQA_GROUP, WINDOW, 1), axis=1))
            dsb = ds.astype(BF16)
            dq = jnp.dot(dsb, kk, preferred_element_type=F32)
            for p, t in enumerate(_unstack_heads(dq, lo)):
                dq_ref[:, (h * pairs + p) * LANES:(h * pairs + p + 1) * LANES] = t
            dk = lax.dot_general(dsb, qs_, _DN["tn"], preferred_element_type=F32)
            dv = lax.dot_general(pn.astype(BF16), dos, _DN["tn"], preferred_element_type=F32)
            dkp_ref[h] = dk[:WINDOW]
            dkc_ref[h] = dk[WINDOW:]
            dvp_ref[h] = dv[:WINDOW]
            dvc_ref[h] = dv[WINDOW:]

    part = jax.ShapeDtypeStruct((n_kv, S, LANES), F32)
    return pl.pallas_call(
        body, name="attn_bwd", grid=(n_kv // kvs, S // WINDOW),
        in_specs=[qs, cur, prev, cur, prev, bias, sink, qs],
        out_specs=[qs, cur, cur, cur, cur, bias, pl.BlockSpec((kvs, 8, LANES), lambda kh, n: (kh, 0, 0))],
        out_shape=[jax.ShapeDtypeStruct((S, qw), F32), part, part, part, part,
                   jax.ShapeDtypeStruct((H, WINDOW, 2 * WINDOW), F32),
                   jax.ShapeDtypeStruct((n_kv, 8, LANES), F32)],
        compiler_params=_cparams(("parallel", "arbitrary")),
    )(q, kcat, kcat, vcat, vcat, biasm, sinks, do)


def _adamw(name, w, g, m, v):
    shape = w.shape
    C = shape[-1]
    R = int(np.prod(shape[:-1]))
    tr = R
    if R * C * 4 > (1 << 20):
        tr = _tile(R, tuple(t for t in (512, 256, 128, 64, 32, 16, 8) if t * C * 4 <= (3 << 19)))
    c1 = 1.0 - ADAM_B1 ** ADAM_STEP
    c2 = 1.0 - ADAM_B2 ** ADAM_STEP

    def body(w_ref, g_ref, m_ref, v_ref, d_ref, nm_ref, nv_ref):
        gv = g_ref[...]
        nm = ADAM_B1 * m_ref[...] + (1.0 - ADAM_B1) * gv
        nv = ADAM_B2 * v_ref[...] + (1.0 - ADAM_B2) * (gv * gv)
        d_ref[...] = -ADAM_LR * ((nm / c1) / (jnp.sqrt(nv / c2) + ADAM_EPS) + ADAM_WD * w_ref[...])
        nm_ref[...] = nm
        nv_ref[...] = nv

    spec = pl.BlockSpec((tr, C), lambda i: (i, 0))
    outs = pl.pallas_call(
        body, name=name, grid=(R // tr,),
        in_specs=[spec] * 4, out_specs=[spec] * 3,
        out_shape=[jax.ShapeDtypeStruct((R, C), F32)] * 3,
        compiler_params=_cparams(("parallel",)),
    )(*[t.reshape(R, C) for t in (w, g, m, v)])
    return [o.reshape(shape) for o in outs]


def _place():
    return lax.axis_index("x"), lax.axis_index("y"), lax.axis_index("c")


def _other_chips(x, y):
    return [(1 - x, y), (x, 1 - y), (1 - x, 1 - y)]


def _add_half(name, g, other, c_arr):
    nb, R, C = g.shape
    half = R // 2
    tr = _tile(half, (256, 128, 64, 32, 16))
    n = half // tr

    def body(c_ref, g_ref, o_ref, s_ref):
        s_ref[...] = (g_ref[...].astype(F32) + o_ref[...].astype(F32)).astype(BF16)

    return pl.pallas_call(
        body, name=name,
        grid_spec=pltpu.PrefetchScalarGridSpec(
            num_scalar_prefetch=1, grid=(nb, n),
            in_specs=[pl.BlockSpec((None, tr, C), lambda b, i, c: (b, c[0] * n + i, 0)),
                      pl.BlockSpec((None, tr, C), lambda b, i, c: (b, i, 0))],
            out_specs=pl.BlockSpec((None, tr, C), lambda b, i, c: (b, i, 0))),
        out_shape=jax.ShapeDtypeStruct((nb, half, C), BF16),
        compiler_params=_cparams(("parallel", "parallel")),
    )(c_arr, g, other)


def _in_hbm(arrays):
    return [pltpu.with_memory_space_constraint(a, pltpu.HBM) for a in arrays]


def _chip_copy(src, dst, send, recv, chip, c):
    return pltpu.make_async_remote_copy(src_ref=src, dst_ref=dst, send_sem=send, recv_sem=recv,
                                        device_id=(chip[0], chip[1], c), device_id_type=MESH)


def _cast_slot(name, w, l, chip_arr, dtype):
    _, R, C = w.shape
    tr = _tile(R, tuple(t for t in (1024, 512, 256, 128, 64, 32, 16) if t * C * 4 <= (1 << 21)))

    def body(chip_ref, w_ref, o_ref):
        o_ref[...] = w_ref[...].astype(dtype)

    return pl.pallas_call(
        body, name=name,
        grid_spec=pltpu.PrefetchScalarGridSpec(
            num_scalar_prefetch=1, grid=(R // tr,),
            in_specs=[pl.BlockSpec((None, tr, C), lambda i, chip: (l, i, 0))],
            out_specs=pl.BlockSpec((None, tr, C), lambda i, chip: (chip[0], i, 0))),
        out_shape=jax.ShapeDtypeStruct((N_CHIPS, R, C), dtype),
        compiler_params=_cparams(("parallel",)),
    )(chip_arr, w)


def _gather_start(name, groups):
    sizes = [len(g) for g in groups]
    flat = [b for g in groups for b in g]
    n, G = len(flat), len(groups)

    def body(*refs):
        bufs = refs[:n]
        sems = refs[n:n + 2 * G]
        token = refs[-1]
        x, y, c = _place()
        chips = _other_chips(x, y)
        me = 2 * x + y
        k = 0
        for l in range(G):
            for t in range(sizes[l]):
                for j in range(3):
                    _chip_copy(bufs[k].at[me], bufs[k].at[me], sems[2 * l].at[3 * t + j], sems[2 * l + 1].at[3 * t + j],
                               chips[j], c).start()
                k += 1
        token[...] = jnp.zeros_like(token)

    sem_shapes = []
    for s in sizes:
        sem_shapes += [pltpu.SemaphoreType.DMA((3 * s,)), pltpu.SemaphoreType.DMA((3 * s,))]
    outs = pl.pallas_call(
        body, name=name,
        out_shape=(*sem_shapes, *[pltpu.HBM(b.shape, b.dtype) for b in flat], jax.ShapeDtypeStruct((8, LANES), F32)),
        in_specs=[_HBM] * n,
        out_specs=(*[_SEM] * (2 * G), *[_HBM] * n, pl.BlockSpec(memory_space=pltpu.VMEM)),
        input_output_aliases={t: 2 * G + t for t in range(n)},
        compiler_params=pltpu.CompilerParams(has_side_effects=_DATAFLOW),
    )(*_in_hbm(flat))
    res, k = [], 2 * G
    for l in range(G):
        res.append((outs[2 * l], outs[2 * l + 1], list(outs[k:k + sizes[l]])))
        k += sizes[l]
    return res, outs[-1]


def _gather_wait(name, send, recv, bufs, after):
    T = len(bufs)

    def body(*refs):
        ins = refs[:T]
        send_ref, recv_ref = refs[T], refs[T + 1]
        x, y, c = _place()
        chips = _other_chips(x, y)
        me = 2 * x + y
        for t in range(T):
            for j in range(3):
                cp = _chip_copy(ins[t].at[me], ins[t].at[2 * chips[j][0] + chips[j][1]], send_ref.at[3 * t + j],
                                recv_ref.at[3 * t + j], chips[j], c)
                cp.wait_send()
                cp.wait_recv()

    return pl.pallas_call(
        body, name=name,
        out_shape=[pltpu.HBM(b.shape, b.dtype) for b in bufs],
        in_specs=[_HBM] * T + [_SEM, _SEM, _ANY],
        out_specs=[_HBM] * T,
        input_output_aliases={t: t for t in range(T)},
        compiler_params=pltpu.CompilerParams(has_side_effects=_DATAFLOW),
    )(*bufs, send, recv, after)


def _split_start(name, bufs, n_copies, plan):
    n = len(bufs)

    def body(*refs):
        send, recv, token = refs[n], refs[n + 1], refs[-1]
        for s, (src, dst, dev) in enumerate(plan(refs[:n], False)):
            pltpu.make_async_remote_copy(src_ref=src, dst_ref=dst, send_sem=send.at[s], recv_sem=recv.at[s],
                                         device_id=dev, device_id_type=MESH).start()
        token[...] = jnp.zeros_like(token)

    outs = pl.pallas_call(
        body, name=name,
        out_shape=(pltpu.SemaphoreType.DMA((n_copies,)), pltpu.SemaphoreType.DMA((n_copies,)),
                   *[pltpu.HBM(b.shape, b.dtype) for b in bufs], jax.ShapeDtypeStruct((8, LANES), F32)),
        in_specs=[_HBM] * n,
        out_specs=(_SEM, _SEM, *[_HBM] * n, pl.BlockSpec(memory_space=pltpu.VMEM)),
        input_output_aliases={t: 2 + t for t in range(n)},
        compiler_params=pltpu.CompilerParams(has_side_effects=_DATAFLOW),
    )(*_in_hbm(bufs))
    return outs[0], outs[1], list(outs[2:2 + n]), outs[-1]


def _split_wait(name, send, recv, bufs, plan, after):
    n = len(bufs)

    def body(*refs):
        send_ref, recv_ref = refs[n], refs[n + 1]
        for s, (src, dst, dev) in enumerate(plan(refs[:n], True)):
            cp = pltpu.make_async_remote_copy(src_ref=src, dst_ref=dst, send_sem=send_ref.at[s], recv_sem=recv_ref.at[s],
                                              device_id=dev, device_id_type=MESH)
            cp.wait_send()
            cp.wait_recv()

    return list(pl.pallas_call(
        body, name=name,
        out_shape=[pltpu.HBM(b.shape, b.dtype) for b in bufs],
        in_specs=[_HBM] * n + [_SEM, _SEM] + [_ANY] * len(after),
        out_specs=[_HBM] * n,
        input_output_aliases={t: t for t in range(n)},
        compiler_params=pltpu.CompilerParams(has_side_effects=_DATAFLOW),
    )(*bufs, send, recv, *after))


def _swap_plan(shapes):
    T = len(shapes)

    def plan(refs, waiting):
        x, y, c = _place()
        out = []
        for t in range(T):
            half = shapes[t][1] // 2
            src = refs[t].at[:, pl.ds(pl.multiple_of((1 - c) * half, 16), half), :]
            out.append((src, refs[T + t], (x, y, 1 - c)))
        return out

    return plan


def _scatter_plan(T):
    def plan(refs, waiting):
        x, y, c = _place()
        chips = _other_chips(x, y)
        return [(refs[t].at[2 * chips[j][0] + chips[j][1]], refs[T + t].at[j], (chips[j][0], chips[j][1], c))
                for t in range(T) for j in range(3)]

    return plan


def _join_plan(T):
    def plan(refs, waiting):
        x, y, c = _place()
        return [(refs[t].at[c], refs[t].at[1 - c if waiting else c], (x, y, 1 - c)) for t in range(T)]

    return plan


def _sum_parts(name, sums, land, chip_arr, c_arr):
    _, R2, C = sums.shape
    tr = _tile(R2, (256, 128, 64, 32, 16))

    def body(chip_ref, c_ref, s_ref, l_ref, o_ref):
        acc = s_ref[...].astype(F32)
        for j in range(3):
            acc = acc + l_ref[j].astype(F32)
        o_ref[...] = acc

    return pl.pallas_call(
        body, name=name,
        grid_spec=pltpu.PrefetchScalarGridSpec(
            num_scalar_prefetch=2, grid=(R2 // tr,),
            in_specs=[pl.BlockSpec((None, tr, C), lambda i, chip, c: (chip[0], i, 0)),
                      pl.BlockSpec((3, tr, C), lambda i, chip, c: (0, i, 0))],
            out_specs=pl.BlockSpec((None, tr, C), lambda i, chip, c: (c[0], i, 0))),
        out_shape=jax.ShapeDtypeStruct((2, R2, C), F32),
        compiler_params=_cparams(("parallel",)),
    )(chip_arr, c_arr, sums, land)


class _Reduction:
    def __init__(self, tag, grads, chip_arr, c_arr):
        self.tag, self.T, self.shapes = tag, len(grads), [g.shape for g in grads]
        self.chip_arr, self.c_arr = chip_arr, c_arr
        lands = [lax.empty((g.shape[0], g.shape[1] // 2, g.shape[2]), g.dtype) for g in grads]
        self.plan = _swap_plan(self.shapes)
        self.send, self.recv, self.bufs, self.token = _split_start("rs_swap_start_" + tag, list(grads) + lands, self.T,
                                                                   self.plan)
        self.stage, self.result = 0, None

    def advance(self, after):
        T, tag = self.T, self.tag
        bufs = _split_wait("rs_wait%d_%s" % (self.stage, tag), self.send, self.recv, self.bufs, self.plan,
                           list(after) or [self.token])
        if self.stage == 0:
            sums = [_add_half("rs_add_%s_%d" % (tag, t), bufs[t], bufs[T + t], self.c_arr) for t in range(T)]
            lands = [lax.empty((3,) + s.shape[1:], s.dtype) for s in sums]
            self.plan = _scatter_plan(T)
            self.send, self.recv, self.bufs, self.token = _split_start("rs_scatter_start_" + tag, sums + lands, 3 * T,
                                                                       self.plan)
        elif self.stage == 1:
            parts = [_sum_parts("rs_sum_%s_%d" % (tag, t), bufs[t], bufs[T + t], self.chip_arr, self.c_arr)
                     for t in range(T)]
            self.plan = _join_plan(T)
            self.send, self.recv, self.bufs, self.token = _split_start("rs_join_start_" + tag, parts, T, self.plan)
        else:
            self.result = [b.reshape(sh[1], sh[2]) for b, sh in zip(bufs, self.shapes)]
            self.token = None
        self.stage += 1
        return self.token


def _all_reduce_small(v):
    R = v.shape[0]

    def body(v_ref, o_ref, buf, send, recv):
        x, y, c = _place()
        me = 4 * x + 2 * y + c
        buf[me] = v_ref[...]

        def copy(k, slot):
            to = (x ^ ((k >> 2) & 1), y ^ ((k >> 1) & 1), c ^ (k & 1))
            return pltpu.make_async_remote_copy(
                src_ref=v_ref, dst_ref=buf.at[slot], send_sem=send.at[k - 1], recv_sem=recv.at[k - 1],
                device_id=to, device_id_type=MESH)

        for k in range(1, 8):
            copy(k, me).start()
        for k in range(1, 8):
            copy(k, me ^ k).wait_recv()
        for k in range(1, 8):
            copy(k, me).wait_send()
        acc = buf[0]
        for s in range(1, 8):
            acc = acc + buf[s]
        o_ref[...] = acc

    vm = pl.BlockSpec(memory_space=pltpu.VMEM)
    return pl.pallas_call(
        body, name="all_reduce_small",
        in_specs=[vm], out_specs=vm,
        out_shape=jax.ShapeDtypeStruct((R, LANES), F32),
        scratch_shapes=[pltpu.VMEM((8, R, LANES), F32), pltpu.SemaphoreType.DMA((7,)), pltpu.SemaphoreType.DMA((7,))],
        compiler_params=pltpu.CompilerParams(has_side_effects=True, vmem_limit_bytes=VMEM_LIMIT),
    )(v)


def _local_step(x, tgt, small, depth, nb, weight, convw, pscale, on_grads):
    S, D = x.shape
    n_q = D // HEAD_DIM
    n_kv = n_q // GQA_GROUP
    Fh = convw[0].shape[2]

    def dup(gain):
        return jnp.tile(gain, 2).reshape(1, LANES)

    biasm = _bias_expand(small["rel_bias"])
    saved = []
    for i in range(depth):
        j = i // 2
        w = {}
        st = {"x0": x, "w": w}
        gm = small["norm_mix"][i].reshape(1, D)
        if i % 2 == 0:
            h = _rms_fwd("rms_mix_fwd", x, gm)
            w["wqkv"] = weight(i, "wqkv", h)
            qkv = _mm_colblk("qkv_fwd", h, w["wqkv"], F32)
            qh, kcat, vcat = _qk_norm_fwd(qkv, dup(small["attn_q_gain"][j]), dup(small["attn_k_gain"][j]), n_q, n_kv)
            o = _attn_fwd(qh, kcat, vcat, biasm, small["attn_sinks"][j])
            w["wo"] = weight(i, "wo", o)
            x = _mm_rowblk_res("wo_fwd", o, w["wo"], x)
            st.update(h=h, qkv=qkv, qh=qh, kcat=kcat, vcat=vcat, o=o)
        else:
            d = _pool_fwd(x, gm)
            w["wp"] = weight(i, "wp", d)
            ypre, x = _pool_mm_fwd(d, w["wp"], x, pscale[j])
            st.update(d=d, ypre=ypre)
        st["x1"] = x
        h2 = _rms_fwd("rms_ffn_fwd", x, small["norm_ffn"][i].reshape(1, D))
        w["wup"] = weight(i, "wup", h2)
        cb = small["ffn_conv_b"][i].reshape(2, 1, Fh)
        u, uc, a = _up_act_fwd(h2, w["wup"], convw[i], cb)
        w["wdn"] = weight(i, "wdn", a)
        x = _mm_rowblk_res("down_fwd", a, w["wdn"], x)
        st.update(h2=h2, u=u, uc=uc, a=a)
        saved.append(st)

    dx, dxb, loss_part = _loss_head(x, tgt)

    big = [dict() for _ in range(depth)]
    sg = {k: [None] * depth for k in ("norm_mix", "norm_ffn", "conv_w", "conv_b")}
    sg.update({k: [None] * ((depth + 1) // 2) for k in ("q_gain", "k_gain", "sinks")})
    sg["pool_scale"] = [None] * (depth // 2)
    dbias_tot = None
    before = None
    for i in reversed(range(depth)):
        j = i // 2
        st = saved[i]
        w = st["w"]
        duc = _down_act_bwd(dxb, w["wdn"], st["uc"], after=before)
        wdn_grad = _mm_tn_rowblk("down_bwd_w", st["a"], dxb, nb)
        du, dh2, dcwb = _up_bwd_in(duc, st["u"], convw[i], w["wup"])
        dcwb = jnp.transpose(dcwb, (1, 0, 2)).reshape(8, 2 * Fh)
        sg["conv_w"][i] = dcwb[0:3]
        sg["conv_b"][i] = dcwb[3]
        wup_grad = _mm_tn_colblk("up_bwd_w", st["h2"], du, nb, split2=True)
        before = on_grads(i, "ffn", dict(wdn=wdn_grad, wup=wup_grad), dx)
        dx, dxb, dg = _rms_bwd("rms_ffn_bwd", st["x1"], small["norm_ffn"][i].reshape(1, D), dh2, dx, after=before)
        sg["norm_ffn"][i] = dg.reshape(D)
        gm = small["norm_mix"][i].reshape(1, D)
        if i % 2 == 0:
            do = _mm_nt_rowblk("wo_bwd_in", dxb, w["wo"], BF16)
            big[i]["wo"] = _mm_tn_rowblk("wo_bwd_w", st["o"], dxb, nb)
            dq, dkc, dkp, dvc, dvp, dbias, dsink = _attn_bwd(
                st["qh"], st["kcat"], st["vcat"], biasm, small["attn_sinks"][j], do)
            dbias_tot = dbias if dbias_tot is None else dbias_tot + dbias
            sg["sinks"][j] = dsink[:, :, 0].reshape(n_q)
            dqkv, dqg, dkg = _qk_norm_bwd(st["qkv"], dup(small["attn_q_gain"][j]), dup(small["attn_k_gain"][j]),
                                          dq, dkc, dkp, dvc, dvp, n_q, n_kv)
            sg["q_gain"][j] = dqg[0, :HEAD_DIM]
            sg["k_gain"][j] = dkg[0, :HEAD_DIM]
            big[i]["wqkv"] = _mm_tn_colblk("qkv_bwd_w", st["h"], dqkv, nb)
            dh = _mm_nt_colblk("qkv_bwd_in", dqkv, w["wqkv"])
            dx, dxb, dg = _rms_bwd("rms_mix_bwd", st["x0"], gm, dh, dx)
        else:
            dyp, dsc = _pool_bwd_pre(dx, st["ypre"], pscale[j])
            sg["pool_scale"][j] = dsc.reshape(D)
            big[i]["wp"] = _pool_mm_bwd_w(st["d"], dyp)
            dd = _pool_mm_bwd_in(dyp, w["wp"])
            dx, dxb, dg = _pool_bwd_post(dd, st["x0"], gm, dx)
        sg["norm_mix"][i] = dg.reshape(D)
        before = on_grads(i, "mix", big[i], dx)
    sg["rel_bias"] = _bias_reduce(dbias_tot)[:, 0, :N_BUCKETS]
    return loss_part, dx, sg


_SMALL_ORDER = ("norm_mix", "norm_ffn", "rel_bias", "q_gain", "k_gain", "sinks", "conv_b", "conv_w", "pool_scale")


def kernel(x, norm_mix, norm_ffn, rel_bias, attn_w_qkv, attn_q_gain, attn_k_gain, attn_sinks, attn_w_o, pool_w, pool_scale, ffn_w_up, ffn_conv_w, ffn_conv_b, ffn_w_down, loss_target, m_norm_mix, m_norm_ffn, m_rel_bias, m_attn_w_qkv, m_attn_q_gain, m_attn_k_gain, m_attn_sinks, m_attn_w_o, m_pool_w, m_pool_scale, m_ffn_w_up, m_ffn_conv_w, m_ffn_conv_b, m_ffn_w_down, v_norm_mix, v_norm_ffn, v_rel_bias, v_attn_w_qkv, v_attn_q_gain, v_attn_k_gain, v_attn_sinks, v_attn_w_o, v_pool_w, v_pool_scale, v_ffn_w_up, v_ffn_conv_w, v_ffn_conv_b, v_ffn_w_down):
    _, S, D = x.shape
    depth = ffn_w_up.shape[0]
    n_attn, n_pool = attn_w_qkv.shape[0], pool_w.shape[0]
    nb = N_CHIPS
    nc_up = ffn_w_up.shape[2]
    Fh = nc_up * nb // 2
    cx, cy, cc = _place()
    chip = 2 * cx + cy
    c_arr = jnp.reshape(cc, (1,)).astype(jnp.int32)
    chip_arr = jnp.reshape(chip, (1,)).astype(jnp.int32)

    pool_w3 = pool_w.reshape(n_pool, -1, pool_w.shape[-1])
    order = [("convw", ffn_conv_w.reshape(1, -1, nc_up), 0, F32), ("pscale", pool_scale[None], 0, F32)]
    for i in range(depth):
        j = i // 2
        mixer = ([("wqkv", attn_w_qkv, j), ("wo", attn_w_o, j)] if i % 2 == 0 else [("wp", pool_w3, j)])
        order += [((i, k), w, l, BF16) for k, w, l in mixer + [("wup", ffn_w_up, i), ("wdn", ffn_w_down, i)]]
    first = 4
    started, tokens = [], []
    for part, lo, hi in (("first", 0, first), ("rest", first, len(order))):
        slots = [_cast_slot("slot_%d" % n, w, l, chip_arr, dt) for n, (_, w, l, dt) in enumerate(order[lo:hi], lo)]
        st, token = _gather_start("gather_start_" + part, [[b] for b in slots])
        started += st
        tokens.append(token)
    index = {key: n for n, (key, _, _, _) in enumerate(order)}

    def gathered(key, after):
        n = index[key]
        send, recv, bufs = started[n]
        return _gather_wait("gather_wait_%d" % n, send, recv, bufs, tokens[1] if n == 2 else after)[0]

    convw_g = gathered("convw", tokens[0]).reshape(nb, depth, 3, nc_up)
    pscale_g = gathered("pscale", tokens[0])
    convw = [jnp.transpose(convw_g[:, i], (1, 0, 2)).reshape(3, 2, Fh).transpose(1, 0, 2) for i in range(depth)]
    pscale = [pscale_g[:, j].reshape(1, D) for j in range(n_pool)]
    small = dict(norm_mix=norm_mix, norm_ffn=norm_ffn, rel_bias=rel_bias, attn_q_gain=attn_q_gain,
                 attn_k_gain=attn_k_gain, attn_sinks=attn_sinks, ffn_conv_b=ffn_conv_b)

    G, gd = pool_w.shape[1], pool_w.shape[3]

    def weight(i, name, after):
        w = gathered((i, name), after)
        if name == "wp":
            w = jnp.transpose(w.reshape(nb, G, gd // nb, gd), (1, 0, 2, 3)).reshape(G, gd, gd)
        return w

    red = {k: [None] * (n_attn if k in ("wqkv", "wo") else n_pool if k == "wp" else depth)
           for k in ("wqkv", "wo", "wp", "wup", "wdn")}
    pending = []

    def advance_all(after):
        tokens = []
        for r in list(pending):
            token = r.advance(after)
            if token is None:
                pending.remove(r)
                for k, o in zip(r.names, r.result):
                    red[k][r.layer // 2 if k in ("wqkv", "wo", "wp") else r.layer] = o
            else:
                tokens.append(token)
        return tokens

    def on_grads(i, tag, grads, dx_i):
        tokens = advance_all([dx_i])
        names = sorted(grads)
        flat = []
        for k in names:
            g = grads[k]
            if k == "wp":
                g = jnp.transpose(g.reshape(G, nb, gd // nb, gd), (1, 0, 2, 3))
            flat.append(g.reshape(nb, -1, g.shape[-1]))
        r = _Reduction("%d%s" % (i, tag), flat, chip_arr, c_arr)
        r.names, r.layer = names, i
        pending.append(r)
        return tokens + [r.advance([]) if i == 0 else r.token]

    loss_part, dx, sg = _local_step(x[0], loss_target[0], small, depth, nb, weight, convw, pscale, on_grads)
    while pending:
        advance_all([])
    loss = lax.psum(jnp.sum(loss_part), ("x", "y", "c"))

    g_wqkv = jnp.stack(red["wqkv"])
    g_wo = jnp.stack(red["wo"])
    g_wp = jnp.stack(red["wp"]).reshape(pool_w.shape)
    g_wup = jnp.stack(red["wup"])
    g_wdn = jnp.stack(red["wdn"])

    parts = [jnp.stack(sg[k]) if isinstance(sg[k], list) else sg[k] for k in _SMALL_ORDER]
    sizes = [int(np.prod(p.shape)) for p in parts]
    total = sum(sizes)
    rows = -(-total // (8 * LANES)) * 8
    packed = jnp.concatenate([p.reshape(-1) for p in parts] + [jnp.zeros((rows * LANES - total,), F32)])
    summed = _all_reduce_small(packed.reshape(rows, LANES)).reshape(-1)
    sm, off = {}, 0
    for k, p, n in zip(_SMALL_ORDER, parts, sizes):
        sm[k] = summed[off:off + n].reshape(p.shape)
        off += n
    g_convw = lax.dynamic_slice_in_dim(sm["conv_w"], chip * nc_up, nc_up, axis=2)
    pc = pool_scale.shape[1]
    g_pscale = lax.dynamic_slice_in_dim(sm["pool_scale"], chip * pc, pc, axis=1)

    grads = [sm["norm_mix"], sm["norm_ffn"], sm["rel_bias"], g_wqkv, sm["q_gain"], sm["k_gain"], sm["sinks"], g_wo,
             g_wp, g_pscale, g_wup, g_convw, sm["conv_b"], g_wdn]
    ws = [norm_mix, norm_ffn, rel_bias, attn_w_qkv, attn_q_gain, attn_k_gain, attn_sinks, attn_w_o, pool_w, pool_scale,
          ffn_w_up, ffn_conv_w, ffn_conv_b, ffn_w_down]
    ms = [m_norm_mix, m_norm_ffn, m_rel_bias, m_attn_w_qkv, m_attn_q_gain, m_attn_k_gain, m_attn_sinks, m_attn_w_o,
          m_pool_w, m_pool_scale, m_ffn_w_up, m_ffn_conv_w, m_ffn_conv_b, m_ffn_w_down]
    vs = [v_norm_mix, v_norm_ffn, v_rel_bias, v_attn_w_qkv, v_attn_q_gain, v_attn_k_gain, v_attn_sinks, v_attn_w_o,
          v_pool_w, v_pool_scale, v_ffn_w_up, v_ffn_conv_w, v_ffn_conv_b, v_ffn_w_down]
    deltas, new_m, new_v = [], [], []
    for idx, (w, g, m, v) in enumerate(zip(ws, grads, ms, vs)):
        d, nm, nv = _adamw("adamw_%d" % idx, w, g, m, v)
        deltas.append(d), new_m.append(nm), new_v.append(nv)
    return (loss, dx.reshape(1, S, D), *grads, *deltas, *new_m, *new_v)
```

```python
import functools

import numpy as np
import jax
import jax.numpy as jnp
from jax import lax
from jax.experimental import pallas as pl
from jax.experimental.pallas import tpu as pltpu

F32 = jnp.float32
BF16 = jnp.bfloat16
MESH = pl.DeviceIdType.MESH
_ANY = pl.BlockSpec(memory_space=pl.ANY)
_HBM = pl.BlockSpec(memory_space=pltpu.HBM)
_SEM = pl.BlockSpec(memory_space=pltpu.SEMAPHORE)
_DATAFLOW = pltpu.SideEffectType.DATAFLOW_SIDE_EFFECTING

N_CHIPS = 4
HEAD_DIM = 64
GQA_GROUP = 8
WINDOW = 128
N_BUCKETS = 32
MAX_DISTANCE = 128
POOL_WINDOWS = (2, 4, 8, 16)
POOL_HALO = 16
ATTN_BWD_KV_PER_STEP = 4
EPS = 1e-6
NEG_INF = -1e30
LANES = 128
VMEM_LIMIT = 56 * 1024 * 1024

ADAM_LR = 0.001
ADAM_B1 = 0.9
ADAM_B2 = 0.999
ADAM_EPS = 1e-08
ADAM_WD = 0.01
ADAM_STEP = 10


def _tile(n, prefs):
    for p in prefs:
        if p <= n and n % p == 0:
            return p
    return n


def _cparams(sem):
    return pltpu.CompilerParams(dimension_semantics=sem, vmem_limit_bytes=VMEM_LIMIT)


_DN = {
    "nn": (((1,), (0,)), ((), ())),
    "nt": (((1,), (1,)), ((), ())),
    "tn": (((0,), (0,)), ((), ())),
}


def _mm(name, kind, a, b, grid, a_spec, b_spec, outs, acc_shape, extras=(), epilogue=None, after=None):
    nk = grid[2]
    n_ex, n_out = len(extras), len(outs)
    order = list(after or [])

    def body(a_ref, b_ref, *rest):
        ex_refs = rest[:n_ex]
        rest = rest[n_ex + len(order):]
        out_refs = rest[:n_out]
        acc_ref = rest[n_out] if nk > 1 else None
        part = lax.dot_general(a_ref[...], b_ref[...], _DN[kind], preferred_element_type=F32)

        def finish(val):
            vals = epilogue(val, *[r[...] for r in ex_refs]) if epilogue else (val,)
            for r, v in zip(out_refs, vals):
                r[...] = v.astype(r.dtype)

        if nk == 1:
            finish(part)
        else:
            k = pl.program_id(2)

            @pl.when(k == 0)
            def _():
                acc_ref[...] = part

            @pl.when(k > 0)
            def _():
                acc_ref[...] += part

            @pl.when(k == nk - 1)
            def _():
                finish(acc_ref[...])

    res = pl.pallas_call(
        body,
        name=name,
        grid=grid,
        in_specs=[a_spec, b_spec] + [s for _, s in extras] + [_ANY] * len(order),
        out_specs=[s for _, _, s in outs],
        out_shape=[jax.ShapeDtypeStruct(sh, dt) for sh, dt, _ in outs],
        scratch_shapes=[pltpu.VMEM(acc_shape, F32)] if nk > 1 else [],
        compiler_params=_cparams(("parallel", "parallel", "arbitrary")),
    )(a, b, *[e for e, _ in extras], *order)
    return res if n_out > 1 else res[0]


def _mm_colblk(name, a, wg, out_dtype):
    S, K = a.shape
    nb, _, nc = wg.shape
    tm = _tile(S, (1024, 512))
    tn = _tile(nc, (1408, 1024, 640, 512, 256, 128))
    npb = nc // tn
    return _mm(
        name, "nn", a, wg, (nb * npb, S // tm, 1),
        pl.BlockSpec((tm, K), lambda p, q, k: (q, 0)),
        pl.BlockSpec((None, K, tn), lambda p, q, k: (p // npb, 0, p % npb)),
        [((S, nb * nc), out_dtype, pl.BlockSpec((tm, tn), lambda p, q, k: (q, p)))], (tm, tn))


def _mm_rowblk_res(name, a, wg, res):
    S = a.shape[0]
    nb, kc, N = wg.shape
    tm = _tile(S, (1024, 512))
    ktot = nb * kc

    def vmem(t):
        return 2 * 2 * (tm * ktot + ktot * t) + 4 * 4 * tm * t

    tn = _tile(N, tuple(t for t in (2048, 1024, 512, 256) if vmem(t) <= (44 << 20)))

    def body(a_ref, b_ref, r_ref, o_ref):
        acc = r_ref[...]
        for r in range(nb):
            acc = acc + jnp.dot(a_ref[:, r * kc:(r + 1) * kc], b_ref[r], preferred_element_type=F32)
        o_ref[...] = acc

    o_spec = pl.BlockSpec((tm, tn), lambda i, j: (i, j))
    return pl.pallas_call(
        body, name=name, grid=(S // tm, N // tn),
        in_specs=[pl.BlockSpec((tm, nb * kc), lambda i, j: (i, 0)),
                  pl.BlockSpec((nb, kc, tn), lambda i, j: (0, 0, j)), o_spec],
        out_specs=o_spec,
        out_shape=jax.ShapeDtypeStruct((S, N), F32),
        compiler_params=_cparams(("parallel", "parallel")),
    )(a, wg, res)


def _mm_nt_rowblk(name, g, wg, out_dtype, after=None):
    S, N = g.shape
    nb, kc, _ = wg.shape
    tm = _tile(S, (1024, 512))
    tn = _tile(kc, (1408, 512, 256, 128))
    kpb = kc // tn
    return _mm(
        name, "nt", g, wg, (nb * kpb, S // tm, 1),
        pl.BlockSpec((tm, N), lambda p, q, k: (q, 0)),
        pl.BlockSpec((None, tn, N), lambda p, q, k: (p // kpb, p % kpb, 0)),
        [((S, nb * kc), out_dtype, pl.BlockSpec((tm, tn), lambda p, q, k: (q, p)))], (tm, tn), after=after)


def _mm_nt_colblk(name, g, wg):
    S = g.shape[0]
    nb, K, nc = wg.shape
    tm = _tile(S, (512,))

    def body(g_ref, b_ref, o_ref):
        acc = lax.dot_general(g_ref[:, 0:nc], b_ref[0], _DN["nt"], preferred_element_type=F32)
        for r in range(1, nb):
            acc = acc + lax.dot_general(g_ref[:, r * nc:(r + 1) * nc], b_ref[r], _DN["nt"],
                                        preferred_element_type=F32)
        o_ref[...] = acc

    return pl.pallas_call(
        body, name=name, grid=(S // tm,),
        in_specs=[pl.BlockSpec((tm, nb * nc), lambda i: (i, 0)), pl.BlockSpec((nb, K, nc), lambda i: (0, 0, 0))],
        out_specs=pl.BlockSpec((tm, K), lambda i: (i, 0)),
        out_shape=jax.ShapeDtypeStruct((S, K), F32),
        compiler_params=_cparams(("parallel",)),
    )(g, wg)


def _mm_tn_colblk(name, a, g, nb, split2=False):
    S, K = a.shape
    ntot = g.shape[-1] * (2 if split2 else 1)
    nc = ntot // nb
    ti = _tile(K, (1024,))
    tn = _tile(nc, (1408, 640, 512, 256, 128))
    ts = _tile(S, (2048, 1024, 512, 256))
    npb = nc // tn
    half = nb * npb // 2
    if split2:
        b_spec = pl.BlockSpec((None, ts, tn), lambda p, q, k: (q // half, k, q % half))
    else:
        b_spec = pl.BlockSpec((ts, tn), lambda p, q, k: (k, q))
    return _mm(
        name, "tn", a, g, (K // ti, nb * npb, S // ts),
        pl.BlockSpec((ts, ti), lambda p, q, k: (k, p)),
        b_spec,
        [((nb, K, nc), BF16, pl.BlockSpec((None, ti, tn), lambda p, q, k: (q // npb, p, q % npb)))], (ti, tn))


def _mm_tn_rowblk(name, a, g, nb):
    S, ktot = a.shape
    N = g.shape[1]
    kc = ktot // nb
    ti = _tile(kc, (1408, 512, 256, 128))
    tj = _tile(N, (1024,))
    ts = _tile(S, (2048, 1024, 512, 256))
    ipb = kc // ti
    return _mm(
        name, "tn", a, g, (nb * ipb, N // tj, S // ts),
        pl.BlockSpec((ts, ti), lambda p, q, k: (k, p)),
        pl.BlockSpec((ts, tj), lambda p, q, k: (k, q)),
        [((nb, kc, N), BF16, pl.BlockSpec((None, ti, tj), lambda p, q, k: (p // ipb, p % ipb, q)))], (ti, tj))


def _rms_fwd(name, x, gain):
    S, D = x.shape
    tm = _tile(S, (512,))

    def body(x_ref, g_ref, o_ref):
        xv = x_ref[...]
        r = lax.rsqrt(jnp.mean(xv * xv, axis=-1, keepdims=True) + EPS)
        o_ref[...] = (xv * r * g_ref[...]).astype(o_ref.dtype)

    return pl.pallas_call(
        body, name=name, grid=(S // tm,),
        in_specs=[pl.BlockSpec((tm, D), lambda i: (i, 0)), pl.BlockSpec((1, D), lambda i: (0, 0))],
        out_specs=pl.BlockSpec((tm, D), lambda i: (i, 0)),
        out_shape=jax.ShapeDtypeStruct((S, D), BF16),
        compiler_params=_cparams(("parallel",)),
    )(x, gain)


def _rms_bwd_math(xv, gain, dh):
    r = lax.rsqrt(jnp.mean(xv * xv, axis=-1, keepdims=True) + EPS)
    xhat = xv * r
    dxhat = dh * gain
    c = jnp.mean(dxhat * xhat, axis=-1, keepdims=True)
    return r * (dxhat - xhat * c), dh * xhat


def _rows_to_8(v):
    tm, C = v.shape
    return jnp.sum(v.reshape(tm // 8, 8, C), axis=0)


def _rms_bwd(name, x, gain, dh, dres, after=None):
    S, D = x.shape
    tm = _tile(S, (256,))
    n = S // tm
    order = list(after or [])

    def body(x_ref, g_ref, dh_ref, dr_ref, *rest):
        dx_ref, dxb_ref, dg_ref, acc_ref = rest[len(order):]
        i = pl.program_id(0)
        dxn, dgr = _rms_bwd_math(x_ref[...], g_ref[...], dh_ref[...])
        dx = dr_ref[...] + dxn
        dx_ref[...] = dx
        dxb_ref[...] = dx.astype(BF16)

        @pl.when(i == 0)
        def _():
            acc_ref[...] = jnp.zeros_like(acc_ref)

        acc_ref[...] += _rows_to_8(dgr)

        @pl.when(i == n - 1)
        def _():
            dg_ref[...] = jnp.sum(acc_ref[...], axis=0, keepdims=True)

    row = pl.BlockSpec((tm, D), lambda i: (i, 0))
    vec = pl.BlockSpec((1, D), lambda i: (0, 0))
    return pl.pallas_call(
        body, name=name, grid=(n,),
        in_specs=[row, vec, row, row] + [_ANY] * len(order),
        out_specs=[row, row, vec],
        out_shape=[jax.ShapeDtypeStruct((S, D), F32), jax.ShapeDtypeStruct((S, D), BF16),
                   jax.ShapeDtypeStruct((1, D), F32)],
        scratch_shapes=[pltpu.VMEM((8, D), F32)],
        compiler_params=_cparams(("arbitrary",)),
    )(x, gain, dh, dres, *order)


def _loss_head(y, tgt):
    S, D = y.shape
    tm = _tile(S, (256,))
    n = S // tm

    def body(y_ref, t_ref, dy_ref, dyb_ref, l_ref):
        i = pl.program_id(0)
        e = y_ref[...] - t_ref[...]
        dy = e * (1.0 / D)
        dy_ref[...] = dy
        dyb_ref[...] = dy.astype(BF16)

        @pl.when(i == 0)
        def _():
            l_ref[...] = jnp.zeros_like(l_ref)

        sq = _rows_to_8(e * e)
        part = sq[:, 0:LANES]
        for t in range(1, D // LANES):
            part = part + sq[:, t * LANES:(t + 1) * LANES]
        l_ref[...] += part * (0.5 / D)

    row = pl.BlockSpec((tm, D), lambda i: (i, 0))
    return pl.pallas_call(
        body, name="loss_head", grid=(n,),
        in_specs=[row, row],
        out_specs=[row, row, pl.BlockSpec((8, LANES), lambda i: (0, 0))],
        out_shape=[jax.ShapeDtypeStruct((S, D), F32), jax.ShapeDtypeStruct((S, D), BF16),
                   jax.ShapeDtypeStruct((8, LANES), F32)],
        compiler_params=_cparams(("arbitrary",)),
    )(y, tgt)


def _up_act_fwd(h2, wg, cw, cb):
    S, K = h2.shape
    nb, _, nc = wg.shape
    Fh = nb * nc // 2
    tm = _tile(S, (2048, 1024, 512))
    tn = _tile(nc, (256, 128))
    sub = _tile(tm, (512,))
    ch = _tile(sub, (32,))
    npb = nc // tn
    hb = nb // 2
    ncol = Fh // tn

    def body(a_ref, bg_ref, bv_ref, cw_ref, cb_ref, u_ref, uc_ref, o_ref, buf_ref, halo_ref):
        j = pl.program_id(1)

        @pl.when(pl.program_id(0) == 0)
        def _():
            halo_ref[j] = jnp.zeros((2, 8, tn), F32)

        buf_ref[:, 0:8, :] = halo_ref[j]
        cws, cbs = [cw_ref[0], cw_ref[1]], [cb_ref[0], cb_ref[1]]
        for r in range(tm // sub):
            for s, b_ref in enumerate((bg_ref, bv_ref)):
                buf_ref[s, 8 + r * sub:8 + (r + 1) * sub, :] = jnp.dot(
                    a_ref[r * sub:(r + 1) * sub, :], b_ref[...], preferred_element_type=F32)
            for c in range(sub // ch):
                base = r * sub + c * ch
                ucs = []
                for s in range(2):
                    ext = buf_ref[s, base:base + ch + 8, :]
                    us = ext[8:]
                    uc = (cws[s][0:1] * pltpu.roll(ext, 2, 0)[8:] + cws[s][1:2] * pltpu.roll(ext, 1, 0)[8:]
                          + cws[s][2:3] * us + cbs[s])
                    u_ref[s, base:base + ch, :] = us.astype(BF16)
                    uc_ref[s, base:base + ch, :] = uc.astype(BF16)
                    ucs.append(uc)
                gate, val = ucs
                o_ref[base:base + ch, :] = (gate * (1.0 / (1.0 + jnp.exp(-gate))) * val).astype(BF16)
        halo_ref[j] = buf_ref[:, tm:tm + 8, :]

    both = pl.BlockSpec((2, tm, tn), lambda i, j: (0, i, j))
    return pl.pallas_call(
        body, name="up_act_fwd", grid=(S // tm, ncol),
        in_specs=[pl.BlockSpec((tm, K), lambda i, j: (i, 0)),
                  pl.BlockSpec((None, K, tn), lambda i, j: (j // npb, 0, j % npb)),
                  pl.BlockSpec((None, K, tn), lambda i, j: (j // npb + hb, 0, j % npb)),
                  pl.BlockSpec((2, 3, tn), lambda i, j: (0, 0, j)),
                  pl.BlockSpec((2, 1, tn), lambda i, j: (0, 0, j))],
        out_specs=[both, both, pl.BlockSpec((tm, tn), lambda i, j: (i, j))],
        out_shape=[jax.ShapeDtypeStruct((2, S, Fh), BF16), jax.ShapeDtypeStruct((2, S, Fh), BF16),
                   jax.ShapeDtypeStruct((S, Fh), BF16)],
        scratch_shapes=[pltpu.VMEM((2, tm + 8, tn), F32), pltpu.VMEM((ncol, 2, 8, tn), F32)],
        compiler_params=_cparams(("arbitrary", "arbitrary")),
    )(h2, wg, wg, cw, cb)


def _down_act_bwd(dxb, wg, uc, after=None):
    S, D = dxb.shape
    nb, kc, _ = wg.shape
    Fh = nb * kc
    tm = _tile(S, (1024, 512))
    sub = _tile(tm, (256,))
    ch = _tile(sub, (32,))
    t128 = kc // LANES
    nt = 4 if (Fh // LANES) % 4 == 0 else 2
    tn = nt * LANES
    order = list(after or [])

    def body(dx_ref, *rest):
        b_refs, uc_ref = rest[:nt], rest[nt]
        duc_ref, bcat_ref, da_ref = rest[nt + 1 + len(order):]
        for q in range(nt):
            bcat_ref[q * LANES:(q + 1) * LANES, :] = b_refs[q][...]
        for r in range(tm // sub):
            rows = slice(r * sub, (r + 1) * sub)
            da_ref[rows, :] = lax.dot_general(dx_ref[rows, :], bcat_ref[...], _DN["nt"], preferred_element_type=F32)
            for c in range(sub // ch):
                cs = slice(r * sub + c * ch, r * sub + (c + 1) * ch)
                da = da_ref[cs, :]
                gate, val = uc_ref[0, cs, :].astype(F32), uc_ref[1, cs, :].astype(F32)
                sig = 1.0 / (1.0 + jnp.exp(-gate))
                duc_ref[0, cs, :] = (da * val * (sig * (1.0 + gate * (1.0 - sig)))).astype(BF16)
                duc_ref[1, cs, :] = (da * (gate * sig)).astype(BF16)

    both = pl.BlockSpec((2, tm, tn), lambda i, j: (0, i, j))
    return pl.pallas_call(
        body, name="down_act_bwd", grid=(S // tm, Fh // tn),
        in_specs=[pl.BlockSpec((tm, D), lambda i, j: (i, 0))]
        + [pl.BlockSpec((None, LANES, D), functools.partial(
            lambda i, j, q: ((nt * j + q) // t128, (nt * j + q) % t128, 0), q=q)) for q in range(nt)]
        + [both] + [_ANY] * len(order),
        out_specs=both,
        out_shape=jax.ShapeDtypeStruct((2, S, Fh), BF16),
        scratch_shapes=[pltpu.VMEM((tn, D), BF16), pltpu.VMEM((tm, tn), F32)],
        compiler_params=_cparams(("parallel", "parallel")),
    )(dxb, *[wg] * nt, uc, *order)


def _up_bwd_in(duc, u, cw, wg):
    _, S, Fh = duc.shape
    nb, K, nc = wg.shape
    tm = _tile(S, (1024, 512))
    tk = _tile(nc, (1408, 640, 512, 256, 128))
    sub = _tile(tm, (256,))
    ch = _tile(sub, (64,))
    hr = 16
    npb = nc // tk
    nk = nb * npb
    half = nk // 2
    n, nh = S // tm, tm // hr

    def body(d_ref, h_ref, u_ref, cw_ref, b_ref, du_ref, o_ref, cg_ref):
        i, k = pl.program_id(0), pl.program_id(1)
        keep = (i < n - 1).astype(F32)

        @pl.when(i == 0)
        def _():
            cg_ref[k] = jnp.zeros((8, tk), F32)

        @pl.when(k == 0)
        def _():
            o_ref[...] = jnp.zeros_like(o_ref)

        for r in range(tm // sub):
            for l in range(tk // LANES):
                ls = slice(l * LANES, (l + 1) * LANES)
                w = cw_ref[:, ls]
                sums = [jnp.zeros((8, LANES), F32) for _ in range(4)]
                for c in range(sub // ch):
                    base = r * sub + c * ch
                    if base + ch + hr <= tm:
                        ext = d_ref[base:base + ch + hr, ls].astype(F32)
                    else:
                        ext = jnp.concatenate([d_ref[base:base + ch, ls].astype(F32),
                                               h_ref[:, ls].astype(F32) * keep], axis=0)
                    d = ext[:ch]
                    d1 = pltpu.roll(ext, ch + hr - 1, 0)[:ch]
                    d2 = pltpu.roll(ext, ch + hr - 2, 0)[:ch]
                    du_ref[base:base + ch, ls] = (w[2:3] * d + w[1:2] * d1 + w[0:1] * d2).astype(BF16)
                    uv = u_ref[base:base + ch, ls].astype(F32)
                    for q, v in enumerate((d2 * uv, d1 * uv, d * uv, d)):
                        sums[q] = sums[q] + _rows_to_8(v)
                for q in range(4):
                    cg_ref[k, q:q + 1, ls] += jnp.sum(sums[q], axis=0, keepdims=True)
            rows = slice(r * sub, (r + 1) * sub)
            o_ref[rows, :] += lax.dot_general(du_ref[rows, :], b_ref[...], _DN["nt"], preferred_element_type=F32)

    blk = pl.BlockSpec((None, tm, tk), lambda i, k: (k // half, i, k % half))
    return pl.pallas_call(
        body, name="up_bwd_in", grid=(n, nk),
        in_specs=[blk,
                  pl.BlockSpec((None, hr, tk), lambda i, k: (k // half, jnp.minimum((i + 1) * nh, S // hr - 1), k % half)),
                  blk,
                  pl.BlockSpec((None, 3, tk), lambda i, k: (k // half, 0, k % half)),
                  pl.BlockSpec((None, K, tk), lambda i, k: (k // npb, 0, k % npb))],
        out_specs=[blk, pl.BlockSpec((tm, K), lambda i, k: (i, 0)), pl.BlockSpec((nk, 8, tk), lambda i, k: (0, 0, 0))],
        out_shape=[jax.ShapeDtypeStruct((2, S, Fh), BF16), jax.ShapeDtypeStruct((S, K), F32),
                   jax.ShapeDtypeStruct((nk, 8, tk), F32)],
        compiler_params=_cparams(("arbitrary", "arbitrary")),
    )(duc, duc, u, cw, wg)


def _pool_counts(i, tm, w):
    t = i * tm + lax.broadcasted_iota(jnp.int32, (tm, 1), 0)
    return jnp.minimum(t + 1, w).astype(F32)


def _pool_fwd(x, gain):
    S, D = x.shape
    tm = _tile(S, (256,))
    gd = D // len(POOL_WINDOWS)
    nh = tm // POOL_HALO

    def body(x_ref, xh_ref, g_ref, d_ref):
        i = pl.program_id(0)

        def norm(v):
            return v * lax.rsqrt(jnp.mean(v * v, axis=-1, keepdims=True) + EPS) * g_ref[...]

        h = norm(x_ref[...])
        hh = norm(xh_ref[...]) * (i > 0).astype(F32)
        ext = jnp.concatenate([hh, h], axis=0)
        for gi, w in enumerate(POOL_WINDOWS):
            sl = slice(gi * gd, (gi + 1) * gd)
            win = ext[:, sl]
            k = 1
            while k < w:
                win = win + pltpu.roll(win, k, 0)
                k *= 2
            mean = win[POOL_HALO:] / _pool_counts(i, tm, w)
            d_ref[:, sl] = (mean - h[:, sl]).astype(BF16)

    return pl.pallas_call(
        body, name="pool_fwd", grid=(S // tm,),
        in_specs=[pl.BlockSpec((tm, D), lambda i: (i, 0)),
                  pl.BlockSpec((POOL_HALO, D), lambda i: (jnp.maximum(i * nh - 1, 0), 0)),
                  pl.BlockSpec((1, D), lambda i: (0, 0))],
        out_specs=pl.BlockSpec((tm, D), lambda i: (i, 0)),
        out_shape=jax.ShapeDtypeStruct((S, D), BF16),
        compiler_params=_cparams(("parallel",)),
    )(x, x, gain)


def _pool_mm_fwd(d, wp, x, scale):
    S, D = d.shape
    G, gd, _ = wp.shape
    tm = _tile(S, (1024, 512))
    o_spec = pl.BlockSpec((tm, gd), lambda p, q, k: (p, q))
    return _mm(
        "pool_mm_fwd", "nn", d, wp, (S // tm, G, 1),
        o_spec, pl.BlockSpec((None, gd, gd), lambda p, q, k: (q, 0, 0)),
        [((S, D), F32, o_spec), ((S, D), F32, o_spec)], (tm, gd),
        extras=[(x, o_spec), (scale, pl.BlockSpec((1, gd), lambda p, q, k: (0, q)))],
        epilogue=lambda acc, xv, sc: (acc, xv + acc * sc))


def _pool_bwd_pre(dx1, ypre, scale):
    S, D = dx1.shape
    tm = _tile(S, (256,))
    n = S // tm

    def body(dx_ref, y_ref, s_ref, dy_ref, ds_ref, acc_ref):
        i = pl.program_id(0)
        dx = dx_ref[...]
        dy_ref[...] = (dx * s_ref[...]).astype(BF16)

        @pl.when(i == 0)
        def _():
            acc_ref[...] = jnp.zeros_like(acc_ref)

        acc_ref[...] += _rows_to_8(dx * y_ref[...])

        @pl.when(i == n - 1)
        def _():
            ds_ref[...] = jnp.sum(acc_ref[...], axis=0, keepdims=True)

    row = pl.BlockSpec((tm, D), lambda i: (i, 0))
    vec = pl.BlockSpec((1, D), lambda i: (0, 0))
    return pl.pallas_call(
        body, name="pool_bwd_pre", grid=(n,),
        in_specs=[row, row, vec], out_specs=[row, vec],
        out_shape=[jax.ShapeDtypeStruct((S, D), BF16), jax.ShapeDtypeStruct((1, D), F32)],
        scratch_shapes=[pltpu.VMEM((8, D), F32)],
        compiler_params=_cparams(("arbitrary",)),
    )(dx1, ypre, scale)


def _pool_mm_bwd_in(dyp, wp):
    S, D = dyp.shape
    G, gd, _ = wp.shape
    tm = _tile(S, (1024, 512))
    spec = pl.BlockSpec((tm, gd), lambda p, q, k: (p, q))
    return _mm(
        "pool_mm_bwd_in", "nt", dyp, wp, (S // tm, G, 1),
        spec, pl.BlockSpec((None, gd, gd), lambda p, q, k: (q, 0, 0)),
        [((S, D), F32, spec)], (tm, gd))


def _pool_mm_bwd_w(d, dyp):
    S, D = d.shape
    G = len(POOL_WINDOWS)
    gd = D // G
    ts = _tile(S, (1024, 512, 256))
    spec = pl.BlockSpec((ts, gd), lambda p, q, k: (k, p))
    return _mm(
        "pool_mm_bwd_w", "tn", d, dyp, (G, 1, S // ts),
        spec, spec,
        [((G, gd, gd), BF16, pl.BlockSpec((None, gd, gd), lambda p, q, k: (p, 0, 0)))], (gd, gd))


def _pool_bwd_post(dd, x, gain, dres):
    S, D = x.shape
    tm = _tile(S, (256,))
    n = S // tm
    gd = D // len(POOL_WINDOWS)
    nh = tm // POOL_HALO

    def body(dd_ref, ddh_ref, x_ref, g_ref, dr_ref, dx_ref, dxb_ref, dg_ref, acc_ref):
        i = pl.program_id(0)
        dd = dd_ref[...]
        halo = ddh_ref[...] * (i < n - 1).astype(F32)
        parts = []
        for gi, w in enumerate(POOL_WINDOWS):
            sl = slice(gi * gd, (gi + 1) * gd)
            ext = jnp.concatenate([dd[:, sl] / _pool_counts(i, tm, w), halo[:, sl] * (1.0 / w)], axis=0)
            k = 1
            while k < w:
                ext = ext + pltpu.roll(ext, tm + POOL_HALO - k, 0)
                k *= 2
            parts.append(ext[:tm] - dd[:, sl])
        dh = jnp.concatenate(parts, axis=1)
        dxn, dgr = _rms_bwd_math(x_ref[...], g_ref[...], dh)
        dx = dr_ref[...] + dxn
        dx_ref[...] = dx
        dxb_ref[...] = dx.astype(BF16)

        @pl.when(i == 0)
        def _():
            acc_ref[...] = jnp.zeros_like(acc_ref)

        acc_ref[...] += _rows_to_8(dgr)

        @pl.when(i == n - 1)
        def _():
            dg_ref[...] = jnp.sum(acc_ref[...], axis=0, keepdims=True)

    row = pl.BlockSpec((tm, D), lambda i: (i, 0))
    vec = pl.BlockSpec((1, D), lambda i: (0, 0))
    nxt = pl.BlockSpec((POOL_HALO, D), lambda i: (jnp.minimum((i + 1) * nh, S // POOL_HALO - 1), 0))
    return pl.pallas_call(
        body, name="pool_bwd_post", grid=(n,),
        in_specs=[row, nxt, row, vec, row],
        out_specs=[row, row, vec],
        out_shape=[jax.ShapeDtypeStruct((S, D), F32), jax.ShapeDtypeStruct((S, D), BF16),
                   jax.ShapeDtypeStruct((1, D), F32)],
        scratch_shapes=[pltpu.VMEM((8, D), F32)],
        compiler_params=_cparams(("arbitrary",)),
    )(dd, dd, x, gain, dres)


def _band_tables():
    i = np.arange(WINDOW)[:, None]
    j = np.arange(2 * WINDOW)[None, :]
    n = np.maximum(WINDOW + i - j, 0)
    max_exact = N_BUCKETS // 2
    nf = np.maximum(n, 1).astype(np.float32)
    large = max_exact + (np.log(nf / max_exact) / np.log(MAX_DISTANCE / max_exact)
                         * (N_BUCKETS - max_exact)).astype(np.int32)
    large = np.minimum(large, N_BUCKETS - 1)
    buckets = np.where(n < max_exact, n, large).astype(np.int32)
    dist = WINDOW + i - j
    in_win = ((dist >= 0) & (dist < WINDOW)).astype(np.int32)
    return buckets, in_win


def _bias_expand(rel_bias):
    H = rel_bias.shape[0]
    buckets, in_win = _band_tables()

    def body(rb_ref, bk_ref, win_ref, o_ref):
        h = pl.program_id(0)
        bk = bk_ref[...]
        acc = jnp.zeros(bk.shape, F32)
        for b in range(N_BUCKETS):
            acc = jnp.where(bk == b, rb_ref[h, b], acc)
        o_ref[...] = jnp.where(win_ref[...] > 0, acc, NEG_INF)

    full = pl.BlockSpec((WINDOW, 2 * WINDOW), lambda h: (0, 0))
    return pl.pallas_call(
        body, name="bias_expand", grid=(H,),
        in_specs=[pl.BlockSpec(memory_space=pltpu.SMEM), full, full],
        out_specs=pl.BlockSpec((None, WINDOW, 2 * WINDOW), lambda h: (h, 0, 0)),
        out_shape=jax.ShapeDtypeStruct((H, WINDOW, 2 * WINDOW), F32),
        compiler_params=_cparams(("parallel",)),
    )(rel_bias, jnp.asarray(buckets), jnp.asarray(in_win))


def _bias_reduce(dbias):
    H = dbias.shape[0]
    buckets, in_win = _band_tables()

    def body(d_ref, bk_ref, win_ref, o_ref):
        bk = jnp.where(win_ref[...] > 0, bk_ref[...], -1)
        d = d_ref[...]
        lane = lax.broadcasted_iota(jnp.int32, (8, LANES), 1)
        out = jnp.zeros((8, LANES), F32)
        for b in range(N_BUCKETS):
            s = jnp.sum(jnp.where(bk == b, d, 0.0))
            out = jnp.where(lane == b, s, out)
        o_ref[...] = out

    full = pl.BlockSpec((WINDOW, 2 * WINDOW), lambda h: (0, 0))
    return pl.pallas_call(
        body, name="bias_reduce", grid=(H,),
        in_specs=[pl.BlockSpec((None, WINDOW, 2 * WINDOW), lambda h: (h, 0, 0)), full, full],
        out_specs=pl.BlockSpec((None, 8, LANES), lambda h: (h, 0, 0)),
        out_shape=jax.ShapeDtypeStruct((H, 8, LANES), F32),
        compiler_params=_cparams(("parallel",)),
    )(dbias, jnp.asarray(buckets), jnp.asarray(in_win))


def _half_rsqrt(t, lo):
    sq = t * t
    s_lo = jnp.sum(jnp.where(lo, sq, 0.0), axis=-1, keepdims=True)
    s_hi = jnp.sum(jnp.where(lo, 0.0, sq), axis=-1, keepdims=True)
    return jnp.where(lo, lax.rsqrt(s_lo * (1.0 / HEAD_DIM) + EPS), lax.rsqrt(s_hi * (1.0 / HEAD_DIM) + EPS))


def _qk_norm_fwd(qkv, qg, kg, n_q, n_kv):
    S, W = qkv.shape
    tm = _tile(S, (256,))
    qw = n_q * HEAD_DIM
    kw = n_kv * HEAD_DIM
    scale = HEAD_DIM ** -0.5

    def body(x_ref, qg_ref, kg_ref, q_ref, k_ref, v_ref):
        lo = lax.broadcasted_iota(jnp.int32, (tm, LANES), 1) < HEAD_DIM
        for t in range(qw // LANES):
            q = x_ref[:, t * LANES:(t + 1) * LANES]
            q_ref[:, t * LANES:(t + 1) * LANES] = (q * _half_rsqrt(q, lo) * qg_ref[...] * scale).astype(BF16)
        for p in range(kw // LANES):
            k = x_ref[:, qw + p * LANES:qw + (p + 1) * LANES]
            kn = k * _half_rsqrt(k, lo) * kg_ref[...]
            v = x_ref[:, qw + kw + p * LANES:qw + kw + (p + 1) * LANES]
            for src, dst in ((kn, k_ref), (v, v_ref)):
                rolled = pltpu.roll(src, HEAD_DIM, 1)
                dst[2 * p] = jnp.where(lo, src, rolled).astype(BF16)
                dst[2 * p + 1] = jnp.where(lo, rolled, src).astype(BF16)

    vec = pl.BlockSpec((1, LANES), lambda i: (0, 0))
    kv = pl.BlockSpec((n_kv, tm, LANES), lambda i: (0, i, 0))
    return pl.pallas_call(
        body, name="qk_norm_fwd", grid=(S // tm,),
        in_specs=[pl.BlockSpec((tm, W), lambda i: (i, 0)), vec, vec],
        out_specs=[pl.BlockSpec((tm, qw), lambda i: (i, 0)), kv, kv],
        out_shape=[jax.ShapeDtypeStruct((S, qw), BF16), jax.ShapeDtypeStruct((n_kv, S, LANES), BF16),
                   jax.ShapeDtypeStruct((n_kv, S, LANES), BF16)],
        compiler_params=_cparams(("parallel",)),
    )(qkv, qg, kg)


def _qk_norm_bwd(qkv, qg, kg, dq, dkc, dkp, dvc, dvp, n_q, n_kv):
    S, W = qkv.shape
    tm = _tile(S, (4 * WINDOW, WINDOW))
    n = S // tm
    per = tm // WINDOW
    qw = n_q * HEAD_DIM
    kw = n_kv * HEAD_DIM
    scale = HEAD_DIM ** -0.5

    def body(x_ref, qg_ref, kg_ref, dq_ref, dkc_ref, dkp_ref, dkn_ref, dvc_ref, dvp_ref, dvn_ref,
             dx_ref, dqg_ref, dkg_ref, accq_ref, acck_ref):
        i = pl.program_id(0)
        lo = lax.broadcasted_iota(jnp.int32, (tm, LANES), 1) < HEAD_DIM
        has_next = (i < n - 1).astype(F32)

        @pl.when(i == 0)
        def _():
            accq_ref[...] = jnp.zeros_like(accq_ref)
            acck_ref[...] = jnp.zeros_like(acck_ref)

        def norm_bwd(t, dy, gain):
            r = _half_rsqrt(t, lo)
            xhat = t * r
            dxhat = dy * gain
            prod = dxhat * xhat
            c_lo = jnp.sum(jnp.where(lo, prod, 0.0), axis=-1, keepdims=True) * (1.0 / HEAD_DIM)
            c_hi = jnp.sum(jnp.where(lo, 0.0, prod), axis=-1, keepdims=True) * (1.0 / HEAD_DIM)
            return r * (dxhat - xhat * jnp.where(lo, c_lo, c_hi)), _rows_to_8(dy * xhat)

        for t in range(qw // LANES):
            sl = slice(t * LANES, (t + 1) * LANES)
            dt, dg = norm_bwd(x_ref[:, sl], dq_ref[:, sl] * scale, qg_ref[...])
            dx_ref[:, sl] = dt.astype(BF16)
            accq_ref[...] += dg

        def pair(cur_ref, prev_ref, next_ref, p):
            folded = []
            for h in (2 * p, 2 * p + 1):
                later = next_ref[h] * has_next
                if per > 1:
                    later = jnp.concatenate([prev_ref[h, WINDOW:, :], later], axis=0)
                tot = cur_ref[h] + later
                folded.append(tot + pltpu.roll(tot, HEAD_DIM, 1))
            return jnp.where(lo, folded[0], folded[1])

        for p in range(kw // LANES):
            sl = slice(qw + p * LANES, qw + (p + 1) * LANES)
            dt, dg = norm_bwd(x_ref[:, sl], pair(dkc_ref, dkp_ref, dkn_ref, p), kg_ref[...])
            dx_ref[:, sl] = dt.astype(BF16)
            acck_ref[...] += dg
            sl = slice(qw + kw + p * LANES, qw + kw + (p + 1) * LANES)
            dx_ref[:, sl] = pair(dvc_ref, dvp_ref, dvn_ref, p).astype(BF16)

        @pl.when(i == n - 1)
        def _():
            for acc_ref, o_ref in ((accq_ref, dqg_ref), (acck_ref, dkg_ref)):
                s = jnp.sum(acc_ref[...], axis=0, keepdims=True)
                o_ref[...] = s + pltpu.roll(s, HEAD_DIM, 1)

    vec = pl.BlockSpec((1, LANES), lambda i: (0, 0))
    cur = pl.BlockSpec((n_kv, tm, LANES), lambda i: (0, i, 0))
    nxt = pl.BlockSpec((n_kv, WINDOW, LANES), lambda i: (0, jnp.minimum((i + 1) * per, S // WINDOW - 1), 0))
    return pl.pallas_call(
        body, name="qk_norm_bwd", grid=(n,),
        in_specs=[pl.BlockSpec((tm, W), lambda i: (i, 0)), vec, vec, pl.BlockSpec((tm, qw), lambda i: (i, 0)),
                  cur, cur, nxt, cur, cur, nxt],
        out_specs=[pl.BlockSpec((tm, W), lambda i: (i, 0)), vec, vec],
        out_shape=[jax.ShapeDtypeStruct((S, W), BF16), jax.ShapeDtypeStruct((1, LANES), F32),
                   jax.ShapeDtypeStruct((1, LANES), F32)],
        scratch_shapes=[pltpu.VMEM((8, LANES), F32), pltpu.VMEM((8, LANES), F32)],
        compiler_params=_cparams(("arbitrary",)),
    )(qkv, qg, kg, dq, dkc, dkp, dkp, dvc, dvp, dvp)


def _attn_specs(n_kv, per_step):
    kvs = per_step if n_kv % per_step == 0 else 1
    q = pl.BlockSpec((WINDOW, kvs * GQA_GROUP * HEAD_DIM), lambda kh, n: (n, kh))
    cur = pl.BlockSpec((kvs, WINDOW, LANES), lambda kh, n: (kh, n, 0))
    prev = pl.BlockSpec((kvs, WINDOW, LANES), lambda kh, n: (kh, jnp.maximum(n - 1, 0), 0))
    bias = pl.BlockSpec((kvs * GQA_GROUP, WINDOW, 2 * WINDOW), lambda kh, n: (kh, 0, 0))
    sink = pl.BlockSpec(memory_space=pltpu.SMEM)
    return kvs, q, cur, prev, bias, sink


def _stack_heads(x_ref, lo, h):
    parts = []
    for g in range(GQA_GROUP):
        c0 = (h * GQA_GROUP // 2 + g // 2) * LANES
        t = x_ref[:, c0:c0 + LANES]
        parts.append(jnp.where(lo if g % 2 == 0 else jnp.logical_not(lo), t, jnp.zeros_like(t)))
    return jnp.concatenate(parts, axis=0)


def _unstack_heads(v, lo):
    return [jnp.where(lo, v[(2 * p) * WINDOW:(2 * p + 1) * WINDOW], v[(2 * p + 1) * WINDOW:(2 * p + 2) * WINDOW])
            for p in range(GQA_GROUP // 2)]


def _attn_probs(qs, kk, bias, s_ref, head0, has_prev):
    rows = GQA_GROUP * WINDOW
    s = lax.dot_general(qs, kk, _DN["nt"], preferred_element_type=F32) + bias.reshape(rows, 2 * WINDOW)
    col = lax.broadcasted_iota(jnp.int32, (rows, 2 * WINDOW), 1)
    s = jnp.where(jnp.logical_or(has_prev, col >= WINDOW), s, NEG_INF)
    sink = jnp.concatenate([jnp.full((WINDOW, 1), s_ref[head0 + g], F32) for g in range(GQA_GROUP)], axis=0)
    m = jnp.maximum(jnp.max(s, axis=-1, keepdims=True), sink)
    p = jnp.exp(s - m)
    es = jnp.exp(sink - m)
    den = jnp.sum(p, axis=-1, keepdims=True) + es
    return p / den, es / den


def _attn_fwd(q, kcat, vcat, biasm, sinks):
    S, qw = q.shape
    n_kv = kcat.shape[0]
    kvs, qs, cur, prev, bias, sink = _attn_specs(n_kv, 4)
    pairs = GQA_GROUP // 2

    def body(q_ref, kc_ref, kp_ref, vc_ref, vp_ref, b_ref, s_ref, o_ref):
        kh, n = pl.program_id(0), pl.program_id(1)
        lo = lax.broadcasted_iota(jnp.int32, (WINDOW, LANES), 1) < HEAD_DIM
        for h in range(kvs):
            kk = jnp.concatenate([kp_ref[h], kc_ref[h]], axis=0)
            vv = jnp.concatenate([vp_ref[h], vc_ref[h]], axis=0)
            pn, _ = _attn_probs(_stack_heads(q_ref, lo, h), kk, b_ref[h * GQA_GROUP:(h + 1) * GQA_GROUP], s_ref,
                                (kh * kvs + h) * GQA_GROUP, n > 0)
            o = jnp.dot(pn.astype(BF16), vv, preferred_element_type=F32)
            for p, t in enumerate(_unstack_heads(o, lo)):
                o_ref[:, (h * pairs + p) * LANES:(h * pairs + p + 1) * LANES] = t.astype(BF16)

    return pl.pallas_call(
        body, name="attn_fwd", grid=(n_kv // kvs, S // WINDOW),
        in_specs=[qs, cur, prev, cur, prev, bias, sink],
        out_specs=qs,
        out_shape=jax.ShapeDtypeStruct((S, qw), BF16),
        compiler_params=_cparams(("parallel", "parallel")),
    )(q, kcat, kcat, vcat, vcat, biasm, sinks)


def _attn_bwd(q, kcat, vcat, biasm, sinks, do):
    S, qw = q.shape
    n_kv = kcat.shape[0]
    H = n_kv * GQA_GROUP
    kvs, qs, cur, prev, bias, sink = _attn_specs(n_kv, ATTN_BWD_KV_PER_STEP)
    pairs = GQA_GROUP // 2

    def body(q_ref, kc_ref, kp_ref, vc_ref, vp_ref, b_ref, s_ref, do_ref,
             dq_ref, dkc_ref, dkp_ref, dvc_ref, dvp_ref, db_ref, ds_ref):
        kh, n = pl.program_id(0), pl.program_id(1)
        lo = lax.broadcasted_iota(jnp.int32, (WINDOW, LANES), 1) < HEAD_DIM

        @pl.when(n == 0)
        def _():
            db_ref[...] = jnp.zeros_like(db_ref)
            ds_ref[...] = jnp.zeros_like(ds_ref)

        for h in range(kvs):
            gs = slice(h * GQA_GROUP, (h + 1) * GQA_GROUP)
            kk = jnp.concatenate([kp_ref[h], kc_ref[h]], axis=0)
            vv = jnp.concatenate([vp_ref[h], vc_ref[h]], axis=0)
            qs_ = _stack_heads(q_ref, lo, h)
            dos = _stack_heads(do_ref, lo, h)
            pn, ps = _attn_probs(qs_, kk, b_ref[gs], s_ref, (kh * kvs + h) * GQA_GROUP, n > 0)
            dp = lax.dot_general(dos, vv, _DN["nt"], preferred_element_type=F32)
            delta = jnp.sum(pn * dp, axis=-1, keepdims=True)
            ds = pn * (dp - delta)
            db_ref[gs] += ds.reshape(GQA_GROUP, WINDOW, 2 * WINDOW)
            ds_ref[h] += (jnp.zeros((GQA_GROUP, LANES), F32)
                          - jnp.sum((ps * delta).reshape(GQA_GROUP, WINDOW, 1), axis=1))
            dsb = ds.astype(BF16)
            dq = jnp.dot(dsb, kk, preferred_element_type=F32)
            for p, t in enumerate(_unstack_heads(dq, lo)):
                dq_ref[:, (h * pairs + p) * LANES:(h * pairs + p + 1) * LANES] = t
            dk = lax.dot_general(dsb, qs_, _DN["tn"], preferred_element_type=F32)
            dv = lax.dot_general(pn.astype(BF16), dos, _DN["tn"], preferred_element_type=F32)
            dkp_ref[h] = dk[:WINDOW]
            dkc_ref[h] = dk[WINDOW:]
            dvp_ref[h] = dv[:WINDOW]
            dvc_ref[h] = dv[WINDOW:]

    part = jax.ShapeDtypeStruct((n_kv, S, LANES), F32)
    return pl.pallas_call(
        body, name="attn_bwd", grid=(n_kv // kvs, S // WINDOW),
        in_specs=[qs, cur, prev, cur, prev, bias, sink, qs],
        out_specs=[qs, cur, cur, cur, cur, bias, pl.BlockSpec((kvs, 8, LANES), lambda kh, n: (kh, 0, 0))],
        out_shape=[jax.ShapeDtypeStruct((S, qw), F32), part, part, part, part,
                   jax.ShapeDtypeStruct((H, WINDOW, 2 * WINDOW), F32),
                   jax.ShapeDtypeStruct((n_kv, 8, LANES), F32)],
        compiler_params=_cparams(("parallel", "arbitrary")),
    )(q, kcat, kcat, vcat, vcat, biasm, sinks, do)


def _adamw(name, w, g, m, v):
    shape = w.shape
    C = shape[-1]
    R = int(np.prod(shape[:-1]))
    tr = R
    if R * C * 4 > (1 << 20):
        tr = _tile(R, tuple(t for t in (512, 256, 128, 64, 32, 16, 8) if t * C * 4 <= (3 << 19)))
    c1 = 1.0 - ADAM_B1 ** ADAM_STEP
    c2 = 1.0 - ADAM_B2 ** ADAM_STEP

    def body(w_ref, g_ref, m_ref, v_ref, d_ref, nm_ref, nv_ref):
        gv = g_ref[...]
        nm = ADAM_B1 * m_ref[...] + (1.0 - ADAM_B1) * gv
        nv = ADAM_B2 * v_ref[...] + (1.0 - ADAM_B2) * (gv * gv)
        d_ref[...] = -ADAM_LR * ((nm / c1) / (jnp.sqrt(nv / c2) + ADAM_EPS) + ADAM_WD * w_ref[...])
        nm_ref[...] = nm
        nv_ref[...] = nv

    spec = pl.BlockSpec((tr, C), lambda i: (i, 0))
    outs = pl.pallas_call(
        body, name=name, grid=(R // tr,),
        in_specs=[spec] * 4, out_specs=[spec] * 3,
        out_shape=[jax.ShapeDtypeStruct((R, C), F32)] * 3,
        compiler_params=_cparams(("parallel",)),
    )(*[t.reshape(R, C) for t in (w, g, m, v)])
    return [o.reshape(shape) for o in outs]


def _place():
    return lax.axis_index("x"), lax.axis_index("y"), lax.axis_index("c")


def _other_chips(x, y):
    return [(1 - x, y), (x, 1 - y), (1 - x, 1 - y)]


def _add_half(name, g, other, c_arr):
    nb, R, C = g.shape
    half = R // 2
    tr = _tile(half, (256, 128, 64, 32, 16))
    n = half // tr

    def body(c_ref, g_ref, o_ref, s_ref):
        s_ref[...] = (g_ref[...].astype(F32) + o_ref[...].astype(F32)).astype(BF16)

    return pl.pallas_call(
        body, name=name,
        grid_spec=pltpu.PrefetchScalarGridSpec(
            num_scalar_prefetch=1, grid=(nb, n),
            in_specs=[pl.BlockSpec((None, tr, C), lambda b, i, c: (b, c[0] * n + i, 0)),
                      pl.BlockSpec((None, tr, C), lambda b, i, c: (b, i, 0))],
            out_specs=pl.BlockSpec((None, tr, C), lambda b, i, c: (b, i, 0))),
        out_shape=jax.ShapeDtypeStruct((nb, half, C), BF16),
        compiler_params=_cparams(("parallel", "parallel")),
    )(c_arr, g, other)


def _in_hbm(arrays):
    return [pltpu.with_memory_space_constraint(a, pltpu.HBM) for a in arrays]


def _chip_copy(src, dst, send, recv, chip, c):
    return pltpu.make_async_remote_copy(src_ref=src, dst_ref=dst, send_sem=send, recv_sem=recv,
                                        device_id=(chip[0], chip[1], c), device_id_type=MESH)


def _cast_slot(name, w, l, chip_arr, dtype):
    _, R, C = w.shape
    tr = _tile(R, tuple(t for t in (1024, 512, 256, 128, 64, 32, 16) if t * C * 4 <= (1 << 21)))

    def body(chip_ref, w_ref, o_ref):
        o_ref[...] = w_ref[...].astype(dtype)

    return pl.pallas_call(
        body, name=name,
        grid_spec=pltpu.PrefetchScalarGridSpec(
            num_scalar_prefetch=1, grid=(R // tr,),
            in_specs=[pl.BlockSpec((None, tr, C), lambda i, chip: (l, i, 0))],
            out_specs=pl.BlockSpec((None, tr, C), lambda i, chip: (chip[0], i, 0))),
        out_shape=jax.ShapeDtypeStruct((N_CHIPS, R, C), dtype),
        compiler_params=_cparams(("parallel",)),
    )(chip_arr, w)


def _gather_start(name, groups):
    sizes = [len(g) for g in groups]
    flat = [b for g in groups for b in g]
    n, G = len(flat), len(groups)

    def body(*refs):
        bufs = refs[:n]
        sems = refs[n:n + 2 * G]
        token = refs[-1]
        x, y, c = _place()
        chips = _other_chips(x, y)
        me = 2 * x + y
        k = 0
        for l in range(G):
            for t in range(sizes[l]):
                for j in range(3):
                    _chip_copy(bufs[k].at[me], bufs[k].at[me], sems[2 * l].at[3 * t + j], sems[2 * l + 1].at[3 * t + j],
                               chips[j], c).start()
                k += 1
        token[...] = jnp.zeros_like(token)

    sem_shapes = []
    for s in sizes:
        sem_shapes += [pltpu.SemaphoreType.DMA((3 * s,)), pltpu.SemaphoreType.DMA((3 * s,))]
    outs = pl.pallas_call(
        body, name=name,
        out_shape=(*sem_shapes, *[pltpu.HBM(b.shape, b.dtype) for b in flat], jax.ShapeDtypeStruct((8, LANES), F32)),
        in_specs=[_HBM] * n,
        out_specs=(*[_SEM] * (2 * G), *[_HBM] * n, pl.BlockSpec(memory_space=pltpu.VMEM)),
        input_output_aliases={t: 2 * G + t for t in range(n)},
        compiler_params=pltpu.CompilerParams(has_side_effects=_DATAFLOW),
    )(*_in_hbm(flat))
    res, k = [], 2 * G
    for l in range(G):
        res.append((outs[2 * l], outs[2 * l + 1], list(outs[k:k + sizes[l]])))
        k += sizes[l]
    return res, outs[-1]


def _gather_wait(name, send, recv, bufs, after):
    T = len(bufs)

    def body(*refs):
        ins = refs[:T]
        send_ref, recv_ref = refs[T], refs[T + 1]
        x, y, c = _place()
        chips = _other_chips(x, y)
        me = 2 * x + y
        for t in range(T):
            for j in range(3):
                cp = _chip_copy(ins[t].at[me], ins[t].at[2 * chips[j][0] + chips[j][1]], send_ref.at[3 * t + j],
                                recv_ref.at[3 * t + j], chips[j], c)
                cp.wait_send()
                cp.wait_recv()

    return pl.pallas_call(
        body, name=name,
        out_shape=[pltpu.HBM(b.shape, b.dtype) for b in bufs],
        in_specs=[_HBM] * T + [_SEM, _SEM, _ANY],
        out_specs=[_HBM] * T,
        input_output_aliases={t: t for t in range(T)},
        compiler_params=pltpu.CompilerParams(has_side_effects=_DATAFLOW),
    )(*bufs, send, recv, after)


def _split_start(name, bufs, n_copies, plan):
    n = len(bufs)

    def body(*refs):
        send, recv, token = refs[n], refs[n + 1], refs[-1]
        for s, (src, dst, dev) in enumerate(plan(refs[:n], False)):
            pltpu.make_async_remote_copy(src_ref=src, dst_ref=dst, send_sem=send.at[s], recv_sem=recv.at[s],
                                         device_id=dev, device_id_type=MESH).start()
        token[...] = jnp.zeros_like(token)

    outs = pl.pallas_call(
        body, name=name,
        out_shape=(pltpu.SemaphoreType.DMA((n_copies,)), pltpu.SemaphoreType.DMA((n_copies,)),
                   *[pltpu.HBM(b.shape, b.dtype) for b in bufs], jax.ShapeDtypeStruct((8, LANES), F32)),
        in_specs=[_HBM] * n,
        out_specs=(_SEM, _SEM, *[_HBM] * n, pl.BlockSpec(memory_space=pltpu.VMEM)),
        input_output_aliases={t: 2 + t for t in range(n)},
        compiler_params=pltpu.CompilerParams(has_side_effects=_DATAFLOW),
    )(*_in_hbm(bufs))
    return outs[0], outs[1], list(outs[2:2 + n]), outs[-1]


def _split_wait(name, send, recv, bufs, plan, after):
    n = len(bufs)

    def body(*refs):
        send_ref, recv_ref = refs[n], refs[n + 1]
        for s, (src, dst, dev) in enumerate(plan(refs[:n], True)):
            cp = pltpu.make_async_remote_copy(src_ref=src, dst_ref=dst, send_sem=send_ref.at[s], recv_sem=recv_ref.at[s],
                                              device_id=dev, device_id_type=MESH)
            cp.wait_send()
            cp.wait_recv()

    return list(pl.pallas_call(
        body, name=name,
        out_shape=[pltpu.HBM(b.shape, b.dtype) for b in bufs],
        in_specs=[_HBM] * n + [_SEM, _SEM] + [_ANY] * len(after),
        out_specs=[_HBM] * n,
        input_output_aliases={t: t for t in range(n)},
        compiler_params=pltpu.CompilerParams(has_side_effects=_DATAFLOW),
    )(*bufs, send, recv, *after))


def _swap_plan(shapes):
    T = len(shapes)

    def plan(refs, waiting):
        x, y, c = _place()
        out = []
        for t in range(T):
            half = shapes[t][1] // 2
            src = refs[t].at[:, pl.ds(pl.multiple_of((1 - c) * half, 16), half), :]
            out.append((src, refs[T + t], (x, y, 1 - c)))
        return out

    return plan


def _scatter_plan(T):
    def plan(refs, waiting):
        x, y, c = _place()
        chips = _other_chips(x, y)
        return [(refs[t].at[2 * chips[j][0] + chips[j][1]], refs[T + t].at[j], (chips[j][0], chips[j][1], c))
                for t in range(T) for j in range(3)]

    return plan


def _join_plan(T):
    def plan(refs, waiting):
        x, y, c = _place()
        return [(refs[t].at[c], refs[t].at[1 - c if waiting else c], (x, y, 1 - c)) for t in range(T)]

    return plan


def _sum_parts(name, sums, land, chip_arr, c_arr):
    _, R2, C = sums.shape
    tr = _tile(R2, (256, 128, 64, 32, 16))

    def body(chip_ref, c_ref, s_ref, l_ref, o_ref):
        acc = s_ref[...].astype(F32)
        for j in range(3):
            acc = acc + l_ref[j].astype(F32)
        o_ref[...] = acc

    return pl.pallas_call(
        body, name=name,
        grid_spec=pltpu.PrefetchScalarGridSpec(
            num_scalar_prefetch=2, grid=(R2 // tr,),
            in_specs=[pl.BlockSpec((None, tr, C), lambda i, chip, c: (chip[0], i, 0)),
                      pl.BlockSpec((3, tr, C), lambda i, chip, c: (0, i, 0))],
            out_specs=pl.BlockSpec((None, tr, C), lambda i, chip, c: (c[0], i, 0))),
        out_shape=jax.ShapeDtypeStruct((2, R2, C), F32),
        compiler_params=_cparams(("parallel",)),
    )(chip_arr, c_arr, sums, land)


class _Reduction:
    def __init__(self, tag, grads, chip_arr, c_arr):
        self.tag, self.T, self.shapes = tag, len(grads), [g.shape for g in grads]
        self.chip_arr, self.c_arr = chip_arr, c_arr
        lands = [lax.empty((g.shape[0], g.shape[1] // 2, g.shape[2]), g.dtype) for g in grads]
        self.plan = _swap_plan(self.shapes)
        self.send, self.recv, self.bufs, self.token = _split_start("rs_swap_start_" + tag, list(grads) + lands, self.T,
                                                                   self.plan)
        self.stage, self.result = 0, None

    def advance(self, after):
        T, tag = self.T, self.tag
        bufs = _split_wait("rs_wait%d_%s" % (self.stage, tag), self.send, self.recv, self.bufs, self.plan,
                           list(after) or [self.token])
        if self.stage == 0:
            sums = [_add_half("rs_add_%s_%d" % (tag, t), bufs[t], bufs[T + t], self.c_arr) for t in range(T)]
            lands = [lax.empty((3,) + s.shape[1:], s.dtype) for s in sums]
            self.plan = _scatter_plan(T)
            self.send, self.recv, self.bufs, self.token = _split_start("rs_scatter_start_" + tag, sums + lands, 3 * T,
                                                                       self.plan)
        elif self.stage == 1:
            parts = [_sum_parts("rs_sum_%s_%d" % (tag, t), bufs[t], bufs[T + t], self.chip_arr, self.c_arr)
                     for t in range(T)]
            self.plan = _join_plan(T)
            self.send, self.recv, self.bufs, self.token = _split_start("rs_join_start_" + tag, parts, T, self.plan)
        else:
            self.result = [b.reshape(sh[1], sh[2]) for b, sh in zip(bufs, self.shapes)]
            self.token = None
        self.stage += 1
        return self.token


def _all_reduce_small(v):
    R = v.shape[0]

    def body(v_ref, o_ref, buf, send, recv):
        x, y, c = _place()
        me = 4 * x + 2 * y + c
        buf[me] = v_ref[...]

        def copy(k, slot):
            to = (x ^ ((k >> 2) & 1), y ^ ((k >> 1) & 1), c ^ (k & 1))
            return pltpu.make_async_remote_copy(
                src_ref=v_ref, dst_ref=buf.at[slot], send_sem=send.at[k - 1], recv_sem=recv.at[k - 1],
                device_id=to, device_id_type=MESH)

        for k in range(1, 8):
            copy(k, me).start()
        for k in range(1, 8):
            copy(k, me ^ k).wait_recv()
        for k in range(1, 8):
            copy(k, me).wait_send()
        acc = buf[0]
        for s in range(1, 8):
            acc = acc + buf[s]
        o_ref[...] = acc

    vm = pl.BlockSpec(memory_space=pltpu.VMEM)
    return pl.pallas_call(
        body, name="all_reduce_small",
        in_specs=[vm], out_specs=vm,
        out_shape=jax.ShapeDtypeStruct((R, LANES), F32),
        scratch_shapes=[pltpu.VMEM((8, R, LANES), F32), pltpu.SemaphoreType.DMA((7,)), pltpu.SemaphoreType.DMA((7,))],
        compiler_params=pltpu.CompilerParams(has_side_effects=True, vmem_limit_bytes=VMEM_LIMIT),
    )(v)


def _local_step(x, tgt, small, depth, nb, weight, convw, pscale, on_grads):
    S, D = x.shape
    n_q = D // HEAD_DIM
    n_kv = n_q // GQA_GROUP
    Fh = convw[0].shape[2]

    def dup(gain):
        return jnp.tile(gain, 2).reshape(1, LANES)

    biasm = _bias_expand(small["rel_bias"])
    saved = []
    for i in range(depth):
        j = i // 2
        w = {}
        st = {"x0": x, "w": w}
        gm = small["norm_mix"][i].reshape(1, D)
        if i % 2 == 0:
            h = _rms_fwd("rms_mix_fwd", x, gm)
            w["wqkv"] = weight(i, "wqkv", h)
            qkv = _mm_colblk("qkv_fwd", h, w["wqkv"], F32)
            qh, kcat, vcat = _qk_norm_fwd(qkv, dup(small["attn_q_gain"][j]), dup(small["attn_k_gain"][j]), n_q, n_kv)
            o = _attn_fwd(qh, kcat, vcat, biasm, small["attn_sinks"][j])
            w["wo"] = weight(i, "wo", o)
            x = _mm_rowblk_res("wo_fwd", o, w["wo"], x)
            st.update(h=h, qkv=qkv, qh=qh, kcat=kcat, vcat=vcat, o=o)
        else:
            d = _pool_fwd(x, gm)
            w["wp"] = weight(i, "wp", d)
            ypre, x = _pool_mm_fwd(d, w["wp"], x, pscale[j])
            st.update(d=d, ypre=ypre)
        st["x1"] = x
        h2 = _rms_fwd("rms_ffn_fwd", x, small["norm_ffn"][i].reshape(1, D))
        w["wup"] = weight(i, "wup", h2)
        cb = small["ffn_conv_b"][i].reshape(2, 1, Fh)
        u, uc, a = _up_act_fwd(h2, w["wup"], convw[i], cb)
        w["wdn"] = weight(i, "wdn", a)
        x = _mm_rowblk_res("down_fwd", a, w["wdn"], x)
        st.update(h2=h2, u=u, uc=uc, a=a)
        saved.append(st)

    dx, dxb, loss_part = _loss_head(x, tgt)

    big = [dict() for _ in range(depth)]
    sg = {k: [None] * depth for k in ("norm_mix", "norm_ffn", "conv_w", "conv_b")}
    sg.update({k: [None] * ((depth + 1) // 2) for k in ("q_gain", "k_gain", "sinks")})
    sg["pool_scale"] = [None] * (depth // 2)
    dbias_tot = None
    before = None
    for i in reversed(range(depth)):
        j = i // 2
        st = saved[i]
        w = st["w"]
        duc = _down_act_bwd(dxb, w["wdn"], st["uc"], after=before)
        wdn_grad = _mm_tn_rowblk("down_bwd_w", st["a"], dxb, nb)
        du, dh2, dcwb = _up_bwd_in(duc, st["u"], convw[i], w["wup"])
        dcwb = jnp.transpose(dcwb, (1, 0, 2)).reshape(8, 2 * Fh)
        sg["conv_w"][i] = dcwb[0:3]
        sg["conv_b"][i] = dcwb[3]
        wup_grad = _mm_tn_colblk("up_bwd_w", st["h2"], du, nb, split2=True)
        before = on_grads(i, "ffn", dict(wdn=wdn_grad, wup=wup_grad), dx)
        dx, dxb, dg = _rms_bwd("rms_ffn_bwd", st["x1"], small["norm_ffn"][i].reshape(1, D), dh2, dx, after=before)
        sg["norm_ffn"][i] = dg.reshape(D)
        gm = small["norm_mix"][i].reshape(1, D)
        if i % 2 == 0:
            do = _mm_nt_rowblk("wo_bwd_in", dxb, w["wo"], BF16)
            big[i]["wo"] = _mm_tn_rowblk("wo_bwd_w", st["o"], dxb, nb)
            dq, dkc, dkp, dvc, dvp, dbias, dsink = _attn_bwd(
                st["qh"], st["kcat"], st["vcat"], biasm, small["attn_sinks"][j], do)
            dbias_tot = dbias if dbias_tot is None else dbias_tot + dbias
            sg["sinks"][j] = dsink[:, :, 0].reshape(n_q)
            dqkv, dqg, dkg = _qk_norm_bwd(st["qkv"], dup(small["attn_q_gain"][j]), dup(small["attn_k_gain"][j]),
                                          dq, dkc, dkp, dvc, dvp, n_q, n_kv)
            sg["q_gain"][j] = dqg[0, :HEAD_DIM]
            sg["k_gain"][j] = dkg[0, :HEAD_DIM]
            big[i]["wqkv"] = _mm_tn_colblk("qkv_bwd_w", st["h"], dqkv, nb)
            dh = _mm_nt_colblk("qkv_bwd_in", dqkv, w["wqkv"])
            dx, dxb, dg = _rms_bwd("rms_mix_bwd", st["x0"], gm, dh, dx)
        else:
            dyp, dsc = _pool_bwd_pre(dx, st["ypre"], pscale[j])
            sg["pool_scale"][j] = dsc.reshape(D)
            big[i]["wp"] = _pool_mm_bwd_w(st["d"], dyp)
            dd = _pool_mm_bwd_in(dyp, w["wp"])
            dx, dxb, dg = _pool_bwd_post(dd, st["x0"], gm, dx)
        sg["norm_mix"][i] = dg.reshape(D)
        before = on_grads(i, "mix", big[i], dx)
    sg["rel_bias"] = _bias_reduce(dbias_tot)[:, 0, :N_BUCKETS]
    return loss_part, dx, sg


_SMALL_ORDER = ("norm_mix", "norm_ffn", "rel_bias", "q_gain", "k_gain", "sinks", "conv_b", "conv_w", "pool_scale")


def kernel(x, norm_mix, norm_ffn, rel_bias, attn_w_qkv, attn_q_gain, attn_k_gain, attn_sinks, attn_w_o, pool_w, pool_scale, ffn_w_up, ffn_conv_w, ffn_conv_b, ffn_w_down, loss_target, m_norm_mix, m_norm_ffn, m_rel_bias, m_attn_w_qkv, m_attn_q_gain, m_attn_k_gain, m_attn_sinks, m_attn_w_o, m_pool_w, m_pool_scale, m_ffn_w_up, m_ffn_conv_w, m_ffn_conv_b, m_ffn_w_down, v_norm_mix, v_norm_ffn, v_rel_bias, v_attn_w_qkv, v_attn_q_gain, v_attn_k_gain, v_attn_sinks, v_attn_w_o, v_pool_w, v_pool_scale, v_ffn_w_up, v_ffn_conv_w, v_ffn_conv_b, v_ffn_w_down):
    _, S, D = x.shape
    depth = ffn_w_up.shape[0]
    n_attn, n_pool = attn_w_qkv.shape[0], pool_w.shape[0]
    nb = N_CHIPS
    nc_up = ffn_w_up.shape[2]
    Fh = nc_up * nb // 2
    cx, cy, cc = _place()
    chip = 2 * cx + cy
    c_arr = jnp.reshape(cc, (1,)).astype(jnp.int32)
    chip_arr = jnp.reshape(chip, (1,)).astype(jnp.int32)

    pool_w3 = pool_w.reshape(n_pool, -1, pool_w.shape[-1])
    order = [("convw", ffn_conv_w.reshape(1, -1, nc_up), 0, F32), ("pscale", pool_scale[None], 0, F32)]
    for i in range(depth):
        j = i // 2
        mixer = ([("wqkv", attn_w_qkv, j), ("wo", attn_w_o, j)] if i % 2 == 0 else [("wp", pool_w3, j)])
        order += [((i, k), w, l, BF16) for k, w, l in mixer + [("wup", ffn_w_up, i), ("wdn", ffn_w_down, i)]]
    first = 4
    started, tokens = [], []
    for part, lo, hi in (("first", 0, first), ("rest", first, len(order))):
        slots = [_cast_slot("slot_%d" % n, w, l, chip_arr, dt) for n, (_, w, l, dt) in enumerate(order[lo:hi], lo)]
        st, token = _gather_start("gather_start_" + part, [[b] for b in slots])
        started += st
        tokens.append(token)
    index = {key: n for n, (key, _, _, _) in enumerate(order)}

    def gathered(key, after):
        n = index[key]
        send, recv, bufs = started[n]
        return _gather_wait("gather_wait_%d" % n, send, recv, bufs, tokens[1] if n == 2 else after)[0]

    convw_g = gathered("convw", tokens[0]).reshape(nb, depth, 3, nc_up)
    pscale_g = gathered("pscale", tokens[0])
    convw = [jnp.transpose(convw_g[:, i], (1, 0, 2)).reshape(3, 2, Fh).transpose(1, 0, 2) for i in range(depth)]
    pscale = [pscale_g[:, j].reshape(1, D) for j in range(n_pool)]
    small = dict(norm_mix=norm_mix, norm_ffn=norm_ffn, rel_bias=rel_bias, attn_q_gain=attn_q_gain,
                 attn_k_gain=attn_k_gain, attn_sinks=attn_sinks, ffn_conv_b=ffn_conv_b)

    G, gd = pool_w.shape[1], pool_w.shape[3]

    def weight(i, name, after):
        w = gathered((i, name), after)
        if name == "wp":
            w = jnp.transpose(w.reshape(nb, G, gd // nb, gd), (1, 0, 2, 3)).reshape(G, gd, gd)
        return w

    red = {k: [None] * (n_attn if k in ("wqkv", "wo") else n_pool if k == "wp" else depth)
           for k in ("wqkv", "wo", "wp", "wup", "wdn")}
    pending = []

    def advance_all(after):
        tokens = []
        for r in list(pending):
            token = r.advance(after)
            if token is None:
                pending.remove(r)
                for k, o in zip(r.names, r.result):
                    red[k][r.layer // 2 if k in ("wqkv", "wo", "wp") else r.layer] = o
            else:
                tokens.append(token)
        return tokens

    def on_grads(i, tag, grads, dx_i):
        tokens = advance_all([dx_i])
        names = sorted(grads)
        flat = []
        for k in names:
            g = grads[k]
            if k == "wp":
                g = jnp.transpose(g.reshape(G, nb, gd // nb, gd), (1, 0, 2, 3))
            flat.append(g.reshape(nb, -1, g.shape[-1]))
        r = _Reduction("%d%s" % (i, tag), flat, chip_arr, c_arr)
        r.names, r.layer = names, i
        pending.append(r)
        return tokens + [r.advance([]) if i == 0 else r.token]

    loss_part, dx, sg = _local_step(x[0], loss_target[0], small, depth, nb, weight, convw, pscale, on_grads)
    while pending:
        advance_all([])
    loss = lax.psum(jnp.sum(loss_part), ("x", "y", "c"))

    g_wqkv = jnp.stack(red["wqkv"])
    g_wo = jnp.stack(red["wo"])
    g_wp = jnp.stack(red["wp"]).reshape(pool_w.shape)
    g_wup = jnp.stack(red["wup"])
    g_wdn = jnp.stack(red["wdn"])

    parts = [jnp.stack(sg[k]) if isinstance(sg[k], list) else sg[k] for k in _SMALL_ORDER]
    sizes = [int(np.prod(p.shape)) for p in parts]
    total = sum(sizes)
    rows = -(-total // (8 * LANES)) * 8
    packed = jnp.concatenate([p.reshape(-1) for p in parts] + [jnp.zeros((rows * LANES - total,), F32)])
    summed = _all_reduce_small(packed.reshape(rows, LANES)).reshape(-1)
    sm, off = {}, 0
    for k, p, n in zip(_SMALL_ORDER, parts, sizes):
        sm[k] = summed[off:off + n].reshape(p.shape)
        off += n
    g_convw = lax.dynamic_slice_in_dim(sm["conv_w"], chip * nc_up, nc_up, axis=2)
    pc = pool_scale.shape[1]
    g_pscale = lax.dynamic_slice_in_dim(sm["pool_scale"], chip * pc, pc, axis=1)

    grads = [sm["norm_mix"], sm["norm_ffn"], sm["rel_bias"], g_wqkv, sm["q_gain"], sm["k_gain"], sm["sinks"], g_wo,
             g_wp, g_pscale, g_wup, g_convw, sm["conv_b"], g_wdn]
    ws = [norm_mix, norm_ffn, rel_bias, attn_w_qkv, attn_q_gain, attn_k_gain, attn_sinks, attn_w_o, pool_w, pool_scale,
          ffn_w_up, ffn_conv_w, ffn_conv_b, ffn_w_down]
    ms = [m_norm_mix, m_norm_ffn, m_rel_bias, m_attn_w_qkv, m_attn_q_gain, m_attn_k_gain, m_attn_sinks, m_attn_w_o,
          m_pool_w, m_pool_scale, m_ffn_w_up, m_ffn_conv_w, m_ffn_conv_b, m_ffn_w_down]
    vs = [v_norm_mix, v_norm_ffn, v_rel_bias, v_attn_w_qkv, v_attn_q_gain, v_attn_k_gain, v_attn_sinks, v_attn_w_o,
          v_pool_w, v_pool_scale, v_ffn_w_up, v_ffn_conv_w, v_ffn_conv_b, v_ffn_w_down]
    deltas, new_m, new_v = [], [], []
    for idx, (w, g, m, v) in enumerate(zip(ws, grads, ms, vs)):
        d, nm, nv = _adamw("adamw_%d" % idx, w, g, m, v)
        deltas.append(d), new_m.append(nm), new_v.append(nv)
    return (loss, dx.reshape(1, S, D), *grads, *deltas, *new_m, *new_v)
```
